```python
import math
import jax
import jax.numpy as jnp
from jax import lax
import numpy as np

D_MODEL = 2048
BATCH = 4
SEQ = 8192
DEPTH = 2

GRID_W = 64
CTX_LEN = 256
EPS = 1e-6
NEG = -1e30
F32 = jnp.float32

ATT_HEADS = 8
ATT_KV_HEADS = 2
ATT_GROUP = ATT_HEADS // ATT_KV_HEADS
HEAD_DIM = 128
ATT_WIDTH = ATT_HEADS * HEAD_DIM
KV_WIDTH = ATT_KV_HEADS * HEAD_DIM
WINDOW = 128
BLOCK = 128
ROPE_BASE = 10000.0

POOL_WINDOWS = (2, 4, 8, 16)
POOL_GROUP = 128
POOL_WIDTH = POOL_GROUP * len(POOL_WINDOWS)

HY_WIDTH = 512
HY_ORDER = 2
HY_PROJ = (HY_ORDER + 1) * HY_WIDTH
HY_SHORT = 3
HY_EMB = 33
HY_BANDS = (HY_EMB - 1) // 2
HY_FILTER_HIDDEN = 64
HY_SHORT_DECAY_PCT = 0.3
HY_LONG_DECAY_PCT = 1.5
HY_TARGET = 1e-2

MIX_WIDTH = ATT_WIDTH + POOL_WIDTH + HY_WIDTH
IN_WIDTH = ATT_WIDTH + 2 * KV_WIDTH + POOL_WIDTH + HY_PROJ

D_FF = 5632
N_EXPERTS = 8
TOP_K = 2
N_DENSE = (DEPTH + 1) // 2
N_MOE = DEPTH // 2

kernel_name = 'hybrid_prefix_dit_block'


def rmsnorm(x, g):
    xf = x.astype(F32)
    y = xf * lax.rsqrt(jnp.mean(xf * xf, axis=-1, keepdims=True) + EPS)
    return (y * g.astype(F32)).astype(x.dtype)


def modulate(h, shift, scale):
    return h * (1.0 + scale[:, None]) + shift[:, None]


def axial_rope_tables(n_tokens):
    rows = n_tokens // GRID_W
    row = jnp.repeat(jnp.arange(rows, dtype=F32), GRID_W)
    col = jnp.tile(jnp.arange(GRID_W, dtype=F32), rows)
    n_freq = HEAD_DIM // 4
    inv = ROPE_BASE ** (-jnp.arange(n_freq, dtype=F32) / n_freq)
    ang_r = row[:, None] * inv
    ang_c = col[:, None] * inv
    return (jnp.cos(ang_r), jnp.sin(ang_r), jnp.cos(ang_c), jnp.sin(ang_c))


def _rotate(x, cos, sin):
    x1, x2 = jnp.split(x, 2, axis=-1)
    cos = cos[:, None, :].astype(x.dtype)
    sin = sin[:, None, :].astype(x.dtype)
    return jnp.concatenate([x1 * cos - x2 * sin, x2 * cos + x1 * sin], axis=-1)


def apply_axial_rope(x, tabs):
    cr, sr, cc, sc = tabs
    half = HEAD_DIM // 2
    return jnp.concatenate([_rotate(x[..., :half], cr, sr), _rotate(x[..., half:], cc, sc)], axis=-1)


def split_proj(p):
    lead = p.shape[:-1]
    o = 0
    q = p[..., o:o + ATT_WIDTH].reshape(lead + (ATT_HEADS, HEAD_DIM))
    o += ATT_WIDTH
    k = p[..., o:o + KV_WIDTH].reshape(lead + (ATT_KV_HEADS, HEAD_DIM))
    o += KV_WIDTH
    v = p[..., o:o + KV_WIDTH].reshape(lead + (ATT_KV_HEADS, HEAD_DIM))
    o += KV_WIDTH
    pool_u = p[..., o:o + POOL_WIDTH]
    o += POOL_WIDTH
    hy_u = p[..., o:o + HY_PROJ]
    return q, k, v, pool_u, hy_u


def window_attention(q, k, v, k_ctx, v_ctx, sink):
    B, S = q.shape[0], q.shape[1]
    C = k_ctx.shape[1]
    nb = S // BLOCK
    qb = q.reshape(B, nb, BLOCK, ATT_KV_HEADS, ATT_GROUP, HEAD_DIM) * (HEAD_DIM ** -0.5)

    def band(t):
        tp = jnp.pad(t, ((0, 0), (BLOCK, BLOCK), (0, 0), (0, 0)))
        tb = tp.reshape(B, nb + 2, BLOCK, ATT_KV_HEADS, HEAD_DIM)
        return jnp.concatenate([tb[:, :-2], tb[:, 1:-1], tb[:, 2:]], axis=2)

    kw, vw = band(k), band(v)
    s_loc = jnp.einsum('bnqhgd,bnkhd->bnhgqk', qb, kw).astype(F32)
    qi = jnp.arange(BLOCK)[:, None]
    ki = jnp.arange(3 * BLOCK)[None, :]
    kpos = jnp.arange(nb)[:, None, None] * BLOCK + (ki - BLOCK)[None]
    mask = (jnp.abs(ki - BLOCK - qi) <= WINDOW)[None] & (kpos >= 0) & (kpos < S)
    s_loc = jnp.where(mask[None, :, None, None], s_loc, NEG)
    s_ctx = jnp.einsum('bnqhgd,bchd->bnhgqc', qb, k_ctx).astype(F32)
    s_sink = jnp.broadcast_to(sink.astype(F32).reshape(1, 1, ATT_KV_HEADS, ATT_GROUP, 1, 1),
                              s_loc.shape[:-1] + (1,))
    p = jax.nn.softmax(jnp.concatenate([s_loc, s_ctx, s_sink], axis=-1), axis=-1).astype(v.dtype)
    o = (jnp.einsum('bnhgqk,bnkhd->bnqhgd', p[..., :3 * BLOCK], vw)
         + jnp.einsum('bnhgqc,bchd->bnqhgd', p[..., 3 * BLOCK:3 * BLOCK + C], v_ctx))
    return o.reshape(B, S, ATT_WIDTH)


def context_attention(q, k, v, sink):
    B, C = q.shape[0], q.shape[1]
    qg = q.reshape(B, C, ATT_KV_HEADS, ATT_GROUP, HEAD_DIM) * (HEAD_DIM ** -0.5)
    s = jnp.einsum('bqhgd,bkhd->bhgqk', qg, k).astype(F32)
    s_sink = jnp.broadcast_to(sink.astype(F32).reshape(1, ATT_KV_HEADS, ATT_GROUP, 1, 1),
                              s.shape[:-1] + (1,))
    p = jax.nn.softmax(jnp.concatenate([s, s_sink], axis=-1), axis=-1).astype(v.dtype)
    o = jnp.einsum('bhgqk,bkhd->bqhgd', p[..., :C], v)
    return o.reshape(B, C, ATT_WIDTH)


def pool_mixer(u, w, scale):
    B, L, _ = u.shape
    uf = u.astype(F32)
    pos = jnp.arange(L)
    outs = []
    for g, win in enumerate(POOL_WINDOWS):
        ug = uf[..., g * POOL_GROUP:(g + 1) * POOL_GROUP]
        cs = jnp.concatenate([jnp.zeros((B, 1, POOL_GROUP), F32), jnp.cumsum(ug, axis=1)], axis=1)
        lo = jnp.clip(pos - win // 2, 0, L)
        hi = jnp.clip(pos + win - win // 2, 0, L)
        mean = (cs[:, hi] - cs[:, lo]) / (hi - lo).astype(F32)[None, :, None]
        outs.append((mean - ug) @ w[g].astype(F32))
    return (jnp.concatenate(outs, axis=-1) * scale.astype(F32)).astype(u.dtype)


def hyena_filters(L, w1, b1, w2, b2, w3, b3, freq):
    m = jnp.arange(L, dtype=F32)
    t = m / max(L - 1, 1)
    w = 2.0 * math.pi * m / L
    f = jnp.linspace(1e-4, HY_BANDS - 1, HY_BANDS, dtype=F32)
    z = jnp.concatenate([t[:, None], jnp.cos(w[:, None] * f), -jnp.sin(w[:, None] * f)], axis=-1)
    fr = freq.astype(F32)
    a = jnp.sin(fr * (z @ w1.astype(F32) + b1.astype(F32)))
    a = jnp.sin(fr * (a @ w2.astype(F32) + b2.astype(F32)))
    h = a @ w3.astype(F32) + b3.astype(F32)
    max_decay = math.log(HY_TARGET) / HY_SHORT_DECAY_PCT
    min_decay = math.log(HY_TARGET) / HY_LONG_DECAY_PCT
    deltas = jnp.linspace(min_decay, max_decay, HY_WIDTH, dtype=F32)
    decay = jnp.exp(-t[:, None] * jnp.abs(deltas)[None])
    h = h.reshape(L, HY_ORDER, 2, HY_WIDTH) * decay[:, None, None, :]
    fwd, bwd = h[:, :, 0], h[:, :, 1]
    two_sided = jnp.concatenate([fwd, jnp.zeros((1, HY_ORDER, HY_WIDTH), F32), bwd[1:][::-1]], axis=0)
    return jnp.fft.rfft(two_sided, axis=0)


def hyena_mixer(u, conv_w, conv_b, kf, d_bias):
    out_dtype = u.dtype
    L = u.shape[1]
    pad = HY_SHORT // 2
    up = jnp.pad(u, ((0, 0), (pad, pad), (0, 0)))
    u = sum(up[:, j:j + L] * conv_w[j] for j in range(HY_SHORT)) + conv_b
    v, x1, x2 = jnp.split(u.astype(F32), HY_ORDER + 1, axis=-1)
    gates = (x1, x2)
    z = v
    for o in range(HY_ORDER):
        zf = jnp.fft.rfft(z, n=2 * L, axis=1)
        conv = jnp.fft.irfft(zf * kf[None, :, o], n=2 * L, axis=1)[:, :L]
        z = gates[o] * (conv + z * d_bias[o].astype(F32))
    return z.astype(out_dtype)


def swiglu(h, w1, w3, w2):
    return (jax.nn.silu(h @ w1) * (h @ w3)) @ w2


def moe_swiglu(h, router_w, w1, w3, w2):
    shp = h.shape
    t = h.reshape(-1, shp[-1])
    logits = (t @ router_w).astype(F32)
    top_v, top_i = lax.top_k(logits, TOP_K)
    wts = jax.nn.softmax(top_v, axis=-1)
    gates = jnp.sum(jax.nn.one_hot(top_i, N_EXPERTS, dtype=F32) * wts[..., None], axis=1)
    out = jnp.zeros_like(t)
    for e in range(N_EXPERTS):
        out = out + gates[:, e:e + 1].astype(t.dtype) * swiglu(t, w1[e], w3[e], w2[e])
    return out.reshape(shp)


def setup_inputs(seed: int = 0) -> dict:
    key = jax.random.key(seed)
    ks = iter(jax.random.split(key, 48))
    D = D_MODEL
    Lr = DEPTH

    def nrm(shape, s):
        return jax.random.normal(next(ks), shape, F32) * s

    return {
        'x': nrm((BATCH, SEQ, D), 1.0),
        'c': nrm((BATCH, D), 1.0),
        'ctx': nrm((BATCH, CTX_LEN, D), 1.0),
        'c_ctx': nrm((D,), 1.0),
        'w_mod': nrm((Lr, D, 6 * D), 0.5 * D ** -0.5),
        'b_mod': nrm((Lr, 6 * D), 0.02),
        'norm1_g': 1.0 + nrm((Lr, D), 0.02),
        'w_in': nrm((Lr, D, IN_WIDTH), D ** -0.5),
        'attn_sink': nrm((Lr, ATT_HEADS), 1.0),
        'pool_w': nrm((Lr, len(POOL_WINDOWS), POOL_GROUP, POOL_GROUP), POOL_GROUP ** -0.5),
        'pool_scale': 1.0 + nrm((Lr, POOL_WIDTH), 0.1),
        'hy_conv_w': nrm((Lr, HY_SHORT, HY_PROJ), HY_SHORT ** -0.5),
        'hy_conv_b': nrm((Lr, HY_PROJ), 0.02),
        'hy_f_w1': nrm((Lr, HY_EMB, HY_FILTER_HIDDEN), HY_EMB ** -0.5),
        'hy_f_b1': nrm((Lr, HY_FILTER_HIDDEN), 0.02),
        'hy_f_w2': nrm((Lr, HY_FILTER_HIDDEN, HY_FILTER_HIDDEN), HY_FILTER_HIDDEN ** -0.5),
        'hy_f_b2': nrm((Lr, HY_FILTER_HIDDEN), 0.02),
        'hy_f_w3': nrm((Lr, HY_FILTER_HIDDEN, HY_ORDER * 2 * HY_WIDTH), HY_FILTER_HIDDEN ** -0.5),
        'hy_f_b3': nrm((Lr, HY_ORDER * 2 * HY_WIDTH), 0.02),
        'hy_f_freq': 1.0 + nrm((Lr, HY_FILTER_HIDDEN), 0.1),
        'hy_bias': nrm((Lr, HY_ORDER, HY_WIDTH), 1.0),
        'g_attn': 1.0 + nrm((Lr, ATT_WIDTH), 0.02),
        'g_pool': 1.0 + nrm((Lr, POOL_WIDTH), 0.02),
        'g_hyena': 1.0 + nrm((Lr, HY_WIDTH), 0.02),
        'w_out': nrm((Lr, MIX_WIDTH, D), MIX_WIDTH ** -0.5),
        'norm2_g': 1.0 + nrm((Lr, D), 0.02),
        'ff_w1': nrm((N_DENSE, D, D_FF), D ** -0.5),
        'ff_w3': nrm((N_DENSE, D, D_FF), D ** -0.5),
        'ff_w2': nrm((N_DENSE, D_FF, D), D_FF ** -0.5),
        'router_w': nrm((N_MOE, D, N_EXPERTS), D ** -0.5),
        'moe_w1': nrm((N_MOE, N_EXPERTS, D, D_FF), D ** -0.5),
        'moe_w3': nrm((N_MOE, N_EXPERTS, D, D_FF), D ** -0.5),
        'moe_w2': nrm((N_MOE, N_EXPERTS, D_FF, D), D_FF ** -0.5),
        'final_g': 1.0 + nrm((D,), 0.02),
    }


def reference(x, c, ctx, c_ctx, w_mod, b_mod, norm1_g, w_in, attn_sink, pool_w, pool_scale,
              hy_conv_w, hy_conv_b, hy_f_w1, hy_f_b1, hy_f_w2, hy_f_b2, hy_f_w3, hy_f_b3,
              hy_f_freq, hy_bias, g_attn, g_pool, g_hyena, w_out, norm2_g,
              ff_w1, ff_w3, ff_w2, router_w, moe_w1, moe_w3, moe_w2, final_g):
    S = x.shape[1]
    D = x.shape[2]
    rope = axial_rope_tables(S)
    silu_c = jax.nn.silu(c)
    silu_cc = jax.nn.silu(c_ctx)[None]
    xc = ctx

    def hyena_branch(l, u):
        kf = hyena_filters(u.shape[1], hy_f_w1[l], hy_f_b1[l], hy_f_w2[l], hy_f_b2[l],
                           hy_f_w3[l], hy_f_b3[l], hy_f_freq[l])
        return hyena_mixer(u, hy_conv_w[l], hy_conv_b[l], kf, hy_bias[l])

    def merge(l, y_att, pool_u, hy_u):
        y = jnp.concatenate([
            rmsnorm(y_att, g_attn[l]),
            rmsnorm(pool_mixer(pool_u, pool_w[l], pool_scale[l]), g_pool[l]),
            rmsnorm(hyena_branch(l, hy_u), g_hyena[l]),
        ], axis=-1)
        return y @ w_out[l]

    def channel_mixer(l, h):
        i = l // 2
        if l % 2 == 0:
            return swiglu(h, ff_w1[i], ff_w3[i], ff_w2[i])
        return moe_swiglu(h, router_w[i], moe_w1[i], moe_w3[i], moe_w2[i])

    for l in range(DEPTH):
        last = l == DEPTH - 1
        mod = silu_c @ w_mod[l] + b_mod[l]
        sh1, sc1, g1, sh2, sc2, g2 = jnp.split(mod, 6, axis=-1)
        h = modulate(rmsnorm(x, norm1_g[l]), sh1, sc1)
        q, k, v, pool_u, hy_u = split_proj(h @ w_in[l])

        if last:
            mod_c = silu_cc @ w_mod[l][:, :2 * D] + b_mod[l][:2 * D]
            csh1, csc1 = jnp.split(mod_c, 2, axis=-1)
            hc = modulate(rmsnorm(xc, norm1_g[l]), csh1, csc1)
            kv_c = hc @ w_in[l][:, ATT_WIDTH:ATT_WIDTH + 2 * KV_WIDTH]
            lead = kv_c.shape[:-1]
            k_c = kv_c[..., :KV_WIDTH].reshape(lead + (ATT_KV_HEADS, HEAD_DIM))
            v_c = kv_c[..., KV_WIDTH:].reshape(lead + (ATT_KV_HEADS, HEAD_DIM))
        else:
            mod_c = silu_cc @ w_mod[l] + b_mod[l]
            csh1, csc1, cg1, csh2, csc2, cg2 = jnp.split(mod_c, 6, axis=-1)
            hc = modulate(rmsnorm(xc, norm1_g[l]), csh1, csc1)
            q_c, k_c, v_c, pool_c, hy_c = split_proj(hc @ w_in[l])
            yc = merge(l, context_attention(q_c, k_c, v_c, attn_sink[l]), pool_c, hy_c)
            xc_new = xc + cg1[:, None] * yc
            hc2 = modulate(rmsnorm(xc_new, norm2_g[l]), csh2, csc2)
            xc_new = xc_new + cg2[:, None] * channel_mixer(l, hc2)

        q = apply_axial_rope(q, rope)
        k = apply_axial_rope(k, rope)
        y_att = window_attention(q, k, v, k_c, v_c, attn_sink[l])
        x = x + g1[:, None] * merge(l, y_att, pool_u, hy_u)
        h2 = modulate(rmsnorm(x, norm2_g[l]), sh2, sc2)
        x = x + g2[:, None] * channel_mixer(l, h2)

        if not last:
            xc = xc_new

    return rmsnorm(x, final_g)
```

```python
import functools
import math

import numpy as np
import jax
import jax.numpy as jnp
from jax import lax
from jax.experimental import pallas as pl
from jax.experimental.pallas import tpu as pltpu

F32 = jnp.float32
BF16 = jnp.bfloat16

EPS = 1e-6
NEG = -1e30
GRID_W = 64
ATT_HEADS = 8
ATT_KV_HEADS = 2
ATT_GROUP = ATT_HEADS // ATT_KV_HEADS
HEAD_DIM = 128
ATT_WIDTH = ATT_HEADS * HEAD_DIM
KV_WIDTH = ATT_KV_HEADS * HEAD_DIM
WINDOW = 128
BLOCK = 128
ROPE_BASE = 10000.0
POOL_WINDOWS = (2, 4, 8, 16)
POOL_GROUP = 128
POOL_WIDTH = POOL_GROUP * len(POOL_WINDOWS)
HY_WIDTH = 512
HY_ORDER = 2
HY_SHORT = 3
HY_EMB = 33
HY_BANDS = (HY_EMB - 1) // 2
HY_SHORT_DECAY_PCT = 0.3
HY_LONG_DECAY_PCT = 1.5
HY_TARGET = 1e-2
N_EXPERTS = 8
TOP_K = 2

LANES = 128
SUBLANES = 8
HALO = SUBLANES
VMEM_LIMIT = 56 * 1024 * 1024


def _cp(sem):
    return pltpu.CompilerParams(dimension_semantics=sem, vmem_limit_bytes=VMEM_LIMIT)


def _tile(n, pref, mult=SUBLANES):
    if n <= pref:
        return n
    t = (pref // mult) * mult
    while t >= mult:
        if n % t == 0:
            return t
        t -= mult
    return n


def _split_bf16(x):
    hi = x.astype(BF16)
    lo = (x - hi.astype(F32)).astype(BF16)
    return hi, lo


def _dot(a, b):
    return jnp.dot(a, b, preferred_element_type=F32)


def _dot3(a, b):
    ah, al = _split_bf16(a)
    bh, bl = _split_bf16(b)
    return _dot(ah, bh) + (_dot(ah, bl) + _dot(al, bh))


def _adaln_kernel(c_ref, w_ref, b_ref, o_ref):
    c = c_ref[...]
    a = c * (1.0 / (1.0 + jnp.exp(-c)))
    o_ref[...] = _dot3(a, w_ref[...]) + b_ref[...]


def _adaln(c_rows, w, b):
    R, D = c_rows.shape
    N = w.shape[1]
    tn = _tile(N, 1536, LANES)
    return pl.pallas_call(
        _adaln_kernel,
        out_shape=jax.ShapeDtypeStruct((R, N), F32),
        grid=(N // tn,),
        in_specs=[pl.BlockSpec((R, D), lambda j: (0, 0)),
                  pl.BlockSpec((D, tn), lambda j: (0, j)),
                  pl.BlockSpec((1, tn), lambda j: (0, j))],
        out_specs=pl.BlockSpec((R, tn), lambda j: (0, j)),
        compiler_params=_cp(("arbitrary",)),
        name="adaln",
    )(c_rows, w, b.reshape(1, N))


def _norm_mod_kernel(x_ref, g_ref, sh_ref, sc_ref, o_ref):
    x = x_ref[...]
    y = x * lax.rsqrt(jnp.mean(x * x, axis=-1, keepdims=True) + EPS) * g_ref[...]
    o_ref[...] = (y * (1.0 + sc_ref[...]) + sh_ref[...]).astype(o_ref.dtype)


def _bcast_map(arr):
    if arr.shape[0] == 1:
        return lambda b, i: (0, 0, 0)
    return lambda b, i: (b, 0, 0)


def _norm_mod(x, g, shift, scale):
    B, L, D = x.shape
    ts = _tile(L, 512)
    shift = shift[:, None, :]
    scale = scale[:, None, :]
    return pl.pallas_call(
        _norm_mod_kernel,
        out_shape=jax.ShapeDtypeStruct((B, L, D), BF16),
        grid=(B, L // ts),
        in_specs=[pl.BlockSpec((None, ts, D), lambda b, i: (b, i, 0)),
                  pl.BlockSpec((1, D), lambda b, i: (0, 0)),
                  pl.BlockSpec((None, 1, D), _bcast_map(shift)),
                  pl.BlockSpec((None, 1, D), _bcast_map(scale))],
        out_specs=pl.BlockSpec((None, ts, D), lambda b, i: (b, i, 0)),
        compiler_params=_cp(("parallel", "parallel")),
        name="norm_mod",
    )(x, g.reshape(1, D), shift, scale)


def _norm_mod_router_kernel(x_ref, g_ref, sh_ref, sc_ref, rw_ref, o_ref, gi_ref, gw_ref):
    x = x_ref[...]
    y = x * lax.rsqrt(jnp.mean(x * x, axis=-1, keepdims=True) + EPS) * g_ref[...]
    h = y * (1.0 + sc_ref[...]) + sh_ref[...]
    o_ref[...] = h.astype(o_ref.dtype)
    logits = _dot3(h, rw_ref[...])
    lane = lax.broadcasted_iota(jnp.int32, logits.shape, 1)
    logits = jnp.where(lane < N_EXPERTS, logits, NEG)
    m1 = jnp.max(logits, axis=-1, keepdims=True)
    i1 = jnp.min(jnp.where(logits == m1, lane, LANES), axis=-1, keepdims=True)
    rest = jnp.where(lane == i1, NEG, logits)
    m2 = jnp.max(rest, axis=-1, keepdims=True)
    i2 = jnp.min(jnp.where(rest == m2, lane, LANES), axis=-1, keepdims=True)
    e2 = jnp.exp(m2 - m1)
    w1 = 1.0 / (1.0 + e2)
    w2 = e2 * w1
    gi_ref[...] = jnp.where(lane == 0, i1, jnp.where(lane == 1, i2, 0))
    gw_ref[...] = jnp.where(lane == 0, w1, jnp.where(lane == 1, w2, 0.0))


def _norm_mod_router(x, g, shift, scale, router_w):
    B, L, D = x.shape
    ts = _tile(L, 512)
    shift = shift[:, None, :]
    scale = scale[:, None, :]
    rw = jnp.pad(router_w, ((0, 0), (0, LANES - router_w.shape[1])))
    row = pl.BlockSpec((None, ts, D), lambda b, i: (b, i, 0))
    small = pl.BlockSpec((None, ts, LANES), lambda b, i: (b, i, 0))
    return pl.pallas_call(
        _norm_mod_router_kernel,
        out_shape=(jax.ShapeDtypeStruct((B, L, D), BF16),
                   jax.ShapeDtypeStruct((B, L, LANES), jnp.int32),
                   jax.ShapeDtypeStruct((B, L, LANES), F32)),
        grid=(B, L // ts),
        in_specs=[row,
                  pl.BlockSpec((1, D), lambda b, i: (0, 0)),
                  pl.BlockSpec((None, 1, D), _bcast_map(shift)),
                  pl.BlockSpec((None, 1, D), _bcast_map(scale)),
                  pl.BlockSpec((D, LANES), lambda b, i: (0, 0))],
        out_specs=(row, small, small),
        compiler_params=_cp(("parallel", "parallel")),
        name="norm_mod_router",
    )(x, g.reshape(1, D), shift, scale, rw)


def _rmsnorm_kernel(x_ref, g_ref, o_ref):
    x = x_ref[...]
    y = x * lax.rsqrt(jnp.mean(x * x, axis=-1, keepdims=True) + EPS) * g_ref[...]
    o_ref[...] = y.astype(o_ref.dtype)


def _rmsnorm(x, g, out_dtype):
    B, L, D = x.shape
    ts = _tile(L, 512)
    return pl.pallas_call(
        _rmsnorm_kernel,
        out_shape=jax.ShapeDtypeStruct((B, L, D), out_dtype),
        grid=(B, L // ts),
        in_specs=[pl.BlockSpec((None, ts, D), lambda b, i: (b, i, 0)),
                  pl.BlockSpec((1, D), lambda b, i: (0, 0))],
        out_specs=pl.BlockSpec((None, ts, D), lambda b, i: (b, i, 0)),
        compiler_params=_cp(("parallel", "parallel")),
        name="rmsnorm",
    )(x, g.reshape(1, D))


def _mm_kernel(a_ref, w_ref, o_ref):
    o_ref[...] = _dot(a_ref[...], w_ref[...]).astype(o_ref.dtype)


def _matmul(a, w, out_dtype=F32, tm=1024, tn=512):
    B, L, K = a.shape
    N = w.shape[1]
    tm = _tile(L, tm)
    tn = _tile(N, tn, LANES)
    return pl.pallas_call(
        _mm_kernel,
        out_shape=jax.ShapeDtypeStruct((B, L, N), out_dtype),
        grid=(B, L // tm, N // tn),
        in_specs=[pl.BlockSpec((None, tm, K), lambda b, i, j: (b, i, 0)),
                  pl.BlockSpec((K, tn), lambda b, i, j: (0, j))],
        out_specs=pl.BlockSpec((None, tm, tn), lambda b, i, j: (b, i, j)),
        compiler_params=_cp(("parallel", "parallel", "arbitrary")),
        name="matmul",
    )(a, w)


def _mm_res_kernel(*refs, n_a):
    a_refs = refs[:n_a]
    w_refs = refs[n_a:2 * n_a]
    res_ref, gate_ref, o_ref = refs[2 * n_a:]
    acc = _dot(a_refs[0][...], w_refs[0][...])
    for a_ref, w_ref in zip(a_refs[1:], w_refs[1:]):
        acc = acc + _dot(a_ref[...], w_ref[...])
    o_ref[...] = res_ref[...] + gate_ref[...] * acc


def _matmul_gated_residual(a_list, w, res, gate, tm=1024, tn=512):
    B, L, N = res.shape
    tm = _tile(L, tm)
    tn = _tile(N, tn, LANES)
    gate = gate[:, None, :]
    widths = [a.shape[-1] for a in a_list]
    unit = math.gcd(*widths) if len(widths) > 1 else widths[0]
    in_specs = [pl.BlockSpec((None, tm, k), lambda b, i, j: (b, i, 0)) for k in widths]
    off = 0
    for k in widths:
        assert off % k == 0 and k % unit == 0
        in_specs.append(pl.BlockSpec((k, tn), functools.partial(
            lambda b, i, j, blk: (blk, j), blk=off // k)))
        off += k
    gmap = (lambda b, i, j: (0, 0, j)) if gate.shape[0] == 1 else (lambda b, i, j: (b, 0, j))
    in_specs += [pl.BlockSpec((None, tm, tn), lambda b, i, j: (b, i, j)),
                 pl.BlockSpec((None, 1, tn), gmap)]
    return pl.pallas_call(
        functools.partial(_mm_res_kernel, n_a=len(a_list)),
        out_shape=jax.ShapeDtypeStruct((B, L, N), F32),
        grid=(B, L // tm, N // tn),
        in_specs=in_specs,
        out_specs=pl.BlockSpec((None, tm, tn), lambda b, i, j: (b, i, j)),
        compiler_params=_cp(("parallel", "parallel", "arbitrary")),
        name="matmul_gated_residual",
    )(*a_list, *([w] * len(a_list)), res, gate)


def _silu(x):
    return x * (1.0 / (1.0 + jnp.exp(-x)))


def _swiglu_kernel(a_ref, w1_ref, w3_ref, o_ref):
    a = a_ref[...]
    o_ref[...] = (_silu(_dot(a, w1_ref[...])) * _dot(a, w3_ref[...])).astype(o_ref.dtype)


def _swiglu_hidden(a, w1, w3, tm=1024, tf=512):
    B, L, D = a.shape
    F = w1.shape[1]
    tm = _tile(L, tm)
    tf = _tile(F, tf, LANES)
    wspec = pl.BlockSpec((D, tf), lambda b, i, j: (0, j))
    return pl.pallas_call(
        _swiglu_kernel,
        out_shape=jax.ShapeDtypeStruct((B, L, F), BF16),
        grid=(B, L // tm, F // tf),
        in_specs=[pl.BlockSpec((None, tm, D), lambda b, i, j: (b, i, 0)), wspec, wspec],
        out_specs=pl.BlockSpec((None, tm, tf), lambda b, i, j: (b, i, j)),
        compiler_params=_cp(("parallel", "parallel", "arbitrary")),
        name="swiglu_hidden",
    )(a, w1, w3)


def _moe_hidden_kernel(te_ref, nt_ref, a_ref, w1_ref, w3_ref, o_ref):
    used = pl.program_id(0) < nt_ref[0]

    @pl.when(used)
    def _():
        a = a_ref[...]
        o_ref[...] = (_silu(_dot(a, w1_ref[...])) * _dot(a, w3_ref[...])).astype(o_ref.dtype)

    @pl.when(jnp.logical_not(used))
    def _():
        o_ref[...] = jnp.zeros(o_ref.shape, o_ref.dtype)


def _moe_out_kernel(te_ref, nt_ref, a_ref, w2_ref, rw_ref, o_ref):
    used = pl.program_id(0) < nt_ref[0]

    @pl.when(used)
    def _():
        o_ref[...] = (_dot(a_ref[...], w2_ref[...]) * rw_ref[...]).astype(o_ref.dtype)

    @pl.when(jnp.logical_not(used))
    def _():
        o_ref[...] = jnp.zeros(o_ref.shape, o_ref.dtype)


def _moe_experts(a_sorted, tile_expert, n_tiles_used, row_w, w1, w3, w2, tm, tf=512, tn=512):
    R, D = a_sorted.shape
    E, _, F = w1.shape
    tf = _tile(F, tf, LANES)
    tn = _tile(D, tn, LANES)
    nt = R // tm
    w13 = pl.BlockSpec((None, D, tf), lambda i, j, te, n: (te[i], 0, j))
    hidden = pl.pallas_call(
        _moe_hidden_kernel,
        out_shape=jax.ShapeDtypeStruct((R, F), BF16),
        grid_spec=pltpu.PrefetchScalarGridSpec(
            num_scalar_prefetch=2,
            grid=(nt, F // tf),
            in_specs=[pl.BlockSpec((tm, D), lambda i, j, te, n: (i, 0)), w13, w13],
            out_specs=pl.BlockSpec((tm, tf), lambda i, j, te, n: (i, j))),
        compiler_params=_cp(("arbitrary", "arbitrary")),
        name="moe_hidden",
    )(tile_expert, n_tiles_used, a_sorted, w1, w3)
    return pl.pallas_call(
        _moe_out_kernel,
        out_shape=jax.ShapeDtypeStruct((R, D), F32),
        grid_spec=pltpu.PrefetchScalarGridSpec(
            num_scalar_prefetch=2,
            grid=(nt, D // tn),
            in_specs=[pl.BlockSpec((tm, F), lambda i, j, te, n: (i, 0)),
                      pl.BlockSpec((None, F, tn), lambda i, j, te, n: (te[i], 0, j)),
                      pl.BlockSpec((tm, 1), lambda i, j, te, n: (i, 0))],
            out_specs=pl.BlockSpec((tm, tn), lambda i, j, te, n: (i, j))),
        compiler_params=_cp(("arbitrary", "arbitrary")),
        name="moe_out",
    )(tile_expert, n_tiles_used, hidden, w2, row_w)


def _moe_combine_kernel(x_ref, ya_ref, yb_ref, gate_ref, g_ref, o_ref):
    x = x_ref[...] + gate_ref[...] * (ya_ref[...] + yb_ref[...])
    y = x * lax.rsqrt(jnp.mean(x * x, axis=-1, keepdims=True) + EPS) * g_ref[...]
    o_ref[...] = y


def _moe_combine_norm(x, ya, yb, gate, g):
    B, L, D = x.shape
    ts = _tile(L, 512)
    row = pl.BlockSpec((None, ts, D), lambda b, i: (b, i, 0))
    return pl.pallas_call(
        _moe_combine_kernel,
        out_shape=jax.ShapeDtypeStruct((B, L, D), F32),
        grid=(B, L // ts),
        in_specs=[row, row, row,
                  pl.BlockSpec((None, 1, D), lambda b, i: (b, 0, 0)),
                  pl.BlockSpec((1, D), lambda b, i: (0, 0))],
        out_specs=row,
        compiler_params=_cp(("parallel", "parallel")),
        name="moe_combine_norm",
    )(x, ya, yb, gate[:, None, :], g.reshape(1, D))


def _moe_layer(x, h, gi, gw, gate, final_g, w1, w3, w2):
    B, L, D = x.shape
    E = w1.shape[0]
    n_tok = B * L
    n_pair = n_tok * TOP_K
    tm = _tile(n_pair, 1024)
    nt = n_pair // tm + E
    e_pair = gi[..., :TOP_K].reshape(n_pair)
    w_pair = gw[..., :TOP_K].reshape(n_pair)
    onehot = (e_pair[:, None] == jnp.arange(E, dtype=jnp.int32)[None, :]).astype(jnp.int32)
    rank = jnp.sum((jnp.cumsum(onehot, axis=0) - 1) * onehot, axis=1)
    counts = jnp.sum(onehot, axis=0)
    tiles_per = (counts + tm - 1) // tm
    tile_end = jnp.cumsum(tiles_per)
    tile_start = tile_end - tiles_per
    pos = tile_start[e_pair] * tm + rank
    tile_ids = jnp.arange(nt, dtype=jnp.int32)
    tile_expert = jnp.minimum(
        jnp.sum((tile_ids[:, None] >= tile_end[None, :]).astype(jnp.int32), axis=1), E - 1)
    n_used = tile_end[-1:].astype(jnp.int32)
    tok_pair = jnp.arange(n_pair, dtype=jnp.int32) // TOP_K
    src = jnp.zeros((nt * tm,), jnp.int32).at[pos].set(tok_pair)
    row_w = jnp.zeros((nt * tm,), F32).at[pos].set(w_pair)
    a_sorted = jnp.take(h.reshape(n_tok, D), src, axis=0)
    y_sorted = _moe_experts(a_sorted, tile_expert.astype(jnp.int32), n_used,
                            row_w[:, None], w1, w3, w2, tm)
    pos2 = pos.reshape(n_tok, TOP_K)
    ya = jnp.take(y_sorted, pos2[:, 0], axis=0).reshape(B, L, D)
    yb = jnp.take(y_sorted, pos2[:, 1], axis=0).reshape(B, L, D)
    return _moe_combine_norm(x, ya, yb, gate, final_g)


def _rope_tables(n_tokens):
    pos = np.arange(n_tokens)
    row = (pos // GRID_W).astype(np.float32)
    col = (pos % GRID_W).astype(np.float32)
    n_freq = HEAD_DIM // 4
    inv = (np.float32(ROPE_BASE) ** (-np.arange(n_freq, dtype=np.float32) / np.float32(n_freq)))
    ang_r = (row[:, None] * inv).astype(np.float64)
    ang_c = (col[:, None] * inv).astype(np.float64)
    cos = np.concatenate([np.cos(ang_r), np.cos(ang_r), np.cos(ang_c), np.cos(ang_c)], axis=1)
    sin = np.concatenate([-np.sin(ang_r), np.sin(ang_r), -np.sin(ang_c), np.sin(ang_c)], axis=1)
    return jnp.asarray(cos, F32), jnp.asarray(sin, F32)


def _swap_halves(x):
    n = x.shape[-1]
    quarter = HEAD_DIM // 4
    lane = lax.broadcasted_iota(jnp.int32, x.shape, 1)
    up = pltpu.roll(x, n - quarter, axis=1)
    down = pltpu.roll(x, quarter, axis=1)
    return jnp.where((lane & quarter) == 0, up, down)


def _qkv_kernel(q_ref, k_ref, v_ref, cos_ref, sin_ref, qo_ref, ko_ref, vo_ref, *, rope):
    q = q_ref[...]
    k = k_ref[...]
    if rope:
        cos = cos_ref[...]
        sin = sin_ref[...]
        q = q * jnp.tile(cos, (1, ATT_HEADS)) + _swap_halves(q) * jnp.tile(sin, (1, ATT_HEADS))
        k = k * jnp.tile(cos, (1, ATT_KV_HEADS)) + _swap_halves(k) * jnp.tile(sin, (1, ATT_KV_HEADS))
    qo_ref[...] = (q * (HEAD_DIM ** -0.5)).astype(BF16)
    ko_ref[...] = k.astype(BF16)
    vo_ref[...] = v_ref[...].astype(BF16)


def _qkv_prepare(proj, rope):
    B, L, _ = proj.shape
    ts = _tile(L, 512)
    cos, sin = _rope_tables(L)
    kvb = ATT_WIDTH // KV_WIDTH
    tab = pl.BlockSpec((ts, HEAD_DIM), lambda b, i: (i, 0))
    return pl.pallas_call(
        functools.partial(_qkv_kernel, rope=rope),
        out_shape=(jax.ShapeDtypeStruct((B, L, ATT_WIDTH), BF16),
                   jax.ShapeDtypeStruct((B, L, KV_WIDTH), BF16),
                   jax.ShapeDtypeStruct((B, L, KV_WIDTH), BF16)),
        grid=(B, L // ts),
        in_specs=[pl.BlockSpec((None, ts, ATT_WIDTH), lambda b, i: (b, i, 0)),
                  pl.BlockSpec((None, ts, KV_WIDTH), lambda b, i: (b, i, kvb)),
                  pl.BlockSpec((None, ts, KV_WIDTH), lambda b, i: (b, i, kvb + 1)),
                  tab, tab],
        out_specs=(pl.BlockSpec((None, ts, ATT_WIDTH), lambda b, i: (b, i, 0)),
                   pl.BlockSpec((None, ts, KV_WIDTH), lambda b, i: (b, i, 0)),
                   pl.BlockSpec((None, ts, KV_WIDTH), lambda b, i: (b, i, 0))),
        compiler_params=_cp(("parallel", "parallel")),
        name="qkv_prepare",
    )(proj, proj, proj, cos, sin)


def _nt_dot(a, b):
    return lax.dot_general(a, b, (((1,), (1,)), ((), ())), preferred_element_type=F32)


def _attn_kernel(*refs, local, seq_len):
    if local:
        (q_ref, kp_ref, kc_ref, kn_ref, vp_ref, vc_ref, vn_ref,
         kx_ref, vx_ref, sink_ref, g_ref, o_ref, acc_ref) = refs
    else:
        q_ref, kx_ref, vx_ref, sink_ref, g_ref, o_ref, acc_ref = refs
    T = q_ref.shape[0]
    G = ATT_GROUP
    n = pl.program_id(1)
    if local:
        qi = lax.broadcasted_iota(jnp.int32, (G * T, 3 * T), 0) % T
        ki = lax.broadcasted_iota(jnp.int32, (G * T, 3 * T), 1)
        kpos = n * T + ki - T
        mask = (jnp.abs(ki - T - qi) <= WINDOW) & (kpos >= 0) & (kpos < seq_len)
    for h in range(ATT_KV_HEADS):
        cols = slice(h * HEAD_DIM, (h + 1) * HEAD_DIM)
        qs = jnp.concatenate(
            [q_ref[:, (h * G + g) * HEAD_DIM:(h * G + g + 1) * HEAD_DIM] for g in range(G)], axis=0)
        sink = jnp.concatenate(
            [jnp.broadcast_to(sink_ref[:, (h * G + g) * HEAD_DIM:(h * G + g) * HEAD_DIM + 1], (T, 1))
             for g in range(G)], axis=0)
        s_ctx = _nt_dot(qs, kx_ref[:, cols])
        m = jnp.maximum(jnp.max(s_ctx, axis=-1, keepdims=True), sink)
        if local:
            k = jnp.concatenate([kp_ref[:, cols], kc_ref[:, cols], kn_ref[:, cols]], axis=0)
            s_loc = jnp.where(mask, _nt_dot(qs, k), NEG)
            m = jnp.maximum(m, jnp.max(s_loc, axis=-1, keepdims=True))
        p_ctx = jnp.exp(s_ctx - m)
        denom = jnp.sum(p_ctx, axis=-1, keepdims=True) + jnp.exp(sink - m)
        o = _dot(p_ctx.astype(BF16), vx_ref[:, cols])
        if local:
            p_loc = jnp.exp(s_loc - m)
            denom = denom + jnp.sum(p_loc, axis=-1, keepdims=True)
            v = jnp.concatenate([vp_ref[:, cols], vc_ref[:, cols], vn_ref[:, cols]], axis=0)
            o = o + _dot(p_loc.astype(BF16), v)
        o = o * (1.0 / denom)
        for g in range(G):
            acc_ref[:, (h * G + g) * HEAD_DIM:(h * G + g + 1) * HEAD_DIM] = o[g * T:(g + 1) * T]
    y = acc_ref[...]
    y = y * lax.rsqrt(jnp.mean(y * y, axis=-1, keepdims=True) + EPS) * g_ref[...]
    o_ref[...] = y.astype(o_ref.dtype)


def _attention(q, k, v, k_ctx, v_ctx, sink, g_attn):
    B, L, _ = q.shape
    C = k_ctx.shape[1]
    local = k is not None
    T = BLOCK
    nb = L // T
    sink_row = jnp.repeat(sink.astype(F32), HEAD_DIM).reshape(1, ATT_WIDTH)
    qspec = pl.BlockSpec((None, T, ATT_WIDTH), lambda b, i: (b, i, 0))
    in_specs = [qspec]
    args = [q]
    if local:
        prev = pl.BlockSpec((None, T, KV_WIDTH), lambda b, i: (b, jnp.maximum(i - 1, 0), 0))
        cur = pl.BlockSpec((None, T, KV_WIDTH), lambda b, i: (b, i, 0))
        nxt = pl.BlockSpec((None, T, KV_WIDTH), lambda b, i: (b, jnp.minimum(i + 1, nb - 1), 0))
        in_specs += [prev, cur, nxt, prev, cur, nxt]
        args += [k, k, k, v, v, v]
    cspec = pl.BlockSpec((None, C, KV_WIDTH), lambda b, i: (b, 0, 0))
    row = pl.BlockSpec((1, ATT_WIDTH), lambda b, i: (0, 0))
    in_specs += [cspec, cspec, row, row]
    args += [k_ctx, v_ctx, sink_row, g_attn.reshape(1, ATT_WIDTH)]
    return pl.pallas_call(
        functools.partial(_attn_kernel, local=local, seq_len=L),
        out_shape=jax.ShapeDtypeStruct((B, L, ATT_WIDTH), BF16),
        grid=(B, nb),
        in_specs=in_specs,
        out_specs=qspec,
        scratch_shapes=[pltpu.VMEM((T, ATT_WIDTH), F32)],
        compiler_params=_cp(("parallel", "arbitrary")),
        name="attention_local" if local else "attention_context",
    )(*args)


def _halo_specs(ts, L, width, col_block):
    nb8 = L // HALO
    per = ts // HALO
    prev = pl.BlockSpec((None, HALO, width),
                        lambda b, i: (b, jnp.maximum(i * per - 1, 0), col_block))
    cur = pl.BlockSpec((None, ts, width), lambda b, i: (b, i, col_block))
    nxt = pl.BlockSpec((None, HALO, width),
                       lambda b, i: (b, jnp.minimum((i + 1) * per, nb8 - 1), col_block))
    return [prev, cur, nxt]


def _with_halo(prev_ref, cur_ref, next_ref):
    i = pl.program_id(1)
    last = pl.num_programs(1) - 1
    prev = jnp.where(i > 0, prev_ref[...], 0.0)
    nxt = jnp.where(i < last, next_ref[...], 0.0)
    return jnp.concatenate([prev, cur_ref[...], nxt], axis=0)


def _pool_kernel(prev_ref, cur_ref, next_ref, w_ref, scale_ref, g_ref, o_ref, *, seq_len):
    ts = cur_ref.shape[0]
    ext = _with_halo(prev_ref, cur_ref, next_ref)
    pos = pl.program_id(1) * ts + lax.broadcasted_iota(jnp.int32, (ts, 1), 0)
    outs = []
    for gidx, win in enumerate(POOL_WINDOWS):
        cols = slice(gidx * POOL_GROUP, (gidx + 1) * POOL_GROUP)
        run = ext[:, cols]
        step = 1
        while step < win:
            run = run[:run.shape[0] - step] + run[step:]
            step *= 2
        lo = HALO - win // 2
        total = run[lo:lo + ts]
        cnt = (jnp.minimum(pos + (win - win // 2), seq_len) - jnp.maximum(pos - win // 2, 0))
        mean = total * (1.0 / cnt.astype(F32))
        outs.append(_dot3(mean - cur_ref[:, cols], w_ref[gidx]))
    y = jnp.concatenate(outs, axis=-1) * scale_ref[...]
    y = y * lax.rsqrt(jnp.mean(y * y, axis=-1, keepdims=True) + EPS) * g_ref[...]
    o_ref[...] = y.astype(o_ref.dtype)


def _pool_mixer(proj, pool_w, pool_scale, g_pool):
    B, L, _ = proj.shape
    ts = _tile(L, 512)
    col_block = (ATT_WIDTH + 2 * KV_WIDTH) // POOL_WIDTH
    assert col_block * POOL_WIDTH == ATT_WIDTH + 2 * KV_WIDTH
    row = pl.BlockSpec((1, POOL_WIDTH), lambda b, i: (0, 0))
    return pl.pallas_call(
        functools.partial(_pool_kernel, seq_len=L),
        out_shape=jax.ShapeDtypeStruct((B, L, POOL_WIDTH), BF16),
        grid=(B, L // ts),
        in_specs=_halo_specs(ts, L, POOL_WIDTH, col_block) + [
            pl.BlockSpec(pool_w.shape, lambda b, i: (0, 0, 0)), row, row],
        out_specs=pl.BlockSpec((None, ts, POOL_WIDTH), lambda b, i: (b, i, 0)),
        compiler_params=_cp(("parallel", "arbitrary")),
        name="pool_mixer",
    )(proj, proj, proj, pool_w, pool_scale.reshape(1, POOL_WIDTH), g_pool.reshape(1, POOL_WIDTH))


def _short_conv_kernel(*refs):
    halo_refs = refs[:9]
    w_ref, b_ref = refs[9:11]
    out_refs = refs[11:]
    ts = out_refs[0].shape[0]
    for part in range(HY_ORDER + 1):
        ext = _with_halo(*halo_refs[3 * part:3 * part + 3])
        cols = slice(part * HY_WIDTH, (part + 1) * HY_WIDTH)
        acc = b_ref[:, cols] + ext[HALO - 1:HALO - 1 + ts] * w_ref[0:1, cols]
        acc = acc + ext[HALO:HALO + ts] * w_ref[1:2, cols]
        acc = acc + ext[HALO + 1:HALO + 1 + ts] * w_ref[2:3, cols]
        out_refs[part][...] = acc


def _short_conv(proj, conv_w, conv_b):
    B, L, _ = proj.shape
    ts = _tile(L, 512)
    first = (ATT_WIDTH + 2 * KV_WIDTH + POOL_WIDTH) // HY_WIDTH
    assert first * HY_WIDTH == ATT_WIDTH + 2 * KV_WIDTH + POOL_WIDTH
    specs = []
    for part in range(HY_ORDER + 1):
        specs += _halo_specs(ts, L, HY_WIDTH, first + part)
    n_col = (HY_ORDER + 1) * HY_WIDTH
    out = pl.BlockSpec((None, ts, HY_WIDTH), lambda b, i: (b, i, 0))
    return pl.pallas_call(
        _short_conv_kernel,
        out_shape=tuple(jax.ShapeDtypeStruct((B, L, HY_WIDTH), F32) for _ in range(HY_ORDER + 1)),
        grid=(B, L // ts),
        in_specs=specs + [pl.BlockSpec((HY_SHORT, n_col), lambda b, i: (0, 0)),
                          pl.BlockSpec((1, n_col), lambda b, i: (0, 0))],
        out_specs=(out,) * (HY_ORDER + 1),
        compiler_params=_cp(("parallel", "arbitrary")),
        name="hyena_short_conv",
    )(*([proj] * 9), conv_w, conv_b.reshape(1, n_col))


def _filter_tables(L):
    m = np.arange(L, dtype=np.float32)
    t = (m / np.float32(max(L - 1, 1))).astype(np.float32)
    w = (np.float32(2.0 * math.pi) * m / np.float32(L)).astype(np.float32)
    f = np.linspace(1e-4, HY_BANDS - 1, HY_BANDS, dtype=np.float32)
    ang = (w[:, None] * f).astype(np.float64)
    z = np.concatenate([t[:, None].astype(np.float64), np.cos(ang), -np.sin(ang)], axis=-1)
    z = np.pad(z, ((0, 0), (0, LANES - HY_EMB)))
    max_decay = math.log(HY_TARGET) / HY_SHORT_DECAY_PCT
    min_decay = math.log(HY_TARGET) / HY_LONG_DECAY_PCT
    deltas = np.linspace(min_decay, max_decay, HY_WIDTH, dtype=np.float32)
    decay = np.exp(-t[:, None].astype(np.float64) * np.abs(deltas)[None].astype(np.float64))
    return jnp.asarray(z, F32), jnp.asarray(decay, F32)


def _filter_kernel(z_ref, decay_ref, w1_ref, b1_ref, w2_ref, b2_ref, w3_ref, b3_ref, fr_ref, o_ref):
    tl = z_ref.shape[0]
    fr = fr_ref[...]
    a = jnp.sin(fr * (_dot3(z_ref[...], w1_ref[...]) + b1_ref[...]))
    a = jnp.sin(fr * (_dot3(a, w2_ref[...]) + b2_ref[...]))
    h = _dot3(a, w3_ref[...]) + b3_ref[...]
    h = h * jnp.tile(decay_ref[...], (1, 2 * HY_ORDER))
    row = pl.program_id(0) * tl + lax.broadcasted_iota(jnp.int32, h.shape, 0)
    col = lax.broadcasted_iota(jnp.int32, h.shape, 1)
    backward = (col // HY_WIDTH) % 2 == 1
    o_ref[...] = jnp.where(backward & (row == 0), 0.0, h)


def _hyena_filter_taps(L, w1, b1, w2, b2, w3, b3, freq):
    z, decay = _filter_tables(L)
    hid = w1.shape[1]
    n_out = w3.shape[1]
    tl = _tile(L, 512)
    w1p = jnp.pad(w1, ((0, LANES - HY_EMB), (0, 0)))
    full = lambda a: pl.BlockSpec(a.shape, lambda i: (0,) * a.ndim)
    args = [w1p, b1.reshape(1, hid), w2, b2.reshape(1, hid), w3, b3.reshape(1, n_out),
            freq.reshape(1, hid)]
    return pl.pallas_call(
        _filter_kernel,
        out_shape=jax.ShapeDtypeStruct((L, n_out), F32),
        grid=(L // tl,),
        in_specs=[pl.BlockSpec((tl, LANES), lambda i: (i, 0)),
                  pl.BlockSpec((tl, HY_WIDTH), lambda i: (i, 0))] + [full(a) for a in args],
        out_specs=pl.BlockSpec((tl, n_out), lambda i: (i, 0)),
        compiler_params=_cp(("arbitrary",)),
        name="hyena_filter_taps",
    )(z, decay, *args)


def _fft_matrices(n1, a_in):
    n = n1 * LANES
    k1 = np.arange(n1)
    a = np.arange(a_in)
    th1 = 2.0 * np.pi * np.outer(k1, a) / n1
    c1, s1 = np.cos(th1), np.sin(th1)
    m1_complex = np.block([[c1, s1], [-s1, c1]])
    m1_real = np.concatenate([c1, -s1], axis=0)
    m3 = np.block([[c1.T, -s1.T], [s1.T, c1.T]])
    b = np.arange(LANES)
    k2 = np.arange(LANES)
    k = k1[:, None, None] + n1 * k2[None, :, None]
    th2 = 2.0 * np.pi * (k * b[None, None, :] % n) / n
    c2, s2 = np.cos(th2), np.sin(th2)
    fwd = np.concatenate([np.concatenate([c2, s2], axis=2),
                          np.concatenate([-s2, c2], axis=2)], axis=1)
    c2t, s2t = np.swapaxes(c2, 1, 2), np.swapaxes(s2, 1, 2)
    inv = np.concatenate([np.concatenate([c2t, -s2t], axis=2),
                          np.concatenate([s2t, c2t], axis=2)], axis=1)
    as_bf16 = lambda m: jnp.asarray(m, F32).astype(BF16)
    return as_bf16(m1_complex), as_bf16(m1_real), as_bf16(m3), as_bf16(fwd), as_bf16(inv)


def _fft_stage1_kernel(*refs, complex_in):
    x_ref, m_ref, o_ref = refs
    if complex_in:
        x = jnp.concatenate([x_ref[0].astype(BF16), x_ref[1].astype(BF16)], axis=0)
    else:
        x = x_ref[...].astype(BF16)
    y = _dot(m_ref[...], x)
    n1 = o_ref.shape[1]
    o_ref[0] = y[:n1]
    o_ref[1] = y[n1:]


def _fft_stage1(x, m1, n1, complex_in):
    P, A, Wl = x.shape[0], x.shape[-2], x.shape[-1]
    tl = _tile(Wl, 4096, LANES)
    if complex_in:
        xspec = pl.BlockSpec((None, 2, A, tl), lambda p, j: (p, 0, 0, j))
    else:
        xspec = pl.BlockSpec((None, A, tl), lambda p, j: (p, 0, j))
    return pl.pallas_call(
        functools.partial(_fft_stage1_kernel, complex_in=complex_in),
        out_shape=jax.ShapeDtypeStruct((P, 2, n1, Wl), F32),
        grid=(P, Wl // tl),
        in_specs=[xspec, pl.BlockSpec(m1.shape, lambda p, j: (0, 0))],
        out_specs=pl.BlockSpec((None, 2, n1, tl), lambda p, j: (p, 0, 0, j)),
        compiler_params=_cp(("parallel", "arbitrary")),
        name="fft_stage1",
    )(x, m1)


def _filter_spectrum_kernel(y_ref, g_ref, o0_ref, o1_ref, *, scale):
    kt = y_ref.shape[1]
    for t in range(kt):
        y = jnp.concatenate([y_ref[0, t], y_ref[1, t]], axis=0).astype(BF16)
        f = _dot(g_ref[t], y)
        fr, fi = f[:LANES], f[LANES:]
        for o, o_ref in enumerate((o0_ref, o1_ref)):
            fw = slice((2 * o) * HY_WIDTH, (2 * o + 1) * HY_WIDTH)
            bw = slice((2 * o + 1) * HY_WIDTH, (2 * o + 2) * HY_WIDTH)
            o_ref[0, t] = (fr[:, fw] + fr[:, bw]) * scale
            o_ref[1, t] = (fi[:, fw] - fi[:, bw]) * scale


def _filter_spectrum(taps, mats, n1):
    L, n_col = taps.shape
    a_in = L // LANES
    _, m1_real, _, g_fwd, _ = mats
    y = _fft_stage1(taps.reshape(1, a_in, LANES * n_col), m1_real, n1, complex_in=False)
    y = y.reshape(2, n1, LANES, n_col)
    kt = _tile(n1, 4, 1)
    out = jax.ShapeDtypeStruct((2, n1, LANES, HY_WIDTH), F32)
    ospec = pl.BlockSpec((2, kt, LANES, HY_WIDTH), lambda i: (0, i, 0, 0))
    return pl.pallas_call(
        functools.partial(_filter_spectrum_kernel, scale=1.0 / (n1 * LANES)),
        out_shape=(out, out),
        grid=(n1 // kt,),
        in_specs=[pl.BlockSpec((2, kt, LANES, n_col), lambda i: (0, i, 0, 0)),
                  pl.BlockSpec((kt, 2 * LANES, 2 * LANES), lambda i: (i, 0, 0))],
        out_specs=(ospec, ospec),
        compiler_params=_cp(("arbitrary",)),
        name="hyena_filter_spectrum",
    )(y, g_fwd)


def _fft_mid_kernel(y_ref, kf_ref, g_ref, gi_ref, o_ref):
    kt = y_ref.shape[1]
    for t in range(kt):
        y = jnp.concatenate([y_ref[0, t], y_ref[1, t]], axis=0).astype(BF16)
        f = _dot(g_ref[t], y)
        fr, fi = f[:LANES], f[LANES:]
        kr, ki = kf_ref[0, t], kf_ref[1, t]
        p = jnp.concatenate([fr * kr - fi * ki, fr * ki + fi * kr], axis=0).astype(BF16)
        u = _dot(gi_ref[t], p)
        o_ref[0, t] = u[:LANES]
        o_ref[1, t] = u[LANES:]


def _fft_mid(y, kf, g_fwd, g_inv):
    P, _, n1, _, W = y.shape
    kt = _tile(n1, 8, 1)
    yspec = pl.BlockSpec((None, 2, kt, LANES, W), lambda i, p: (p, 0, i, 0, 0))
    gspec = pl.BlockSpec((kt, 2 * LANES, 2 * LANES), lambda i, p: (i, 0, 0))
    return pl.pallas_call(
        _fft_mid_kernel,
        out_shape=jax.ShapeDtypeStruct(y.shape, F32),
        grid=(n1 // kt, P),
        in_specs=[yspec, pl.BlockSpec((2, kt, LANES, W), lambda i, p: (0, i, 0, 0)), gspec, gspec],
        out_specs=yspec,
        compiler_params=_cp(("arbitrary", "arbitrary")),
        name="fft_mid",
    )(y, kf, g_fwd, g_inv)


def _fft_stage3_kernel(u_ref, m_ref, z_ref, x_ref, bias_ref, o_ref):
    u = jnp.concatenate([u_ref[0], u_ref[1]], axis=0).astype(BF16)
    conv = _dot(m_ref[...], u)
    a_out = z_ref.shape[1]
    bias = bias_ref[...]
    o_ref[0] = x_ref[0] * (conv[:a_out] + z_ref[0] * bias)
    o_ref[1] = x_ref[1] * (conv[a_out:] + z_ref[1] * bias)


def _fft_stage3(u, m3, z, gate, bias_row):
    P, _, n1, Wl = u.shape
    A = z.shape[2]
    tl = _tile(Wl, 4096, LANES)
    pair = pl.BlockSpec((None, 2, A, tl), lambda p, j: (p, 0, 0, j))
    return pl.pallas_call(
        _fft_stage3_kernel,
        out_shape=jax.ShapeDtypeStruct(z.shape, F32),
        grid=(P, Wl // tl),
        in_specs=[pl.BlockSpec((None, 2, n1, tl), lambda p, j: (p, 0, 0, j)),
                  pl.BlockSpec(m3.shape, lambda p, j: (0, 0)),
                  pair, pair,
                  pl.BlockSpec((1, tl), lambda p, j: (0, j))],
        out_specs=pair,
        compiler_params=_cp(("parallel", "arbitrary")),
        name="fft_stage3",
    )(u, m3, z, gate, bias_row)


def _hyena_mixer(v, gates, kfs, d_bias, mats, n1):
    B, L, W = v.shape
    a_in = L // LANES
    pair_shape = (B // 2, 2, a_in, LANES * W)
    m1_complex, _, m3, g_fwd, g_inv = mats
    z = v.reshape(pair_shape)
    for o in range(HY_ORDER):
        y = _fft_stage1(z, m1_complex, n1, complex_in=True)
        u = _fft_mid(y.reshape(B // 2, 2, n1, LANES, W), kfs[o], g_fwd, g_inv)
        bias_row = jnp.tile(d_bias[o].astype(F32), LANES).reshape(1, LANES * W)
        z = _fft_stage3(u.reshape(B // 2, 2, n1, LANES * W), m3, z,
                        gates[o].reshape(pair_shape), bias_row)
    return z.reshape(B, L, W)


def kernel(x, c, ctx, c_ctx, w_mod, b_mod, norm1_g, w_in, attn_sink, pool_w, pool_scale,
           hy_conv_w, hy_conv_b, hy_f_w1, hy_f_b1, hy_f_w2, hy_f_b2, hy_f_w3, hy_f_b3,
           hy_f_freq, hy_bias, g_attn, g_pool, g_hyena, w_out, norm2_g,
           ff_w1, ff_w3, ff_w2, router_w, moe_w1, moe_w3, moe_w2, final_g):
    B, S, D = x.shape
    C = ctx.shape[1]
    depth = w_mod.shape[0]
    assert B % 2 == 0 and S % LANES == 0 and S >= C
    n1 = 2 * S // LANES
    mats = _fft_matrices(n1, S // LANES)

    c_rows = jnp.concatenate([c, c_ctx[None], jnp.zeros((SUBLANES - B - 1, D), F32)], axis=0)
    xc = ctx

    def hyena_branch(l, proj, seq_len):
        taps = _hyena_filter_taps(seq_len, hy_f_w1[l], hy_f_b1[l], hy_f_w2[l], hy_f_b2[l],
                                  hy_f_w3[l], hy_f_b3[l], hy_f_freq[l])
        v, x1, x2 = _short_conv(proj, hy_conv_w[l], hy_conv_b[l])
        if seq_len < S:
            pad = lambda t: jnp.pad(t, ((0, 0), (0, S - seq_len), (0, 0)))
            taps = jnp.pad(taps, ((0, S - seq_len), (0, 0)))
            v, x1, x2 = pad(v), pad(x1), pad(x2)
        kfs = _filter_spectrum(taps, mats, n1)
        z = _hyena_mixer(v, (x1, x2), kfs, hy_bias[l], mats, n1)
        return _rmsnorm(z[:, :seq_len], g_hyena[l], BF16)

    def mix(l, proj, y_att, res, gate, w_out_l, seq_len):
        y_pool = _pool_mixer(proj, pool_w[l], pool_scale[l], g_pool[l])
        y_hy = hyena_branch(l, proj, seq_len)
        return _matmul_gated_residual([y_att, y_pool, y_hy], w_out_l, res, gate)

    for l in range(depth):
        last = l == depth - 1
        mod = _adaln(c_rows, w_mod[l], b_mod[l])
        sh1, sc1, g1, sh2, sc2, g2 = [mod[:B, j * D:(j + 1) * D] for j in range(6)]
        csh1, csc1, cg1, csh2, csc2, cg2 = [mod[B:B + 1, j * D:(j + 1) * D] for j in range(6)]
        w_in_l = w_in[l].astype(BF16)
        w_out_l = w_out[l].astype(BF16)

        hc = _norm_mod(xc, norm1_g[l], csh1, csc1)
        if last:
            kv_w = w_in_l[:, ATT_WIDTH:ATT_WIDTH + 2 * KV_WIDTH]
            kv_c = _matmul(hc, kv_w, out_dtype=BF16)
            k_c, v_c = kv_c[..., :KV_WIDTH], kv_c[..., KV_WIDTH:]
        else:
            proj_c = _matmul(hc, w_in_l)
            q_c, k_c, v_c = _qkv_prepare(proj_c, rope=False)
            y_att_c = _attention(q_c, None, None, k_c, v_c, attn_sink[l], g_attn[l])
            xc_new = mix(l, proj_c, y_att_c, xc, cg1, w_out_l, C)
            hc2 = _norm_mod(xc_new, norm2_g[l], csh2, csc2)
            i = l // 2
            if l % 2 == 0:
                hid_c = _swiglu_hidden(hc2, ff_w1[i].astype(BF16), ff_w3[i].astype(BF16))
                xc_new = _matmul_gated_residual([hid_c], ff_w2[i].astype(BF16), xc_new, cg2)
            else:
                raise NotImplementedError("context tokens through an expert layer")

        h = _norm_mod(x, norm1_g[l], sh1, sc1)
        proj = _matmul(h, w_in_l)
        q, k, v = _qkv_prepare(proj, rope=True)
        y_att = _attention(q, k, v, k_c, v_c, attn_sink[l], g_attn[l])
        x = mix(l, proj, y_att, x, g1, w_out_l, S)

        i = l // 2
        if l % 2 == 0:
            h2 = _norm_mod(x, norm2_g[l], sh2, sc2)
            hid = _swiglu_hidden(h2, ff_w1[i].astype(BF16), ff_w3[i].astype(BF16))
            x = _matmul_gated_residual([hid], ff_w2[i].astype(BF16), x, g2)
            if last:
                x = _rmsnorm(x, final_g, F32)
        else:
            h2, gi, gw = _norm_mod_router(x, norm2_g[l], sh2, sc2, router_w[i])
            assert last
            x = _moe_layer(x, h2, gi, gw, g2, final_g, moe_w1[i].astype(BF16),
                           moe_w3[i].astype(BF16), moe_w2[i].astype(BF16))
        if not last:
            xc = xc_new
    return x
```

```python
import functools
import math

import numpy as np
import jax
import jax.numpy as jnp
from jax import lax
from jax.experimental import pallas as pl
from jax.experimental.pallas import tpu as pltpu

F32 = jnp.float32
BF16 = jnp.bfloat16

EPS = 1e-6
NEG = -1e30
GRID_W = 64
ATT_HEADS = 8
ATT_KV_HEADS = 2
ATT_GROUP = ATT_HEADS // ATT_KV_HEADS
HEAD_DIM = 128
ATT_WIDTH = ATT_HEADS * HEAD_DIM
KV_WIDTH = ATT_KV_HEADS * HEAD_DIM
WINDOW = 128
BLOCK = 128
ROPE_BASE = 10000.0
POOL_WINDOWS = (2, 4, 8, 16)
POOL_GROUP = 128
POOL_WIDTH = POOL_GROUP * len(POOL_WINDOWS)
HY_WIDTH = 512
HY_ORDER = 2
HY_SHORT = 3
HY_EMB = 33
HY_BANDS = (HY_EMB - 1) // 2
HY_SHORT_DECAY_PCT = 0.3
HY_LONG_DECAY_PCT = 1.5
HY_TARGET = 1e-2
N_EXPERTS = 8
TOP_K = 2

LANES = 128
SUBLANES = 8
HALO = SUBLANES
FFT_MIN_ROWS = 2048
VMEM_LIMIT = 56 * 1024 * 1024


def _cp(sem):
    return pltpu.CompilerParams(dimension_semantics=sem, vmem_limit_bytes=VMEM_LIMIT)


def _tile(n, pref, mult=SUBLANES):
    if n <= pref:
        return n
    t = (pref // mult) * mult
    while t >= mult:
        if n % t == 0:
            return t
        t -= mult
    return n


def _split_bf16(x):
    hi = x.astype(BF16)
    lo = (x - hi.astype(F32)).astype(BF16)
    return hi, lo


def _dot(a, b):
    return jnp.dot(a, b, preferred_element_type=F32)


def _dot3(a, b):
    ah, al = _split_bf16(a)
    bh, bl = _split_bf16(b)
    return _dot(ah, bh) + (_dot(ah, bl) + _dot(al, bh))


def _adaln_kernel(c_ref, w_ref, b_ref, o_ref):
    c = c_ref[...]
    a = c * (1.0 / (1.0 + jnp.exp(-c)))
    o_ref[...] = _dot3(a, w_ref[...]) + b_ref[...]


def _adaln(c_rows, w, b):
    R, D = c_rows.shape
    N = w.shape[1]
    tn = _tile(N, 1536, LANES)
    return pl.pallas_call(
        _adaln_kernel,
        out_shape=jax.ShapeDtypeStruct((R, N), F32),
        grid=(N // tn,),
        in_specs=[pl.BlockSpec((R, D), lambda j: (0, 0)),
                  pl.BlockSpec((D, tn), lambda j: (0, j)),
                  pl.BlockSpec((1, tn), lambda j: (0, j))],
        out_specs=pl.BlockSpec((R, tn), lambda j: (0, j)),
        compiler_params=_cp(("arbitrary",)),
        name="adaln",
    )(c_rows, w, b.reshape(1, N))


def _norm_mod_kernel(x_ref, g_ref, sh_ref, sc_ref, o_ref):
    x = x_ref[...]
    y = x * lax.rsqrt(jnp.mean(x * x, axis=-1, keepdims=True) + EPS) * g_ref[...]
    o_ref[...] = (y * (1.0 + sc_ref[...]) + sh_ref[...]).astype(o_ref.dtype)


def _bcast_map(arr):
    if arr.shape[0] == 1:
        return lambda b, i: (0, 0, 0)
    return lambda b, i: (b, 0, 0)


def _norm_mod(x, g, shift, scale):
    B, L, D = x.shape
    ts = _tile(L, 512)
    shift = shift[:, None, :]
    scale = scale[:, None, :]
    return pl.pallas_call(
        _norm_mod_kernel,
        out_shape=jax.ShapeDtypeStruct((B, L, D), BF16),
        grid=(B, L // ts),
        in_specs=[pl.BlockSpec((None, ts, D), lambda b, i: (b, i, 0)),
                  pl.BlockSpec((1, D), lambda b, i: (0, 0)),
                  pl.BlockSpec((None, 1, D), _bcast_map(shift)),
                  pl.BlockSpec((None, 1, D), _bcast_map(scale))],
        out_specs=pl.BlockSpec((None, ts, D), lambda b, i: (b, i, 0)),
        compiler_params=_cp(("parallel", "parallel")),
        name="norm_mod",
    )(x, g.reshape(1, D), shift, scale)


def _norm_mod_router_kernel(x_ref, g_ref, sh_ref, sc_ref, rw_ref, o_ref, gi_ref, gw_ref):
    x = x_ref[...]
    y = x * lax.rsqrt(jnp.mean(x * x, axis=-1, keepdims=True) + EPS) * g_ref[...]
    h = y * (1.0 + sc_ref[...]) + sh_ref[...]
    o_ref[...] = h.astype(o_ref.dtype)
    logits = _dot3(h, rw_ref[...])
    lane = lax.broadcasted_iota(jnp.int32, logits.shape, 1)
    logits = jnp.where(lane < N_EXPERTS, logits, NEG)
    m1 = jnp.max(logits, axis=-1, keepdims=True)
    i1 = jnp.min(jnp.where(logits == m1, lane, LANES), axis=-1, keepdims=True)
    rest = jnp.where(lane == i1, NEG, logits)
    m2 = jnp.max(rest, axis=-1, keepdims=True)
    i2 = jnp.min(jnp.where(rest == m2, lane, LANES), axis=-1, keepdims=True)
    e2 = jnp.exp(m2 - m1)
    w1 = 1.0 / (1.0 + e2)
    w2 = e2 * w1
    gi_ref[...] = jnp.where(lane == 0, i1, jnp.where(lane == 1, i2, 0))
    gw_ref[...] = jnp.where(lane == 0, w1, jnp.where(lane == 1, w2, 0.0))


def _norm_mod_router(x, g, shift, scale, router_w):
    B, L, D = x.shape
    ts = _tile(L, 512)
    shift = shift[:, None, :]
    scale = scale[:, None, :]
    rw = jnp.pad(router_w, ((0, 0), (0, LANES - router_w.shape[1])))
    row = pl.BlockSpec((None, ts, D), lambda b, i: (b, i, 0))
    small = pl.BlockSpec((None, ts, LANES), lambda b, i: (b, i, 0))
    return pl.pallas_call(
        _norm_mod_router_kernel,
        out_shape=(jax.ShapeDtypeStruct((B, L, D), BF16),
                   jax.ShapeDtypeStruct((B, L, LANES), jnp.int32),
                   jax.ShapeDtypeStruct((B, L, LANES), F32)),
        grid=(B, L // ts),
        in_specs=[row,
                  pl.BlockSpec((1, D), lambda b, i: (0, 0)),
                  pl.BlockSpec((None, 1, D), _bcast_map(shift)),
                  pl.BlockSpec((None, 1, D), _bcast_map(scale)),
                  pl.BlockSpec((D, LANES), lambda b, i: (0, 0))],
        out_specs=(row, small, small),
        compiler_params=_cp(("parallel", "parallel")),
        name="norm_mod_router",
    )(x, g.reshape(1, D), shift, scale, rw)


def _rmsnorm_kernel(x_ref, g_ref, o_ref):
    x = x_ref[...]
    y = x * lax.rsqrt(jnp.mean(x * x, axis=-1, keepdims=True) + EPS) * g_ref[...]
    o_ref[...] = y.astype(o_ref.dtype)


def _rmsnorm(x, g, out_dtype):
    B, L, D = x.shape
    ts = _tile(L, 512)
    return pl.pallas_call(
        _rmsnorm_kernel,
        out_shape=jax.ShapeDtypeStruct((B, L, D), out_dtype),
        grid=(B, L // ts),
        in_specs=[pl.BlockSpec((None, ts, D), lambda b, i: (b, i, 0)),
                  pl.BlockSpec((1, D), lambda b, i: (0, 0))],
        out_specs=pl.BlockSpec((None, ts, D), lambda b, i: (b, i, 0)),
        compiler_params=_cp(("parallel", "parallel")),
        name="rmsnorm",
    )(x, g.reshape(1, D))


def _mm_kernel(a_ref, w_ref, o_ref):
    o_ref[...] = _dot(a_ref[...], w_ref[...]).astype(o_ref.dtype)


def _matmul(a, w, out_dtype=F32, tm=1024, tn=512):
    B, L, K = a.shape
    N = w.shape[1]
    tm = _tile(L, tm)
    tn = _tile(N, tn, LANES)
    return pl.pallas_call(
        _mm_kernel,
        out_shape=jax.ShapeDtypeStruct((B, L, N), out_dtype),
        grid=(B, L // tm, N // tn),
        in_specs=[pl.BlockSpec((None, tm, K), lambda b, i, j: (b, i, 0)),
                  pl.BlockSpec((K, tn), lambda b, i, j: (0, j))],
        out_specs=pl.BlockSpec((None, tm, tn), lambda b, i, j: (b, i, j)),
        compiler_params=_cp(("parallel", "parallel", "arbitrary")),
        name="matmul",
    )(a, w)


def _mm_res_kernel(*refs, n_a):
    a_refs = refs[:n_a]
    w_refs = refs[n_a:2 * n_a]
    res_ref, gate_ref, o_ref = refs[2 * n_a:]
    acc = _dot(a_refs[0][...], w_refs[0][...])
    for a_ref, w_ref in zip(a_refs[1:], w_refs[1:]):
        acc = acc + _dot(a_ref[...], w_ref[...])
    o_ref[...] = res_ref[...] + gate_ref[...] * acc


def _matmul_gated_residual(a_list, w, res, gate, tm=1024, tn=512):
    B, L, N = res.shape
    tm = _tile(L, tm)
    tn = _tile(N, tn, LANES)
    gate = gate[:, None, :]
    widths = [a.shape[-1] for a in a_list]
    unit = math.gcd(*widths) if len(widths) > 1 else widths[0]
    in_specs = [pl.BlockSpec((None, tm, k), lambda b, i, j: (b, i, 0)) for k in widths]
    off = 0
    for k in widths:
        assert off % k == 0 and k % unit == 0
        in_specs.append(pl.BlockSpec((k, tn), functools.partial(
            lambda b, i, j, blk: (blk, j), blk=off // k)))
        off += k
    gmap = (lambda b, i, j: (0, 0, j)) if gate.shape[0] == 1 else (lambda b, i, j: (b, 0, j))
    in_specs += [pl.BlockSpec((None, tm, tn), lambda b, i, j: (b, i, j)),
                 pl.BlockSpec((None, 1, tn), gmap)]
    return pl.pallas_call(
        functools.partial(_mm_res_kernel, n_a=len(a_list)),
        out_shape=jax.ShapeDtypeStruct((B, L, N), F32),
        grid=(B, L // tm, N // tn),
        in_specs=in_specs,
        out_specs=pl.BlockSpec((None, tm, tn), lambda b, i, j: (b, i, j)),
        compiler_params=_cp(("parallel", "parallel", "arbitrary")),
        name="matmul_gated_residual",
    )(*a_list, *([w] * len(a_list)), res, gate)


def _silu(x):
    return x * (1.0 / (1.0 + jnp.exp(-x)))


def _swiglu_kernel(a_ref, w1_ref, w3_ref, o_ref):
    a = a_ref[...]
    o_ref[...] = (_silu(_dot(a, w1_ref[...])) * _dot(a, w3_ref[...])).astype(o_ref.dtype)


def _swiglu_hidden(a, w1, w3, tm=1024, tf=512):
    B, L, D = a.shape
    F = w1.shape[1]
    tm = _tile(L, tm)
    tf = _tile(F, tf, LANES)
    wspec = pl.BlockSpec((D, tf), lambda b, i, j: (0, j))
    return pl.pallas_call(
        _swiglu_kernel,
        out_shape=jax.ShapeDtypeStruct((B, L, F), BF16),
        grid=(B, L // tm, F // tf),
        in_specs=[pl.BlockSpec((None, tm, D), lambda b, i, j: (b, i, 0)), wspec, wspec],
        out_specs=pl.BlockSpec((None, tm, tf), lambda b, i, j: (b, i, j)),
        compiler_params=_cp(("parallel", "parallel", "arbitrary")),
        name="swiglu_hidden",
    )(a, w1, w3)


def _moe_hidden_kernel(te_ref, nt_ref, a_ref, w1_ref, w3_ref, o_ref):
    used = pl.program_id(0) < nt_ref[0]

    @pl.when(used)
    def _():
        a = a_ref[...]
        w1 = w1_ref[...].astype(BF16)
        w3 = w3_ref[...].astype(BF16)
        o_ref[...] = (_silu(_dot(a, w1)) * _dot(a, w3)).astype(o_ref.dtype)

    @pl.when(jnp.logical_not(used))
    def _():
        o_ref[...] = jnp.zeros(o_ref.shape, o_ref.dtype)


def _moe_out_kernel(te_ref, nt_ref, a_ref, w2_ref, o_ref):
    used = pl.program_id(0) < nt_ref[0]

    @pl.when(used)
    def _():
        o_ref[...] = _dot(a_ref[...], w2_ref[...].astype(BF16)).astype(o_ref.dtype)

    @pl.when(jnp.logical_not(used))
    def _():
        o_ref[...] = jnp.zeros(o_ref.shape, o_ref.dtype)


def _moe_experts(a_sorted, tile_expert, n_tiles_used, w1, w3, w2, tm, tf=512, tn=256):
    R, D = a_sorted.shape
    E, _, F = w1.shape
    tf = _tile(F, tf, LANES)
    tn = _tile(D, tn, LANES)
    nt = R // tm
    w13 = pl.BlockSpec((None, D, tf), lambda i, j, te, n: (te[i], 0, j))
    hidden = pl.pallas_call(
        _moe_hidden_kernel,
        out_shape=jax.ShapeDtypeStruct((R, F), BF16),
        grid_spec=pltpu.PrefetchScalarGridSpec(
            num_scalar_prefetch=2,
            grid=(nt, F // tf),
            in_specs=[pl.BlockSpec((tm, D), lambda i, j, te, n: (i, 0)), w13, w13],
            out_specs=pl.BlockSpec((tm, tf), lambda i, j, te, n: (i, j))),
        compiler_params=_cp(("arbitrary", "arbitrary")),
        name="moe_hidden",
    )(tile_expert, n_tiles_used, a_sorted, w1, w3)
    return pl.pallas_call(
        _moe_out_kernel,
        out_shape=jax.ShapeDtypeStruct((R, D), BF16),
        grid_spec=pltpu.PrefetchScalarGridSpec(
            num_scalar_prefetch=2,
            grid=(nt, D // tn),
            in_specs=[pl.BlockSpec((tm, F), lambda i, j, te, n: (i, 0)),
                      pl.BlockSpec((None, F, tn), lambda i, j, te, n: (te[i], 0, j))],
            out_specs=pl.BlockSpec((tm, tn), lambda i, j, te, n: (i, j))),
        compiler_params=_cp(("arbitrary", "arbitrary")),
        name="moe_out",
    )(tile_expert, n_tiles_used, hidden, w2)


def _moe_combine_kernel(x_ref, ya_ref, yb_ref, gw_ref, gate_ref, g_ref, o_ref):
    gw = gw_ref[...]
    moe = gw[:, 0:1] * ya_ref[...].astype(F32) + gw[:, 1:2] * yb_ref[...].astype(F32)
    x = x_ref[...] + gate_ref[...] * moe
    y = x * lax.rsqrt(jnp.mean(x * x, axis=-1, keepdims=True) + EPS) * g_ref[...]
    o_ref[...] = y


def _moe_combine_norm(x, ya, yb, gw, gate, g):
    B, L, D = x.shape
    ts = _tile(L, 512)
    row = pl.BlockSpec((None, ts, D), lambda b, i: (b, i, 0))
    return pl.pallas_call(
        _moe_combine_kernel,
        out_shape=jax.ShapeDtypeStruct((B, L, D), F32),
        grid=(B, L // ts),
        in_specs=[row, row, row,
                  pl.BlockSpec((None, ts, LANES), lambda b, i: (b, i, 0)),
                  pl.BlockSpec((None, 1, D), lambda b, i: (b, 0, 0)),
                  pl.BlockSpec((1, D), lambda b, i: (0, 0))],
        out_specs=row,
        compiler_params=_cp(("parallel", "parallel")),
        name="moe_combine_norm",
    )(x, ya, yb, gw, gate[:, None, :], g.reshape(1, D))


def _moe_layer(x, h, gi, gw, gate, final_g, w1, w3, w2):
    B, L, D = x.shape
    E = w1.shape[0]
    n_tok = B * L
    n_pair = n_tok * TOP_K
    tm = _tile(n_pair, 1024)
    nt = n_pair // tm + E
    e_pair = gi[..., :TOP_K].reshape(n_pair)
    blk = _tile(n_pair, LANES)
    onehot = (e_pair[:, None] == jnp.arange(E, dtype=jnp.int32)[None, :]).astype(F32)
    onehot = onehot.reshape(n_pair // blk, blk, E)
    within = jnp.einsum("ij,bjk->bik", jnp.tril(jnp.ones((blk, blk), F32)), onehot)
    blk_tot = within[:, -1, :]
    blk_off = jnp.cumsum(blk_tot, axis=0) - blk_tot
    rank = jnp.sum((within - 1.0 + blk_off[:, None, :]) * onehot, axis=-1)
    rank = rank.reshape(n_pair).astype(jnp.int32)
    counts = jnp.sum(blk_tot, axis=0).astype(jnp.int32)
    tiles_per = (counts + tm - 1) // tm
    tile_end = jnp.cumsum(tiles_per)
    tile_start = tile_end - tiles_per
    pos = tile_start[e_pair] * tm + rank
    tile_ids = jnp.arange(nt, dtype=jnp.int32)
    tile_expert = jnp.minimum(
        jnp.sum((tile_ids[:, None] >= tile_end[None, :]).astype(jnp.int32), axis=1), E - 1)
    n_used = tile_end[-1:].astype(jnp.int32)
    tok_pair = jnp.arange(n_pair, dtype=jnp.int32) // TOP_K
    src = jnp.zeros((nt * tm,), jnp.int32).at[pos].set(tok_pair, unique_indices=True)
    a_sorted = jnp.take(h.reshape(n_tok, D), src, axis=0)
    y_sorted = _moe_experts(a_sorted, tile_expert.astype(jnp.int32), n_used, w1, w3, w2, tm)
    pos2 = pos.reshape(n_tok, TOP_K)
    ya = jnp.take(y_sorted, pos2[:, 0], axis=0).reshape(B, L, D)
    yb = jnp.take(y_sorted, pos2[:, 1], axis=0).reshape(B, L, D)
    return _moe_combine_norm(x, ya, yb, gw, gate, final_g)


def _rope_tables(n_tokens):
    pos = np.arange(n_tokens)
    row = (pos // GRID_W).astype(np.float32)
    col = (pos % GRID_W).astype(np.float32)
    n_freq = HEAD_DIM // 4
    inv = (np.float32(ROPE_BASE) ** (-np.arange(n_freq, dtype=np.float32) / np.float32(n_freq)))
    ang_r = (row[:, None] * inv).astype(np.float64)
    ang_c = (col[:, None] * inv).astype(np.float64)
    cos = np.concatenate([np.cos(ang_r), np.cos(ang_r), np.cos(ang_c), np.cos(ang_c)], axis=1)
    sin = np.concatenate([-np.sin(ang_r), np.sin(ang_r), -np.sin(ang_c), np.sin(ang_c)], axis=1)
    return jnp.asarray(cos, F32), jnp.asarray(sin, F32)


def _swap_halves(x):
    n = x.shape[-1]
    quarter = HEAD_DIM // 4
    lane = lax.broadcasted_iota(jnp.int32, x.shape, 1)
    up = pltpu.roll(x, n - quarter, axis=1)
    down = pltpu.roll(x, quarter, axis=1)
    return jnp.where((lane & quarter) == 0, up, down)


def _qkv_kernel(q_ref, k_ref, v_ref, cos_ref, sin_ref, qo_ref, ko_ref, vo_ref, *, rope):
    q = q_ref[...]
    k = k_ref[...]
    if rope:
        cos = cos_ref[...]
        sin = sin_ref[...]
        q = q * jnp.tile(cos, (1, ATT_HEADS)) + _swap_halves(q) * jnp.tile(sin, (1, ATT_HEADS))
        k = k * jnp.tile(cos, (1, ATT_KV_HEADS)) + _swap_halves(k) * jnp.tile(sin, (1, ATT_KV_HEADS))
    qo_ref[...] = (q * (HEAD_DIM ** -0.5)).astype(BF16)
    ko_ref[...] = k.astype(BF16)
    vo_ref[...] = v_ref[...].astype(BF16)


def _qkv_prepare(proj, rope):
    B, L, _ = proj.shape
    ts = _tile(L, 512)
    cos, sin = _rope_tables(L)
    kvb = ATT_WIDTH // KV_WIDTH
    tab = pl.BlockSpec((ts, HEAD_DIM), lambda b, i: (i, 0))
    return pl.pallas_call(
        functools.partial(_qkv_kernel, rope=rope),
        out_shape=(jax.ShapeDtypeStruct((B, L, ATT_WIDTH), BF16),
                   jax.ShapeDtypeStruct((B, L, KV_WIDTH), BF16),
                   jax.ShapeDtypeStruct((B, L, KV_WIDTH), BF16)),
        grid=(B, L // ts),
        in_specs=[pl.BlockSpec((None, ts, ATT_WIDTH), lambda b, i: (b, i, 0)),
                  pl.BlockSpec((None, ts, KV_WIDTH), lambda b, i: (b, i, kvb)),
                  pl.BlockSpec((None, ts, KV_WIDTH), lambda b, i: (b, i, kvb + 1)),
                  tab, tab],
        out_specs=(pl.BlockSpec((None, ts, ATT_WIDTH), lambda b, i: (b, i, 0)),
                   pl.BlockSpec((None, ts, KV_WIDTH), lambda b, i: (b, i, 0)),
                   pl.BlockSpec((None, ts, KV_WIDTH), lambda b, i: (b, i, 0))),
        compiler_params=_cp(("parallel", "parallel")),
        name="qkv_prepare",
    )(proj, proj, proj, cos, sin)


def _nt_dot(a, b):
    return lax.dot_general(a, b, (((1,), (1,)), ((), ())), preferred_element_type=F32)


def _attn_bias_tables():
    T, G = BLOCK, ATT_GROUP
    qi = np.arange(G * T)[:, None] % T
    ki = np.arange(3 * T)[None, :]
    band = np.abs(ki - T - qi) <= WINDOW
    after_start = ki >= T
    before_end = ki < 2 * T
    masks = [band & after_start, band, band & before_end, band & after_start & before_end]
    return jnp.asarray(np.stack([np.where(m, 0.0, NEG) for m in masks]), F32)


def _attn_kernel(*refs, local, n_blocks):
    if local:
        (q_ref, kp_ref, kc_ref, kn_ref, vp_ref, vc_ref, vn_ref,
         kx_ref, vx_ref, bias_ref, sink_ref, g_ref, o_ref, acc_ref) = refs
    else:
        q_ref, kx_ref, vx_ref, sink_ref, g_ref, o_ref, acc_ref = refs
    T = BLOCK
    G = ATT_GROUP
    n_sub = q_ref.shape[0] // T
    for h in range(ATT_KV_HEADS):
        cols = slice(h * HEAD_DIM, (h + 1) * HEAD_DIM)
        sink = jnp.concatenate(
            [jnp.broadcast_to(sink_ref[:, (h * G + g) * HEAD_DIM:(h * G + g) * HEAD_DIM + 1], (T, 1))
             for g in range(G)], axis=0)
        kx = kx_ref[:, cols]
        vx = vx_ref[:, cols]
        if local:
            k_band = jnp.concatenate([kp_ref[:, cols], kc_ref[:, cols], kn_ref[:, cols]], axis=0)
            v_band = jnp.concatenate([vp_ref[:, cols], vc_ref[:, cols], vn_ref[:, cols]], axis=0)
        for j in range(n_sub):
            rows = slice(j * T, (j + 1) * T)
            qs = jnp.concatenate(
                [q_ref[rows, (h * G + g) * HEAD_DIM:(h * G + g + 1) * HEAD_DIM] for g in range(G)],
                axis=0)
            s_ctx = _nt_dot(qs, kx)
            m = jnp.maximum(jnp.max(s_ctx, axis=-1, keepdims=True), sink)
            if local:
                blk = pl.program_id(1) * n_sub + j
                is_first = blk == 0
                is_last = blk == n_blocks - 1
                table = jnp.where(is_first, jnp.where(is_last, 3, 0), jnp.where(is_last, 2, 1))
                s_loc = _nt_dot(qs, k_band[j * T:(j + 3) * T]) + bias_ref[table]
                m = jnp.maximum(m, jnp.max(s_loc, axis=-1, keepdims=True))
            p_ctx = jnp.exp(s_ctx - m)
            denom = jnp.sum(p_ctx, axis=-1, keepdims=True) + jnp.exp(sink - m)
            o = _dot(p_ctx.astype(BF16), vx)
            if local:
                p_loc = jnp.exp(s_loc - m)
                denom = denom + jnp.sum(p_loc, axis=-1, keepdims=True)
                o = o + _dot(p_loc.astype(BF16), v_band[j * T:(j + 3) * T])
            o = o * (1.0 / denom)
            for g in range(G):
                acc_ref[rows, (h * G + g) * HEAD_DIM:(h * G + g + 1) * HEAD_DIM] = o[g * T:(g + 1) * T]
    y = acc_ref[...]
    y = y * lax.rsqrt(jnp.mean(y * y, axis=-1, keepdims=True) + EPS) * g_ref[...]
    o_ref[...] = y.astype(o_ref.dtype)


def _attention(q, k, v, k_ctx, v_ctx, sink, g_attn):
    B, L, _ = q.shape
    C = k_ctx.shape[1]
    local = k is not None
    T = BLOCK
    nb = L // T
    n_sub = 2 if nb % 2 == 0 else 1
    sink_row = jnp.repeat(sink.astype(F32), HEAD_DIM).reshape(1, ATT_WIDTH)
    qspec = pl.BlockSpec((None, n_sub * T, ATT_WIDTH), lambda b, i: (b, i, 0))
    in_specs = [qspec]
    args = [q]
    cspec = pl.BlockSpec((None, C, KV_WIDTH), lambda b, i: (b, 0, 0))
    row = pl.BlockSpec((1, ATT_WIDTH), lambda b, i: (0, 0))
    if local:
        prev = pl.BlockSpec((None, T, KV_WIDTH), lambda b, i: (b, jnp.maximum(i * n_sub - 1, 0), 0))
        cur = pl.BlockSpec((None, n_sub * T, KV_WIDTH), lambda b, i: (b, i, 0))
        nxt = pl.BlockSpec((None, T, KV_WIDTH),
                           lambda b, i: (b, jnp.minimum((i + 1) * n_sub, nb - 1), 0))
        bias = _attn_bias_tables()
        in_specs += [prev, cur, nxt, prev, cur, nxt, cspec, cspec,
                     pl.BlockSpec(bias.shape, lambda b, i: (0, 0, 0))]
        args += [k, k, k, v, v, v, k_ctx, v_ctx, bias]
    else:
        in_specs += [cspec, cspec]
        args += [k_ctx, v_ctx]
    in_specs += [row, row]
    args += [sink_row, g_attn.reshape(1, ATT_WIDTH)]
    return pl.pallas_call(
        functools.partial(_attn_kernel, local=local, n_blocks=nb),
        out_shape=jax.ShapeDtypeStruct((B, L, ATT_WIDTH), BF16),
        grid=(B, nb // n_sub),
        in_specs=in_specs,
        out_specs=qspec,
        scratch_shapes=[pltpu.VMEM((n_sub * T, ATT_WIDTH), F32)],
        compiler_params=_cp(("parallel", "arbitrary")),
        name="attention_local" if local else "attention_context",
    )(*args)


def _halo_specs(ts, L, width, col_block):
    nb8 = L // HALO
    per = ts // HALO
    prev = pl.BlockSpec((None, HALO, width),
                        lambda b, i: (b, jnp.maximum(i * per - 1, 0), col_block))
    cur = pl.BlockSpec((None, ts, width), lambda b, i: (b, i, col_block))
    nxt = pl.BlockSpec((None, HALO, width),
                       lambda b, i: (b, jnp.minimum((i + 1) * per, nb8 - 1), col_block))
    return [prev, cur, nxt]


def _with_halo(prev_ref, cur_ref, next_ref):
    i = pl.program_id(1)
    last = pl.num_programs(1) - 1
    prev = jnp.where(i > 0, prev_ref[...], 0.0)
    nxt = jnp.where(i < last, next_ref[...], 0.0)
    return jnp.concatenate([prev, cur_ref[...], nxt], axis=0)


def _pool_kernel(prev_ref, cur_ref, next_ref, w_ref, scale_ref, g_ref, o_ref, *, seq_len):
    ts = cur_ref.shape[0]
    ext = _with_halo(prev_ref, cur_ref, next_ref)
    pos = pl.program_id(1) * ts + lax.broadcasted_iota(jnp.int32, (ts, 1), 0)
    outs = []
    for gidx, win in enumerate(POOL_WINDOWS):
        cols = slice(gidx * POOL_GROUP, (gidx + 1) * POOL_GROUP)
        run = ext[:, cols]
        step = 1
        while step < win:
            run = run[:run.shape[0] - step] + run[step:]
            step *= 2
        lo = HALO - win // 2
        total = run[lo:lo + ts]
        cnt = (jnp.minimum(pos + (win - win // 2), seq_len) - jnp.maximum(pos - win // 2, 0))
        mean = total * (1.0 / cnt.astype(F32))
        outs.append(_dot3(mean - cur_ref[:, cols], w_ref[gidx]))
    y = jnp.concatenate(outs, axis=-1) * scale_ref[...]
    y = y * lax.rsqrt(jnp.mean(y * y, axis=-1, keepdims=True) + EPS) * g_ref[...]
    o_ref[...] = y.astype(o_ref.dtype)


def _pool_mixer(proj, pool_w, pool_scale, g_pool):
    B, L, _ = proj.shape
    ts = _tile(L, 512)
    col_block = (ATT_WIDTH + 2 * KV_WIDTH) // POOL_WIDTH
    assert col_block * POOL_WIDTH == ATT_WIDTH + 2 * KV_WIDTH
    row = pl.BlockSpec((1, POOL_WIDTH), lambda b, i: (0, 0))
    return pl.pallas_call(
        functools.partial(_pool_kernel, seq_len=L),
        out_shape=jax.ShapeDtypeStruct((B, L, POOL_WIDTH), BF16),
        grid=(B, L // ts),
        in_specs=_halo_specs(ts, L, POOL_WIDTH, col_block) + [
            pl.BlockSpec(pool_w.shape, lambda b, i: (0, 0, 0)), row, row],
        out_specs=pl.BlockSpec((None, ts, POOL_WIDTH), lambda b, i: (b, i, 0)),
        compiler_params=_cp(("parallel", "arbitrary")),
        name="pool_mixer",
    )(proj, proj, proj, pool_w, pool_scale.reshape(1, POOL_WIDTH), g_pool.reshape(1, POOL_WIDTH))


def _short_conv_kernel(*refs):
    halo_refs = refs[:9]
    w_ref, b_ref = refs[9:11]
    out_refs = refs[11:]
    ts = out_refs[0].shape[0]
    for part in range(HY_ORDER + 1):
        ext = _with_halo(*halo_refs[3 * part:3 * part + 3])
        cols = slice(part * HY_WIDTH, (part + 1) * HY_WIDTH)
        acc = b_ref[:, cols] + ext[HALO - 1:HALO - 1 + ts] * w_ref[0:1, cols]
        acc = acc + ext[HALO:HALO + ts] * w_ref[1:2, cols]
        acc = acc + ext[HALO + 1:HALO + 1 + ts] * w_ref[2:3, cols]
        out_refs[part][...] = acc


def _short_conv(proj, conv_w, conv_b):
    B, L, _ = proj.shape
    ts = _tile(L, 512)
    first = (ATT_WIDTH + 2 * KV_WIDTH + POOL_WIDTH) // HY_WIDTH
    assert first * HY_WIDTH == ATT_WIDTH + 2 * KV_WIDTH + POOL_WIDTH
    specs = []
    for part in range(HY_ORDER + 1):
        specs += _halo_specs(ts, L, HY_WIDTH, first + part)
    n_col = (HY_ORDER + 1) * HY_WIDTH
    out = pl.BlockSpec((None, ts, HY_WIDTH), lambda b, i: (b, i, 0))
    return pl.pallas_call(
        _short_conv_kernel,
        out_shape=tuple(jax.ShapeDtypeStruct((B, L, HY_WIDTH), F32) for _ in range(HY_ORDER + 1)),
        grid=(B, L // ts),
        in_specs=specs + [pl.BlockSpec((HY_SHORT, n_col), lambda b, i: (0, 0)),
                          pl.BlockSpec((1, n_col), lambda b, i: (0, 0))],
        out_specs=(out,) * (HY_ORDER + 1),
        compiler_params=_cp(("parallel", "arbitrary")),
        name="hyena_short_conv",
    )(*([proj] * 9), conv_w, conv_b.reshape(1, n_col))


def _filter_tables(L):
    m = np.arange(L, dtype=np.float32)
    t = (m / np.float32(max(L - 1, 1))).astype(np.float32)
    w = (np.float32(2.0 * math.pi) * m / np.float32(L)).astype(np.float32)
    f = np.linspace(1e-4, HY_BANDS - 1, HY_BANDS, dtype=np.float32)
    ang = (w[:, None] * f).astype(np.float64)
    z = np.concatenate([t[:, None].astype(np.float64), np.cos(ang), -np.sin(ang)], axis=-1)
    z = np.pad(z, ((0, 0), (0, LANES - HY_EMB)))
    max_decay = math.log(HY_TARGET) / HY_SHORT_DECAY_PCT
    min_decay = math.log(HY_TARGET) / HY_LONG_DECAY_PCT
    deltas = np.linspace(min_decay, max_decay, HY_WIDTH, dtype=np.float32)
    decay = np.exp(-t[:, None].astype(np.float64) * np.abs(deltas)[None].astype(np.float64))
    return jnp.asarray(z, F32), jnp.asarray(decay, F32)


def _filter_kernel(z_ref, decay_ref, w1_ref, b1_ref, w2_ref, b2_ref, w3_ref, b3_ref, fr_ref, o_ref):
    tl = z_ref.shape[0]
    fr = fr_ref[...]
    a = jnp.sin(fr * (_dot3(z_ref[...], w1_ref[...]) + b1_ref[...]))
    a = jnp.sin(fr * (_dot3(a, w2_ref[...]) + b2_ref[...]))
    h = _dot3(a, w3_ref[...]) + b3_ref[...]
    h = h * jnp.tile(decay_ref[...], (1, 2 * HY_ORDER))
    row = pl.program_id(0) * tl + lax.broadcasted_iota(jnp.int32, h.shape, 0)
    col = lax.broadcasted_iota(jnp.int32, h.shape, 1)
    backward = (col // HY_WIDTH) % 2 == 1
    o_ref[...] = jnp.where(backward & (row == 0), 0.0, h)


def _hyena_filter_taps(L, w1, b1, w2, b2, w3, b3, freq):
    z, decay = _filter_tables(L)
    hid = w1.shape[1]
    n_out = w3.shape[1]
    tl = _tile(L, 512)
    w1p = jnp.pad(w1, ((0, LANES - HY_EMB), (0, 0)))
    full = lambda a: pl.BlockSpec(a.shape, lambda i: (0,) * a.ndim)
    args = [w1p, b1.reshape(1, hid), w2, b2.reshape(1, hid), w3, b3.reshape(1, n_out),
            freq.reshape(1, hid)]
    return pl.pallas_call(
        _filter_kernel,
        out_shape=jax.ShapeDtypeStruct((L, n_out), F32),
        grid=(L // tl,),
        in_specs=[pl.BlockSpec((tl, LANES), lambda i: (i, 0)),
                  pl.BlockSpec((tl, HY_WIDTH), lambda i: (i, 0))] + [full(a) for a in args],
        out_specs=pl.BlockSpec((tl, n_out), lambda i: (i, 0)),
        compiler_params=_cp(("arbitrary",)),
        name="hyena_filter_taps",
    )(z, decay, *args)


def _fft_matrices(n1, a_in):
    n = n1 * LANES
    k1 = np.arange(n1)
    a = np.arange(a_in)
    th1 = 2.0 * np.pi * np.outer(k1, a) / n1
    c1, s1 = np.cos(th1), np.sin(th1)
    m1_complex = np.block([[c1, s1], [-s1, c1]])
    m1_real = np.concatenate([c1, -s1], axis=0)
    m3 = np.block([[c1.T, -s1.T], [s1.T, c1.T]])
    b = np.arange(LANES)
    k2 = np.arange(LANES)
    k = k1[:, None, None] + n1 * k2[None, :, None]
    th2 = 2.0 * np.pi * (k * b[None, None, :] % n) / n
    c2, s2 = np.cos(th2), np.sin(th2)
    fwd = np.concatenate([np.concatenate([c2, s2], axis=2),
                          np.concatenate([-s2, c2], axis=2)], axis=1)
    c2t, s2t = np.swapaxes(c2, 1, 2), np.swapaxes(s2, 1, 2)
    inv = np.concatenate([np.concatenate([c2t, -s2t], axis=2),
                          np.concatenate([s2t, c2t], axis=2)], axis=1)
    as_bf16 = lambda m: jnp.asarray(m, F32).astype(BF16)
    return as_bf16(m1_complex), as_bf16(m1_real), as_bf16(m3), as_bf16(fwd), as_bf16(inv)


def _fft_stage1_kernel(*refs, complex_in):
    x_ref, m_ref, o_ref = refs
    if complex_in:
        x = jnp.concatenate([x_ref[0].astype(BF16), x_ref[1].astype(BF16)], axis=0)
    else:
        x = x_ref[...].astype(BF16)
    y = _dot(m_ref[...], x).astype(o_ref.dtype)
    n1 = o_ref.shape[1]
    o_ref[0] = y[:n1]
    o_ref[1] = y[n1:]


def _fft_stage1(x, m1, n1, complex_in):
    P, A, Wl = x.shape[0], x.shape[-2], x.shape[-1]
    tl = _tile(Wl, 4096, LANES)
    if complex_in:
        xspec = pl.BlockSpec((None, 2, A, tl), lambda p, j: (p, 0, 0, j))
    else:
        xspec = pl.BlockSpec((None, A, tl), lambda p, j: (p, 0, j))
    return pl.pallas_call(
        functools.partial(_fft_stage1_kernel, complex_in=complex_in),
        out_shape=jax.ShapeDtypeStruct((P, 2, n1, Wl), BF16),
        grid=(P, Wl // tl),
        in_specs=[xspec, pl.BlockSpec(m1.shape, lambda p, j: (0, 0))],
        out_specs=pl.BlockSpec((None, 2, n1, tl), lambda p, j: (p, 0, 0, j)),
        compiler_params=_cp(("parallel", "arbitrary")),
        name="fft_stage1",
    )(x, m1)


def _filter_spectrum_kernel(y_ref, g_ref, o0_ref, o1_ref, *, scale):
    kt = y_ref.shape[1]
    for t in range(kt):
        y = jnp.concatenate([y_ref[0, t], y_ref[1, t]], axis=0)
        f = _dot(g_ref[t], y)
        fr, fi = f[:LANES], f[LANES:]
        for o, o_ref in enumerate((o0_ref, o1_ref)):
            fw = slice((2 * o) * HY_WIDTH, (2 * o + 1) * HY_WIDTH)
            bw = slice((2 * o + 1) * HY_WIDTH, (2 * o + 2) * HY_WIDTH)
            o_ref[0, t] = (fr[:, fw] + fr[:, bw]) * scale
            o_ref[1, t] = (fi[:, fw] - fi[:, bw]) * scale


def _filter_spectrum(taps, mats, n1):
    L, n_col = taps.shape
    a_in = L // LANES
    _, m1_real, _, g_fwd, _ = mats
    y = _fft_stage1(taps.reshape(1, a_in, LANES * n_col), m1_real, n1, complex_in=False)
    y = y.reshape(2, n1, LANES, n_col)
    kt = _tile(n1, 4, 1)
    out = jax.ShapeDtypeStruct((2, n1, LANES, HY_WIDTH), F32)
    ospec = pl.BlockSpec((2, kt, LANES, HY_WIDTH), lambda i: (0, i, 0, 0))
    return pl.pallas_call(
        functools.partial(_filter_spectrum_kernel, scale=1.0 / (n1 * LANES)),
        out_shape=(out, out),
        grid=(n1 // kt,),
        in_specs=[pl.BlockSpec((2, kt, LANES, n_col), lambda i: (0, i, 0, 0)),
                  pl.BlockSpec((kt, 2 * LANES, 2 * LANES), lambda i: (i, 0, 0))],
        out_specs=(ospec, ospec),
        compiler_params=_cp(("arbitrary",)),
        name="hyena_filter_spectrum",
    )(y, g_fwd)


def _fft_mid_kernel(y_ref, kf_ref, g_ref, gi_ref, o_ref):
    kt = y_ref.shape[1]
    for t in range(kt):
        y = jnp.concatenate([y_ref[0, t], y_ref[1, t]], axis=0)
        f = _dot(g_ref[t], y)
        fr, fi = f[:LANES], f[LANES:]
        kr, ki = kf_ref[0, t], kf_ref[1, t]
        p = jnp.concatenate([fr * kr - fi * ki, fr * ki + fi * kr], axis=0).astype(BF16)
        u = _dot(gi_ref[t], p).astype(o_ref.dtype)
        o_ref[0, t] = u[:LANES]
        o_ref[1, t] = u[LANES:]


def _fft_mid(y, kf, g_fwd, g_inv):
    P, _, n1, _, W = y.shape
    kt = _tile(n1, 8, 1)
    yspec = pl.BlockSpec((None, 2, kt, LANES, W), lambda i, p: (p, 0, i, 0, 0))
    gspec = pl.BlockSpec((kt, 2 * LANES, 2 * LANES), lambda i, p: (i, 0, 0))
    return pl.pallas_call(
        _fft_mid_kernel,
        out_shape=jax.ShapeDtypeStruct(y.shape, BF16),
        grid=(n1 // kt, P),
        in_specs=[yspec, pl.BlockSpec((2, kt, LANES, W), lambda i, p: (0, i, 0, 0)), gspec, gspec],
        out_specs=yspec,
        compiler_params=_cp(("arbitrary", "arbitrary")),
        name="fft_mid",
    )(y, kf, g_fwd, g_inv)


def _fft_stage3_kernel(u_ref, m_ref, z_ref, x_ref, bias_ref, o_ref):
    u = jnp.concatenate([u_ref[0], u_ref[1]], axis=0)
    conv = _dot(m_ref[...], u)
    a_out = z_ref.shape[1]
    bias = bias_ref[...]
    o_ref[0] = x_ref[0] * (conv[:a_out] + z_ref[0] * bias)
    o_ref[1] = x_ref[1] * (conv[a_out:] + z_ref[1] * bias)


def _fft_stage3(u, m3, z, gate, bias_row):
    P, _, n1, Wl = u.shape
    A = z.shape[2]
    tl = _tile(Wl, 4096, LANES)
    pair = pl.BlockSpec((None, 2, A, tl), lambda p, j: (p, 0, 0, j))
    return pl.pallas_call(
        _fft_stage3_kernel,
        out_shape=jax.ShapeDtypeStruct(z.shape, F32),
        grid=(P, Wl // tl),
        in_specs=[pl.BlockSpec((None, 2, n1, tl), lambda p, j: (p, 0, 0, j)),
                  pl.BlockSpec(m3.shape, lambda p, j: (0, 0)),
                  pair, pair,
                  pl.BlockSpec((1, tl), lambda p, j: (0, j))],
        out_specs=pair,
        compiler_params=_cp(("parallel", "arbitrary")),
        name="fft_stage3",
    )(u, m3, z, gate, bias_row)


def _hyena_mixer(v, gates, kfs, d_bias, mats, n1):
    B, L, W = v.shape
    a_in = L // LANES
    pair_shape = (B // 2, 2, a_in, LANES * W)
    m1_complex, _, m3, g_fwd, g_inv = mats
    z = v.reshape(pair_shape)
    for o in range(HY_ORDER):
        y = _fft_stage1(z, m1_complex, n1, complex_in=True)
        u = _fft_mid(y.reshape(B // 2, 2, n1, LANES, W), kfs[o], g_fwd, g_inv)
        bias_row = jnp.tile(d_bias[o].astype(F32), LANES).reshape(1, LANES * W)
        z = _fft_stage3(u.reshape(B // 2, 2, n1, LANES * W), m3, z,
                        gates[o].reshape(pair_shape), bias_row)
    return z.reshape(B, L, W)


def kernel(x, c, ctx, c_ctx, w_mod, b_mod, norm1_g, w_in, attn_sink, pool_w, pool_scale,
           hy_conv_w, hy_conv_b, hy_f_w1, hy_f_b1, hy_f_w2, hy_f_b2, hy_f_w3, hy_f_b3,
           hy_f_freq, hy_bias, g_attn, g_pool, g_hyena, w_out, norm2_g,
           ff_w1, ff_w3, ff_w2, router_w, moe_w1, moe_w3, moe_w2, final_g):
    B, S, D = x.shape
    C = ctx.shape[1]
    depth = w_mod.shape[0]
    assert B % 2 == 0 and S % BLOCK == 0
    fft_mats = {}

    c_rows = jnp.concatenate([c, c_ctx[None], jnp.zeros((SUBLANES - B - 1, D), F32)], axis=0)
    xc = ctx

    def hyena_branch(l, proj, seq_len):
        rows = -(-seq_len // FFT_MIN_ROWS) * FFT_MIN_ROWS
        n1 = 2 * rows // LANES
        if rows not in fft_mats:
            fft_mats[rows] = _fft_matrices(n1, rows // LANES)
        mats = fft_mats[rows]
        taps = _hyena_filter_taps(seq_len, hy_f_w1[l], hy_f_b1[l], hy_f_w2[l], hy_f_b2[l],
                                  hy_f_w3[l], hy_f_b3[l], hy_f_freq[l])
        v, x1, x2 = _short_conv(proj, hy_conv_w[l], hy_conv_b[l])
        if seq_len < rows:
            pad = lambda t: jnp.pad(t, ((0, 0), (0, rows - seq_len), (0, 0)))
            taps = jnp.pad(taps, ((0, rows - seq_len), (0, 0)))
            v, x1, x2 = pad(v), pad(x1), pad(x2)
        kfs = _filter_spectrum(taps, mats, n1)
        z = _hyena_mixer(v, (x1, x2), kfs, hy_bias[l], mats, n1)
        return _rmsnorm(z[:, :seq_len], g_hyena[l], BF16)

    def mix(l, proj, y_att, res, gate, w_out_l, seq_len):
        y_pool = _pool_mixer(proj, pool_w[l], pool_scale[l], g_pool[l])
        y_hy = hyena_branch(l, proj, seq_len)
        return _matmul_gated_residual([y_att, y_pool, y_hy], w_out_l, res, gate)

    for l in range(depth):
        last = l == depth - 1
        mod = _adaln(c_rows, w_mod[l], b_mod[l])
        sh1, sc1, g1, sh2, sc2, g2 = [mod[:B, j * D:(j + 1) * D] for j in range(6)]
        csh1, csc1, cg1, csh2, csc2, cg2 = [mod[B:B + 1, j * D:(j + 1) * D] for j in range(6)]
        w_in_l = w_in[l].astype(BF16)
        w_out_l = w_out[l].astype(BF16)

        hc = _norm_mod(xc, norm1_g[l], csh1, csc1)
        if last:
            kv_w = w_in_l[:, ATT_WIDTH:ATT_WIDTH + 2 * KV_WIDTH]
            kv_c = _matmul(hc, kv_w, out_dtype=BF16)
            k_c, v_c = kv_c[..., :KV_WIDTH], kv_c[..., KV_WIDTH:]
        else:
            proj_c = _matmul(hc, w_in_l)
            q_c, k_c, v_c = _qkv_prepare(proj_c, rope=False)
            y_att_c = _attention(q_c, None, None, k_c, v_c, attn_sink[l], g_attn[l])
            xc_new = mix(l, proj_c, y_att_c, xc, cg1, w_out_l, C)
            hc2 = _norm_mod(xc_new, norm2_g[l], csh2, csc2)
            i = l // 2
            if l % 2 == 0:
                hid_c = _swiglu_hidden(hc2, ff_w1[i].astype(BF16), ff_w3[i].astype(BF16))
                xc_new = _matmul_gated_residual([hid_c], ff_w2[i].astype(BF16), xc_new, cg2)
            else:
                raise NotImplementedError("context tokens through an expert layer")

        h = _norm_mod(x, norm1_g[l], sh1, sc1)
        proj = _matmul(h, w_in_l)
        q, k, v = _qkv_prepare(proj, rope=True)
        y_att = _attention(q, k, v, k_c, v_c, attn_sink[l], g_attn[l])
        x = mix(l, proj, y_att, x, g1, w_out_l, S)

        i = l // 2
        if l % 2 == 0:
            h2 = _norm_mod(x, norm2_g[l], sh2, sc2)
            hid = _swiglu_hidden(h2, ff_w1[i].astype(BF16), ff_w3[i].astype(BF16))
            x = _matmul_gated_residual([hid], ff_w2[i].astype(BF16), x, g2)
            if last:
                x = _rmsnorm(x, final_g, F32)
        else:
            h2, gi, gw = _norm_mod_router(x, norm2_g[l], sh2, sc2, router_w[i])
            assert last
            x = _moe_layer(x, h2, gi, gw, g2, final_g, moe_w1[i], moe_w3[i], moe_w2[i])
        if not last:
            xc = xc_new
    return x
```

```python
import functools
import math

import numpy as np
import jax
import jax.numpy as jnp
from jax import lax
from jax.experimental import pallas as pl
from jax.experimental.pallas import tpu as pltpu

F32 = jnp.float32
BF16 = jnp.bfloat16

EPS = 1e-6
NEG = -1e30
GRID_W = 64
ATT_HEADS = 8
ATT_KV_HEADS = 2
ATT_GROUP = ATT_HEADS // ATT_KV_HEADS
HEAD_DIM = 128
ATT_WIDTH = ATT_HEADS * HEAD_DIM
KV_WIDTH = ATT_KV_HEADS * HEAD_DIM
WINDOW = 128
BLOCK = 128
ROPE_BASE = 10000.0
POOL_WINDOWS = (2, 4, 8, 16)
POOL_GROUP = 128
POOL_WIDTH = POOL_GROUP * len(POOL_WINDOWS)
HY_WIDTH = 512
HY_ORDER = 2
HY_SHORT = 3
HY_EMB = 33
HY_BANDS = (HY_EMB - 1) // 2
HY_SHORT_DECAY_PCT = 0.3
HY_LONG_DECAY_PCT = 1.5
HY_TARGET = 1e-2
N_EXPERTS = 8
TOP_K = 2

LANES = 128
SUBLANES = 8
HALO = SUBLANES
FFT_MIN_ROWS = 2048
FFT_ROWS = SUBLANES
VMEM_LIMIT = 56 * 1024 * 1024


def _cp(sem):
    return pltpu.CompilerParams(dimension_semantics=sem, vmem_limit_bytes=VMEM_LIMIT)


def _tile(n, pref, mult=SUBLANES):
    if n <= pref:
        return n
    t = (pref // mult) * mult
    while t >= mult:
        if n % t == 0:
            return t
        t -= mult
    return n


def _split_bf16(x):
    hi = x.astype(BF16)
    lo = (x - hi.astype(F32)).astype(BF16)
    return hi, lo


def _dot(a, b):
    return jnp.dot(a, b, preferred_element_type=F32)


def _dot3(a, b):
    ah, al = _split_bf16(a)
    bh, bl = _split_bf16(b)
    return _dot(ah, bh) + (_dot(ah, bl) + _dot(al, bh))


def _adaln_kernel(c_ref, w_ref, b_ref, o_ref):
    c = c_ref[...]
    a = c * (1.0 / (1.0 + jnp.exp(-c)))
    o_ref[...] = _dot3(a, w_ref[...]) + b_ref[...]


def _adaln(c_rows, w, b):
    R, D = c_rows.shape
    N = w.shape[1]
    tn = _tile(N, 1536, LANES)
    return pl.pallas_call(
        _adaln_kernel,
        out_shape=jax.ShapeDtypeStruct((R, N), F32),
        grid=(N // tn,),
        in_specs=[pl.BlockSpec((R, D), lambda j: (0, 0)),
                  pl.BlockSpec((D, tn), lambda j: (0, j)),
                  pl.BlockSpec((1, tn), lambda j: (0, j))],
        out_specs=pl.BlockSpec((R, tn), lambda j: (0, j)),
        compiler_params=_cp(("arbitrary",)),
        name="adaln",
    )(c_rows, w, b.reshape(1, N))


def _norm_mod_kernel(x_ref, g_ref, sh_ref, sc_ref, o_ref):
    x = x_ref[...]
    y = x * lax.rsqrt(jnp.mean(x * x, axis=-1, keepdims=True) + EPS) * g_ref[...]
    o_ref[...] = (y * (1.0 + sc_ref[...]) + sh_ref[...]).astype(o_ref.dtype)


def _bcast_map(arr):
    if arr.shape[0] == 1:
        return lambda b, i: (0, 0, 0)
    return lambda b, i: (b, 0, 0)


def _norm_mod(x, g, shift, scale):
    B, L, D = x.shape
    ts = _tile(L, 512)
    shift = shift[:, None, :]
    scale = scale[:, None, :]
    return pl.pallas_call(
        _norm_mod_kernel,
        out_shape=jax.ShapeDtypeStruct((B, L, D), BF16),
        grid=(B, L // ts),
        in_specs=[pl.BlockSpec((None, ts, D), lambda b, i: (b, i, 0)),
                  pl.BlockSpec((1, D), lambda b, i: (0, 0)),
                  pl.BlockSpec((None, 1, D), _bcast_map(shift)),
                  pl.BlockSpec((None, 1, D), _bcast_map(scale))],
        out_specs=pl.BlockSpec((None, ts, D), lambda b, i: (b, i, 0)),
        compiler_params=_cp(("parallel", "parallel")),
        name="norm_mod",
    )(x, g.reshape(1, D), shift, scale)


def _norm_mod_router_kernel(x_ref, g_ref, sh_ref, sc_ref, rw_ref, tri_ref,
                            o_ref, gi_ref, gw_ref, cnt_ref, run_ref):
    @pl.when((pl.program_id(0) == 0) & (pl.program_id(1) == 0))
    def _():
        run_ref[...] = jnp.zeros(run_ref.shape, run_ref.dtype)

    x = x_ref[...]
    y = x * lax.rsqrt(jnp.mean(x * x, axis=-1, keepdims=True) + EPS) * g_ref[...]
    h = y * (1.0 + sc_ref[...]) + sh_ref[...]
    o_ref[...] = h.astype(o_ref.dtype)
    logits = _dot3(h, rw_ref[...])
    lane = lax.broadcasted_iota(jnp.int32, logits.shape, 1)
    logits = jnp.where(lane < N_EXPERTS, logits, NEG)
    m1 = jnp.max(logits, axis=-1, keepdims=True)
    i1 = jnp.min(jnp.where(logits == m1, lane, LANES), axis=-1, keepdims=True)
    rest = jnp.where(lane == i1, NEG, logits)
    m2 = jnp.max(rest, axis=-1, keepdims=True)
    i2 = jnp.min(jnp.where(rest == m2, lane, LANES), axis=-1, keepdims=True)
    e2 = jnp.exp(m2 - m1)
    w1 = 1.0 / (1.0 + e2)
    w2 = e2 * w1
    pick1 = lane == i1
    pick2 = lane == i2
    both = jnp.where(pick1 | pick2, 1.0, 0.0)
    before = _dot(tri_ref[...], both.astype(BF16)) + run_ref[...]
    r1 = jnp.sum(jnp.where(pick1, before, 0.0), axis=-1, keepdims=True).astype(jnp.int32)
    r2 = jnp.sum(jnp.where(pick2, before, 0.0), axis=-1, keepdims=True).astype(jnp.int32)
    run_ref[...] = run_ref[...] + jnp.sum(both, axis=0, keepdims=True)
    cnt_ref[...] = run_ref[...]
    gi_ref[...] = jnp.where(lane == 0, i1, jnp.where(lane == 1, i2,
                            jnp.where(lane == 2, r1, jnp.where(lane == 3, r2, 0))))
    gw_ref[...] = jnp.where(lane == 0, w1, jnp.where(lane == 1, w2, 0.0))


def _norm_mod_router(x, g, shift, scale, router_w):
    B, L, D = x.shape
    ts = _tile(L, 512)
    shift = shift[:, None, :]
    scale = scale[:, None, :]
    rw = jnp.pad(router_w, ((0, 0), (0, LANES - router_w.shape[1])))
    tri = jnp.asarray(np.tril(np.ones((ts, ts), np.float32), -1), BF16)
    row = pl.BlockSpec((None, ts, D), lambda b, i: (b, i, 0))
    small = pl.BlockSpec((None, ts, LANES), lambda b, i: (b, i, 0))
    return pl.pallas_call(
        _norm_mod_router_kernel,
        out_shape=(jax.ShapeDtypeStruct((B, L, D), BF16),
                   jax.ShapeDtypeStruct((B, L, LANES), jnp.int32),
                   jax.ShapeDtypeStruct((B, L, LANES), F32),
                   jax.ShapeDtypeStruct((1, LANES), F32)),
        grid=(B, L // ts),
        in_specs=[row,
                  pl.BlockSpec((1, D), lambda b, i: (0, 0)),
                  pl.BlockSpec((None, 1, D), _bcast_map(shift)),
                  pl.BlockSpec((None, 1, D), _bcast_map(scale)),
                  pl.BlockSpec((D, LANES), lambda b, i: (0, 0)),
                  pl.BlockSpec((ts, ts), lambda b, i: (0, 0))],
        out_specs=(row, small, small, pl.BlockSpec((1, LANES), lambda b, i: (0, 0))),
        scratch_shapes=[pltpu.VMEM((1, LANES), F32)],
        compiler_params=_cp(("arbitrary", "arbitrary")),
        name="norm_mod_router",
    )(x, g.reshape(1, D), shift, scale, rw, tri)


def _rmsnorm_kernel(x_ref, g_ref, o_ref):
    x = x_ref[...]
    y = x * lax.rsqrt(jnp.mean(x * x, axis=-1, keepdims=True) + EPS) * g_ref[...]
    o_ref[...] = y.astype(o_ref.dtype)


def _rmsnorm(x, g, out_dtype):
    B, L, D = x.shape
    ts = _tile(L, 512)
    return pl.pallas_call(
        _rmsnorm_kernel,
        out_shape=jax.ShapeDtypeStruct((B, L, D), out_dtype),
        grid=(B, L // ts),
        in_specs=[pl.BlockSpec((None, ts, D), lambda b, i: (b, i, 0)),
                  pl.BlockSpec((1, D), lambda b, i: (0, 0))],
        out_specs=pl.BlockSpec((None, ts, D), lambda b, i: (b, i, 0)),
        compiler_params=_cp(("parallel", "parallel")),
        name="rmsnorm",
    )(x, g.reshape(1, D))


def _mm_kernel(a_ref, w_ref, o_ref):
    o_ref[...] = _dot(a_ref[...], w_ref[...]).astype(o_ref.dtype)


def _matmul(a, w, out_dtype=F32, tm=1024, tn=512):
    B, L, K = a.shape
    N = w.shape[1]
    tm = _tile(L, tm)
    tn = _tile(N, tn, LANES)
    return pl.pallas_call(
        _mm_kernel,
        out_shape=jax.ShapeDtypeStruct((B, L, N), out_dtype),
        grid=(B, L // tm, N // tn),
        in_specs=[pl.BlockSpec((None, tm, K), lambda b, i, j: (b, i, 0)),
                  pl.BlockSpec((K, tn), lambda b, i, j: (0, j))],
        out_specs=pl.BlockSpec((None, tm, tn), lambda b, i, j: (b, i, j)),
        compiler_params=_cp(("parallel", "parallel", "arbitrary")),
        name="matmul",
    )(a, w)


def _mm_res_kernel(*refs, n_a):
    a_refs = refs[:n_a]
    w_refs = refs[n_a:2 * n_a]
    res_ref, gate_ref, o_ref = refs[2 * n_a:]
    acc = _dot(a_refs[0][...], w_refs[0][...])
    for a_ref, w_ref in zip(a_refs[1:], w_refs[1:]):
        acc = acc + _dot(a_ref[...], w_ref[...])
    o_ref[...] = res_ref[...] + gate_ref[...] * acc


def _matmul_gated_residual(a_list, w, res, gate, tm=1024, tn=512):
    B, L, N = res.shape
    tm = _tile(L, tm)
    tn = _tile(N, tn, LANES)
    gate = gate[:, None, :]
    widths = [a.shape[-1] for a in a_list]
    unit = math.gcd(*widths) if len(widths) > 1 else widths[0]
    in_specs = [pl.BlockSpec((None, tm, k), lambda b, i, j: (b, i, 0)) for k in widths]
    off = 0
    for k in widths:
        assert off % k == 0 and k % unit == 0
        in_specs.append(pl.BlockSpec((k, tn), functools.partial(
            lambda b, i, j, blk: (blk, j), blk=off // k)))
        off += k
    gmap = (lambda b, i, j: (0, 0, j)) if gate.shape[0] == 1 else (lambda b, i, j: (b, 0, j))
    in_specs += [pl.BlockSpec((None, tm, tn), lambda b, i, j: (b, i, j)),
                 pl.BlockSpec((None, 1, tn), gmap)]
    return pl.pallas_call(
        functools.partial(_mm_res_kernel, n_a=len(a_list)),
        out_shape=jax.ShapeDtypeStruct((B, L, N), F32),
        grid=(B, L // tm, N // tn),
        in_specs=in_specs,
        out_specs=pl.BlockSpec((None, tm, tn), lambda b, i, j: (b, i, j)),
        compiler_params=_cp(("parallel", "parallel", "arbitrary")),
        name="matmul_gated_residual",
    )(*a_list, *([w] * len(a_list)), res, gate)


def _silu(x):
    return x * (1.0 / (1.0 + jnp.exp(-x)))


def _swiglu_kernel(a_ref, w1_ref, w3_ref, o_ref):
    a = a_ref[...]
    o_ref[...] = (_silu(_dot(a, w1_ref[...])) * _dot(a, w3_ref[...])).astype(o_ref.dtype)


def _swiglu_hidden(a, w1, w3, tm=1024, tf=512):
    B, L, D = a.shape
    F = w1.shape[1]
    tm = _tile(L, tm)
    tf = _tile(F, tf, LANES)
    wspec = pl.BlockSpec((D, tf), lambda b, i, j: (0, j))
    return pl.pallas_call(
        _swiglu_kernel,
        out_shape=jax.ShapeDtypeStruct((B, L, F), BF16),
        grid=(B, L // tm, F // tf),
        in_specs=[pl.BlockSpec((None, tm, D), lambda b, i, j: (b, i, 0)), wspec, wspec],
        out_specs=pl.BlockSpec((None, tm, tf), lambda b, i, j: (b, i, j)),
        compiler_params=_cp(("parallel", "parallel", "arbitrary")),
        name="swiglu_hidden",
    )(a, w1, w3)


def _moe_hidden_kernel(te_ref, nt_ref, a_ref, w1_ref, w3_ref, o_ref):
    used = pl.program_id(0) < nt_ref[0]

    @pl.when(used)
    def _():
        a = a_ref[...]
        w1 = w1_ref[...].astype(BF16)
        w3 = w3_ref[...].astype(BF16)
        o_ref[...] = (_silu(_dot(a, w1)) * _dot(a, w3)).astype(o_ref.dtype)

    @pl.when(jnp.logical_not(used))
    def _():
        o_ref[...] = jnp.zeros(o_ref.shape, o_ref.dtype)


def _moe_out_kernel(te_ref, nt_ref, a_ref, w2_ref, o_ref):
    used = pl.program_id(0) < nt_ref[0]

    @pl.when(used)
    def _():
        o_ref[...] = _dot(a_ref[...], w2_ref[...].astype(BF16)).astype(o_ref.dtype)

    @pl.when(jnp.logical_not(used))
    def _():
        o_ref[...] = jnp.zeros(o_ref.shape, o_ref.dtype)


def _moe_experts(a_sorted, tile_expert, n_tiles_used, w1, w3, w2, tm, tf=512, tn=256):
    R, D = a_sorted.shape
    E, _, F = w1.shape
    tf = _tile(F, tf, LANES)
    tn = _tile(D, tn, LANES)
    nt = R // tm
    w13 = pl.BlockSpec((None, D, tf), lambda i, j, te, n: (te[i], 0, j))
    hidden = pl.pallas_call(
        _moe_hidden_kernel,
        out_shape=jax.ShapeDtypeStruct((R, F), BF16),
        grid_spec=pltpu.PrefetchScalarGridSpec(
            num_scalar_prefetch=2,
            grid=(nt, F // tf),
            in_specs=[pl.BlockSpec((tm, D), lambda i, j, te, n: (i, 0)), w13, w13],
            out_specs=pl.BlockSpec((tm, tf), lambda i, j, te, n: (i, j))),
        compiler_params=_cp(("arbitrary", "arbitrary")),
        name="moe_hidden",
    )(tile_expert, n_tiles_used, a_sorted, w1, w3)
    return pl.pallas_call(
        _moe_out_kernel,
        out_shape=jax.ShapeDtypeStruct((R, D), F32),
        grid_spec=pltpu.PrefetchScalarGridSpec(
            num_scalar_prefetch=2,
            grid=(nt, D // tn),
            in_specs=[pl.BlockSpec((tm, F), lambda i, j, te, n: (i, 0)),
                      pl.BlockSpec((None, F, tn), lambda i, j, te, n: (te[i], 0, j))],
            out_specs=pl.BlockSpec((tm, tn), lambda i, j, te, n: (i, j))),
        compiler_params=_cp(("arbitrary", "arbitrary")),
        name="moe_out",
    )(tile_expert, n_tiles_used, hidden, w2)


def _moe_combine_kernel(x_ref, ya_ref, yb_ref, gw_ref, gate_ref, g_ref, o_ref):
    gw = gw_ref[...]
    moe = gw[:, 0:1] * ya_ref[...] + gw[:, 1:2] * yb_ref[...]
    x = x_ref[...] + gate_ref[...] * moe
    y = x * lax.rsqrt(jnp.mean(x * x, axis=-1, keepdims=True) + EPS) * g_ref[...]
    o_ref[...] = y


def _moe_combine_norm(x, ya, yb, gw, gate, g):
    B, L, D = x.shape
    ts = _tile(L, 512)
    row = pl.BlockSpec((None, ts, D), lambda b, i: (b, i, 0))
    return pl.pallas_call(
        _moe_combine_kernel,
        out_shape=jax.ShapeDtypeStruct((B, L, D), F32),
        grid=(B, L // ts),
        in_specs=[row, row, row,
                  pl.BlockSpec((None, ts, LANES), lambda b, i: (b, i, 0)),
                  pl.BlockSpec((None, 1, D), lambda b, i: (b, 0, 0)),
                  pl.BlockSpec((1, D), lambda b, i: (0, 0))],
        out_specs=row,
        compiler_params=_cp(("parallel", "parallel")),
        name="moe_combine_norm",
    )(x, ya, yb, gw, gate[:, None, :], g.reshape(1, D))


def _moe_layer(x, h, gi, gw, counts, gate, final_g, w1, w3, w2):
    B, L, D = x.shape
    E = w1.shape[0]
    n_tok = B * L
    n_pair = n_tok * TOP_K
    tm = _tile(n_pair, 1024)
    nt = n_pair // tm + E
    e_pair = gi[..., :TOP_K].reshape(n_pair)
    rank = gi[..., TOP_K:2 * TOP_K].reshape(n_pair)
    counts = counts[0, :E].astype(jnp.int32)
    tiles_per = (counts + tm - 1) // tm
    tile_end = jnp.cumsum(tiles_per)
    tile_start = tile_end - tiles_per
    row_start = jnp.zeros((n_pair,), jnp.int32)
    for e in range(E):
        row_start = jnp.where(e_pair == e, tile_start[e] * tm, row_start)
    pos = row_start + rank
    tile_ids = jnp.arange(nt, dtype=jnp.int32)
    tile_expert = jnp.minimum(
        jnp.sum((tile_ids[:, None] >= tile_end[None, :]).astype(jnp.int32), axis=1), E - 1)
    n_used = tile_end[-1:].astype(jnp.int32)
    tok_pair = jnp.arange(n_pair, dtype=jnp.int32) // TOP_K
    src = jnp.zeros((nt * tm,), jnp.int32).at[pos].set(tok_pair, unique_indices=True)
    a_sorted = jnp.take(h.reshape(n_tok, D), src, axis=0)
    y_sorted = _moe_experts(a_sorted, tile_expert.astype(jnp.int32), n_used, w1, w3, w2, tm)
    pos2 = pos.reshape(n_tok, TOP_K)
    ya = jnp.take(y_sorted, pos2[:, 0], axis=0).reshape(B, L, D)
    yb = jnp.take(y_sorted, pos2[:, 1], axis=0).reshape(B, L, D)
    return _moe_combine_norm(x, ya, yb, gw, gate, final_g)


def _rope_tables(n_tokens):
    pos = np.arange(n_tokens)
    row = (pos // GRID_W).astype(np.float32)
    col = (pos % GRID_W).astype(np.float32)
    n_freq = HEAD_DIM // 4
    inv = (np.float32(ROPE_BASE) ** (-np.arange(n_freq, dtype=np.float32) / np.float32(n_freq)))
    ang_r = (row[:, None] * inv).astype(np.float64)
    ang_c = (col[:, None] * inv).astype(np.float64)
    cos = np.concatenate([np.cos(ang_r), np.cos(ang_r), np.cos(ang_c), np.cos(ang_c)], axis=1)
    sin = np.concatenate([-np.sin(ang_r), np.sin(ang_r), -np.sin(ang_c), np.sin(ang_c)], axis=1)
    return jnp.asarray(cos, F32), jnp.asarray(sin, F32)


def _swap_halves(x):
    n = x.shape[-1]
    quarter = HEAD_DIM // 4
    lane = lax.broadcasted_iota(jnp.int32, x.shape, 1)
    up = pltpu.roll(x, n - quarter, axis=1)
    down = pltpu.roll(x, quarter, axis=1)
    return jnp.where((lane & quarter) == 0, up, down)


def _qkv_kernel(q_ref, k_ref, v_ref, cos_ref, sin_ref, qo_ref, ko_ref, vo_ref, *, rope):
    q = q_ref[...]
    k = k_ref[...]
    if rope:
        cos = cos_ref[...]
        sin = sin_ref[...]
        q = q * jnp.tile(cos, (1, ATT_HEADS)) + _swap_halves(q) * jnp.tile(sin, (1, ATT_HEADS))
        k = k * jnp.tile(cos, (1, ATT_KV_HEADS)) + _swap_halves(k) * jnp.tile(sin, (1, ATT_KV_HEADS))
    qo_ref[...] = (q * (HEAD_DIM ** -0.5)).astype(BF16)
    ko_ref[...] = k.astype(BF16)
    vo_ref[...] = v_ref[...].astype(BF16)


def _qkv_prepare(proj, rope):
    B, L, _ = proj.shape
    ts = _tile(L, 512)
    cos, sin = _rope_tables(L)
    kvb = ATT_WIDTH // KV_WIDTH
    tab = pl.BlockSpec((ts, HEAD_DIM), lambda b, i: (i, 0))
    return pl.pallas_call(
        functools.partial(_qkv_kernel, rope=rope),
        out_shape=(jax.ShapeDtypeStruct((B, L, ATT_WIDTH), BF16),
                   jax.ShapeDtypeStruct((B, L, KV_WIDTH), BF16),
                   jax.ShapeDtypeStruct((B, L, KV_WIDTH), BF16)),
        grid=(B, L // ts),
        in_specs=[pl.BlockSpec((None, ts, ATT_WIDTH), lambda b, i: (b, i, 0)),
                  pl.BlockSpec((None, ts, KV_WIDTH), lambda b, i: (b, i, kvb)),
                  pl.BlockSpec((None, ts, KV_WIDTH), lambda b, i: (b, i, kvb + 1)),
                  tab, tab],
        out_specs=(pl.BlockSpec((None, ts, ATT_WIDTH), lambda b, i: (b, i, 0)),
                   pl.BlockSpec((None, ts, KV_WIDTH), lambda b, i: (b, i, 0)),
                   pl.BlockSpec((None, ts, KV_WIDTH), lambda b, i: (b, i, 0))),
        compiler_params=_cp(("parallel", "parallel")),
        name="qkv_prepare",
    )(proj, proj, proj, cos, sin)


def _nt_dot(a, b):
    return lax.dot_general(a, b, (((1,), (1,)), ((), ())), preferred_element_type=F32)


def _attn_bias_tables():
    T, G = BLOCK, ATT_GROUP
    qi = np.arange(G * T)[:, None] % T
    ki = np.arange(3 * T)[None, :]
    band = np.abs(ki - T - qi) <= WINDOW
    after_start = ki >= T
    before_end = ki < 2 * T
    masks = [band & after_start, band, band & before_end, band & after_start & before_end]
    return jnp.asarray(np.stack([np.where(m, 0.0, NEG) for m in masks]), F32)


def _attn_kernel(*refs, local, n_blocks):
    if local:
        (q_ref, kp_ref, kc_ref, kn_ref, vp_ref, vc_ref, vn_ref,
         kx_ref, vx_ref, bias_ref, sink_ref, g_ref, o_ref, acc_ref) = refs
    else:
        q_ref, kx_ref, vx_ref, sink_ref, g_ref, o_ref, acc_ref = refs
    T = BLOCK
    G = ATT_GROUP
    n_sub = q_ref.shape[0] // T
    for h in range(ATT_KV_HEADS):
        cols = slice(h * HEAD_DIM, (h + 1) * HEAD_DIM)
        sink = jnp.concatenate(
            [jnp.broadcast_to(sink_ref[:, (h * G + g) * HEAD_DIM:(h * G + g) * HEAD_DIM + 1], (T, 1))
             for g in range(G)], axis=0)
        kx = kx_ref[:, cols]
        vx = vx_ref[:, cols]
        if local:
            k_band = jnp.concatenate([kp_ref[:, cols], kc_ref[:, cols], kn_ref[:, cols]], axis=0)
            v_band = jnp.concatenate([vp_ref[:, cols], vc_ref[:, cols], vn_ref[:, cols]], axis=0)
        for j in range(n_sub):
            rows = slice(j * T, (j + 1) * T)
            qs = jnp.concatenate(
                [q_ref[rows, (h * G + g) * HEAD_DIM:(h * G + g + 1) * HEAD_DIM] for g in range(G)],
                axis=0)
            s_ctx = _nt_dot(qs, kx)
            m = jnp.maximum(jnp.max(s_ctx, axis=-1, keepdims=True), sink)
            if local:
                blk = pl.program_id(1) * n_sub + j
                is_first = blk == 0
                is_last = blk == n_blocks - 1
                table = jnp.where(is_first, jnp.where(is_last, 3, 0), jnp.where(is_last, 2, 1))
                s_loc = _nt_dot(qs, k_band[j * T:(j + 3) * T]) + bias_ref[table]
                m = jnp.maximum(m, jnp.max(s_loc, axis=-1, keepdims=True))
            p_ctx = jnp.exp(s_ctx - m)
            denom = jnp.sum(p_ctx, axis=-1, keepdims=True) + jnp.exp(sink - m)
            o = _dot(p_ctx.astype(BF16), vx)
            if local:
                p_loc = jnp.exp(s_loc - m)
                denom = denom + jnp.sum(p_loc, axis=-1, keepdims=True)
                o = o + _dot(p_loc.astype(BF16), v_band[j * T:(j + 3) * T])
            o = o * (1.0 / denom)
            for g in range(G):
                acc_ref[rows, (h * G + g) * HEAD_DIM:(h * G + g + 1) * HEAD_DIM] = o[g * T:(g + 1) * T]
    y = acc_ref[...]
    y = y * lax.rsqrt(jnp.mean(y * y, axis=-1, keepdims=True) + EPS) * g_ref[...]
    o_ref[...] = y.astype(o_ref.dtype)


def _attention(q, k, v, k_ctx, v_ctx, sink, g_attn):
    B, L, _ = q.shape
    C = k_ctx.shape[1]
    local = k is not None
    T = BLOCK
    nb = L // T
    n_sub = 2 if nb % 2 == 0 else 1
    sink_row = jnp.repeat(sink.astype(F32), HEAD_DIM).reshape(1, ATT_WIDTH)
    qspec = pl.BlockSpec((None, n_sub * T, ATT_WIDTH), lambda b, i: (b, i, 0))
    in_specs = [qspec]
    args = [q]
    cspec = pl.BlockSpec((None, C, KV_WIDTH), lambda b, i: (b, 0, 0))
    row = pl.BlockSpec((1, ATT_WIDTH), lambda b, i: (0, 0))
    if local:
        prev = pl.BlockSpec((None, T, KV_WIDTH), lambda b, i: (b, jnp.maximum(i * n_sub - 1, 0), 0))
        cur = pl.BlockSpec((None, n_sub * T, KV_WIDTH), lambda b, i: (b, i, 0))
        nxt = pl.BlockSpec((None, T, KV_WIDTH),
                           lambda b, i: (b, jnp.minimum((i + 1) * n_sub, nb - 1), 0))
        bias = _attn_bias_tables()
        in_specs += [prev, cur, nxt, prev, cur, nxt, cspec, cspec,
                     pl.BlockSpec(bias.shape, lambda b, i: (0, 0, 0))]
        args += [k, k, k, v, v, v, k_ctx, v_ctx, bias]
    else:
        in_specs += [cspec, cspec]
        args += [k_ctx, v_ctx]
    in_specs += [row, row]
    args += [sink_row, g_attn.reshape(1, ATT_WIDTH)]
    return pl.pallas_call(
        functools.partial(_attn_kernel, local=local, n_blocks=nb),
        out_shape=jax.ShapeDtypeStruct((B, L, ATT_WIDTH), BF16),
        grid=(B, nb // n_sub),
        in_specs=in_specs,
        out_specs=qspec,
        scratch_shapes=[pltpu.VMEM((n_sub * T, ATT_WIDTH), F32)],
        compiler_params=_cp(("parallel", "arbitrary")),
        name="attention_local" if local else "attention_context",
    )(*args)


def _halo_specs(ts, L, width, col_block):
    nb8 = L // HALO
    per = ts // HALO
    prev = pl.BlockSpec((None, HALO, width),
                        lambda b, i: (b, jnp.maximum(i * per - 1, 0), col_block))
    cur = pl.BlockSpec((None, ts, width), lambda b, i: (b, i, col_block))
    nxt = pl.BlockSpec((None, HALO, width),
                       lambda b, i: (b, jnp.minimum((i + 1) * per, nb8 - 1), col_block))
    return [prev, cur, nxt]


def _with_halo(prev_ref, cur_ref, next_ref):
    i = pl.program_id(1)
    last = pl.num_programs(1) - 1
    prev = jnp.where(i > 0, prev_ref[...], 0.0)
    nxt = jnp.where(i < last, next_ref[...], 0.0)
    return jnp.concatenate([prev, cur_ref[...], nxt], axis=0)


def _pool_kernel(prev_ref, cur_ref, next_ref, w_ref, scale_ref, g_ref, o_ref, *, seq_len):
    ts = cur_ref.shape[0]
    ext = _with_halo(prev_ref, cur_ref, next_ref)
    pos = pl.program_id(1) * ts + lax.broadcasted_iota(jnp.int32, (ts, 1), 0)
    outs = []
    for gidx, win in enumerate(POOL_WINDOWS):
        cols = slice(gidx * POOL_GROUP, (gidx + 1) * POOL_GROUP)
        run = ext[:, cols]
        step = 1
        while step < win:
            run = run[:run.shape[0] - step] + run[step:]
            step *= 2
        lo = HALO - win // 2
        total = run[lo:lo + ts]
        cnt = (jnp.minimum(pos + (win - win // 2), seq_len) - jnp.maximum(pos - win // 2, 0))
        mean = total * (1.0 / cnt.astype(F32))
        outs.append(_dot3(mean - cur_ref[:, cols], w_ref[gidx]))
    y = jnp.concatenate(outs, axis=-1) * scale_ref[...]
    y = y * lax.rsqrt(jnp.mean(y * y, axis=-1, keepdims=True) + EPS) * g_ref[...]
    o_ref[...] = y.astype(o_ref.dtype)


def _pool_mixer(proj, pool_w, pool_scale, g_pool):
    B, L, _ = proj.shape
    ts = _tile(L, 512)
    col_block = (ATT_WIDTH + 2 * KV_WIDTH) // POOL_WIDTH
    assert col_block * POOL_WIDTH == ATT_WIDTH + 2 * KV_WIDTH
    row = pl.BlockSpec((1, POOL_WIDTH), lambda b, i: (0, 0))
    return pl.pallas_call(
        functools.partial(_pool_kernel, seq_len=L),
        out_shape=jax.ShapeDtypeStruct((B, L, POOL_WIDTH), BF16),
        grid=(B, L // ts),
        in_specs=_halo_specs(ts, L, POOL_WIDTH, col_block) + [
            pl.BlockSpec(pool_w.shape, lambda b, i: (0, 0, 0)), row, row],
        out_specs=pl.BlockSpec((None, ts, POOL_WIDTH), lambda b, i: (b, i, 0)),
        compiler_params=_cp(("parallel", "arbitrary")),
        name="pool_mixer",
    )(proj, proj, proj, pool_w, pool_scale.reshape(1, POOL_WIDTH), g_pool.reshape(1, POOL_WIDTH))


def _short_conv_kernel(*refs):
    halo_refs = refs[:9]
    w_ref, b_ref = refs[9:11]
    out_refs = refs[11:]
    ts = out_refs[0].shape[0]
    for part in range(HY_ORDER + 1):
        ext = _with_halo(*halo_refs[3 * part:3 * part + 3])
        cols = slice(part * HY_WIDTH, (part + 1) * HY_WIDTH)
        acc = b_ref[:, cols] + ext[HALO - 1:HALO - 1 + ts] * w_ref[0:1, cols]
        acc = acc + ext[HALO:HALO + ts] * w_ref[1:2, cols]
        acc = acc + ext[HALO + 1:HALO + 1 + ts] * w_ref[2:3, cols]
        out_refs[part][...] = acc


def _short_conv(proj, conv_w, conv_b):
    B, L, _ = proj.shape
    ts = _tile(L, 512)
    first = (ATT_WIDTH + 2 * KV_WIDTH + POOL_WIDTH) // HY_WIDTH
    assert first * HY_WIDTH == ATT_WIDTH + 2 * KV_WIDTH + POOL_WIDTH
    specs = []
    for part in range(HY_ORDER + 1):
        specs += _halo_specs(ts, L, HY_WIDTH, first + part)
    n_col = (HY_ORDER + 1) * HY_WIDTH
    out = pl.BlockSpec((None, ts, HY_WIDTH), lambda b, i: (b, i, 0))
    return pl.pallas_call(
        _short_conv_kernel,
        out_shape=tuple(jax.ShapeDtypeStruct((B, L, HY_WIDTH), F32) for _ in range(HY_ORDER + 1)),
        grid=(B, L // ts),
        in_specs=specs + [pl.BlockSpec((HY_SHORT, n_col), lambda b, i: (0, 0)),
                          pl.BlockSpec((1, n_col), lambda b, i: (0, 0))],
        out_specs=(out,) * (HY_ORDER + 1),
        compiler_params=_cp(("parallel", "arbitrary")),
        name="hyena_short_conv",
    )(*([proj] * 9), conv_w, conv_b.reshape(1, n_col))


def _filter_tables(L):
    m = np.arange(L, dtype=np.float32)
    t = (m / np.float32(max(L - 1, 1))).astype(np.float32)
    w = (np.float32(2.0 * math.pi) * m / np.float32(L)).astype(np.float32)
    f = np.linspace(1e-4, HY_BANDS - 1, HY_BANDS, dtype=np.float32)
    ang = (w[:, None] * f).astype(np.float64)
    z = np.concatenate([t[:, None].astype(np.float64), np.cos(ang), -np.sin(ang)], axis=-1)
    z = np.pad(z, ((0, 0), (0, LANES - HY_EMB)))
    max_decay = math.log(HY_TARGET) / HY_SHORT_DECAY_PCT
    min_decay = math.log(HY_TARGET) / HY_LONG_DECAY_PCT
    deltas = np.linspace(min_decay, max_decay, HY_WIDTH, dtype=np.float32)
    decay = np.exp(-t[:, None].astype(np.float64) * np.abs(deltas)[None].astype(np.float64))
    return jnp.asarray(z, F32), jnp.asarray(decay, F32)


def _filter_kernel(z_ref, decay_ref, w1_ref, b1_ref, w2_ref, b2_ref, w3_ref, b3_ref, fr_ref, o_ref):
    tl = z_ref.shape[0]
    fr = fr_ref[...]
    a = jnp.sin(fr * (_dot3(z_ref[...], w1_ref[...]) + b1_ref[...]))
    a = jnp.sin(fr * (_dot3(a, w2_ref[...]) + b2_ref[...]))
    h = _dot3(a, w3_ref[...]) + b3_ref[...]
    h = h * jnp.tile(decay_ref[...], (1, 2 * HY_ORDER))
    row = pl.program_id(0) * tl + lax.broadcasted_iota(jnp.int32, h.shape, 0)
    col = lax.broadcasted_iota(jnp.int32, h.shape, 1)
    backward = (col // HY_WIDTH) % 2 == 1
    o_ref[...] = jnp.where(backward & (row == 0), 0.0, h)


def _hyena_filter_taps(L, w1, b1, w2, b2, w3, b3, freq):
    z, decay = _filter_tables(L)
    hid = w1.shape[1]
    n_out = w3.shape[1]
    tl = _tile(L, 512)
    w1p = jnp.pad(w1, ((0, LANES - HY_EMB), (0, 0)))
    full = lambda a: pl.BlockSpec(a.shape, lambda i: (0,) * a.ndim)
    args = [w1p, b1.reshape(1, hid), w2, b2.reshape(1, hid), w3, b3.reshape(1, n_out),
            freq.reshape(1, hid)]
    return pl.pallas_call(
        _filter_kernel,
        out_shape=jax.ShapeDtypeStruct((L, n_out), F32),
        grid=(L // tl,),
        in_specs=[pl.BlockSpec((tl, LANES), lambda i: (i, 0)),
                  pl.BlockSpec((tl, HY_WIDTH), lambda i: (i, 0))] + [full(a) for a in args],
        out_specs=pl.BlockSpec((tl, n_out), lambda i: (i, 0)),
        compiler_params=_cp(("arbitrary",)),
        name="hyena_filter_taps",
    )(z, decay, *args)


def _fft_matrices(n1, a_in):
    n = n1 * LANES
    k1 = np.arange(n1)
    a = np.arange(a_in)
    th1 = 2.0 * np.pi * np.outer(k1, a) / n1
    c1, s1 = np.cos(th1), np.sin(th1)
    m1_complex = np.block([[c1, s1], [-s1, c1]])
    m1_real = np.concatenate([c1, -s1], axis=0)
    m3 = np.block([[c1.T, -s1.T], [s1.T, c1.T]])
    b = np.arange(LANES)
    k2 = np.arange(LANES)
    k = k1[:, None, None] + n1 * k2[None, :, None]
    th2 = 2.0 * np.pi * (k * b[None, None, :] % n) / n
    c2, s2 = np.cos(th2), np.sin(th2)
    fwd = np.concatenate([np.concatenate([c2, s2], axis=2),
                          np.concatenate([-s2, c2], axis=2)], axis=1)
    c2t, s2t = np.swapaxes(c2, 1, 2), np.swapaxes(s2, 1, 2)
    inv = np.concatenate([np.concatenate([c2t, -s2t], axis=2),
                          np.concatenate([s2t, c2t], axis=2)], axis=1)
    as_bf16 = lambda m: jnp.asarray(m, F32).astype(BF16)
    return as_bf16(m1_complex), as_bf16(m1_real), as_bf16(m3), as_bf16(fwd), as_bf16(inv)


def _fft_stage1_kernel(x_ref, m_ref, o_ref, *, complex_in):
    n1 = o_ref.shape[1]
    for s in range(o_ref.shape[2]):
        if complex_in:
            x = jnp.concatenate([x_ref[0, :, s, :], x_ref[1, :, s, :]], axis=0)
        else:
            x = x_ref[:, s, :]
        y = _dot(m_ref[...], x.astype(BF16))
        o_ref[0, :, s, :] = y[:n1]
        o_ref[1, :, s, :] = y[n1:]


def _fft_stage1(x, m1, n1, complex_in):
    P, A, Wt = x.shape[0], x.shape[-3], x.shape[-1]
    tw = _tile(Wt, 512, LANES)
    if complex_in:
        xspec = pl.BlockSpec((None, 2, A, FFT_ROWS, tw), lambda p, i, j: (p, 0, 0, i, j))
    else:
        xspec = pl.BlockSpec((None, A, FFT_ROWS, tw), lambda p, i, j: (p, 0, i, j))
    return pl.pallas_call(
        functools.partial(_fft_stage1_kernel, complex_in=complex_in),
        out_shape=jax.ShapeDtypeStruct((P, 2, n1, LANES, Wt), F32),
        grid=(P, LANES // FFT_ROWS, Wt // tw),
        in_specs=[xspec, pl.BlockSpec(m1.shape, lambda p, i, j: (0, 0))],
        out_specs=pl.BlockSpec((None, 2, n1, FFT_ROWS, tw), lambda p, i, j: (p, 0, 0, i, j)),
        compiler_params=_cp(("parallel", "arbitrary", "arbitrary")),
        name="fft_stage1",
    )(x, m1)


def _filter_spectrum_kernel(y_ref, g_ref, o0_ref, o1_ref, *, scale):
    kt = y_ref.shape[1]
    for t in range(kt):
        y = jnp.concatenate([y_ref[0, t], y_ref[1, t]], axis=0).astype(BF16)
        f = _dot(g_ref[t], y)
        fr, fi = f[:LANES], f[LANES:]
        for o, o_ref in enumerate((o0_ref, o1_ref)):
            fw = slice((2 * o) * HY_WIDTH, (2 * o + 1) * HY_WIDTH)
            bw = slice((2 * o + 1) * HY_WIDTH, (2 * o + 2) * HY_WIDTH)
            o_ref[0, t] = (fr[:, fw] + fr[:, bw]) * scale
            o_ref[1, t] = (fi[:, fw] - fi[:, bw]) * scale


def _filter_spectrum(taps, mats, n1):
    L, n_col = taps.shape
    a_in = L // LANES
    _, m1_real, _, g_fwd, _ = mats
    y = _fft_stage1(taps.reshape(1, a_in, LANES, n_col), m1_real, n1, complex_in=False)
    kt = _tile(n1, 4, 1)
    out = jax.ShapeDtypeStruct((2, n1, LANES, HY_WIDTH), F32)
    ospec = pl.BlockSpec((2, kt, LANES, HY_WIDTH), lambda i: (0, i, 0, 0))
    return pl.pallas_call(
        functools.partial(_filter_spectrum_kernel, scale=1.0 / (n1 * LANES)),
        out_shape=(out, out),
        grid=(n1 // kt,),
        in_specs=[pl.BlockSpec((None, 2, kt, LANES, n_col), lambda i: (0, 0, i, 0, 0)),
                  pl.BlockSpec((kt, 2 * LANES, 2 * LANES), lambda i: (i, 0, 0))],
        out_specs=(ospec, ospec),
        compiler_params=_cp(("arbitrary",)),
        name="hyena_filter_spectrum",
    )(y, g_fwd)


def _fft_mid_kernel(y_ref, kf_ref, g_ref, gi_ref, o_ref):
    kt = y_ref.shape[1]
    for t in range(kt):
        y = jnp.concatenate([y_ref[0, t], y_ref[1, t]], axis=0).astype(BF16)
        f = _dot(g_ref[t], y)
        fr, fi = f[:LANES], f[LANES:]
        kr, ki = kf_ref[0, t], kf_ref[1, t]
        p = jnp.concatenate([fr * kr - fi * ki, fr * ki + fi * kr], axis=0).astype(BF16)
        u = _dot(gi_ref[t], p)
        o_ref[0, t] = u[:LANES]
        o_ref[1, t] = u[LANES:]


def _fft_mid(y, kf, g_fwd, g_inv):
    P, _, n1, _, W = y.shape
    kt = _tile(n1, 8, 1)
    yspec = pl.BlockSpec((None, 2, kt, LANES, W), lambda i, p: (p, 0, i, 0, 0))
    gspec = pl.BlockSpec((kt, 2 * LANES, 2 * LANES), lambda i, p: (i, 0, 0))
    return pl.pallas_call(
        _fft_mid_kernel,
        out_shape=jax.ShapeDtypeStruct(y.shape, F32),
        grid=(n1 // kt, P),
        in_specs=[yspec, pl.BlockSpec((2, kt, LANES, W), lambda i, p: (0, i, 0, 0)), gspec, gspec],
        out_specs=yspec,
        compiler_params=_cp(("arbitrary", "arbitrary")),
        name="fft_mid",
    )(y, kf, g_fwd, g_inv)


def _fft_stage3_kernel(u_ref, m_ref, z_ref, x_ref, bias_ref, o_ref):
    a_out = z_ref.shape[1]
    bias = bias_ref[...]
    for s in range(o_ref.shape[2]):
        u = jnp.concatenate([u_ref[0, :, s, :], u_ref[1, :, s, :]], axis=0).astype(BF16)
        conv = _dot(m_ref[...], u)
        for part in range(2):
            o_ref[part, :, s, :] = x_ref[part, :, s, :] * (
                conv[part * a_out:(part + 1) * a_out] + z_ref[part, :, s, :] * bias)


def _fft_stage3(u, m3, z, gate, bias_row):
    P, _, n1, _, W = u.shape
    A = z.shape[2]
    pair = pl.BlockSpec((None, 2, A, FFT_ROWS, W), lambda p, i: (p, 0, 0, i, 0))
    return pl.pallas_call(
        _fft_stage3_kernel,
        out_shape=jax.ShapeDtypeStruct(z.shape, F32),
        grid=(P, LANES // FFT_ROWS),
        in_specs=[pl.BlockSpec((None, 2, n1, FFT_ROWS, W), lambda p, i: (p, 0, 0, i, 0)),
                  pl.BlockSpec(m3.shape, lambda p, i: (0, 0)),
                  pair, pair,
                  pl.BlockSpec((1, W), lambda p, i: (0, 0))],
        out_specs=pair,
        compiler_params=_cp(("parallel", "arbitrary")),
        name="fft_stage3",
    )(u, m3, z, gate, bias_row)


def _hyena_mixer(v, gates, kfs, d_bias, mats, n1):
    B, L, W = v.shape
    pair_shape = (B // 2, 2, L // LANES, LANES, W)
    m1_complex, _, m3, g_fwd, g_inv = mats
    z = v.reshape(pair_shape)
    for o in range(HY_ORDER):
        y = _fft_stage1(z, m1_complex, n1, complex_in=True)
        u = _fft_mid(y, kfs[o], g_fwd, g_inv)
        z = _fft_stage3(u, m3, z, gates[o].reshape(pair_shape), d_bias[o].reshape(1, W))
    return z.reshape(B, L, W)


def kernel(x, c, ctx, c_ctx, w_mod, b_mod, norm1_g, w_in, attn_sink, pool_w, pool_scale,
           hy_conv_w, hy_conv_b, hy_f_w1, hy_f_b1, hy_f_w2, hy_f_b2, hy_f_w3, hy_f_b3,
           hy_f_freq, hy_bias, g_attn, g_pool, g_hyena, w_out, norm2_g,
           ff_w1, ff_w3, ff_w2, router_w, moe_w1, moe_w3, moe_w2, final_g):
    B, S, D = x.shape
    C = ctx.shape[1]
    depth = w_mod.shape[0]
    assert B % 2 == 0 and S % BLOCK == 0
    fft_mats = {}

    c_rows = jnp.concatenate([c, c_ctx[None], jnp.zeros((SUBLANES - B - 1, D), F32)], axis=0)
    xc = ctx

    def hyena_branch(l, proj, seq_len):
        rows = -(-seq_len // FFT_MIN_ROWS) * FFT_MIN_ROWS
        n1 = 2 * rows // LANES
        if rows not in fft_mats:
            fft_mats[rows] = _fft_matrices(n1, rows // LANES)
        mats = fft_mats[rows]
        taps = _hyena_filter_taps(seq_len, hy_f_w1[l], hy_f_b1[l], hy_f_w2[l], hy_f_b2[l],
                                  hy_f_w3[l], hy_f_b3[l], hy_f_freq[l])
        v, x1, x2 = _short_conv(proj, hy_conv_w[l], hy_conv_b[l])
        if seq_len < rows:
            pad = lambda t: jnp.pad(t, ((0, 0), (0, rows - seq_len), (0, 0)))
            taps = jnp.pad(taps, ((0, rows - seq_len), (0, 0)))
            v, x1, x2 = pad(v), pad(x1), pad(x2)
        kfs = _filter_spectrum(taps, mats, n1)
        z = _hyena_mixer(v, (x1, x2), kfs, hy_bias[l], mats, n1)
        return _rmsnorm(z[:, :seq_len], g_hyena[l], BF16)

    def mix(l, proj, y_att, res, gate, w_out_l, seq_len):
        y_pool = _pool_mixer(proj, pool_w[l], pool_scale[l], g_pool[l])
        y_hy = hyena_branch(l, proj, seq_len)
        return _matmul_gated_residual([y_att, y_pool, y_hy], w_out_l, res, gate)

    for l in range(depth):
        last = l == depth - 1
        mod = _adaln(c_rows, w_mod[l], b_mod[l])
        sh1, sc1, g1, sh2, sc2, g2 = [mod[:B, j * D:(j + 1) * D] for j in range(6)]
        csh1, csc1, cg1, csh2, csc2, cg2 = [mod[B:B + 1, j * D:(j + 1) * D] for j in range(6)]
        w_in_l = w_in[l].astype(BF16)
        w_out_l = w_out[l].astype(BF16)

        hc = _norm_mod(xc, norm1_g[l], csh1, csc1)
        if last:
            kv_w = w_in_l[:, ATT_WIDTH:ATT_WIDTH + 2 * KV_WIDTH]
            kv_c = _matmul(hc, kv_w, out_dtype=BF16)
            k_c, v_c = kv_c[..., :KV_WIDTH], kv_c[..., KV_WIDTH:]
        else:
            proj_c = _matmul(hc, w_in_l)
            q_c, k_c, v_c = _qkv_prepare(proj_c, rope=False)
            y_att_c = _attention(q_c, None, None, k_c, v_c, attn_sink[l], g_attn[l])
            xc_new = mix(l, proj_c, y_att_c, xc, cg1, w_out_l, C)
            hc2 = _norm_mod(xc_new, norm2_g[l], csh2, csc2)
            i = l // 2
            if l % 2 == 0:
                hid_c = _swiglu_hidden(hc2, ff_w1[i].astype(BF16), ff_w3[i].astype(BF16))
                xc_new = _matmul_gated_residual([hid_c], ff_w2[i].astype(BF16), xc_new, cg2)
            else:
                raise NotImplementedError("context tokens through an expert layer")

        h = _norm_mod(x, norm1_g[l], sh1, sc1)
        proj = _matmul(h, w_in_l)
        q, k, v = _qkv_prepare(proj, rope=True)
        y_att = _attention(q, k, v, k_c, v_c, attn_sink[l], g_attn[l])
        x = mix(l, proj, y_att, x, g1, w_out_l, S)

        i = l // 2
        if l % 2 == 0:
            h2 = _norm_mod(x, norm2_g[l], sh2, sc2)
            hid = _swiglu_hidden(h2, ff_w1[i].astype(BF16), ff_w3[i].astype(BF16))
            x = _matmul_gated_residual([hid], ff_w2[i].astype(BF16), x, g2)
            if last:
                x = _rmsnorm(x, final_g, F32)
        else:
            h2, gi, gw, counts = _norm_mod_router(x, norm2_g[l], sh2, sc2, router_w[i])
            assert last
            x = _moe_layer(x, h2, gi, gw, counts, g2, final_g, moe_w1[i], moe_w3[i], moe_w2[i])
        if not last:
            xc = xc_new
    return x
```

```python
import functools
import math

import numpy as np
import jax
import jax.numpy as jnp
from jax import lax
from jax.experimental import pallas as pl
from jax.experimental.pallas import tpu as pltpu

F32 = jnp.float32
BF16 = jnp.bfloat16

EPS = 1e-6
NEG = -1e30
GRID_W = 64
ATT_HEADS = 8
ATT_KV_HEADS = 2
ATT_GROUP = ATT_HEADS // ATT_KV_HEADS
HEAD_DIM = 128
ATT_WIDTH = ATT_HEADS * HEAD_DIM
KV_WIDTH = ATT_KV_HEADS * HEAD_DIM
WINDOW = 128
BLOCK = 128
ROPE_BASE = 10000.0
POOL_WINDOWS = (2, 4, 8, 16)
POOL_GROUP = 128
POOL_WIDTH = POOL_GROUP * len(POOL_WINDOWS)
HY_WIDTH = 512
HY_ORDER = 2
HY_SHORT = 3
HY_EMB = 33
HY_BANDS = (HY_EMB - 1) // 2
HY_SHORT_DECAY_PCT = 0.3
HY_LONG_DECAY_PCT = 1.5
HY_TARGET = 1e-2
N_EXPERTS = 8
TOP_K = 2

LANES = 128
SUBLANES = 8
HALO = SUBLANES
FFT_MIN_ROWS = 2048
FFT_ROWS = SUBLANES
MOE_OUT_K_CHUNKS = 4
VMEM_LIMIT = 56 * 1024 * 1024


def _cp(sem):
    return pltpu.CompilerParams(dimension_semantics=sem, vmem_limit_bytes=VMEM_LIMIT)


def _tile(n, pref, mult=SUBLANES):
    if n <= pref:
        return n
    t = (pref // mult) * mult
    while t >= mult:
        if n % t == 0:
            return t
        t -= mult
    return n


def _split_bf16(x):
    hi = x.astype(BF16)
    lo = (x - hi.astype(F32)).astype(BF16)
    return hi, lo


def _dot(a, b):
    return jnp.dot(a, b, preferred_element_type=F32)


def _pack_bf16_pairs(x):
    n = x.shape[-1] // 2
    bits = lax.bitcast_convert_type(x.astype(BF16).astype(F32), jnp.uint32)
    return (bits[:, :n] >> 16) | (bits[:, n:] & jnp.uint32(0xFFFF0000))


def _unpack_bf16_pairs(u):
    lo = lax.bitcast_convert_type(u << 16, F32).astype(BF16)
    hi = lax.bitcast_convert_type(u & jnp.uint32(0xFFFF0000), F32).astype(BF16)
    return lo, hi


def _dot3(a, b):
    ah, al = _split_bf16(a)
    bh, bl = _split_bf16(b)
    return _dot(ah, bh) + (_dot(ah, bl) + _dot(al, bh))


def _adaln_kernel(c_ref, w_ref, b_ref, o_ref):
    c = c_ref[...]
    a = c * (1.0 / (1.0 + jnp.exp(-c)))
    o_ref[...] = _dot3(a, w_ref[...]) + b_ref[...]


def _adaln(c_rows, w, b):
    R, D = c_rows.shape
    N = w.shape[1]
    tn = _tile(N, 1536, LANES)
    return pl.pallas_call(
        _adaln_kernel,
        out_shape=jax.ShapeDtypeStruct((R, N), F32),
        grid=(N // tn,),
        in_specs=[pl.BlockSpec((R, D), lambda j: (0, 0)),
                  pl.BlockSpec((D, tn), lambda j: (0, j)),
                  pl.BlockSpec((1, tn), lambda j: (0, j))],
        out_specs=pl.BlockSpec((R, tn), lambda j: (0, j)),
        compiler_params=_cp(("arbitrary",)),
        name="adaln",
    )(c_rows, w, b.reshape(1, N))


def _norm_mod_kernel(x_ref, g_ref, sh_ref, sc_ref, o_ref):
    x = x_ref[...]
    y = x * lax.rsqrt(jnp.mean(x * x, axis=-1, keepdims=True) + EPS) * g_ref[...]
    o_ref[...] = (y * (1.0 + sc_ref[...]) + sh_ref[...]).astype(o_ref.dtype)


def _bcast_map(arr):
    if arr.shape[0] == 1:
        return lambda b, i: (0, 0, 0)
    return lambda b, i: (b, 0, 0)


def _norm_mod(x, g, shift, scale):
    B, L, D = x.shape
    ts = _tile(L, 512)
    shift = shift[:, None, :]
    scale = scale[:, None, :]
    return pl.pallas_call(
        _norm_mod_kernel,
        out_shape=jax.ShapeDtypeStruct((B, L, D), BF16),
        grid=(B, L // ts),
        in_specs=[pl.BlockSpec((None, ts, D), lambda b, i: (b, i, 0)),
                  pl.BlockSpec((1, D), lambda b, i: (0, 0)),
                  pl.BlockSpec((None, 1, D), _bcast_map(shift)),
                  pl.BlockSpec((None, 1, D), _bcast_map(scale))],
        out_specs=pl.BlockSpec((None, ts, D), lambda b, i: (b, i, 0)),
        compiler_params=_cp(("parallel", "parallel")),
        name="norm_mod",
    )(x, g.reshape(1, D), shift, scale)


def _norm_mod_router_kernel(x_ref, g_ref, sh_ref, sc_ref, rw_ref, tri_ref,
                            o_ref, gi_ref, gw_ref, cnt_ref, run_ref):
    @pl.when((pl.program_id(0) == 0) & (pl.program_id(1) == 0))
    def _():
        run_ref[...] = jnp.zeros(run_ref.shape, run_ref.dtype)

    x = x_ref[...]
    y = x * lax.rsqrt(jnp.mean(x * x, axis=-1, keepdims=True) + EPS) * g_ref[...]
    h = y * (1.0 + sc_ref[...]) + sh_ref[...]
    o_ref[...] = _pack_bf16_pairs(h)
    logits = _dot3(h, rw_ref[...])
    lane = lax.broadcasted_iota(jnp.int32, logits.shape, 1)
    logits = jnp.where(lane < N_EXPERTS, logits, NEG)
    m1 = jnp.max(logits, axis=-1, keepdims=True)
    i1 = jnp.min(jnp.where(logits == m1, lane, LANES), axis=-1, keepdims=True)
    rest = jnp.where(lane == i1, NEG, logits)
    m2 = jnp.max(rest, axis=-1, keepdims=True)
    i2 = jnp.min(jnp.where(rest == m2, lane, LANES), axis=-1, keepdims=True)
    e2 = jnp.exp(m2 - m1)
    w1 = 1.0 / (1.0 + e2)
    w2 = e2 * w1
    pick1 = lane == i1
    pick2 = lane == i2
    both = jnp.where(pick1 | pick2, 1.0, 0.0)
    before = _dot(tri_ref[...], both.astype(BF16)) + run_ref[...]
    r1 = jnp.sum(jnp.where(pick1, before, 0.0), axis=-1, keepdims=True).astype(jnp.int32)
    r2 = jnp.sum(jnp.where(pick2, before, 0.0), axis=-1, keepdims=True).astype(jnp.int32)
    run_ref[...] = run_ref[...] + jnp.sum(both, axis=0, keepdims=True)
    cnt_ref[...] = run_ref[...]
    gi_ref[...] = jnp.where(lane == 0, i1, jnp.where(lane == 1, i2,
                            jnp.where(lane == 2, r1, jnp.where(lane == 3, r2, 0))))
    gw_ref[...] = jnp.where(lane == 0, w1, jnp.where(lane == 1, w2, 0.0))


def _norm_mod_router(x, g, shift, scale, router_w):
    B, L, D = x.shape
    ts = _tile(L, 512)
    shift = shift[:, None, :]
    scale = scale[:, None, :]
    rw = jnp.pad(router_w, ((0, 0), (0, LANES - router_w.shape[1])))
    tri = jnp.asarray(np.tril(np.ones((ts, ts), np.float32), -1), BF16)
    row = pl.BlockSpec((None, ts, D), lambda b, i: (b, i, 0))
    small = pl.BlockSpec((None, ts, LANES), lambda b, i: (b, i, 0))
    return pl.pallas_call(
        _norm_mod_router_kernel,
        out_shape=(jax.ShapeDtypeStruct((B, L, D // 2), jnp.uint32),
                   jax.ShapeDtypeStruct((B, L, LANES), jnp.int32),
                   jax.ShapeDtypeStruct((B, L, LANES), F32),
                   jax.ShapeDtypeStruct((1, LANES), F32)),
        grid=(B, L // ts),
        in_specs=[row,
                  pl.BlockSpec((1, D), lambda b, i: (0, 0)),
                  pl.BlockSpec((None, 1, D), _bcast_map(shift)),
                  pl.BlockSpec((None, 1, D), _bcast_map(scale)),
                  pl.BlockSpec((D, LANES), lambda b, i: (0, 0)),
                  pl.BlockSpec((ts, ts), lambda b, i: (0, 0))],
        out_specs=(pl.BlockSpec((None, ts, D // 2), lambda b, i: (b, i, 0)), small, small,
                   pl.BlockSpec((1, LANES), lambda b, i: (0, 0))),
        scratch_shapes=[pltpu.VMEM((1, LANES), F32)],
        compiler_params=_cp(("arbitrary", "arbitrary")),
        name="norm_mod_router",
    )(x, g.reshape(1, D), shift, scale, rw, tri)


def _rmsnorm_kernel(x_ref, g_ref, o_ref):
    x = x_ref[...]
    y = x * lax.rsqrt(jnp.mean(x * x, axis=-1, keepdims=True) + EPS) * g_ref[...]
    o_ref[...] = y.astype(o_ref.dtype)


def _rmsnorm(x, g, out_dtype):
    B, L, D = x.shape
    ts = _tile(L, 512)
    return pl.pallas_call(
        _rmsnorm_kernel,
        out_shape=jax.ShapeDtypeStruct((B, L, D), out_dtype),
        grid=(B, L // ts),
        in_specs=[pl.BlockSpec((None, ts, D), lambda b, i: (b, i, 0)),
                  pl.BlockSpec((1, D), lambda b, i: (0, 0))],
        out_specs=pl.BlockSpec((None, ts, D), lambda b, i: (b, i, 0)),
        compiler_params=_cp(("parallel", "parallel")),
        name="rmsnorm",
    )(x, g.reshape(1, D))


def _mm_kernel(a_ref, w_ref, o_ref):
    o_ref[...] = _dot(a_ref[...], w_ref[...]).astype(o_ref.dtype)


def _matmul(a, w, out_dtype=F32, tm=1024, tn=512):
    B, L, K = a.shape
    N = w.shape[1]
    tm = _tile(L, tm)
    tn = _tile(N, tn, LANES)
    return pl.pallas_call(
        _mm_kernel,
        out_shape=jax.ShapeDtypeStruct((B, L, N), out_dtype),
        grid=(B, L // tm, N // tn),
        in_specs=[pl.BlockSpec((None, tm, K), lambda b, i, j: (b, i, 0)),
                  pl.BlockSpec((K, tn), lambda b, i, j: (0, j))],
        out_specs=pl.BlockSpec((None, tm, tn), lambda b, i, j: (b, i, j)),
        compiler_params=_cp(("parallel", "parallel", "arbitrary")),
        name="matmul",
    )(a, w)


def _mm_res_kernel(*refs, n_a):
    a_refs = refs[:n_a]
    w_refs = refs[n_a:2 * n_a]
    res_ref, gate_ref, o_ref = refs[2 * n_a:]
    acc = _dot(a_refs[0][...].astype(BF16), w_refs[0][...])
    for a_ref, w_ref in zip(a_refs[1:], w_refs[1:]):
        acc = acc + _dot(a_ref[...].astype(BF16), w_ref[...])
    o_ref[...] = res_ref[...] + gate_ref[...] * acc


def _matmul_gated_residual(a_list, w, res, gate, tm=1024, tn=512):
    B, L, N = res.shape
    tm = _tile(L, tm)
    tn = _tile(N, tn, LANES)
    gate = gate[:, None, :]
    widths = [a.shape[-1] for a in a_list]
    unit = math.gcd(*widths) if len(widths) > 1 else widths[0]
    in_specs = [pl.BlockSpec((None, tm, k), lambda b, i, j: (b, i, 0)) for k in widths]
    off = 0
    for k in widths:
        assert off % k == 0 and k % unit == 0
        in_specs.append(pl.BlockSpec((k, tn), functools.partial(
            lambda b, i, j, blk: (blk, j), blk=off // k)))
        off += k
    gmap = (lambda b, i, j: (0, 0, j)) if gate.shape[0] == 1 else (lambda b, i, j: (b, 0, j))
    in_specs += [pl.BlockSpec((None, tm, tn), lambda b, i, j: (b, i, j)),
                 pl.BlockSpec((None, 1, tn), gmap)]
    return pl.pallas_call(
        functools.partial(_mm_res_kernel, n_a=len(a_list)),
        out_shape=jax.ShapeDtypeStruct((B, L, N), F32),
        grid=(B, L // tm, N // tn),
        in_specs=in_specs,
        out_specs=pl.BlockSpec((None, tm, tn), lambda b, i, j: (b, i, j)),
        compiler_params=_cp(("parallel", "parallel", "arbitrary")),
        name="matmul_gated_residual",
    )(*a_list, *([w] * len(a_list)), res, gate)


def _silu(x):
    return x * (1.0 / (1.0 + jnp.exp(-x)))


def _swiglu_kernel(a_ref, w1_ref, w3_ref, o_ref):
    a = a_ref[...]
    o_ref[...] = (_silu(_dot(a, w1_ref[...])) * _dot(a, w3_ref[...])).astype(o_ref.dtype)


def _swiglu_hidden(a, w1, w3, tm=1024, tf=512):
    B, L, D = a.shape
    F = w1.shape[1]
    tm = _tile(L, tm)
    tf = _tile(F, tf, LANES)
    wspec = pl.BlockSpec((D, tf), lambda b, i, j: (0, j))
    return pl.pallas_call(
        _swiglu_kernel,
        out_shape=jax.ShapeDtypeStruct((B, L, F), BF16),
        grid=(B, L // tm, F // tf),
        in_specs=[pl.BlockSpec((None, tm, D), lambda b, i, j: (b, i, 0)), wspec, wspec],
        out_specs=pl.BlockSpec((None, tm, tf), lambda b, i, j: (b, i, j)),
        compiler_params=_cp(("parallel", "parallel", "arbitrary")),
        name="swiglu_hidden",
    )(a, w1, w3)


def _moe_hidden_kernel(te_ref, nt_ref, a_ref, w1_ref, w3_ref, o_ref, abf_ref):
    used = pl.program_id(0) < nt_ref[0]
    half = a_ref.shape[1]

    @pl.when(used & (pl.program_id(1) == 0))
    def _():
        lo, hi = _unpack_bf16_pairs(a_ref[...])
        abf_ref[:, :half] = lo
        abf_ref[:, half:] = hi

    @pl.when(used)
    def _():
        a = abf_ref[...]
        w1 = w1_ref[...].astype(BF16)
        w3 = w3_ref[...].astype(BF16)
        o_ref[...] = (_silu(_dot(a, w1)) * _dot(a, w3)).astype(o_ref.dtype)

    @pl.when(jnp.logical_not(used))
    def _():
        o_ref[...] = jnp.zeros(o_ref.shape, o_ref.dtype)


def _moe_out_kernel(te_ref, nt_ref, a_ref, w2_ref, o_ref):
    used = pl.program_id(0) < nt_ref[0]

    @pl.when(used)
    def _():
        kc = a_ref.shape[1] // MOE_OUT_K_CHUNKS
        acc = _dot(a_ref[:, :kc], w2_ref[:kc, :].astype(BF16))
        for c in range(1, MOE_OUT_K_CHUNKS):
            acc = acc + _dot(a_ref[:, c * kc:(c + 1) * kc], w2_ref[c * kc:(c + 1) * kc, :].astype(BF16))
        o_ref[...] = acc.astype(o_ref.dtype)

    @pl.when(jnp.logical_not(used))
    def _():
        o_ref[...] = jnp.zeros(o_ref.shape, o_ref.dtype)


def _moe_experts(a_sorted, tile_expert, n_tiles_used, w1, w3, w2, tm, tf=512, tn=512):
    R = a_sorted.shape[0]
    E, D, F = w1.shape
    tf = _tile(F, tf, LANES)
    tn = _tile(D, tn, LANES)
    nt = R // tm
    w13 = pl.BlockSpec((None, D, tf), lambda i, j, te, n: (te[i], 0, j))
    hidden = pl.pallas_call(
        _moe_hidden_kernel,
        out_shape=jax.ShapeDtypeStruct((R, F), BF16),
        grid_spec=pltpu.PrefetchScalarGridSpec(
            num_scalar_prefetch=2,
            grid=(nt, F // tf),
            in_specs=[pl.BlockSpec((tm, D // 2), lambda i, j, te, n: (i, 0)), w13, w13],
            out_specs=pl.BlockSpec((tm, tf), lambda i, j, te, n: (i, j)),
            scratch_shapes=[pltpu.VMEM((tm, D), BF16)]),
        compiler_params=_cp(("arbitrary", "arbitrary")),
        name="moe_hidden",
    )(tile_expert, n_tiles_used, a_sorted, w1, w3)
    return pl.pallas_call(
        _moe_out_kernel,
        out_shape=jax.ShapeDtypeStruct((R, D), F32),
        grid_spec=pltpu.PrefetchScalarGridSpec(
            num_scalar_prefetch=2,
            grid=(nt, D // tn),
            in_specs=[pl.BlockSpec((tm, F), lambda i, j, te, n: (i, 0)),
                      pl.BlockSpec((None, F, tn), lambda i, j, te, n: (te[i], 0, j))],
            out_specs=pl.BlockSpec((tm, tn), lambda i, j, te, n: (i, j))),
        compiler_params=_cp(("arbitrary", "arbitrary")),
        name="moe_out",
    )(tile_expert, n_tiles_used, hidden, w2)


def _moe_combine_kernel(x_ref, ya_ref, yb_ref, gw_ref, gate_ref, g_ref, o_ref):
    gw = gw_ref[...]
    moe = gw[:, 0:1] * ya_ref[...] + gw[:, 1:2] * yb_ref[...]
    x = x_ref[...] + gate_ref[...] * moe
    y = x * lax.rsqrt(jnp.mean(x * x, axis=-1, keepdims=True) + EPS) * g_ref[...]
    o_ref[...] = y


def _moe_combine_norm(x, ya, yb, gw, gate, g):
    B, L, D = x.shape
    ts = _tile(L, 512)
    row = pl.BlockSpec((None, ts, D), lambda b, i: (b, i, 0))
    return pl.pallas_call(
        _moe_combine_kernel,
        out_shape=jax.ShapeDtypeStruct((B, L, D), F32),
        grid=(B, L // ts),
        in_specs=[row, row, row,
                  pl.BlockSpec((None, ts, LANES), lambda b, i: (b, i, 0)),
                  pl.BlockSpec((None, 1, D), lambda b, i: (b, 0, 0)),
                  pl.BlockSpec((1, D), lambda b, i: (0, 0))],
        out_specs=row,
        compiler_params=_cp(("parallel", "parallel")),
        name="moe_combine_norm",
    )(x, ya, yb, gw, gate[:, None, :], g.reshape(1, D))


def _moe_layer(x, h, gi, gw, counts, gate, final_g, w1, w3, w2):
    B, L, D = x.shape
    E = w1.shape[0]
    n_tok = B * L
    n_pair = n_tok * TOP_K
    tm = _tile(n_pair, 1024)
    nt = n_pair // tm + E
    e_pair = gi[..., :TOP_K].reshape(n_pair)
    rank = gi[..., TOP_K:2 * TOP_K].reshape(n_pair)
    counts = counts[0, :E].astype(jnp.int32)
    tiles_per = (counts + tm - 1) // tm
    tile_end = jnp.cumsum(tiles_per)
    tile_start = tile_end - tiles_per
    row_start = jnp.zeros((n_pair,), jnp.int32)
    for e in range(E):
        row_start = jnp.where(e_pair == e, tile_start[e] * tm, row_start)
    pos = row_start + rank
    tile_ids = jnp.arange(nt, dtype=jnp.int32)
    tile_expert = jnp.minimum(
        jnp.sum((tile_ids[:, None] >= tile_end[None, :]).astype(jnp.int32), axis=1), E - 1)
    n_used = tile_end[-1:].astype(jnp.int32)
    tok_pair = jnp.arange(n_pair, dtype=jnp.int32) // TOP_K
    src = jnp.zeros((nt * tm,), jnp.int32).at[pos].set(
        tok_pair, unique_indices=True, mode="promise_in_bounds")
    gather_rows = lambda rows, idx: rows.at[idx].get(mode="promise_in_bounds")
    a_sorted = gather_rows(h.reshape(n_tok, h.shape[-1]), src)
    y_sorted = _moe_experts(a_sorted, tile_expert.astype(jnp.int32), n_used, w1, w3, w2, tm)
    pos2 = pos.reshape(n_tok, TOP_K)
    ya = gather_rows(y_sorted, pos2[:, 0]).reshape(B, L, D)
    yb = gather_rows(y_sorted, pos2[:, 1]).reshape(B, L, D)
    return _moe_combine_norm(x, ya, yb, gw, gate, final_g)


def _rope_tables(n_tokens):
    pos = np.arange(n_tokens)
    row = (pos // GRID_W).astype(np.float32)
    col = (pos % GRID_W).astype(np.float32)
    n_freq = HEAD_DIM // 4
    inv = (np.float32(ROPE_BASE) ** (-np.arange(n_freq, dtype=np.float32) / np.float32(n_freq)))
    ang_r = (row[:, None] * inv).astype(np.float64)
    ang_c = (col[:, None] * inv).astype(np.float64)
    cos = np.concatenate([np.cos(ang_r), np.cos(ang_r), np.cos(ang_c), np.cos(ang_c)], axis=1)
    sin = np.concatenate([-np.sin(ang_r), np.sin(ang_r), -np.sin(ang_c), np.sin(ang_c)], axis=1)
    return jnp.asarray(cos, F32), jnp.asarray(sin, F32)


def _swap_halves(x):
    n = x.shape[-1]
    quarter = HEAD_DIM // 4
    lane = lax.broadcasted_iota(jnp.int32, x.shape, 1)
    up = pltpu.roll(x, n - quarter, axis=1)
    down = pltpu.roll(x, quarter, axis=1)
    return jnp.where((lane & quarter) == 0, up, down)


def _qkv_kernel(q_ref, k_ref, v_ref, cos_ref, sin_ref, qo_ref, ko_ref, vo_ref, *, rope):
    q = q_ref[...]
    k = k_ref[...]
    if rope:
        cos = cos_ref[...]
        sin = sin_ref[...]
        q = q * jnp.tile(cos, (1, ATT_HEADS)) + _swap_halves(q) * jnp.tile(sin, (1, ATT_HEADS))
        k = k * jnp.tile(cos, (1, ATT_KV_HEADS)) + _swap_halves(k) * jnp.tile(sin, (1, ATT_KV_HEADS))
    qo_ref[...] = (q * (HEAD_DIM ** -0.5)).astype(BF16)
    ko_ref[...] = k.astype(BF16)
    vo_ref[...] = v_ref[...].astype(BF16)


def _qkv_prepare(proj, rope):
    B, L, _ = proj.shape
    ts = _tile(L, 512)
    cos, sin = _rope_tables(L)
    kvb = ATT_WIDTH // KV_WIDTH
    tab = pl.BlockSpec((ts, HEAD_DIM), lambda b, i: (i, 0))
    return pl.pallas_call(
        functools.partial(_qkv_kernel, rope=rope),
        out_shape=(jax.ShapeDtypeStruct((B, L, ATT_WIDTH), BF16),
                   jax.ShapeDtypeStruct((B, L, KV_WIDTH), BF16),
                   jax.ShapeDtypeStruct((B, L, KV_WIDTH), BF16)),
        grid=(B, L // ts),
        in_specs=[pl.BlockSpec((None, ts, ATT_WIDTH), lambda b, i: (b, i, 0)),
                  pl.BlockSpec((None, ts, KV_WIDTH), lambda b, i: (b, i, kvb)),
                  pl.BlockSpec((None, ts, KV_WIDTH), lambda b, i: (b, i, kvb + 1)),
                  tab, tab],
        out_specs=(pl.BlockSpec((None, ts, ATT_WIDTH), lambda b, i: (b, i, 0)),
                   pl.BlockSpec((None, ts, KV_WIDTH), lambda b, i: (b, i, 0)),
                   pl.BlockSpec((None, ts, KV_WIDTH), lambda b, i: (b, i, 0))),
        compiler_params=_cp(("parallel", "parallel")),
        name="qkv_prepare",
    )(proj, proj, proj, cos, sin)


def _nt_dot(a, b):
    return lax.dot_general(a, b, (((1,), (1,)), ((), ())), preferred_element_type=F32)


def _attn_bias_tables():
    T, G = BLOCK, ATT_GROUP
    qi = np.arange(G * T)[:, None] % T
    ki = np.arange(3 * T)[None, :]
    band = np.abs(ki - T - qi) <= WINDOW
    after_start = ki >= T
    before_end = ki < 2 * T
    masks = [band & after_start, band, band & before_end, band & after_start & before_end]
    return jnp.asarray(np.stack([np.where(m, 0.0, NEG) for m in masks]), F32)


def _attn_kernel(*refs, local, n_blocks):
    if local:
        (q_ref, kp_ref, kc_ref, kn_ref, vp_ref, vc_ref, vn_ref,
         kx_ref, vx_ref, bias_ref, sink_ref, g_ref, o_ref, acc_ref) = refs
    else:
        q_ref, kx_ref, vx_ref, sink_ref, g_ref, o_ref, acc_ref = refs
    T = BLOCK
    G = ATT_GROUP
    n_sub = q_ref.shape[0] // T
    for h in range(ATT_KV_HEADS):
        cols = slice(h * HEAD_DIM, (h + 1) * HEAD_DIM)
        sink = jnp.concatenate(
            [jnp.broadcast_to(sink_ref[:, (h * G + g) * HEAD_DIM:(h * G + g) * HEAD_DIM + 1], (T, 1))
             for g in range(G)], axis=0)
        kx = kx_ref[:, cols]
        vx = vx_ref[:, cols]
        if local:
            k_band = jnp.concatenate([kp_ref[:, cols], kc_ref[:, cols], kn_ref[:, cols]], axis=0)
            v_band = jnp.concatenate([vp_ref[:, cols], vc_ref[:, cols], vn_ref[:, cols]], axis=0)
        for j in range(n_sub):
            rows = slice(j * T, (j + 1) * T)
            qs = jnp.concatenate(
                [q_ref[rows, (h * G + g) * HEAD_DIM:(h * G + g + 1) * HEAD_DIM] for g in range(G)],
                axis=0)
            s_ctx = _nt_dot(qs, kx)
            m = jnp.maximum(jnp.max(s_ctx, axis=-1, keepdims=True), sink)
            if local:
                blk = pl.program_id(1) * n_sub + j
                is_first = blk == 0
                is_last = blk == n_blocks - 1
                table = jnp.where(is_first, jnp.where(is_last, 3, 0), jnp.where(is_last, 2, 1))
                s_loc = _nt_dot(qs, k_band[j * T:(j + 3) * T]) + bias_ref[table]
                m = jnp.maximum(m, jnp.max(s_loc, axis=-1, keepdims=True))
            p_ctx = jnp.exp(s_ctx - m)
            denom = jnp.sum(p_ctx, axis=-1, keepdims=True) + jnp.exp(sink - m)
            o = _dot(p_ctx.astype(BF16), vx)
            if local:
                p_loc = jnp.exp(s_loc - m)
                denom = denom + jnp.sum(p_loc, axis=-1, keepdims=True)
                o = o + _dot(p_loc.astype(BF16), v_band[j * T:(j + 3) * T])
            o = o * (1.0 / denom)
            for g in range(G):
                acc_ref[rows, (h * G + g) * HEAD_DIM:(h * G + g + 1) * HEAD_DIM] = o[g * T:(g + 1) * T]
    y = acc_ref[...]
    y = y * lax.rsqrt(jnp.mean(y * y, axis=-1, keepdims=True) + EPS) * g_ref[...]
    o_ref[...] = y.astype(o_ref.dtype)


def _attention(q, k, v, k_ctx, v_ctx, sink, g_attn):
    B, L, _ = q.shape
    C = k_ctx.shape[1]
    local = k is not None
    T = BLOCK
    nb = L // T
    n_sub = 2 if nb % 2 == 0 else 1
    sink_row = jnp.repeat(sink.astype(F32), HEAD_DIM).reshape(1, ATT_WIDTH)
    qspec = pl.BlockSpec((None, n_sub * T, ATT_WIDTH), lambda b, i: (b, i, 0))
    in_specs = [qspec]
    args = [q]
    cspec = pl.BlockSpec((None, C, KV_WIDTH), lambda b, i: (b, 0, 0))
    row = pl.BlockSpec((1, ATT_WIDTH), lambda b, i: (0, 0))
    if local:
        prev = pl.BlockSpec((None, T, KV_WIDTH), lambda b, i: (b, jnp.maximum(i * n_sub - 1, 0), 0))
        cur = pl.BlockSpec((None, n_sub * T, KV_WIDTH), lambda b, i: (b, i, 0))
        nxt = pl.BlockSpec((None, T, KV_WIDTH),
                           lambda b, i: (b, jnp.minimum((i + 1) * n_sub, nb - 1), 0))
        bias = _attn_bias_tables()
        in_specs += [prev, cur, nxt, prev, cur, nxt, cspec, cspec,
                     pl.BlockSpec(bias.shape, lambda b, i: (0, 0, 0))]
        args += [k, k, k, v, v, v, k_ctx, v_ctx, bias]
    else:
        in_specs += [cspec, cspec]
        args += [k_ctx, v_ctx]
    in_specs += [row, row]
    args += [sink_row, g_attn.reshape(1, ATT_WIDTH)]
    return pl.pallas_call(
        functools.partial(_attn_kernel, local=local, n_blocks=nb),
        out_shape=jax.ShapeDtypeStruct((B, L, ATT_WIDTH), BF16),
        grid=(B, nb // n_sub),
        in_specs=in_specs,
        out_specs=qspec,
        scratch_shapes=[pltpu.VMEM((n_sub * T, ATT_WIDTH), F32)],
        compiler_params=_cp(("parallel", "arbitrary")),
        name="attention_local" if local else "attention_context",
    )(*args)


def _halo_specs(ts, L, width, col_block):
    nb8 = L // HALO
    per = ts // HALO
    prev = pl.BlockSpec((None, HALO, width),
                        lambda b, i: (b, jnp.maximum(i * per - 1, 0), col_block))
    cur = pl.BlockSpec((None, ts, width), lambda b, i: (b, i, col_block))
    nxt = pl.BlockSpec((None, HALO, width),
                       lambda b, i: (b, jnp.minimum((i + 1) * per, nb8 - 1), col_block))
    return [prev, cur, nxt]


def _with_halo(prev_ref, cur_ref, next_ref):
    i = pl.program_id(1)
    last = pl.num_programs(1) - 1
    prev = jnp.where(i > 0, prev_ref[...], 0.0)
    nxt = jnp.where(i < last, next_ref[...], 0.0)
    return jnp.concatenate([prev, cur_ref[...], nxt], axis=0)


def _pool_kernel(prev_ref, cur_ref, next_ref, w_ref, scale_ref, g_ref, o_ref, *, seq_len):
    ts = cur_ref.shape[0]
    ext = _with_halo(prev_ref, cur_ref, next_ref)
    pos = pl.program_id(1) * ts + lax.broadcasted_iota(jnp.int32, (ts, 1), 0)
    outs = []
    for gidx, win in enumerate(POOL_WINDOWS):
        cols = slice(gidx * POOL_GROUP, (gidx + 1) * POOL_GROUP)
        run = ext[:, cols]
        step = 1
        while step < win:
            run = run[:run.shape[0] - step] + run[step:]
            step *= 2
        lo = HALO - win // 2
        total = run[lo:lo + ts]
        cnt = (jnp.minimum(pos + (win - win // 2), seq_len) - jnp.maximum(pos - win // 2, 0))
        mean = total * (1.0 / cnt.astype(F32))
        outs.append(_dot3(mean - cur_ref[:, cols], w_ref[gidx]))
    y = jnp.concatenate(outs, axis=-1) * scale_ref[...]
    y = y * lax.rsqrt(jnp.mean(y * y, axis=-1, keepdims=True) + EPS) * g_ref[...]
    o_ref[...] = y.astype(o_ref.dtype)


def _pool_mixer(proj, pool_w, pool_scale, g_pool):
    B, L, _ = proj.shape
    ts = _tile(L, 512)
    col_block = (ATT_WIDTH + 2 * KV_WIDTH) // POOL_WIDTH
    assert col_block * POOL_WIDTH == ATT_WIDTH + 2 * KV_WIDTH
    row = pl.BlockSpec((1, POOL_WIDTH), lambda b, i: (0, 0))
    return pl.pallas_call(
        functools.partial(_pool_kernel, seq_len=L),
        out_shape=jax.ShapeDtypeStruct((B, L, POOL_WIDTH), BF16),
        grid=(B, L // ts),
        in_specs=_halo_specs(ts, L, POOL_WIDTH, col_block) + [
            pl.BlockSpec(pool_w.shape, lambda b, i: (0, 0, 0)), row, row],
        out_specs=pl.BlockSpec((None, ts, POOL_WIDTH), lambda b, i: (b, i, 0)),
        compiler_params=_cp(("parallel", "arbitrary")),
        name="pool_mixer",
    )(proj, proj, proj, pool_w, pool_scale.reshape(1, POOL_WIDTH), g_pool.reshape(1, POOL_WIDTH))


def _short_conv_kernel(*refs):
    halo_refs = refs[:9]
    w_ref, b_ref = refs[9:11]
    out_refs = refs[11:]
    ts = out_refs[0].shape[0]
    for part in range(HY_ORDER + 1):
        ext = _with_halo(*halo_refs[3 * part:3 * part + 3])
        cols = slice(part * HY_WIDTH, (part + 1) * HY_WIDTH)
        acc = b_ref[:, cols] + ext[HALO - 1:HALO - 1 + ts] * w_ref[0:1, cols]
        acc = acc + ext[HALO:HALO + ts] * w_ref[1:2, cols]
        acc = acc + ext[HALO + 1:HALO + 1 + ts] * w_ref[2:3, cols]
        out_refs[part][...] = acc


def _short_conv(proj, conv_w, conv_b):
    B, L, _ = proj.shape
    ts = _tile(L, 512)
    first = (ATT_WIDTH + 2 * KV_WIDTH + POOL_WIDTH) // HY_WIDTH
    assert first * HY_WIDTH == ATT_WIDTH + 2 * KV_WIDTH + POOL_WIDTH
    specs = []
    for part in range(HY_ORDER + 1):
        specs += _halo_specs(ts, L, HY_WIDTH, first + part)
    n_col = (HY_ORDER + 1) * HY_WIDTH
    out = pl.BlockSpec((None, ts, HY_WIDTH), lambda b, i: (b, i, 0))
    return pl.pallas_call(
        _short_conv_kernel,
        out_shape=tuple(jax.ShapeDtypeStruct((B, L, HY_WIDTH), F32) for _ in range(HY_ORDER + 1)),
        grid=(B, L // ts),
        in_specs=specs + [pl.BlockSpec((HY_SHORT, n_col), lambda b, i: (0, 0)),
                          pl.BlockSpec((1, n_col), lambda b, i: (0, 0))],
        out_specs=(out,) * (HY_ORDER + 1),
        compiler_params=_cp(("parallel", "arbitrary")),
        name="hyena_short_conv",
    )(*([proj] * 9), conv_w, conv_b.reshape(1, n_col))


def _filter_tables(L):
    m = np.arange(L, dtype=np.float32)
    t = (m / np.float32(max(L - 1, 1))).astype(np.float32)
    w = (np.float32(2.0 * math.pi) * m / np.float32(L)).astype(np.float32)
    f = np.linspace(1e-4, HY_BANDS - 1, HY_BANDS, dtype=np.float32)
    ang = (w[:, None] * f).astype(np.float64)
    z = np.concatenate([t[:, None].astype(np.float64), np.cos(ang), -np.sin(ang)], axis=-1)
    z = np.pad(z, ((0, 0), (0, LANES - HY_EMB)))
    max_decay = math.log(HY_TARGET) / HY_SHORT_DECAY_PCT
    min_decay = math.log(HY_TARGET) / HY_LONG_DECAY_PCT
    deltas = np.linspace(min_decay, max_decay, HY_WIDTH, dtype=np.float32)
    decay = np.exp(-t[:, None].astype(np.float64) * np.abs(deltas)[None].astype(np.float64))
    return jnp.asarray(z, F32), jnp.asarray(decay, F32)


def _filter_kernel(z_ref, decay_ref, w1_ref, b1_ref, w2_ref, b2_ref, w3_ref, b3_ref, fr_ref, o_ref):
    tl = z_ref.shape[0]
    fr = fr_ref[...]
    a = jnp.sin(fr * (_dot3(z_ref[...], w1_ref[...]) + b1_ref[...]))
    a = jnp.sin(fr * (_dot3(a, w2_ref[...]) + b2_ref[...]))
    h = _dot3(a, w3_ref[...]) + b3_ref[...]
    h = h * jnp.tile(decay_ref[...], (1, 2 * HY_ORDER))
    row = pl.program_id(0) * tl + lax.broadcasted_iota(jnp.int32, h.shape, 0)
    col = lax.broadcasted_iota(jnp.int32, h.shape, 1)
    backward = (col // HY_WIDTH) % 2 == 1
    o_ref[...] = jnp.where(backward & (row == 0), 0.0, h)


def _hyena_filter_taps(L, w1, b1, w2, b2, w3, b3, freq):
    z, decay = _filter_tables(L)
    hid = w1.shape[1]
    n_out = w3.shape[1]
    tl = _tile(L, 512)
    w1p = jnp.pad(w1, ((0, LANES - HY_EMB), (0, 0)))
    full = lambda a: pl.BlockSpec(a.shape, lambda i: (0,) * a.ndim)
    args = [w1p, b1.reshape(1, hid), w2, b2.reshape(1, hid), w3, b3.reshape(1, n_out),
            freq.reshape(1, hid)]
    return pl.pallas_call(
        _filter_kernel,
        out_shape=jax.ShapeDtypeStruct((L, n_out), F32),
        grid=(L // tl,),
        in_specs=[pl.BlockSpec((tl, LANES), lambda i: (i, 0)),
                  pl.BlockSpec((tl, HY_WIDTH), lambda i: (i, 0))] + [full(a) for a in args],
        out_specs=pl.BlockSpec((tl, n_out), lambda i: (i, 0)),
        compiler_params=_cp(("arbitrary",)),
        name="hyena_filter_taps",
    )(z, decay, *args)


def _fft_matrices(n1, a_in):
    n = n1 * LANES
    k1 = np.arange(n1)
    a = np.arange(a_in)
    th1 = 2.0 * np.pi * np.outer(k1, a) / n1
    c1, s1 = np.cos(th1), np.sin(th1)
    m1_complex = np.block([[c1, s1], [-s1, c1]])
    m1_real = np.concatenate([c1, -s1], axis=0)
    m3 = np.block([[c1.T, -s1.T], [s1.T, c1.T]])
    b = np.arange(LANES)
    k2 = np.arange(LANES)
    k = k1[:, None, None] + n1 * k2[None, :, None]
    th2 = 2.0 * np.pi * (k * b[None, None, :] % n) / n
    c2, s2 = np.cos(th2), np.sin(th2)
    fwd = np.concatenate([np.concatenate([c2, s2], axis=2),
                          np.concatenate([-s2, c2], axis=2)], axis=1)
    c2t, s2t = np.swapaxes(c2, 1, 2), np.swapaxes(s2, 1, 2)
    inv = np.concatenate([np.concatenate([c2t, -s2t], axis=2),
                          np.concatenate([s2t, c2t], axis=2)], axis=1)
    as_bf16 = lambda m: jnp.asarray(m, F32).astype(BF16)
    return as_bf16(m1_complex), as_bf16(m1_real), as_bf16(m3), as_bf16(fwd), as_bf16(inv)


def _fft_stage1_kernel(x_ref, m_ref, o_ref, *, complex_in):
    n1 = o_ref.shape[1]
    for s in range(o_ref.shape[2]):
        if complex_in:
            x = jnp.concatenate([x_ref[0, :, s, :], x_ref[1, :, s, :]], axis=0)
        else:
            x = x_ref[:, s, :]
        y = _dot(m_ref[...], x.astype(BF16))
        o_ref[0, :, s, :] = y[:n1]
        o_ref[1, :, s, :] = y[n1:]


def _fft_stage1(x, m1, n1, complex_in):
    P, A, Wt = x.shape[0], x.shape[-3], x.shape[-1]
    tw = _tile(Wt, 512, LANES)
    if complex_in:
        xspec = pl.BlockSpec((None, 2, A, FFT_ROWS, tw), lambda p, i, j: (p, 0, 0, i, j))
    else:
        xspec = pl.BlockSpec((None, A, FFT_ROWS, tw), lambda p, i, j: (p, 0, i, j))
    return pl.pallas_call(
        functools.partial(_fft_stage1_kernel, complex_in=complex_in),
        out_shape=jax.ShapeDtypeStruct((P, 2, n1, LANES, Wt), F32),
        grid=(P, LANES // FFT_ROWS, Wt // tw),
        in_specs=[xspec, pl.BlockSpec(m1.shape, lambda p, i, j: (0, 0))],
        out_specs=pl.BlockSpec((None, 2, n1, FFT_ROWS, tw), lambda p, i, j: (p, 0, 0, i, j)),
        compiler_params=_cp(("parallel", "arbitrary", "arbitrary")),
        name="fft_stage1",
    )(x, m1)


def _filter_spectrum_kernel(y_ref, g_ref, o0_ref, o1_ref, *, scale):
    kt = y_ref.shape[1]
    for t in range(kt):
        y = jnp.concatenate([y_ref[0, t], y_ref[1, t]], axis=0).astype(BF16)
        f = _dot(g_ref[t], y)
        fr, fi = f[:LANES], f[LANES:]
        for o, o_ref in enumerate((o0_ref, o1_ref)):
            fw = slice((2 * o) * HY_WIDTH, (2 * o + 1) * HY_WIDTH)
            bw = slice((2 * o + 1) * HY_WIDTH, (2 * o + 2) * HY_WIDTH)
            o_ref[0, t] = (fr[:, fw] + fr[:, bw]) * scale
            o_ref[1, t] = (fi[:, fw] - fi[:, bw]) * scale


def _filter_spectrum(taps, mats, n1):
    L, n_col = taps.shape
    a_in = L // LANES
    _, m1_real, _, g_fwd, _ = mats
    y = _fft_stage1(taps.reshape(1, a_in, LANES, n_col), m1_real, n1, complex_in=False)
    kt = _tile(n1, 4, 1)
    out = jax.ShapeDtypeStruct((2, n1, LANES, HY_WIDTH), F32)
    ospec = pl.BlockSpec((2, kt, LANES, HY_WIDTH), lambda i: (0, i, 0, 0))
    return pl.pallas_call(
        functools.partial(_filter_spectrum_kernel, scale=1.0 / (n1 * LANES)),
        out_shape=(out, out),
        grid=(n1 // kt,),
        in_specs=[pl.BlockSpec((None, 2, kt, LANES, n_col), lambda i: (0, 0, i, 0, 0)),
                  pl.BlockSpec((kt, 2 * LANES, 2 * LANES), lambda i: (i, 0, 0))],
        out_specs=(ospec, ospec),
        compiler_params=_cp(("arbitrary",)),
        name="hyena_filter_spectrum",
    )(y, g_fwd)


def _fft_mid_kernel(y_ref, kf_ref, g_ref, gi_ref, o_ref):
    kt = y_ref.shape[1]
    for t in range(kt):
        y = jnp.concatenate([y_ref[0, t], y_ref[1, t]], axis=0).astype(BF16)
        f = _dot(g_ref[t], y)
        fr, fi = f[:LANES], f[LANES:]
        kr, ki = kf_ref[0, t], kf_ref[1, t]
        p = jnp.concatenate([fr * kr - fi * ki, fr * ki + fi * kr], axis=0).astype(BF16)
        u = _dot(gi_ref[t], p)
        o_ref[0, t] = u[:LANES]
        o_ref[1, t] = u[LANES:]


def _fft_mid(y, kf, g_fwd, g_inv):
    P, _, n1, _, W = y.shape
    kt = _tile(n1, 8, 1)
    yspec = pl.BlockSpec((None, 2, kt, LANES, W), lambda i, p: (p, 0, i, 0, 0))
    gspec = pl.BlockSpec((kt, 2 * LANES, 2 * LANES), lambda i, p: (i, 0, 0))
    return pl.pallas_call(
        _fft_mid_kernel,
        out_shape=jax.ShapeDtypeStruct(y.shape, F32),
        grid=(n1 // kt, P),
        in_specs=[yspec, pl.BlockSpec((2, kt, LANES, W), lambda i, p: (0, i, 0, 0)), gspec, gspec],
        out_specs=yspec,
        compiler_params=_cp(("arbitrary", "arbitrary")),
        name="fft_mid",
    )(y, kf, g_fwd, g_inv)


def _fft_stage3_kernel(u_ref, m_ref, z_ref, x_ref, bias_ref, g_ref, o_ref, *, normalise):
    a_out = z_ref.shape[1]
    bias = bias_ref[...]
    for s in range(o_ref.shape[2]):
        u = jnp.concatenate([u_ref[0, :, s, :], u_ref[1, :, s, :]], axis=0).astype(BF16)
        conv = _dot(m_ref[...], u)
        for part in range(2):
            y = x_ref[part, :, s, :] * (
                conv[part * a_out:(part + 1) * a_out] + z_ref[part, :, s, :] * bias)
            if normalise:
                y = y * lax.rsqrt(jnp.mean(y * y, axis=-1, keepdims=True) + EPS) * g_ref[...]
            o_ref[part, :, s, :] = y


def _fft_stage3(u, m3, z, gate, bias_row, norm_g=None):
    P, _, n1, _, W = u.shape
    A = z.shape[2]
    pair = pl.BlockSpec((None, 2, A, FFT_ROWS, W), lambda p, i: (p, 0, 0, i, 0))
    row = pl.BlockSpec((1, W), lambda p, i: (0, 0))
    normalise = norm_g is not None
    return pl.pallas_call(
        functools.partial(_fft_stage3_kernel, normalise=normalise),
        out_shape=jax.ShapeDtypeStruct(z.shape, F32),
        grid=(P, LANES // FFT_ROWS),
        in_specs=[pl.BlockSpec((None, 2, n1, FFT_ROWS, W), lambda p, i: (p, 0, 0, i, 0)),
                  pl.BlockSpec(m3.shape, lambda p, i: (0, 0)),
                  pair, pair, row, row],
        out_specs=pair,
        compiler_params=_cp(("parallel", "arbitrary")),
        name="fft_stage3",
    )(u, m3, z, gate, bias_row, norm_g.reshape(1, W) if normalise else bias_row)


def _hyena_mixer(v, gates, kfs, d_bias, norm_g, mats, n1):
    B, L, W = v.shape
    pair_shape = (B // 2, 2, L // LANES, LANES, W)
    m1_complex, _, m3, g_fwd, g_inv = mats
    z = v.reshape(pair_shape)
    for o in range(HY_ORDER):
        y = _fft_stage1(z, m1_complex, n1, complex_in=True)
        u = _fft_mid(y, kfs[o], g_fwd, g_inv)
        z = _fft_stage3(u, m3, z, gates[o].reshape(pair_shape), d_bias[o].reshape(1, W),
                        norm_g if o == HY_ORDER - 1 else None)
    return z.reshape(B, L, W)


def kernel(x, c, ctx, c_ctx, w_mod, b_mod, norm1_g, w_in, attn_sink, pool_w, pool_scale,
           hy_conv_w, hy_conv_b, hy_f_w1, hy_f_b1, hy_f_w2, hy_f_b2, hy_f_w3, hy_f_b3,
           hy_f_freq, hy_bias, g_attn, g_pool, g_hyena, w_out, norm2_g,
           ff_w1, ff_w3, ff_w2, router_w, moe_w1, moe_w3, moe_w2, final_g):
    B, S, D = x.shape
    C = ctx.shape[1]
    depth = w_mod.shape[0]
    assert B % 2 == 0 and S % BLOCK == 0
    fft_mats = {}

    c_rows = jnp.concatenate([c, c_ctx[None], jnp.zeros((SUBLANES - B - 1, D), F32)], axis=0)
    xc = ctx

    def hyena_branch(l, proj, seq_len):
        rows = -(-seq_len // FFT_MIN_ROWS) * FFT_MIN_ROWS
        n1 = 2 * rows // LANES
        if rows not in fft_mats:
            fft_mats[rows] = _fft_matrices(n1, rows // LANES)
        mats = fft_mats[rows]
        taps = _hyena_filter_taps(seq_len, hy_f_w1[l], hy_f_b1[l], hy_f_w2[l], hy_f_b2[l],
                                  hy_f_w3[l], hy_f_b3[l], hy_f_freq[l])
        v, x1, x2 = _short_conv(proj, hy_conv_w[l], hy_conv_b[l])
        if seq_len < rows:
            pad = lambda t: jnp.pad(t, ((0, 0), (0, rows - seq_len), (0, 0)))
            taps = jnp.pad(taps, ((0, rows - seq_len), (0, 0)))
            v, x1, x2 = pad(v), pad(x1), pad(x2)
        kfs = _filter_spectrum(taps, mats, n1)
        z = _hyena_mixer(v, (x1, x2), kfs, hy_bias[l], g_hyena[l], mats, n1)
        return z[:, :seq_len]

    def mix(l, proj, y_att, res, gate, w_out_l, seq_len):
        y_pool = _pool_mixer(proj, pool_w[l], pool_scale[l], g_pool[l])
        y_hy = hyena_branch(l, proj, seq_len)
        return _matmul_gated_residual([y_att, y_pool, y_hy], w_out_l, res, gate)

    for l in range(depth):
        last = l == depth - 1
        mod = _adaln(c_rows, w_mod[l], b_mod[l])
        sh1, sc1, g1, sh2, sc2, g2 = [mod[:B, j * D:(j + 1) * D] for j in range(6)]
        csh1, csc1, cg1, csh2, csc2, cg2 = [mod[B:B + 1, j * D:(j + 1) * D] for j in range(6)]
        w_in_l = w_in[l].astype(BF16)
        w_out_l = w_out[l].astype(BF16)

        hc = _norm_mod(xc, norm1_g[l], csh1, csc1)
        if last:
            kv_w = w_in_l[:, ATT_WIDTH:ATT_WIDTH + 2 * KV_WIDTH]
            kv_c = _matmul(hc, kv_w, out_dtype=BF16)
            k_c, v_c = kv_c[..., :KV_WIDTH], kv_c[..., KV_WIDTH:]
        else:
            proj_c = _matmul(hc, w_in_l)
            q_c, k_c, v_c = _qkv_prepare(proj_c, rope=False)
            y_att_c = _attention(q_c, None, None, k_c, v_c, attn_sink[l], g_attn[l])
            xc_new = mix(l, proj_c, y_att_c, xc, cg1, w_out_l, C)
            hc2 = _norm_mod(xc_new, norm2_g[l], csh2, csc2)
            i = l // 2
            if l % 2 == 0:
                hid_c = _swiglu_hidden(hc2, ff_w1[i].astype(BF16), ff_w3[i].astype(BF16))
                xc_new = _matmul_gated_residual([hid_c], ff_w2[i].astype(BF16), xc_new, cg2)
            else:
                raise NotImplementedError("context tokens through an expert layer")

        h = _norm_mod(x, norm1_g[l], sh1, sc1)
        proj = _matmul(h, w_in_l)
        q, k, v = _qkv_prepare(proj, rope=True)
        y_att = _attention(q, k, v, k_c, v_c, attn_sink[l], g_attn[l])
        x = mix(l, proj, y_att, x, g1, w_out_l, S)

        i = l // 2
        if l % 2 == 0:
            h2 = _norm_mod(x, norm2_g[l], sh2, sc2)
            hid = _swiglu_hidden(h2, ff_w1[i].astype(BF16), ff_w3[i].astype(BF16))
            x = _matmul_gated_residual([hid], ff_w2[i].astype(BF16), x, g2)
            if last:
                x = _rmsnorm(x, final_g, F32)
        else:
            h2, gi, gw, counts = _norm_mod_router(x, norm2_g[l], sh2, sc2, router_w[i])
            assert last
            x = _moe_layer(x, h2, gi, gw, counts, g2, final_g, moe_w1[i], moe_w3[i], moe_w2[i])
        if not last:
            xc = xc_new
    return x
```

```python
import functools
import math

import numpy as np
import jax
import jax.numpy as jnp
from jax import lax
from jax.experimental import pallas as pl
from jax.experimental.pallas import tpu as pltpu

F32 = jnp.float32
BF16 = jnp.bfloat16

EPS = 1e-6
NEG = -1e30
GRID_W = 64
ATT_HEADS = 8
ATT_KV_HEADS = 2
ATT_GROUP = ATT_HEADS // ATT_KV_HEADS
HEAD_DIM = 128
ATT_WIDTH = ATT_HEADS * HEAD_DIM
KV_WIDTH = ATT_KV_HEADS * HEAD_DIM
WINDOW = 128
BLOCK = 128
ROPE_BASE = 10000.0
POOL_WINDOWS = (2, 4, 8, 16)
POOL_GROUP = 128
POOL_WIDTH = POOL_GROUP * len(POOL_WINDOWS)
HY_WIDTH = 512
HY_ORDER = 2
HY_SHORT = 3
HY_EMB = 33
HY_BANDS = (HY_EMB - 1) // 2
HY_SHORT_DECAY_PCT = 0.3
HY_LONG_DECAY_PCT = 1.5
HY_TARGET = 1e-2
N_EXPERTS = 8
TOP_K = 2

LANES = 128
SUBLANES = 8
HALO = SUBLANES
FFT_MIN_ROWS = 2048
FFT_ROWS = SUBLANES
MOE_OUT_K_CHUNKS = 4
VMEM_LIMIT = 56 * 1024 * 1024


def _cp(sem):
    return pltpu.CompilerParams(dimension_semantics=sem, vmem_limit_bytes=VMEM_LIMIT)


def _tile(n, pref, mult=SUBLANES):
    if n <= pref:
        return n
    t = (pref // mult) * mult
    while t >= mult:
        if n % t == 0:
            return t
        t -= mult
    return n


def _split_bf16(x):
    hi = x.astype(BF16)
    lo = (x - hi.astype(F32)).astype(BF16)
    return hi, lo


def _dot(a, b):
    return jnp.dot(a, b, preferred_element_type=F32)


def _pack_bf16_pairs(x):
    n = x.shape[-1] // 2
    bits = lax.bitcast_convert_type(x.astype(BF16).astype(F32), jnp.uint32)
    return (bits[:, :n] >> 16) | (bits[:, n:] & jnp.uint32(0xFFFF0000))


def _unpack_bf16_pairs(u):
    lo = lax.bitcast_convert_type(u << 16, F32).astype(BF16)
    hi = lax.bitcast_convert_type(u & jnp.uint32(0xFFFF0000), F32).astype(BF16)
    return lo, hi


def _dot3(a, b):
    ah, al = _split_bf16(a)
    bh, bl = _split_bf16(b)
    return _dot(ah, bh) + (_dot(ah, bl) + _dot(al, bh))


def _adaln_kernel(c_ref, w_ref, b_ref, o_ref):
    c = c_ref[...]
    a = c * (1.0 / (1.0 + jnp.exp(-c)))
    o_ref[...] = _dot3(a, w_ref[...]) + b_ref[...]


def _adaln(c_rows, w, b):
    R, D = c_rows.shape
    N = w.shape[1]
    tn = _tile(N, 1536, LANES)
    return pl.pallas_call(
        _adaln_kernel,
        out_shape=jax.ShapeDtypeStruct((R, N), F32),
        grid=(N // tn,),
        in_specs=[pl.BlockSpec((R, D), lambda j: (0, 0)),
                  pl.BlockSpec((D, tn), lambda j: (0, j)),
                  pl.BlockSpec((1, tn), lambda j: (0, j))],
        out_specs=pl.BlockSpec((R, tn), lambda j: (0, j)),
        compiler_params=_cp(("arbitrary",)),
        name="adaln",
    )(c_rows, w, b.reshape(1, N))


def _norm_mod_kernel(x_ref, g_ref, sh_ref, sc_ref, o_ref):
    x = x_ref[...]
    y = x * lax.rsqrt(jnp.mean(x * x, axis=-1, keepdims=True) + EPS) * g_ref[...]
    o_ref[...] = (y * (1.0 + sc_ref[...]) + sh_ref[...]).astype(o_ref.dtype)


def _bcast_map(arr):
    if arr.shape[0] == 1:
        return lambda b, i: (0, 0, 0)
    return lambda b, i: (b, 0, 0)


def _norm_mod(x, g, shift, scale):
    B, L, D = x.shape
    ts = _tile(L, 512)
    shift = shift[:, None, :]
    scale = scale[:, None, :]
    return pl.pallas_call(
        _norm_mod_kernel,
        out_shape=jax.ShapeDtypeStruct((B, L, D), BF16),
        grid=(B, L // ts),
        in_specs=[pl.BlockSpec((None, ts, D), lambda b, i: (b, i, 0)),
                  pl.BlockSpec((1, D), lambda b, i: (0, 0)),
                  pl.BlockSpec((None, 1, D), _bcast_map(shift)),
                  pl.BlockSpec((None, 1, D), _bcast_map(scale))],
        out_specs=pl.BlockSpec((None, ts, D), lambda b, i: (b, i, 0)),
        compiler_params=_cp(("parallel", "parallel")),
        name="norm_mod",
    )(x, g.reshape(1, D), shift, scale)


def _norm_mod_router_kernel(x_ref, g_ref, sh_ref, sc_ref, rw_ref, tri_ref,
                            o_ref, gi_ref, gw_ref, cnt_ref, run_ref):
    @pl.when((pl.program_id(0) == 0) & (pl.program_id(1) == 0))
    def _():
        run_ref[...] = jnp.zeros(run_ref.shape, run_ref.dtype)

    x = x_ref[...]
    y = x * lax.rsqrt(jnp.mean(x * x, axis=-1, keepdims=True) + EPS) * g_ref[...]
    h = y * (1.0 + sc_ref[...]) + sh_ref[...]
    o_ref[...] = _pack_bf16_pairs(h)
    logits = _dot3(h, rw_ref[...])
    lane = lax.broadcasted_iota(jnp.int32, logits.shape, 1)
    logits = jnp.where(lane < N_EXPERTS, logits, NEG)
    m1 = jnp.max(logits, axis=-1, keepdims=True)
    i1 = jnp.min(jnp.where(logits == m1, lane, LANES), axis=-1, keepdims=True)
    rest = jnp.where(lane == i1, NEG, logits)
    m2 = jnp.max(rest, axis=-1, keepdims=True)
    i2 = jnp.min(jnp.where(rest == m2, lane, LANES), axis=-1, keepdims=True)
    e2 = jnp.exp(m2 - m1)
    w1 = 1.0 / (1.0 + e2)
    w2 = e2 * w1
    pick1 = lane == i1
    pick2 = lane == i2
    both = jnp.where(pick1 | pick2, 1.0, 0.0)
    before = _dot(tri_ref[...], both.astype(BF16)) + run_ref[...]
    r1 = jnp.sum(jnp.where(pick1, before, 0.0), axis=-1, keepdims=True).astype(jnp.int32)
    r2 = jnp.sum(jnp.where(pick2, before, 0.0), axis=-1, keepdims=True).astype(jnp.int32)
    run_ref[...] = run_ref[...] + jnp.sum(both, axis=0, keepdims=True)
    cnt_ref[...] = run_ref[...]
    gi_ref[...] = jnp.where(lane == 0, i1, jnp.where(lane == 1, i2,
                            jnp.where(lane == 2, r1, jnp.where(lane == 3, r2, 0))))
    gw_ref[...] = jnp.where(lane == 0, w1, jnp.where(lane == 1, w2, 0.0))


def _norm_mod_router(x, g, shift, scale, router_w):
    B, L, D = x.shape
    ts = _tile(L, 512)
    shift = shift[:, None, :]
    scale = scale[:, None, :]
    rw = jnp.pad(router_w, ((0, 0), (0, LANES - router_w.shape[1])))
    tri = jnp.asarray(np.tril(np.ones((ts, ts), np.float32), -1), BF16)
    row = pl.BlockSpec((None, ts, D), lambda b, i: (b, i, 0))
    small = pl.BlockSpec((None, ts, LANES), lambda b, i: (b, i, 0))
    return pl.pallas_call(
        _norm_mod_router_kernel,
        out_shape=(jax.ShapeDtypeStruct((B, L, D // 2), jnp.uint32),
                   jax.ShapeDtypeStruct((B, L, LANES), jnp.int32),
                   jax.ShapeDtypeStruct((B, L, LANES), F32),
                   jax.ShapeDtypeStruct((1, LANES), F32)),
        grid=(B, L // ts),
        in_specs=[row,
                  pl.BlockSpec((1, D), lambda b, i: (0, 0)),
                  pl.BlockSpec((None, 1, D), _bcast_map(shift)),
                  pl.BlockSpec((None, 1, D), _bcast_map(scale)),
                  pl.BlockSpec((D, LANES), lambda b, i: (0, 0)),
                  pl.BlockSpec((ts, ts), lambda b, i: (0, 0))],
        out_specs=(pl.BlockSpec((None, ts, D // 2), lambda b, i: (b, i, 0)), small, small,
                   pl.BlockSpec((1, LANES), lambda b, i: (0, 0))),
        scratch_shapes=[pltpu.VMEM((1, LANES), F32)],
        compiler_params=_cp(("arbitrary", "arbitrary")),
        name="norm_mod_router",
    )(x, g.reshape(1, D), shift, scale, rw, tri)


def _rmsnorm_kernel(x_ref, g_ref, o_ref):
    x = x_ref[...]
    y = x * lax.rsqrt(jnp.mean(x * x, axis=-1, keepdims=True) + EPS) * g_ref[...]
    o_ref[...] = y.astype(o_ref.dtype)


def _rmsnorm(x, g, out_dtype):
    B, L, D = x.shape
    ts = _tile(L, 512)
    return pl.pallas_call(
        _rmsnorm_kernel,
        out_shape=jax.ShapeDtypeStruct((B, L, D), out_dtype),
        grid=(B, L // ts),
        in_specs=[pl.BlockSpec((None, ts, D), lambda b, i: (b, i, 0)),
                  pl.BlockSpec((1, D), lambda b, i: (0, 0))],
        out_specs=pl.BlockSpec((None, ts, D), lambda b, i: (b, i, 0)),
        compiler_params=_cp(("parallel", "parallel")),
        name="rmsnorm",
    )(x, g.reshape(1, D))


def _mm_kernel(a_ref, w_ref, o_ref):
    o_ref[...] = _dot(a_ref[...], w_ref[...]).astype(o_ref.dtype)


def _matmul(a, w, out_dtype=F32, tm=1024, tn=512):
    B, L, K = a.shape
    N = w.shape[1]
    tm = _tile(L, tm)
    tn = _tile(N, tn, LANES)
    return pl.pallas_call(
        _mm_kernel,
        out_shape=jax.ShapeDtypeStruct((B, L, N), out_dtype),
        grid=(B, L // tm, N // tn),
        in_specs=[pl.BlockSpec((None, tm, K), lambda b, i, j: (b, i, 0)),
                  pl.BlockSpec((K, tn), lambda b, i, j: (0, j))],
        out_specs=pl.BlockSpec((None, tm, tn), lambda b, i, j: (b, i, j)),
        compiler_params=_cp(("parallel", "parallel", "arbitrary")),
        name="matmul",
    )(a, w)


def _mm_res_kernel(*refs, n_a):
    a_refs = refs[:n_a]
    w_refs = refs[n_a:2 * n_a]
    res_ref, gate_ref, o_ref = refs[2 * n_a:]
    acc = _dot(a_refs[0][...].astype(BF16), w_refs[0][...])
    for a_ref, w_ref in zip(a_refs[1:], w_refs[1:]):
        acc = acc + _dot(a_ref[...].astype(BF16), w_ref[...])
    o_ref[...] = res_ref[...] + gate_ref[...] * acc


def _matmul_gated_residual(a_list, w, res, gate, tm=1024, tn=512):
    B, L, N = res.shape
    tm = _tile(L, tm)
    tn = _tile(N, tn, LANES)
    gate = gate[:, None, :]
    widths = [a.shape[-1] for a in a_list]
    unit = math.gcd(*widths) if len(widths) > 1 else widths[0]
    in_specs = [pl.BlockSpec((None, tm, k), lambda b, i, j: (b, i, 0)) for k in widths]
    off = 0
    for k in widths:
        assert off % k == 0 and k % unit == 0
        in_specs.append(pl.BlockSpec((k, tn), functools.partial(
            lambda b, i, j, blk: (blk, j), blk=off // k)))
        off += k
    gmap = (lambda b, i, j: (0, 0, j)) if gate.shape[0] == 1 else (lambda b, i, j: (b, 0, j))
    in_specs += [pl.BlockSpec((None, tm, tn), lambda b, i, j: (b, i, j)),
                 pl.BlockSpec((None, 1, tn), gmap)]
    return pl.pallas_call(
        functools.partial(_mm_res_kernel, n_a=len(a_list)),
        out_shape=jax.ShapeDtypeStruct((B, L, N), F32),
        grid=(B, L // tm, N // tn),
        in_specs=in_specs,
        out_specs=pl.BlockSpec((None, tm, tn), lambda b, i, j: (b, i, j)),
        compiler_params=_cp(("parallel", "parallel", "arbitrary")),
        name="matmul_gated_residual",
    )(*a_list, *([w] * len(a_list)), res, gate)


def _silu(x):
    return x * (1.0 / (1.0 + jnp.exp(-x)))


def _swiglu_kernel(a_ref, w1_ref, w3_ref, o_ref):
    a = a_ref[...]
    o_ref[...] = (_silu(_dot(a, w1_ref[...])) * _dot(a, w3_ref[...])).astype(o_ref.dtype)


def _swiglu_hidden(a, w1, w3, tm=1024, tf=512):
    B, L, D = a.shape
    F = w1.shape[1]
    tm = _tile(L, tm)
    tf = _tile(F, tf, LANES)
    wspec = pl.BlockSpec((D, tf), lambda b, i, j: (0, j))
    return pl.pallas_call(
        _swiglu_kernel,
        out_shape=jax.ShapeDtypeStruct((B, L, F), BF16),
        grid=(B, L // tm, F // tf),
        in_specs=[pl.BlockSpec((None, tm, D), lambda b, i, j: (b, i, 0)), wspec, wspec],
        out_specs=pl.BlockSpec((None, tm, tf), lambda b, i, j: (b, i, j)),
        compiler_params=_cp(("parallel", "parallel", "arbitrary")),
        name="swiglu_hidden",
    )(a, w1, w3)


def _moe_hidden_kernel(te_ref, nt_ref, a_ref, w1_ref, w3_ref, o_ref, w1b_ref, w3b_ref):
    i = pl.program_id(1)
    used = i < nt_ref[0]
    half = a_ref.shape[1]
    new_weights = (i == 0) | (te_ref[i] != te_ref[jnp.maximum(i - 1, 0)])

    @pl.when(used & new_weights)
    def _():
        w1b_ref[...] = w1_ref[...].astype(BF16)
        w3b_ref[...] = w3_ref[...].astype(BF16)

    @pl.when(used)
    def _():
        lo, hi = _unpack_bf16_pairs(a_ref[...])
        gate = _dot(lo, w1b_ref[:half, :]) + _dot(hi, w1b_ref[half:, :])
        up = _dot(lo, w3b_ref[:half, :]) + _dot(hi, w3b_ref[half:, :])
        o_ref[...] = (_silu(gate) * up).astype(o_ref.dtype)

    @pl.when(jnp.logical_not(used))
    def _():
        o_ref[...] = jnp.zeros(o_ref.shape, o_ref.dtype)


def _moe_out_kernel(te_ref, nt_ref, a_ref, w2_ref, o_ref):
    used = pl.program_id(0) < nt_ref[0]

    @pl.when(used)
    def _():
        kc = a_ref.shape[1] // MOE_OUT_K_CHUNKS
        acc = _dot(a_ref[:, :kc], w2_ref[:kc, :].astype(BF16))
        for c in range(1, MOE_OUT_K_CHUNKS):
            acc = acc + _dot(a_ref[:, c * kc:(c + 1) * kc], w2_ref[c * kc:(c + 1) * kc, :].astype(BF16))
        o_ref[...] = acc.astype(o_ref.dtype)

    @pl.when(jnp.logical_not(used))
    def _():
        o_ref[...] = jnp.zeros(o_ref.shape, o_ref.dtype)


def _moe_experts(a_sorted, tile_expert, n_tiles_used, w1, w3, w2, tm, tf=512, tn=512):
    R = a_sorted.shape[0]
    E, D, F = w1.shape
    tf = _tile(F, tf, LANES)
    tn = _tile(D, tn, LANES)
    nt = R // tm
    w13 = pl.BlockSpec((None, D, tf), lambda j, i, te, n: (te[i], 0, j))
    hidden = pl.pallas_call(
        _moe_hidden_kernel,
        out_shape=jax.ShapeDtypeStruct((R, F), BF16),
        grid_spec=pltpu.PrefetchScalarGridSpec(
            num_scalar_prefetch=2,
            grid=(F // tf, nt),
            in_specs=[pl.BlockSpec((tm, D // 2), lambda j, i, te, n: (i, 0)), w13, w13],
            out_specs=pl.BlockSpec((tm, tf), lambda j, i, te, n: (i, j)),
            scratch_shapes=[pltpu.VMEM((D, tf), BF16), pltpu.VMEM((D, tf), BF16)]),
        compiler_params=_cp(("arbitrary", "arbitrary")),
        name="moe_hidden",
    )(tile_expert, n_tiles_used, a_sorted, w1, w3)
    return pl.pallas_call(
        _moe_out_kernel,
        out_shape=jax.ShapeDtypeStruct((R, D), F32),
        grid_spec=pltpu.PrefetchScalarGridSpec(
            num_scalar_prefetch=2,
            grid=(nt, D // tn),
            in_specs=[pl.BlockSpec((tm, F), lambda i, j, te, n: (i, 0)),
                      pl.BlockSpec((None, F, tn), lambda i, j, te, n: (te[i], 0, j))],
            out_specs=pl.BlockSpec((tm, tn), lambda i, j, te, n: (i, j))),
        compiler_params=_cp(("arbitrary", "arbitrary")),
        name="moe_out",
    )(tile_expert, n_tiles_used, hidden, w2)


def _moe_combine_kernel(x_ref, ya_ref, yb_ref, gw_ref, gate_ref, g_ref, o_ref):
    gw = gw_ref[...]
    moe = gw[:, 0:1] * ya_ref[...] + gw[:, 1:2] * yb_ref[...]
    x = x_ref[...] + gate_ref[...] * moe
    y = x * lax.rsqrt(jnp.mean(x * x, axis=-1, keepdims=True) + EPS) * g_ref[...]
    o_ref[...] = y


def _moe_combine_norm(x, ya, yb, gw, gate, g):
    B, L, D = x.shape
    ts = _tile(L, 512)
    row = pl.BlockSpec((None, ts, D), lambda b, i: (b, i, 0))
    return pl.pallas_call(
        _moe_combine_kernel,
        out_shape=jax.ShapeDtypeStruct((B, L, D), F32),
        grid=(B, L // ts),
        in_specs=[row, row, row,
                  pl.BlockSpec((None, ts, LANES), lambda b, i: (b, i, 0)),
                  pl.BlockSpec((None, 1, D), lambda b, i: (b, 0, 0)),
                  pl.BlockSpec((1, D), lambda b, i: (0, 0))],
        out_specs=row,
        compiler_params=_cp(("parallel", "parallel")),
        name="moe_combine_norm",
    )(x, ya, yb, gw, gate[:, None, :], g.reshape(1, D))


def _moe_layer(x, h, gi, gw, counts, gate, final_g, w1, w3, w2):
    B, L, D = x.shape
    E = w1.shape[0]
    n_tok = B * L
    n_pair = n_tok * TOP_K
    tm = _tile(n_pair, 1024)
    nt = n_pair // tm + E
    e_pair = gi[..., :TOP_K].reshape(n_pair)
    rank = gi[..., TOP_K:2 * TOP_K].reshape(n_pair)
    counts = counts[0, :E].astype(jnp.int32)
    tiles_per = (counts + tm - 1) // tm
    tile_end = jnp.cumsum(tiles_per)
    tile_start = tile_end - tiles_per
    row_start = jnp.zeros((n_pair,), jnp.int32)
    for e in range(E):
        row_start = jnp.where(e_pair == e, tile_start[e] * tm, row_start)
    pos = row_start + rank
    tile_ids = jnp.arange(nt, dtype=jnp.int32)
    tile_expert = jnp.minimum(
        jnp.sum((tile_ids[:, None] >= tile_end[None, :]).astype(jnp.int32), axis=1), E - 1)
    n_used = tile_end[-1:].astype(jnp.int32)
    tok_pair = jnp.arange(n_pair, dtype=jnp.int32) // TOP_K
    src = (jnp.arange(nt * tm, dtype=jnp.int32) % n_tok).at[pos].set(
        tok_pair, unique_indices=True, mode="promise_in_bounds")
    gather_rows = lambda rows, idx: rows.at[idx].get(mode="promise_in_bounds")
    a_sorted = gather_rows(h.reshape(n_tok, h.shape[-1]), src)
    y_sorted = _moe_experts(a_sorted, tile_expert.astype(jnp.int32), n_used, w1, w3, w2, tm)
    pos2 = pos.reshape(n_tok, TOP_K)
    ya = gather_rows(y_sorted, pos2[:, 0]).reshape(B, L, D)
    yb = gather_rows(y_sorted, pos2[:, 1]).reshape(B, L, D)
    return _moe_combine_norm(x, ya, yb, gw, gate, final_g)


def _rope_tables(n_tokens):
    pos = np.arange(n_tokens)
    row = (pos // GRID_W).astype(np.float32)
    col = (pos % GRID_W).astype(np.float32)
    n_freq = HEAD_DIM // 4
    inv = (np.float32(ROPE_BASE) ** (-np.arange(n_freq, dtype=np.float32) / np.float32(n_freq)))
    ang_r = (row[:, None] * inv).astype(np.float64)
    ang_c = (col[:, None] * inv).astype(np.float64)
    cos = np.concatenate([np.cos(ang_r), np.cos(ang_r), np.cos(ang_c), np.cos(ang_c)], axis=1)
    sin = np.concatenate([-np.sin(ang_r), np.sin(ang_r), -np.sin(ang_c), np.sin(ang_c)], axis=1)
    return jnp.asarray(cos, F32), jnp.asarray(sin, F32)


def _swap_halves(x):
    n = x.shape[-1]
    quarter = HEAD_DIM // 4
    lane = lax.broadcasted_iota(jnp.int32, x.shape, 1)
    up = pltpu.roll(x, n - quarter, axis=1)
    down = pltpu.roll(x, quarter, axis=1)
    return jnp.where((lane & quarter) == 0, up, down)


def _qkv_kernel(q_ref, k_ref, v_ref, cos_ref, sin_ref, qo_ref, ko_ref, vo_ref, *, rope):
    q = q_ref[...]
    k = k_ref[...]
    if rope:
        cos = cos_ref[...]
        sin = sin_ref[...]
        q = q * jnp.tile(cos, (1, ATT_HEADS)) + _swap_halves(q) * jnp.tile(sin, (1, ATT_HEADS))
        k = k * jnp.tile(cos, (1, ATT_KV_HEADS)) + _swap_halves(k) * jnp.tile(sin, (1, ATT_KV_HEADS))
    qo_ref[...] = (q * (HEAD_DIM ** -0.5)).astype(BF16)
    ko_ref[...] = k.astype(BF16)
    vo_ref[...] = v_ref[...].astype(BF16)


def _qkv_prepare(proj, rope):
    B, L, _ = proj.shape
    ts = _tile(L, 512)
    cos, sin = _rope_tables(L)
    kvb = ATT_WIDTH // KV_WIDTH
    tab = pl.BlockSpec((ts, HEAD_DIM), lambda b, i: (i, 0))
    return pl.pallas_call(
        functools.partial(_qkv_kernel, rope=rope),
        out_shape=(jax.ShapeDtypeStruct((B, L, ATT_WIDTH), BF16),
                   jax.ShapeDtypeStruct((B, L, KV_WIDTH), BF16),
                   jax.ShapeDtypeStruct((B, L, KV_WIDTH), BF16)),
        grid=(B, L // ts),
        in_specs=[pl.BlockSpec((None, ts, ATT_WIDTH), lambda b, i: (b, i, 0)),
                  pl.BlockSpec((None, ts, KV_WIDTH), lambda b, i: (b, i, kvb)),
                  pl.BlockSpec((None, ts, KV_WIDTH), lambda b, i: (b, i, kvb + 1)),
                  tab, tab],
        out_specs=(pl.BlockSpec((None, ts, ATT_WIDTH), lambda b, i: (b, i, 0)),
                   pl.BlockSpec((None, ts, KV_WIDTH), lambda b, i: (b, i, 0)),
                   pl.BlockSpec((None, ts, KV_WIDTH), lambda b, i: (b, i, 0))),
        compiler_params=_cp(("parallel", "parallel")),
        name="qkv_prepare",
    )(proj, proj, proj, cos, sin)


def _nt_dot(a, b):
    return lax.dot_general(a, b, (((1,), (1,)), ((), ())), preferred_element_type=F32)


def _attn_bias_tables():
    T, G = BLOCK, ATT_GROUP
    qi = np.arange(G * T)[:, None] % T
    ki = np.arange(3 * T)[None, :]
    band = np.abs(ki - T - qi) <= WINDOW
    after_start = ki >= T
    before_end = ki < 2 * T
    masks = [band & after_start, band, band & before_end, band & after_start & before_end]
    return jnp.asarray(np.stack([np.where(m, 0.0, NEG) for m in masks]), F32)


def _attn_kernel(*refs, local, n_blocks):
    if local:
        (q_ref, kp_ref, kc_ref, kn_ref, vp_ref, vc_ref, vn_ref,
         kx_ref, vx_ref, bias_ref, sink_ref, g_ref, o_ref, acc_ref) = refs
    else:
        q_ref, kx_ref, vx_ref, sink_ref, g_ref, o_ref, acc_ref = refs
    T = BLOCK
    G = ATT_GROUP
    n_sub = q_ref.shape[0] // T
    for h in range(ATT_KV_HEADS):
        cols = slice(h * HEAD_DIM, (h + 1) * HEAD_DIM)
        sink = jnp.concatenate(
            [jnp.broadcast_to(sink_ref[:, (h * G + g) * HEAD_DIM:(h * G + g) * HEAD_DIM + 1], (T, 1))
             for g in range(G)], axis=0)
        kx = kx_ref[:, cols]
        vx = jnp.concatenate([vx_ref[:, cols], jnp.ones((kx_ref.shape[0], HEAD_DIM), BF16)], axis=1)
        if local:
            k_band = jnp.concatenate([kp_ref[:, cols], kc_ref[:, cols], kn_ref[:, cols]], axis=0)
            v_band = jnp.concatenate([vp_ref[:, cols], vc_ref[:, cols], vn_ref[:, cols]], axis=0)
            v_band = jnp.concatenate([v_band, jnp.ones(v_band.shape, BF16)], axis=1)
        for j in range(n_sub):
            rows = slice(j * T, (j + 1) * T)
            qs = jnp.concatenate(
                [q_ref[rows, (h * G + g) * HEAD_DIM:(h * G + g + 1) * HEAD_DIM] for g in range(G)],
                axis=0)
            s_ctx = _nt_dot(qs, kx)
            m = jnp.maximum(jnp.max(s_ctx, axis=-1, keepdims=True), sink)
            if local:
                blk = pl.program_id(1) * n_sub + j
                is_first = blk == 0
                is_last = blk == n_blocks - 1
                table = jnp.where(is_first, jnp.where(is_last, 3, 0), jnp.where(is_last, 2, 1))
                s_loc = _nt_dot(qs, k_band[j * T:(j + 3) * T]) + bias_ref[table]
                m = jnp.maximum(m, jnp.max(s_loc, axis=-1, keepdims=True))
            o = _dot(jnp.exp((s_ctx - m).astype(BF16)), vx)
            if local:
                o = o + _dot(jnp.exp((s_loc - m).astype(BF16)), v_band[j * T:(j + 3) * T])
            denom = o[:, HEAD_DIM:HEAD_DIM + 1] + jnp.exp(sink - m)
            o = o[:, :HEAD_DIM] * (1.0 / denom)
            for g in range(G):
                acc_ref[rows, (h * G + g) * HEAD_DIM:(h * G + g + 1) * HEAD_DIM] = o[g * T:(g + 1) * T]
    y = acc_ref[...]
    y = y * lax.rsqrt(jnp.mean(y * y, axis=-1, keepdims=True) + EPS) * g_ref[...]
    o_ref[...] = y.astype(o_ref.dtype)


def _attention(q, k, v, k_ctx, v_ctx, sink, g_attn):
    B, L, _ = q.shape
    C = k_ctx.shape[1]
    local = k is not None
    T = BLOCK
    nb = L // T
    n_sub = 2 if nb % 2 == 0 else 1
    sink_row = jnp.repeat(sink.astype(F32), HEAD_DIM).reshape(1, ATT_WIDTH)
    qspec = pl.BlockSpec((None, n_sub * T, ATT_WIDTH), lambda b, i: (b, i, 0))
    in_specs = [qspec]
    args = [q]
    cspec = pl.BlockSpec((None, C, KV_WIDTH), lambda b, i: (b, 0, 0))
    row = pl.BlockSpec((1, ATT_WIDTH), lambda b, i: (0, 0))
    if local:
        prev = pl.BlockSpec((None, T, KV_WIDTH), lambda b, i: (b, jnp.maximum(i * n_sub - 1, 0), 0))
        cur = pl.BlockSpec((None, n_sub * T, KV_WIDTH), lambda b, i: (b, i, 0))
        nxt = pl.BlockSpec((None, T, KV_WIDTH),
                           lambda b, i: (b, jnp.minimum((i + 1) * n_sub, nb - 1), 0))
        bias = _attn_bias_tables()
        in_specs += [prev, cur, nxt, prev, cur, nxt, cspec, cspec,
                     pl.BlockSpec(bias.shape, lambda b, i: (0, 0, 0))]
        args += [k, k, k, v, v, v, k_ctx, v_ctx, bias]
    else:
        in_specs += [cspec, cspec]
        args += [k_ctx, v_ctx]
    in_specs += [row, row]
    args += [sink_row, g_attn.reshape(1, ATT_WIDTH)]
    return pl.pallas_call(
        functools.partial(_attn_kernel, local=local, n_blocks=nb),
        out_shape=jax.ShapeDtypeStruct((B, L, ATT_WIDTH), BF16),
        grid=(B, nb // n_sub),
        in_specs=in_specs,
        out_specs=qspec,
        scratch_shapes=[pltpu.VMEM((n_sub * T, ATT_WIDTH), F32)],
        compiler_params=_cp(("parallel", "arbitrary")),
        name="attention_local" if local else "attention_context",
    )(*args)


def _halo_specs(ts, L, width, col_block):
    nb8 = L // HALO
    per = ts // HALO
    prev = pl.BlockSpec((None, HALO, width),
                        lambda b, i: (b, jnp.maximum(i * per - 1, 0), col_block))
    cur = pl.BlockSpec((None, ts, width), lambda b, i: (b, i, col_block))
    nxt = pl.BlockSpec((None, HALO, width),
                       lambda b, i: (b, jnp.minimum((i + 1) * per, nb8 - 1), col_block))
    return [prev, cur, nxt]


def _with_halo(prev_ref, cur_ref, next_ref):
    i = pl.program_id(1)
    last = pl.num_programs(1) - 1
    prev = jnp.where(i > 0, prev_ref[...], 0.0)
    nxt = jnp.where(i < last, next_ref[...], 0.0)
    return jnp.concatenate([prev, cur_ref[...], nxt], axis=0)


def _pool_kernel(prev_ref, cur_ref, next_ref, w_ref, scale_ref, g_ref, o_ref, *, seq_len):
    ts = cur_ref.shape[0]
    ext = _with_halo(prev_ref, cur_ref, next_ref)
    pos = pl.program_id(1) * ts + lax.broadcasted_iota(jnp.int32, (ts, 1), 0)
    outs = []
    for gidx, win in enumerate(POOL_WINDOWS):
        cols = slice(gidx * POOL_GROUP, (gidx + 1) * POOL_GROUP)
        run = ext[:, cols]
        step = 1
        while step < win:
            run = run[:run.shape[0] - step] + run[step:]
            step *= 2
        lo = HALO - win // 2
        total = run[lo:lo + ts]
        cnt = (jnp.minimum(pos + (win - win // 2), seq_len) - jnp.maximum(pos - win // 2, 0))
        mean = total * (1.0 / cnt.astype(F32))
        outs.append(_dot3(mean - cur_ref[:, cols], w_ref[gidx]))
    y = jnp.concatenate(outs, axis=-1) * scale_ref[...]
    y = y * lax.rsqrt(jnp.mean(y * y, axis=-1, keepdims=True) + EPS) * g_ref[...]
    o_ref[...] = y.astype(o_ref.dtype)


def _pool_mixer(proj, pool_w, pool_scale, g_pool):
    B, L, _ = proj.shape
    ts = _tile(L, 512)
    col_block = (ATT_WIDTH + 2 * KV_WIDTH) // POOL_WIDTH
    assert col_block * POOL_WIDTH == ATT_WIDTH + 2 * KV_WIDTH
    row = pl.BlockSpec((1, POOL_WIDTH), lambda b, i: (0, 0))
    return pl.pallas_call(
        functools.partial(_pool_kernel, seq_len=L),
        out_shape=jax.ShapeDtypeStruct((B, L, POOL_WIDTH), BF16),
        grid=(B, L // ts),
        in_specs=_halo_specs(ts, L, POOL_WIDTH, col_block) + [
            pl.BlockSpec(pool_w.shape, lambda b, i: (0, 0, 0)), row, row],
        out_specs=pl.BlockSpec((None, ts, POOL_WIDTH), lambda b, i: (b, i, 0)),
        compiler_params=_cp(("parallel", "arbitrary")),
        name="pool_mixer",
    )(proj, proj, proj, pool_w, pool_scale.reshape(1, POOL_WIDTH), g_pool.reshape(1, POOL_WIDTH))


def _short_conv_kernel(*refs):
    halo_refs = refs[:9]
    w_ref, b_ref = refs[9:11]
    out_refs = refs[11:]
    ts = out_refs[0].shape[0]
    for part in range(HY_ORDER + 1):
        ext = _with_halo(*halo_refs[3 * part:3 * part + 3])
        cols = slice(part * HY_WIDTH, (part + 1) * HY_WIDTH)
        acc = b_ref[:, cols] + ext[HALO - 1:HALO - 1 + ts] * w_ref[0:1, cols]
        acc = acc + ext[HALO:HALO + ts] * w_ref[1:2, cols]
        acc = acc + ext[HALO + 1:HALO + 1 + ts] * w_ref[2:3, cols]
        out_refs[part][...] = acc


def _short_conv(proj, conv_w, conv_b):
    B, L, _ = proj.shape
    ts = _tile(L, 512)
    first = (ATT_WIDTH + 2 * KV_WIDTH + POOL_WIDTH) // HY_WIDTH
    assert first * HY_WIDTH == ATT_WIDTH + 2 * KV_WIDTH + POOL_WIDTH
    specs = []
    for part in range(HY_ORDER + 1):
        specs += _halo_specs(ts, L, HY_WIDTH, first + part)
    n_col = (HY_ORDER + 1) * HY_WIDTH
    out = pl.BlockSpec((None, ts, HY_WIDTH), lambda b, i: (b, i, 0))
    return pl.pallas_call(
        _short_conv_kernel,
        out_shape=tuple(jax.ShapeDtypeStruct((B, L, HY_WIDTH), F32) for _ in range(HY_ORDER + 1)),
        grid=(B, L // ts),
        in_specs=specs + [pl.BlockSpec((HY_SHORT, n_col), lambda b, i: (0, 0)),
                          pl.BlockSpec((1, n_col), lambda b, i: (0, 0))],
        out_specs=(out,) * (HY_ORDER + 1),
        compiler_params=_cp(("parallel", "arbitrary")),
        name="hyena_short_conv",
    )(*([proj] * 9), conv_w, conv_b.reshape(1, n_col))


def _filter_tables(L):
    m = np.arange(L, dtype=np.float32)
    t = (m / np.float32(max(L - 1, 1))).astype(np.float32)
    w = (np.float32(2.0 * math.pi) * m / np.float32(L)).astype(np.float32)
    f = np.linspace(1e-4, HY_BANDS - 1, HY_BANDS, dtype=np.float32)
    ang = (w[:, None] * f).astype(np.float64)
    z = np.concatenate([t[:, None].astype(np.float64), np.cos(ang), -np.sin(ang)], axis=-1)
    z = np.pad(z, ((0, 0), (0, LANES - HY_EMB)))
    max_decay = math.log(HY_TARGET) / HY_SHORT_DECAY_PCT
    min_decay = math.log(HY_TARGET) / HY_LONG_DECAY_PCT
    deltas = np.linspace(min_decay, max_decay, HY_WIDTH, dtype=np.float32)
    decay = np.exp(-t[:, None].astype(np.float64) * np.abs(deltas)[None].astype(np.float64))
    return jnp.asarray(z, F32), jnp.asarray(decay, F32)


def _filter_kernel(z_ref, decay_ref, w1_ref, b1_ref, w2_ref, b2_ref, w3_ref, b3_ref, fr_ref, o_ref):
    tl = z_ref.shape[0]
    fr = fr_ref[...]
    a = jnp.sin(fr * (_dot3(z_ref[...], w1_ref[...]) + b1_ref[...]))
    a = jnp.sin(fr * (_dot3(a, w2_ref[...]) + b2_ref[...]))
    h = _dot3(a, w3_ref[...]) + b3_ref[...]
    h = h * jnp.tile(decay_ref[...], (1, 2 * HY_ORDER))
    row = pl.program_id(0) * tl + lax.broadcasted_iota(jnp.int32, h.shape, 0)
    col = lax.broadcasted_iota(jnp.int32, h.shape, 1)
    backward = (col // HY_WIDTH) % 2 == 1
    o_ref[...] = jnp.where(backward & (row == 0), 0.0, h)


def _hyena_filter_taps(L, w1, b1, w2, b2, w3, b3, freq):
    z, decay = _filter_tables(L)
    hid = w1.shape[1]
    n_out = w3.shape[1]
    tl = _tile(L, 512)
    w1p = jnp.pad(w1, ((0, LANES - HY_EMB), (0, 0)))
    full = lambda a: pl.BlockSpec(a.shape, lambda i: (0,) * a.ndim)
    args = [w1p, b1.reshape(1, hid), w2, b2.reshape(1, hid), w3, b3.reshape(1, n_out),
            freq.reshape(1, hid)]
    return pl.pallas_call(
        _filter_kernel,
        out_shape=jax.ShapeDtypeStruct((L, n_out), F32),
        grid=(L // tl,),
        in_specs=[pl.BlockSpec((tl, LANES), lambda i: (i, 0)),
                  pl.BlockSpec((tl, HY_WIDTH), lambda i: (i, 0))] + [full(a) for a in args],
        out_specs=pl.BlockSpec((tl, n_out), lambda i: (i, 0)),
        compiler_params=_cp(("arbitrary",)),
        name="hyena_filter_taps",
    )(z, decay, *args)


def _fft_matrices(n1, a_in):
    n = n1 * LANES
    k1 = np.arange(n1)
    a = np.arange(a_in)
    th1 = 2.0 * np.pi * np.outer(k1, a) / n1
    c1, s1 = np.cos(th1), np.sin(th1)
    m1_complex = np.block([[c1, s1], [-s1, c1]])
    m1_real = np.concatenate([c1, -s1], axis=0)
    m3 = np.block([[c1.T, -s1.T], [s1.T, c1.T]])
    b = np.arange(LANES)
    k2 = np.arange(LANES)
    k = k1[:, None, None] + n1 * k2[None, :, None]
    th2 = 2.0 * np.pi * (k * b[None, None, :] % n) / n
    c2, s2 = np.cos(th2), np.sin(th2)
    fwd = np.concatenate([np.concatenate([c2, s2], axis=2),
                          np.concatenate([-s2, c2], axis=2)], axis=1)
    c2t, s2t = np.swapaxes(c2, 1, 2), np.swapaxes(s2, 1, 2)
    inv = np.concatenate([np.concatenate([c2t, -s2t], axis=2),
                          np.concatenate([s2t, c2t], axis=2)], axis=1)
    as_bf16 = lambda m: jnp.asarray(m, F32).astype(BF16)
    return as_bf16(m1_complex), as_bf16(m1_real), as_bf16(m3), as_bf16(fwd), as_bf16(inv)


def _fft_stage1_kernel(x_ref, m_ref, o_ref, *, complex_in):
    n1 = o_ref.shape[1]
    for s in range(o_ref.shape[2]):
        if complex_in:
            x = jnp.concatenate([x_ref[0, :, s, :], x_ref[1, :, s, :]], axis=0)
        else:
            x = x_ref[:, s, :]
        y = _dot(m_ref[...], x.astype(BF16))
        o_ref[0, :, s, :] = y[:n1]
        o_ref[1, :, s, :] = y[n1:]


def _fft_stage1(x, m1, n1, complex_in):
    P, A, Wt = x.shape[0], x.shape[-3], x.shape[-1]
    tw = _tile(Wt, 512, LANES)
    if complex_in:
        xspec = pl.BlockSpec((None, 2, A, FFT_ROWS, tw), lambda p, i, j: (p, 0, 0, i, j))
    else:
        xspec = pl.BlockSpec((None, A, FFT_ROWS, tw), lambda p, i, j: (p, 0, i, j))
    return pl.pallas_call(
        functools.partial(_fft_stage1_kernel, complex_in=complex_in),
        out_shape=jax.ShapeDtypeStruct((P, 2, n1, LANES, Wt), F32),
        grid=(P, LANES // FFT_ROWS, Wt // tw),
        in_specs=[xspec, pl.BlockSpec(m1.shape, lambda p, i, j: (0, 0))],
        out_specs=pl.BlockSpec((None, 2, n1, FFT_ROWS, tw), lambda p, i, j: (p, 0, 0, i, j)),
        compiler_params=_cp(("parallel", "arbitrary", "arbitrary")),
        name="fft_stage1",
    )(x, m1)


def _filter_spectrum_kernel(y_ref, g_ref, o0_ref, o1_ref, *, scale):
    kt = y_ref.shape[1]
    for t in range(kt):
        y = jnp.concatenate([y_ref[0, t], y_ref[1, t]], axis=0).astype(BF16)
        f = _dot(g_ref[t], y)
        fr, fi = f[:LANES], f[LANES:]
        for o, o_ref in enumerate((o0_ref, o1_ref)):
            fw = slice((2 * o) * HY_WIDTH, (2 * o + 1) * HY_WIDTH)
            bw = slice((2 * o + 1) * HY_WIDTH, (2 * o + 2) * HY_WIDTH)
            o_ref[0, t] = (fr[:, fw] + fr[:, bw]) * scale
            o_ref[1, t] = (fi[:, fw] - fi[:, bw]) * scale


def _filter_spectrum(taps, mats, n1):
    L, n_col = taps.shape
    a_in = L // LANES
    _, m1_real, _, g_fwd, _ = mats
    y = _fft_stage1(taps.reshape(1, a_in, LANES, n_col), m1_real, n1, complex_in=False)
    kt = _tile(n1, 4, 1)
    out = jax.ShapeDtypeStruct((2, n1, LANES, HY_WIDTH), F32)
    ospec = pl.BlockSpec((2, kt, LANES, HY_WIDTH), lambda i: (0, i, 0, 0))
    return pl.pallas_call(
        functools.partial(_filter_spectrum_kernel, scale=1.0 / (n1 * LANES)),
        out_shape=(out, out),
        grid=(n1 // kt,),
        in_specs=[pl.BlockSpec((None, 2, kt, LANES, n_col), lambda i: (0, 0, i, 0, 0)),
                  pl.BlockSpec((kt, 2 * LANES, 2 * LANES), lambda i: (i, 0, 0))],
        out_specs=(ospec, ospec),
        compiler_params=_cp(("arbitrary",)),
        name="hyena_filter_spectrum",
    )(y, g_fwd)


def _fft_mid_kernel(y_ref, kf_ref, g_ref, gi_ref, o_ref):
    kt = y_ref.shape[1]
    for t in range(kt):
        y = jnp.concatenate([y_ref[0, t], y_ref[1, t]], axis=0).astype(BF16)
        f = _dot(g_ref[t], y)
        fr, fi = f[:LANES], f[LANES:]
        kr, ki = kf_ref[0, t], kf_ref[1, t]
        p = jnp.concatenate([fr * kr - fi * ki, fr * ki + fi * kr], axis=0).astype(BF16)
        u = _dot(gi_ref[t], p)
        o_ref[0, t] = u[:LANES]
        o_ref[1, t] = u[LANES:]


def _fft_mid(y, kf, g_fwd, g_inv):
    P, _, n1, _, W = y.shape
    kt = _tile(n1, 8, 1)
    yspec = pl.BlockSpec((None, 2, kt, LANES, W), lambda i, p: (p, 0, i, 0, 0))
    gspec = pl.BlockSpec((kt, 2 * LANES, 2 * LANES), lambda i, p: (i, 0, 0))
    return pl.pallas_call(
        _fft_mid_kernel,
        out_shape=jax.ShapeDtypeStruct(y.shape, F32),
        grid=(n1 // kt, P),
        in_specs=[yspec, pl.BlockSpec((2, kt, LANES, W), lambda i, p: (0, i, 0, 0)), gspec, gspec],
        out_specs=yspec,
        compiler_params=_cp(("arbitrary", "arbitrary")),
        name="fft_mid",
    )(y, kf, g_fwd, g_inv)


def _fft_stage3_kernel(u_ref, m_ref, z_ref, x_ref, bias_ref, g_ref, o_ref, *, normalise):
    a_out = z_ref.shape[1]
    bias = bias_ref[...]
    for s in range(o_ref.shape[2]):
        u = jnp.concatenate([u_ref[0, :, s, :], u_ref[1, :, s, :]], axis=0).astype(BF16)
        conv = _dot(m_ref[...], u)
        for part in range(2):
            y = x_ref[part, :, s, :] * (
                conv[part * a_out:(part + 1) * a_out] + z_ref[part, :, s, :] * bias)
            if normalise:
                y = y * lax.rsqrt(jnp.mean(y * y, axis=-1, keepdims=True) + EPS) * g_ref[...]
            o_ref[part, :, s, :] = y


def _fft_stage3(u, m3, z, gate, bias_row, norm_g=None):
    P, _, n1, _, W = u.shape
    A = z.shape[2]
    pair = pl.BlockSpec((None, 2, A, FFT_ROWS, W), lambda p, i: (p, 0, 0, i, 0))
    row = pl.BlockSpec((1, W), lambda p, i: (0, 0))
    normalise = norm_g is not None
    return pl.pallas_call(
        functools.partial(_fft_stage3_kernel, normalise=normalise),
        out_shape=jax.ShapeDtypeStruct(z.shape, F32),
        grid=(P, LANES // FFT_ROWS),
        in_specs=[pl.BlockSpec((None, 2, n1, FFT_ROWS, W), lambda p, i: (p, 0, 0, i, 0)),
                  pl.BlockSpec(m3.shape, lambda p, i: (0, 0)),
                  pair, pair, row, row],
        out_specs=pair,
        compiler_params=_cp(("parallel", "arbitrary")),
        name="fft_stage3",
    )(u, m3, z, gate, bias_row, norm_g.reshape(1, W) if normalise else bias_row)


def _hyena_mixer(v, gates, kfs, d_bias, norm_g, mats, n1):
    B, L, W = v.shape
    pair_shape = (B // 2, 2, L // LANES, LANES, W)
    m1_complex, _, m3, g_fwd, g_inv = mats
    z = v.reshape(pair_shape)
    for o in range(HY_ORDER):
        y = _fft_stage1(z, m1_complex, n1, complex_in=True)
        u = _fft_mid(y, kfs[o], g_fwd, g_inv)
        z = _fft_stage3(u, m3, z, gates[o].reshape(pair_shape), d_bias[o].reshape(1, W),
                        norm_g if o == HY_ORDER - 1 else None)
    return z.reshape(B, L, W)


def kernel(x, c, ctx, c_ctx, w_mod, b_mod, norm1_g, w_in, attn_sink, pool_w, pool_scale,
           hy_conv_w, hy_conv_b, hy_f_w1, hy_f_b1, hy_f_w2, hy_f_b2, hy_f_w3, hy_f_b3,
           hy_f_freq, hy_bias, g_attn, g_pool, g_hyena, w_out, norm2_g,
           ff_w1, ff_w3, ff_w2, router_w, moe_w1, moe_w3, moe_w2, final_g):
    B, S, D = x.shape
    C = ctx.shape[1]
    depth = w_mod.shape[0]
    assert B % 2 == 0 and S % BLOCK == 0
    fft_mats = {}

    c_rows = jnp.concatenate([c, c_ctx[None], jnp.zeros((SUBLANES - B - 1, D), F32)], axis=0)
    xc = ctx

    def hyena_branch(l, proj, seq_len):
        rows = -(-seq_len // FFT_MIN_ROWS) * FFT_MIN_ROWS
        n1 = 2 * rows // LANES
        if rows not in fft_mats:
            fft_mats[rows] = _fft_matrices(n1, rows // LANES)
        mats = fft_mats[rows]
        taps = _hyena_filter_taps(seq_len, hy_f_w1[l], hy_f_b1[l], hy_f_w2[l], hy_f_b2[l],
                                  hy_f_w3[l], hy_f_b3[l], hy_f_freq[l])
        v, x1, x2 = _short_conv(proj, hy_conv_w[l], hy_conv_b[l])
        if seq_len < rows:
            pad = lambda t: jnp.pad(t, ((0, 0), (0, rows - seq_len), (0, 0)))
            taps = jnp.pad(taps, ((0, rows - seq_len), (0, 0)))
            v, x1, x2 = pad(v), pad(x1), pad(x2)
        kfs = _filter_spectrum(taps, mats, n1)
        z = _hyena_mixer(v, (x1, x2), kfs, hy_bias[l], g_hyena[l], mats, n1)
        return z[:, :seq_len]

    def mix(l, proj, y_att, res, gate, w_out_l, seq_len):
        y_pool = _pool_mixer(proj, pool_w[l], pool_scale[l], g_pool[l])
        y_hy = hyena_branch(l, proj, seq_len)
        return _matmul_gated_residual([y_att, y_pool, y_hy], w_out_l, res, gate)

    for l in range(depth):
        last = l == depth - 1
        mod = _adaln(c_rows, w_mod[l], b_mod[l])
        sh1, sc1, g1, sh2, sc2, g2 = [mod[:B, j * D:(j + 1) * D] for j in range(6)]
        csh1, csc1, cg1, csh2, csc2, cg2 = [mod[B:B + 1, j * D:(j + 1) * D] for j in range(6)]
        w_in_l = w_in[l].astype(BF16)
        w_out_l = w_out[l].astype(BF16)

        hc = _norm_mod(xc, norm1_g[l], csh1, csc1)
        if last:
            kv_w = w_in_l[:, ATT_WIDTH:ATT_WIDTH + 2 * KV_WIDTH]
            kv_c = _matmul(hc, kv_w, out_dtype=BF16)
            k_c, v_c = kv_c[..., :KV_WIDTH], kv_c[..., KV_WIDTH:]
        else:
            proj_c = _matmul(hc, w_in_l)
            q_c, k_c, v_c = _qkv_prepare(proj_c, rope=False)
            y_att_c = _attention(q_c, None, None, k_c, v_c, attn_sink[l], g_attn[l])
            xc_new = mix(l, proj_c, y_att_c, xc, cg1, w_out_l, C)
            hc2 = _norm_mod(xc_new, norm2_g[l], csh2, csc2)
            i = l // 2
            if l % 2 == 0:
                hid_c = _swiglu_hidden(hc2, ff_w1[i].astype(BF16), ff_w3[i].astype(BF16))
                xc_new = _matmul_gated_residual([hid_c], ff_w2[i].astype(BF16), xc_new, cg2)
            else:
                raise NotImplementedError("context tokens through an expert layer")

        h = _norm_mod(x, norm1_g[l], sh1, sc1)
        proj = _matmul(h, w_in_l)
        q, k, v = _qkv_prepare(proj, rope=True)
        y_att = _attention(q, k, v, k_c, v_c, attn_sink[l], g_attn[l])
        x = mix(l, proj, y_att, x, g1, w_out_l, S)

        i = l // 2
        if l % 2 == 0:
            h2 = _norm_mod(x, norm2_g[l], sh2, sc2)
            hid = _swiglu_hidden(h2, ff_w1[i].astype(BF16), ff_w3[i].astype(BF16))
            x = _matmul_gated_residual([hid], ff_w2[i].astype(BF16), x, g2)
            if last:
                x = _rmsnorm(x, final_g, F32)
        else:
            h2, gi, gw, counts = _norm_mod_router(x, norm2_g[l], sh2, sc2, router_w[i])
            assert last
            x = _moe_layer(x, h2, gi, gw, counts, g2, final_g, moe_w1[i], moe_w3[i], moe_w2[i])
        if not last:
            xc = xc_new
    return x
```

```python
import functools
import math

import numpy as np
import jax
import jax.numpy as jnp
from jax import lax
from jax.experimental import pallas as pl
from jax.experimental.pallas import tpu as pltpu

F32 = jnp.float32
BF16 = jnp.bfloat16

EPS = 1e-6
NEG = -1e30
GRID_W = 64
ATT_HEADS = 8
ATT_KV_HEADS = 2
ATT_GROUP = ATT_HEADS // ATT_KV_HEADS
HEAD_DIM = 128
ATT_WIDTH = ATT_HEADS * HEAD_DIM
KV_WIDTH = ATT_KV_HEADS * HEAD_DIM
WINDOW = 128
BLOCK = 128
ROPE_BASE = 10000.0
POOL_WINDOWS = (2, 4, 8, 16)
POOL_GROUP = 128
POOL_WIDTH = POOL_GROUP * len(POOL_WINDOWS)
HY_WIDTH = 512
HY_ORDER = 2
HY_SHORT = 3
HY_EMB = 33
HY_BANDS = (HY_EMB - 1) // 2
HY_SHORT_DECAY_PCT = 0.3
HY_LONG_DECAY_PCT = 1.5
HY_TARGET = 1e-2
N_EXPERTS = 8
TOP_K = 2

LANES = 128
SUBLANES = 8
HALO = SUBLANES
FFT_MIN_ROWS = 2048
FFT_ROWS = SUBLANES
MOE_OUT_K_CHUNKS = 4
VMEM_LIMIT = 56 * 1024 * 1024


def _cp(sem):
    return pltpu.CompilerParams(dimension_semantics=sem, vmem_limit_bytes=VMEM_LIMIT)


def _tile(n, pref, mult=SUBLANES):
    if n <= pref:
        return n
    t = (pref // mult) * mult
    while t >= mult:
        if n % t == 0:
            return t
        t -= mult
    return n


def _split_bf16(x):
    hi = x.astype(BF16)
    lo = (x - hi.astype(F32)).astype(BF16)
    return hi, lo


def _dot(a, b):
    return jnp.dot(a, b, preferred_element_type=F32)


def _pack_bf16_pairs(x):
    n = x.shape[-1] // 2
    bits = lax.bitcast_convert_type(x.astype(BF16).astype(F32), jnp.uint32)
    return (bits[:, :n] >> 16) | (bits[:, n:] & jnp.uint32(0xFFFF0000))


def _unpack_bf16_pairs(u):
    lo = lax.bitcast_convert_type(u << 16, F32).astype(BF16)
    hi = lax.bitcast_convert_type(u & jnp.uint32(0xFFFF0000), F32).astype(BF16)
    return lo, hi


def _dot3(a, b):
    ah, al = _split_bf16(a)
    bh, bl = _split_bf16(b)
    return _dot(ah, bh) + (_dot(ah, bl) + _dot(al, bh))


def _adaln_kernel(c_ref, w_ref, b_ref, o_ref):
    c = c_ref[...]
    a = c * (1.0 / (1.0 + jnp.exp(-c)))
    o_ref[...] = _dot3(a, w_ref[...]) + b_ref[...]


def _adaln(c_rows, w, b):
    R, D = c_rows.shape
    N = w.shape[1]
    tn = _tile(N, 1536, LANES)
    return pl.pallas_call(
        _adaln_kernel,
        out_shape=jax.ShapeDtypeStruct((R, N), F32),
        grid=(N // tn,),
        in_specs=[pl.BlockSpec((R, D), lambda j: (0, 0)),
                  pl.BlockSpec((D, tn), lambda j: (0, j)),
                  pl.BlockSpec((1, tn), lambda j: (0, j))],
        out_specs=pl.BlockSpec((R, tn), lambda j: (0, j)),
        compiler_params=_cp(("arbitrary",)),
        name="adaln",
    )(c_rows, w, b.reshape(1, N))


def _norm_mod_kernel(x_ref, g_ref, sh_ref, sc_ref, o_ref):
    x = x_ref[...]
    y = x * lax.rsqrt(jnp.mean(x * x, axis=-1, keepdims=True) + EPS) * g_ref[...]
    o_ref[...] = (y * (1.0 + sc_ref[...]) + sh_ref[...]).astype(o_ref.dtype)


def _bcast_map(arr):
    if arr.shape[0] == 1:
        return lambda b, i: (0, 0, 0)
    return lambda b, i: (b, 0, 0)


def _norm_mod(x, g, shift, scale):
    B, L, D = x.shape
    ts = _tile(L, 512)
    shift = shift[:, None, :]
    scale = scale[:, None, :]
    return pl.pallas_call(
        _norm_mod_kernel,
        out_shape=jax.ShapeDtypeStruct((B, L, D), BF16),
        grid=(B, L // ts),
        in_specs=[pl.BlockSpec((None, ts, D), lambda b, i: (b, i, 0)),
                  pl.BlockSpec((1, D), lambda b, i: (0, 0)),
                  pl.BlockSpec((None, 1, D), _bcast_map(shift)),
                  pl.BlockSpec((None, 1, D), _bcast_map(scale))],
        out_specs=pl.BlockSpec((None, ts, D), lambda b, i: (b, i, 0)),
        compiler_params=_cp(("parallel", "parallel")),
        name="norm_mod",
    )(x, g.reshape(1, D), shift, scale)


def _route(h, rw_ref, tri_ref, run_ref, gi_ref, gw_ref, cnt_ref):
    @pl.when((pl.program_id(0) == 0) & (pl.program_id(1) == 0))
    def _():
        run_ref[...] = jnp.zeros(run_ref.shape, run_ref.dtype)

    logits = _dot3(h, rw_ref[...])
    lane = lax.broadcasted_iota(jnp.int32, logits.shape, 1)
    logits = jnp.where(lane < N_EXPERTS, logits, NEG)
    m1 = jnp.max(logits, axis=-1, keepdims=True)
    i1 = jnp.min(jnp.where(logits == m1, lane, LANES), axis=-1, keepdims=True)
    rest = jnp.where(lane == i1, NEG, logits)
    m2 = jnp.max(rest, axis=-1, keepdims=True)
    i2 = jnp.min(jnp.where(rest == m2, lane, LANES), axis=-1, keepdims=True)
    e2 = jnp.exp(m2 - m1)
    w1 = 1.0 / (1.0 + e2)
    w2 = e2 * w1
    pick1 = lane == i1
    pick2 = lane == i2
    both = jnp.where(pick1 | pick2, 1.0, 0.0)
    before = _dot(tri_ref[...], both.astype(BF16)) + run_ref[...]
    r1 = jnp.sum(jnp.where(pick1, before, 0.0), axis=-1, keepdims=True).astype(jnp.int32)
    r2 = jnp.sum(jnp.where(pick2, before, 0.0), axis=-1, keepdims=True).astype(jnp.int32)
    run_ref[...] = run_ref[...] + jnp.sum(both, axis=0, keepdims=True)
    cnt_ref[...] = run_ref[...]
    gi_ref[...] = jnp.where(lane == 0, i1, jnp.where(lane == 1, i2,
                            jnp.where(lane == 2, r1, jnp.where(lane == 3, r2, 0))))
    gw_ref[...] = jnp.where(lane == 0, w1, jnp.where(lane == 1, w2, 0.0))


def _norm_mod_router_kernel(x_ref, g_ref, sh_ref, sc_ref, rw_ref, tri_ref,
                            o_ref, gi_ref, gw_ref, cnt_ref, run_ref):
    x = x_ref[...]
    y = x * lax.rsqrt(jnp.mean(x * x, axis=-1, keepdims=True) + EPS) * g_ref[...]
    h = y * (1.0 + sc_ref[...]) + sh_ref[...]
    o_ref[...] = _pack_bf16_pairs(h)
    _route(h, rw_ref, tri_ref, run_ref, gi_ref, gw_ref, cnt_ref)


def _norm_mod_router(x, g, shift, scale, router_w):
    B, L, D = x.shape
    ts = _tile(L, 512)
    shift = shift[:, None, :]
    scale = scale[:, None, :]
    rw = jnp.pad(router_w, ((0, 0), (0, LANES - router_w.shape[1])))
    tri = jnp.asarray(np.tril(np.ones((ts, ts), np.float32), -1), BF16)
    row = pl.BlockSpec((None, ts, D), lambda b, i: (b, i, 0))
    small = pl.BlockSpec((None, ts, LANES), lambda b, i: (b, i, 0))
    return pl.pallas_call(
        _norm_mod_router_kernel,
        out_shape=(jax.ShapeDtypeStruct((B, L, D // 2), jnp.uint32),
                   jax.ShapeDtypeStruct((B, L, LANES), jnp.int32),
                   jax.ShapeDtypeStruct((B, L, LANES), F32),
                   jax.ShapeDtypeStruct((1, LANES), F32)),
        grid=(B, L // ts),
        in_specs=[row,
                  pl.BlockSpec((1, D), lambda b, i: (0, 0)),
                  pl.BlockSpec((None, 1, D), _bcast_map(shift)),
                  pl.BlockSpec((None, 1, D), _bcast_map(scale)),
                  pl.BlockSpec((D, LANES), lambda b, i: (0, 0)),
                  pl.BlockSpec((ts, ts), lambda b, i: (0, 0))],
        out_specs=(pl.BlockSpec((None, ts, D // 2), lambda b, i: (b, i, 0)), small, small,
                   pl.BlockSpec((1, LANES), lambda b, i: (0, 0))),
        scratch_shapes=[pltpu.VMEM((1, LANES), F32)],
        compiler_params=_cp(("arbitrary", "arbitrary")),
        name="norm_mod_router",
    )(x, g.reshape(1, D), shift, scale, rw, tri)


def _mix_out_kernel(*refs, n_a, emit_h):
    a_refs = refs[:n_a]
    w_refs = refs[n_a:2 * n_a]
    res_ref, gate_ref = refs[2 * n_a:2 * n_a + 2]
    acc = _dot(a_refs[0][...].astype(BF16), w_refs[0][...])
    for a_ref, w_ref in zip(a_refs[1:], w_refs[1:]):
        acc = acc + _dot(a_ref[...].astype(BF16), w_ref[...])
    x = res_ref[...] + gate_ref[...] * acc
    if emit_h:
        g_ref, sh_ref, sc_ref, x_ref, h_ref = refs[2 * n_a + 2:]
        y = x * lax.rsqrt(jnp.mean(x * x, axis=-1, keepdims=True) + EPS) * g_ref[...]
        h_ref[...] = (y * (1.0 + sc_ref[...]) + sh_ref[...]).astype(h_ref.dtype)
    else:
        x_ref, = refs[2 * n_a + 2:]
    x_ref[...] = x


def _mix_out(a_list, w, res, gate, norm=None):
    B, L, D = res.shape
    tm = _tile(L, 512)
    emit_h = norm is not None
    gate = gate[:, None, :]
    in_specs = [pl.BlockSpec((None, tm, a.shape[-1]), lambda b, i: (b, i, 0)) for a in a_list]
    off = 0
    for a in a_list:
        k = a.shape[-1]
        assert off % k == 0
        in_specs.append(pl.BlockSpec((k, D), functools.partial(lambda b, i, blk: (blk, 0), blk=off // k)))
        off += k
    row = pl.BlockSpec((None, tm, D), lambda b, i: (b, i, 0))
    in_specs += [row, pl.BlockSpec((None, 1, D), _bcast_map(gate))]
    args = [*a_list, *([w] * len(a_list)), res, gate]
    out_shape = [jax.ShapeDtypeStruct((B, L, D), F32)]
    out_specs = [row]
    if emit_h:
        g, shift, scale = norm
        shift, scale = shift[:, None, :], scale[:, None, :]
        in_specs += [pl.BlockSpec((1, D), lambda b, i: (0, 0)),
                     pl.BlockSpec((None, 1, D), _bcast_map(shift)),
                     pl.BlockSpec((None, 1, D), _bcast_map(scale))]
        args += [g.reshape(1, D), shift, scale]
        out_shape.append(jax.ShapeDtypeStruct((B, L, D), BF16))
        out_specs.append(row)
    out = pl.pallas_call(
        functools.partial(_mix_out_kernel, n_a=len(a_list), emit_h=emit_h),
        out_shape=tuple(out_shape),
        grid=(B, L // tm),
        in_specs=in_specs,
        out_specs=tuple(out_specs),
        compiler_params=_cp(("parallel", "parallel")),
        name="mix_out",
    )(*args)
    return out if emit_h else out[0]


def _rmsnorm_kernel(x_ref, g_ref, o_ref):
    x = x_ref[...]
    y = x * lax.rsqrt(jnp.mean(x * x, axis=-1, keepdims=True) + EPS) * g_ref[...]
    o_ref[...] = y.astype(o_ref.dtype)


def _rmsnorm(x, g, out_dtype):
    B, L, D = x.shape
    ts = _tile(L, 512)
    return pl.pallas_call(
        _rmsnorm_kernel,
        out_shape=jax.ShapeDtypeStruct((B, L, D), out_dtype),
        grid=(B, L // ts),
        in_specs=[pl.BlockSpec((None, ts, D), lambda b, i: (b, i, 0)),
                  pl.BlockSpec((1, D), lambda b, i: (0, 0))],
        out_specs=pl.BlockSpec((None, ts, D), lambda b, i: (b, i, 0)),
        compiler_params=_cp(("parallel", "parallel")),
        name="rmsnorm",
    )(x, g.reshape(1, D))


def _mm_kernel(a_ref, w_ref, o_ref):
    o_ref[...] = _dot(a_ref[...], w_ref[...]).astype(o_ref.dtype)


def _matmul(a, w, out_dtype=F32, tm=1024, tn=512):
    B, L, K = a.shape
    N = w.shape[1]
    tm = _tile(L, tm)
    tn = _tile(N, tn, LANES)
    return pl.pallas_call(
        _mm_kernel,
        out_shape=jax.ShapeDtypeStruct((B, L, N), out_dtype),
        grid=(B, L // tm, N // tn),
        in_specs=[pl.BlockSpec((None, tm, K), lambda b, i, j: (b, i, 0)),
                  pl.BlockSpec((K, tn), lambda b, i, j: (0, j))],
        out_specs=pl.BlockSpec((None, tm, tn), lambda b, i, j: (b, i, j)),
        compiler_params=_cp(("parallel", "parallel", "arbitrary")),
        name="matmul",
    )(a, w)


def _mm_res_kernel(*refs, n_a):
    a_refs = refs[:n_a]
    w_refs = refs[n_a:2 * n_a]
    res_ref, gate_ref, o_ref = refs[2 * n_a:]
    acc = _dot(a_refs[0][...].astype(BF16), w_refs[0][...])
    for a_ref, w_ref in zip(a_refs[1:], w_refs[1:]):
        acc = acc + _dot(a_ref[...].astype(BF16), w_ref[...])
    o_ref[...] = res_ref[...] + gate_ref[...] * acc


def _matmul_gated_residual(a_list, w, res, gate, tm=1024, tn=512):
    B, L, N = res.shape
    tm = _tile(L, tm)
    tn = _tile(N, tn, LANES)
    gate = gate[:, None, :]
    widths = [a.shape[-1] for a in a_list]
    unit = math.gcd(*widths) if len(widths) > 1 else widths[0]
    in_specs = [pl.BlockSpec((None, tm, k), lambda b, i, j: (b, i, 0)) for k in widths]
    off = 0
    for k in widths:
        assert off % k == 0 and k % unit == 0
        in_specs.append(pl.BlockSpec((k, tn), functools.partial(
            lambda b, i, j, blk: (blk, j), blk=off // k)))
        off += k
    gmap = (lambda b, i, j: (0, 0, j)) if gate.shape[0] == 1 else (lambda b, i, j: (b, 0, j))
    in_specs += [pl.BlockSpec((None, tm, tn), lambda b, i, j: (b, i, j)),
                 pl.BlockSpec((None, 1, tn), gmap)]
    return pl.pallas_call(
        functools.partial(_mm_res_kernel, n_a=len(a_list)),
        out_shape=jax.ShapeDtypeStruct((B, L, N), F32),
        grid=(B, L // tm, N // tn),
        in_specs=in_specs,
        out_specs=pl.BlockSpec((None, tm, tn), lambda b, i, j: (b, i, j)),
        compiler_params=_cp(("parallel", "parallel", "arbitrary")),
        name="matmul_gated_residual",
    )(*a_list, *([w] * len(a_list)), res, gate)


def _silu(x):
    return x * (1.0 / (1.0 + jnp.exp(-x)))


def _swiglu_kernel(a_ref, w1_ref, w3_ref, o_ref):
    a = a_ref[...]
    o_ref[...] = (_silu(_dot(a, w1_ref[...])) * _dot(a, w3_ref[...])).astype(o_ref.dtype)


def _swiglu_hidden(a, w1, w3, tm=1024, tf=512):
    B, L, D = a.shape
    F = w1.shape[1]
    tm = _tile(L, tm)
    tf = _tile(F, tf, LANES)
    wspec = pl.BlockSpec((D, tf), lambda b, i, j: (0, j))
    return pl.pallas_call(
        _swiglu_kernel,
        out_shape=jax.ShapeDtypeStruct((B, L, F), BF16),
        grid=(B, L // tm, F // tf),
        in_specs=[pl.BlockSpec((None, tm, D), lambda b, i, j: (b, i, 0)), wspec, wspec],
        out_specs=pl.BlockSpec((None, tm, tf), lambda b, i, j: (b, i, j)),
        compiler_params=_cp(("parallel", "parallel", "arbitrary")),
        name="swiglu_hidden",
    )(a, w1, w3)


def _moe_hidden_kernel(te_ref, nt_ref, a_ref, w1_ref, w3_ref, o_ref, w1b_ref, w3b_ref):
    i = pl.program_id(1)
    used = i < nt_ref[0]
    half = a_ref.shape[1]
    new_weights = (i == 0) | (te_ref[i] != te_ref[jnp.maximum(i - 1, 0)])

    @pl.when(used & new_weights)
    def _():
        w1b_ref[...] = w1_ref[...].astype(BF16)
        w3b_ref[...] = w3_ref[...].astype(BF16)

    @pl.when(used)
    def _():
        lo, hi = _unpack_bf16_pairs(a_ref[...])
        gate = _dot(lo, w1b_ref[:half, :]) + _dot(hi, w1b_ref[half:, :])
        up = _dot(lo, w3b_ref[:half, :]) + _dot(hi, w3b_ref[half:, :])
        o_ref[...] = (_silu(gate) * up).astype(o_ref.dtype)

    @pl.when(jnp.logical_not(used))
    def _():
        o_ref[...] = jnp.zeros(o_ref.shape, o_ref.dtype)


def _moe_out_kernel(te_ref, nt_ref, a_ref, w2_ref, o_ref):
    used = pl.program_id(0) < nt_ref[0]

    @pl.when(used)
    def _():
        kc = a_ref.shape[1] // MOE_OUT_K_CHUNKS
        acc = _dot(a_ref[:, :kc], w2_ref[:kc, :].astype(BF16))
        for c in range(1, MOE_OUT_K_CHUNKS):
            acc = acc + _dot(a_ref[:, c * kc:(c + 1) * kc], w2_ref[c * kc:(c + 1) * kc, :].astype(BF16))
        o_ref[...] = _pack_bf16_pairs(acc)

    @pl.when(jnp.logical_not(used))
    def _():
        o_ref[...] = jnp.zeros(o_ref.shape, o_ref.dtype)


def _moe_out_cols(d_model):
    return _tile(d_model, 512, 2 * LANES)


def _moe_experts(a_sorted, tile_expert, n_tiles_used, w1, w3, w2, tm, tf=512):
    R = a_sorted.shape[0]
    E, D, F = w1.shape
    tf = _tile(F, tf, LANES)
    tn = _moe_out_cols(D)
    nt = R // tm
    w13 = pl.BlockSpec((None, D, tf), lambda j, i, te, n: (te[i], 0, j))
    hidden = pl.pallas_call(
        _moe_hidden_kernel,
        out_shape=jax.ShapeDtypeStruct((R, F), BF16),
        grid_spec=pltpu.PrefetchScalarGridSpec(
            num_scalar_prefetch=2,
            grid=(F // tf, nt),
            in_specs=[pl.BlockSpec((tm, D // 2), lambda j, i, te, n: (i, 0)), w13, w13],
            out_specs=pl.BlockSpec((tm, tf), lambda j, i, te, n: (i, j)),
            scratch_shapes=[pltpu.VMEM((D, tf), BF16), pltpu.VMEM((D, tf), BF16)]),
        compiler_params=_cp(("arbitrary", "arbitrary")),
        name="moe_hidden",
    )(tile_expert, n_tiles_used, a_sorted, w1, w3)
    return pl.pallas_call(
        _moe_out_kernel,
        out_shape=jax.ShapeDtypeStruct((R, D // 2), jnp.uint32),
        grid_spec=pltpu.PrefetchScalarGridSpec(
            num_scalar_prefetch=2,
            grid=(nt, D // tn),
            in_specs=[pl.BlockSpec((tm, F), lambda i, j, te, n: (i, 0)),
                      pl.BlockSpec((None, F, tn), lambda i, j, te, n: (te[i], 0, j))],
            out_specs=pl.BlockSpec((tm, tn // 2), lambda i, j, te, n: (i, j))),
        compiler_params=_cp(("arbitrary", "arbitrary")),
        name="moe_out",
    )(tile_expert, n_tiles_used, hidden, w2)


def _unpack_column_blocks(u, block):
    parts = []
    for c in range(u.shape[1] // block):
        w = u[:, c * block:(c + 1) * block]
        parts.append(lax.bitcast_convert_type(w << 16, F32))
        parts.append(lax.bitcast_convert_type(w & jnp.uint32(0xFFFF0000), F32))
    return jnp.concatenate(parts, axis=1)


def _moe_combine_kernel(x_ref, ya_ref, yb_ref, gw_ref, gate_ref, g_ref, o_ref, *, block):
    gw = gw_ref[...]
    moe = (gw[:, 0:1] * _unpack_column_blocks(ya_ref[...], block)
           + gw[:, 1:2] * _unpack_column_blocks(yb_ref[...], block))
    x = x_ref[...] + gate_ref[...] * moe
    y = x * lax.rsqrt(jnp.mean(x * x, axis=-1, keepdims=True) + EPS) * g_ref[...]
    o_ref[...] = y


def _moe_combine_norm(x, ya, yb, gw, gate, g):
    B, L, D = x.shape
    ts = _tile(L, 512)
    row = pl.BlockSpec((None, ts, D), lambda b, i: (b, i, 0))
    packed = pl.BlockSpec((None, ts, D // 2), lambda b, i: (b, i, 0))
    return pl.pallas_call(
        functools.partial(_moe_combine_kernel, block=_moe_out_cols(D) // 2),
        out_shape=jax.ShapeDtypeStruct((B, L, D), F32),
        grid=(B, L // ts),
        in_specs=[row, packed, packed,
                  pl.BlockSpec((None, ts, LANES), lambda b, i: (b, i, 0)),
                  pl.BlockSpec((None, 1, D), lambda b, i: (b, 0, 0)),
                  pl.BlockSpec((1, D), lambda b, i: (0, 0))],
        out_specs=row,
        compiler_params=_cp(("parallel", "parallel")),
        name="moe_combine_norm",
    )(x, ya, yb, gw, gate[:, None, :], g.reshape(1, D))


def _moe_layer(x, h, gi, gw, counts, gate, final_g, w1, w3, w2):
    B, L, D = x.shape
    E = w1.shape[0]
    n_tok = B * L
    n_pair = n_tok * TOP_K
    tm = _tile(n_pair, 1024)
    nt = n_pair // tm + E
    e_pair = gi[..., :TOP_K].reshape(n_pair)
    rank = gi[..., TOP_K:2 * TOP_K].reshape(n_pair)
    counts = counts[0, :E].astype(jnp.int32)
    tiles_per = (counts + tm - 1) // tm
    tile_end = jnp.cumsum(tiles_per)
    tile_start = tile_end - tiles_per
    row_start = jnp.zeros((n_pair,), jnp.int32)
    for e in range(E):
        row_start = jnp.where(e_pair == e, tile_start[e] * tm, row_start)
    pos = row_start + rank
    tile_ids = jnp.arange(nt, dtype=jnp.int32)
    tile_expert = jnp.minimum(
        jnp.sum((tile_ids[:, None] >= tile_end[None, :]).astype(jnp.int32), axis=1), E - 1)
    n_used = tile_end[-1:].astype(jnp.int32)
    tok_pair = jnp.arange(n_pair, dtype=jnp.int32) // TOP_K
    src = (jnp.arange(nt * tm, dtype=jnp.int32) % n_tok).at[pos].set(
        tok_pair, unique_indices=True, mode="promise_in_bounds")
    gather_rows = lambda rows, idx: rows.at[idx].get(mode="promise_in_bounds")
    a_sorted = gather_rows(h.reshape(n_tok, h.shape[-1]), src)
    y_sorted = _moe_experts(a_sorted, tile_expert.astype(jnp.int32), n_used, w1, w3, w2, tm)
    pos2 = pos.reshape(n_tok, TOP_K)
    ya = gather_rows(y_sorted, pos2[:, 0]).reshape(B, L, D // 2)
    yb = gather_rows(y_sorted, pos2[:, 1]).reshape(B, L, D // 2)
    return _moe_combine_norm(x, ya, yb, gw, gate, final_g)


def _rope_tables(n_tokens):
    pos = np.arange(n_tokens)
    row = (pos // GRID_W).astype(np.float32)
    col = (pos % GRID_W).astype(np.float32)
    n_freq = HEAD_DIM // 4
    inv = (np.float32(ROPE_BASE) ** (-np.arange(n_freq, dtype=np.float32) / np.float32(n_freq)))
    ang_r = (row[:, None] * inv).astype(np.float64)
    ang_c = (col[:, None] * inv).astype(np.float64)
    cos = np.concatenate([np.cos(ang_r), np.cos(ang_r), np.cos(ang_c), np.cos(ang_c)], axis=1)
    sin = np.concatenate([-np.sin(ang_r), np.sin(ang_r), -np.sin(ang_c), np.sin(ang_c)], axis=1)
    return jnp.asarray(cos, F32), jnp.asarray(sin, F32)


def _swap_halves(x):
    n = x.shape[-1]
    quarter = HEAD_DIM // 4
    lane = lax.broadcasted_iota(jnp.int32, x.shape, 1)
    up = pltpu.roll(x, n - quarter, axis=1)
    down = pltpu.roll(x, quarter, axis=1)
    return jnp.where((lane & quarter) == 0, up, down)


def _qkv_kernel(q_ref, k_ref, v_ref, cos_ref, sin_ref, qo_ref, ko_ref, vo_ref, *, rope):
    q = q_ref[...]
    k = k_ref[...]
    if rope:
        cos = cos_ref[...]
        sin = sin_ref[...]
        q = q * jnp.tile(cos, (1, ATT_HEADS)) + _swap_halves(q) * jnp.tile(sin, (1, ATT_HEADS))
        k = k * jnp.tile(cos, (1, ATT_KV_HEADS)) + _swap_halves(k) * jnp.tile(sin, (1, ATT_KV_HEADS))
    qo_ref[...] = (q * (HEAD_DIM ** -0.5)).astype(BF16)
    ko_ref[...] = k.astype(BF16)
    vo_ref[...] = v_ref[...].astype(BF16)


def _qkv_prepare(proj, rope):
    B, L, _ = proj.shape
    ts = _tile(L, 512)
    cos, sin = _rope_tables(L)
    kvb = ATT_WIDTH // KV_WIDTH
    tab = pl.BlockSpec((ts, HEAD_DIM), lambda b, i: (i, 0))
    return pl.pallas_call(
        functools.partial(_qkv_kernel, rope=rope),
        out_shape=(jax.ShapeDtypeStruct((B, L, ATT_WIDTH), BF16),
                   jax.ShapeDtypeStruct((B, L, KV_WIDTH), BF16),
                   jax.ShapeDtypeStruct((B, L, KV_WIDTH), BF16)),
        grid=(B, L // ts),
        in_specs=[pl.BlockSpec((None, ts, ATT_WIDTH), lambda b, i: (b, i, 0)),
                  pl.BlockSpec((None, ts, KV_WIDTH), lambda b, i: (b, i, kvb)),
                  pl.BlockSpec((None, ts, KV_WIDTH), lambda b, i: (b, i, kvb + 1)),
                  tab, tab],
        out_specs=(pl.BlockSpec((None, ts, ATT_WIDTH), lambda b, i: (b, i, 0)),
                   pl.BlockSpec((None, ts, KV_WIDTH), lambda b, i: (b, i, 0)),
                   pl.BlockSpec((None, ts, KV_WIDTH), lambda b, i: (b, i, 0))),
        compiler_params=_cp(("parallel", "parallel")),
        name="qkv_prepare",
    )(proj, proj, proj, cos, sin)


def _nt_dot(a, b):
    return lax.dot_general(a, b, (((1,), (1,)), ((), ())), preferred_element_type=F32)


def _attn_bias_tables():
    T, G = BLOCK, ATT_GROUP
    qi = np.arange(G * T)[:, None] % T
    ki = np.arange(3 * T)[None, :]
    band = np.abs(ki - T - qi) <= WINDOW
    after_start = ki >= T
    before_end = ki < 2 * T
    masks = [band & after_start, band, band & before_end, band & after_start & before_end]
    return jnp.asarray(np.stack([np.where(m, 0.0, NEG) for m in masks]), F32)


def _attn_kernel(*refs, local, n_blocks):
    if local:
        (q_ref, kp_ref, kc_ref, kn_ref, vp_ref, vc_ref, vn_ref,
         kx_ref, vx_ref, bias_ref, sink_ref, g_ref, o_ref, acc_ref) = refs
    else:
        q_ref, kx_ref, vx_ref, sink_ref, g_ref, o_ref, acc_ref = refs
    T = BLOCK
    G = ATT_GROUP
    n_sub = q_ref.shape[0] // T
    for h in range(ATT_KV_HEADS):
        cols = slice(h * HEAD_DIM, (h + 1) * HEAD_DIM)
        sink = jnp.concatenate(
            [jnp.broadcast_to(sink_ref[:, (h * G + g) * HEAD_DIM:(h * G + g) * HEAD_DIM + 1], (T, 1))
             for g in range(G)], axis=0)
        kx = kx_ref[:, cols]
        vx = jnp.concatenate([vx_ref[:, cols], jnp.ones((kx_ref.shape[0], HEAD_DIM), BF16)], axis=1)
        if local:
            k_band = jnp.concatenate([kp_ref[:, cols], kc_ref[:, cols], kn_ref[:, cols]], axis=0)
            v_band = jnp.concatenate([vp_ref[:, cols], vc_ref[:, cols], vn_ref[:, cols]], axis=0)
            v_band = jnp.concatenate([v_band, jnp.ones(v_band.shape, BF16)], axis=1)
        for j in range(n_sub):
            rows = slice(j * T, (j + 1) * T)
            qs = jnp.concatenate(
                [q_ref[rows, (h * G + g) * HEAD_DIM:(h * G + g + 1) * HEAD_DIM] for g in range(G)],
                axis=0)
            s_ctx = _nt_dot(qs, kx)
            m = jnp.maximum(jnp.max(s_ctx, axis=-1, keepdims=True), sink)
            if local:
                blk = pl.program_id(1) * n_sub + j
                is_first = blk == 0
                is_last = blk == n_blocks - 1
                table = jnp.where(is_first, jnp.where(is_last, 3, 0), jnp.where(is_last, 2, 1))
                s_loc = _nt_dot(qs, k_band[j * T:(j + 3) * T]) + bias_ref[table]
                m = jnp.maximum(m, jnp.max(s_loc, axis=-1, keepdims=True))
            o = _dot(jnp.exp((s_ctx - m).astype(BF16)), vx)
            if local:
                o = o + _dot(jnp.exp((s_loc - m).astype(BF16)), v_band[j * T:(j + 3) * T])
            denom = o[:, HEAD_DIM:HEAD_DIM + 1] + jnp.exp(sink - m)
            o = o[:, :HEAD_DIM] * (1.0 / denom)
            for g in range(G):
                acc_ref[rows, (h * G + g) * HEAD_DIM:(h * G + g + 1) * HEAD_DIM] = o[g * T:(g + 1) * T]
    y = acc_ref[...]
    y = y * lax.rsqrt(jnp.mean(y * y, axis=-1, keepdims=True) + EPS) * g_ref[...]
    o_ref[...] = y.astype(o_ref.dtype)


def _attention(q, k, v, k_ctx, v_ctx, sink, g_attn):
    B, L, _ = q.shape
    C = k_ctx.shape[1]
    local = k is not None
    T = BLOCK
    nb = L // T
    n_sub = 2 if nb % 2 == 0 else 1
    sink_row = jnp.repeat(sink.astype(F32), HEAD_DIM).reshape(1, ATT_WIDTH)
    qspec = pl.BlockSpec((None, n_sub * T, ATT_WIDTH), lambda b, i: (b, i, 0))
    in_specs = [qspec]
    args = [q]
    cspec = pl.BlockSpec((None, C, KV_WIDTH), lambda b, i: (b, 0, 0))
    row = pl.BlockSpec((1, ATT_WIDTH), lambda b, i: (0, 0))
    if local:
        prev = pl.BlockSpec((None, T, KV_WIDTH), lambda b, i: (b, jnp.maximum(i * n_sub - 1, 0), 0))
        cur = pl.BlockSpec((None, n_sub * T, KV_WIDTH), lambda b, i: (b, i, 0))
        nxt = pl.BlockSpec((None, T, KV_WIDTH),
                           lambda b, i: (b, jnp.minimum((i + 1) * n_sub, nb - 1), 0))
        bias = _attn_bias_tables()
        in_specs += [prev, cur, nxt, prev, cur, nxt, cspec, cspec,
                     pl.BlockSpec(bias.shape, lambda b, i: (0, 0, 0))]
        args += [k, k, k, v, v, v, k_ctx, v_ctx, bias]
    else:
        in_specs += [cspec, cspec]
        args += [k_ctx, v_ctx]
    in_specs += [row, row]
    args += [sink_row, g_attn.reshape(1, ATT_WIDTH)]
    return pl.pallas_call(
        functools.partial(_attn_kernel, local=local, n_blocks=nb),
        out_shape=jax.ShapeDtypeStruct((B, L, ATT_WIDTH), BF16),
        grid=(B, nb // n_sub),
        in_specs=in_specs,
        out_specs=qspec,
        scratch_shapes=[pltpu.VMEM((n_sub * T, ATT_WIDTH), F32)],
        compiler_params=_cp(("parallel", "arbitrary")),
        name="attention_local" if local else "attention_context",
    )(*args)


def _halo_specs(ts, L, width, col_block):
    nb8 = L // HALO
    per = ts // HALO
    prev = pl.BlockSpec((None, HALO, width),
                        lambda b, i: (b, jnp.maximum(i * per - 1, 0), col_block))
    cur = pl.BlockSpec((None, ts, width), lambda b, i: (b, i, col_block))
    nxt = pl.BlockSpec((None, HALO, width),
                       lambda b, i: (b, jnp.minimum((i + 1) * per, nb8 - 1), col_block))
    return [prev, cur, nxt]


def _with_halo(prev_ref, cur_ref, next_ref):
    i = pl.program_id(1)
    last = pl.num_programs(1) - 1
    prev = jnp.where(i > 0, prev_ref[...], 0.0)
    nxt = jnp.where(i < last, next_ref[...], 0.0)
    return jnp.concatenate([prev, cur_ref[...], nxt], axis=0)


def _pool_kernel(prev_ref, cur_ref, next_ref, w_ref, scale_ref, g_ref, o_ref, *, seq_len):
    ts = cur_ref.shape[0]
    ext = _with_halo(prev_ref, cur_ref, next_ref)
    pos = pl.program_id(1) * ts + lax.broadcasted_iota(jnp.int32, (ts, 1), 0)
    outs = []
    for gidx, win in enumerate(POOL_WINDOWS):
        cols = slice(gidx * POOL_GROUP, (gidx + 1) * POOL_GROUP)
        run = ext[:, cols]
        step = 1
        while step < win:
            run = run[:run.shape[0] - step] + run[step:]
            step *= 2
        lo = HALO - win // 2
        total = run[lo:lo + ts]
        cnt = (jnp.minimum(pos + (win - win // 2), seq_len) - jnp.maximum(pos - win // 2, 0))
        mean = total * (1.0 / cnt.astype(F32))
        outs.append(_dot3(mean - cur_ref[:, cols], w_ref[gidx]))
    y = jnp.concatenate(outs, axis=-1) * scale_ref[...]
    y = y * lax.rsqrt(jnp.mean(y * y, axis=-1, keepdims=True) + EPS) * g_ref[...]
    o_ref[...] = y.astype(o_ref.dtype)


def _pool_mixer(proj, pool_w, pool_scale, g_pool):
    B, L, _ = proj.shape
    ts = _tile(L, 512)
    col_block = (ATT_WIDTH + 2 * KV_WIDTH) // POOL_WIDTH
    assert col_block * POOL_WIDTH == ATT_WIDTH + 2 * KV_WIDTH
    row = pl.BlockSpec((1, POOL_WIDTH), lambda b, i: (0, 0))
    return pl.pallas_call(
        functools.partial(_pool_kernel, seq_len=L),
        out_shape=jax.ShapeDtypeStruct((B, L, POOL_WIDTH), BF16),
        grid=(B, L // ts),
        in_specs=_halo_specs(ts, L, POOL_WIDTH, col_block) + [
            pl.BlockSpec(pool_w.shape, lambda b, i: (0, 0, 0)), row, row],
        out_specs=pl.BlockSpec((None, ts, POOL_WIDTH), lambda b, i: (b, i, 0)),
        compiler_params=_cp(("parallel", "arbitrary")),
        name="pool_mixer",
    )(proj, proj, proj, pool_w, pool_scale.reshape(1, POOL_WIDTH), g_pool.reshape(1, POOL_WIDTH))


def _short_conv_kernel(*refs):
    halo_refs = refs[:9]
    w_ref, b_ref = refs[9:11]
    out_refs = refs[11:]
    ts = out_refs[0].shape[0]
    for part in range(HY_ORDER + 1):
        ext = _with_halo(*halo_refs[3 * part:3 * part + 3])
        cols = slice(part * HY_WIDTH, (part + 1) * HY_WIDTH)
        acc = b_ref[:, cols] + ext[HALO - 1:HALO - 1 + ts] * w_ref[0:1, cols]
        acc = acc + ext[HALO:HALO + ts] * w_ref[1:2, cols]
        acc = acc + ext[HALO + 1:HALO + 1 + ts] * w_ref[2:3, cols]
        out_refs[part][...] = acc


def _short_conv(proj, conv_w, conv_b):
    B, L, _ = proj.shape
    ts = _tile(L, 512)
    first = (ATT_WIDTH + 2 * KV_WIDTH + POOL_WIDTH) // HY_WIDTH
    assert first * HY_WIDTH == ATT_WIDTH + 2 * KV_WIDTH + POOL_WIDTH
    specs = []
    for part in range(HY_ORDER + 1):
        specs += _halo_specs(ts, L, HY_WIDTH, first + part)
    n_col = (HY_ORDER + 1) * HY_WIDTH
    out = pl.BlockSpec((None, ts, HY_WIDTH), lambda b, i: (b, i, 0))
    return pl.pallas_call(
        _short_conv_kernel,
        out_shape=tuple(jax.ShapeDtypeStruct((B, L, HY_WIDTH), F32) for _ in range(HY_ORDER + 1)),
        grid=(B, L // ts),
        in_specs=specs + [pl.BlockSpec((HY_SHORT, n_col), lambda b, i: (0, 0)),
                          pl.BlockSpec((1, n_col), lambda b, i: (0, 0))],
        out_specs=(out,) * (HY_ORDER + 1),
        compiler_params=_cp(("parallel", "arbitrary")),
        name="hyena_short_conv",
    )(*([proj] * 9), conv_w, conv_b.reshape(1, n_col))


def _filter_tables(L):
    m = np.arange(L, dtype=np.float32)
    t = (m / np.float32(max(L - 1, 1))).astype(np.float32)
    w = (np.float32(2.0 * math.pi) * m / np.float32(L)).astype(np.float32)
    f = np.linspace(1e-4, HY_BANDS - 1, HY_BANDS, dtype=np.float32)
    ang = (w[:, None] * f).astype(np.float64)
    z = np.concatenate([t[:, None].astype(np.float64), np.cos(ang), -np.sin(ang)], axis=-1)
    z = np.pad(z, ((0, 0), (0, LANES - HY_EMB)))
    max_decay = math.log(HY_TARGET) / HY_SHORT_DECAY_PCT
    min_decay = math.log(HY_TARGET) / HY_LONG_DECAY_PCT
    deltas = np.linspace(min_decay, max_decay, HY_WIDTH, dtype=np.float32)
    decay = np.exp(-t[:, None].astype(np.float64) * np.abs(deltas)[None].astype(np.float64))
    return jnp.asarray(z, F32), jnp.asarray(decay, F32)


def _filter_kernel(z_ref, decay_ref, w1_ref, b1_ref, w2_ref, b2_ref, w3_ref, b3_ref, fr_ref, o_ref):
    tl = z_ref.shape[0]
    fr = fr_ref[...]
    a = jnp.sin(fr * (_dot3(z_ref[...], w1_ref[...]) + b1_ref[...]))
    a = jnp.sin(fr * (_dot3(a, w2_ref[...]) + b2_ref[...]))
    h = _dot3(a, w3_ref[...]) + b3_ref[...]
    h = h * jnp.tile(decay_ref[...], (1, 2 * HY_ORDER))
    row = pl.program_id(0) * tl + lax.broadcasted_iota(jnp.int32, h.shape, 0)
    col = lax.broadcasted_iota(jnp.int32, h.shape, 1)
    backward = (col // HY_WIDTH) % 2 == 1
    o_ref[...] = jnp.where(backward & (row == 0), 0.0, h)


def _hyena_filter_taps(L, w1, b1, w2, b2, w3, b3, freq):
    z, decay = _filter_tables(L)
    hid = w1.shape[1]
    n_out = w3.shape[1]
    tl = _tile(L, 512)
    w1p = jnp.pad(w1, ((0, LANES - HY_EMB), (0, 0)))
    full = lambda a: pl.BlockSpec(a.shape, lambda i: (0,) * a.ndim)
    args = [w1p, b1.reshape(1, hid), w2, b2.reshape(1, hid), w3, b3.reshape(1, n_out),
            freq.reshape(1, hid)]
    return pl.pallas_call(
        _filter_kernel,
        out_shape=jax.ShapeDtypeStruct((L, n_out), F32),
        grid=(L // tl,),
        in_specs=[pl.BlockSpec((tl, LANES), lambda i: (i, 0)),
                  pl.BlockSpec((tl, HY_WIDTH), lambda i: (i, 0))] + [full(a) for a in args],
        out_specs=pl.BlockSpec((tl, n_out), lambda i: (i, 0)),
        compiler_params=_cp(("arbitrary",)),
        name="hyena_filter_taps",
    )(z, decay, *args)


def _fft_matrices(n1, a_in):
    n = n1 * LANES
    k1 = np.arange(n1)
    a = np.arange(a_in)
    th1 = 2.0 * np.pi * np.outer(k1, a) / n1
    c1, s1 = np.cos(th1), np.sin(th1)
    m1_complex = np.block([[c1, s1], [-s1, c1]])
    m1_real = np.concatenate([c1, -s1], axis=0)
    m3 = np.block([[c1.T, -s1.T], [s1.T, c1.T]])
    b = np.arange(LANES)
    k2 = np.arange(LANES)
    k = k1[:, None, None] + n1 * k2[None, :, None]
    th2 = 2.0 * np.pi * (k * b[None, None, :] % n) / n
    c2, s2 = np.cos(th2), np.sin(th2)
    fwd = np.concatenate([np.concatenate([c2, s2], axis=2),
                          np.concatenate([-s2, c2], axis=2)], axis=1)
    c2t, s2t = np.swapaxes(c2, 1, 2), np.swapaxes(s2, 1, 2)
    inv = np.concatenate([np.concatenate([c2t, -s2t], axis=2),
                          np.concatenate([s2t, c2t], axis=2)], axis=1)
    as_bf16 = lambda m: jnp.asarray(m, F32).astype(BF16)
    return as_bf16(m1_complex), as_bf16(m1_real), as_bf16(m3), as_bf16(fwd), as_bf16(inv)


def _fft_stage1_kernel(x_ref, m_ref, o_ref, *, complex_in):
    n1 = o_ref.shape[1]
    for s in range(o_ref.shape[2]):
        if complex_in:
            x = jnp.concatenate([x_ref[0, :, s, :], x_ref[1, :, s, :]], axis=0)
        else:
            x = x_ref[:, s, :]
        y = _dot(m_ref[...], x.astype(BF16))
        o_ref[0, :, s, :] = y[:n1]
        o_ref[1, :, s, :] = y[n1:]


def _fft_stage1(x, m1, n1, complex_in):
    P, A, Wt = x.shape[0], x.shape[-3], x.shape[-1]
    tw = _tile(Wt, 512, LANES)
    if complex_in:
        xspec = pl.BlockSpec((None, 2, A, FFT_ROWS, tw), lambda p, i, j: (p, 0, 0, i, j))
    else:
        xspec = pl.BlockSpec((None, A, FFT_ROWS, tw), lambda p, i, j: (p, 0, i, j))
    return pl.pallas_call(
        functools.partial(_fft_stage1_kernel, complex_in=complex_in),
        out_shape=jax.ShapeDtypeStruct((P, 2, n1, LANES, Wt), F32),
        grid=(P, LANES // FFT_ROWS, Wt // tw),
        in_specs=[xspec, pl.BlockSpec(m1.shape, lambda p, i, j: (0, 0))],
        out_specs=pl.BlockSpec((None, 2, n1, FFT_ROWS, tw), lambda p, i, j: (p, 0, 0, i, j)),
        compiler_params=_cp(("parallel", "arbitrary", "arbitrary")),
        name="fft_stage1",
    )(x, m1)


def _filter_spectrum_kernel(y_ref, g_ref, o0_ref, o1_ref, *, scale):
    kt = y_ref.shape[1]
    for t in range(kt):
        y = jnp.concatenate([y_ref[0, t], y_ref[1, t]], axis=0).astype(BF16)
        f = _dot(g_ref[t], y)
        fr, fi = f[:LANES], f[LANES:]
        for o, o_ref in enumerate((o0_ref, o1_ref)):
            fw = slice((2 * o) * HY_WIDTH, (2 * o + 1) * HY_WIDTH)
            bw = slice((2 * o + 1) * HY_WIDTH, (2 * o + 2) * HY_WIDTH)
            o_ref[0, t] = (fr[:, fw] + fr[:, bw]) * scale
            o_ref[1, t] = (fi[:, fw] - fi[:, bw]) * scale


def _filter_spectrum(taps, mats, n1):
    L, n_col = taps.shape
    a_in = L // LANES
    _, m1_real, _, g_fwd, _ = mats
    y = _fft_stage1(taps.reshape(1, a_in, LANES, n_col), m1_real, n1, complex_in=False)
    kt = _tile(n1, 4, 1)
    out = jax.ShapeDtypeStruct((2, n1, LANES, HY_WIDTH), F32)
    ospec = pl.BlockSpec((2, kt, LANES, HY_WIDTH), lambda i: (0, i, 0, 0))
    return pl.pallas_call(
        functools.partial(_filter_spectrum_kernel, scale=1.0 / (n1 * LANES)),
        out_shape=(out, out),
        grid=(n1 // kt,),
        in_specs=[pl.BlockSpec((None, 2, kt, LANES, n_col), lambda i: (0, 0, i, 0, 0)),
                  pl.BlockSpec((kt, 2 * LANES, 2 * LANES), lambda i: (i, 0, 0))],
        out_specs=(ospec, ospec),
        compiler_params=_cp(("arbitrary",)),
        name="hyena_filter_spectrum",
    )(y, g_fwd)


def _fft_mid_kernel(y_ref, kf_ref, g_ref, gi_ref, o_ref):
    kt = y_ref.shape[1]
    for t in range(kt):
        y = jnp.concatenate([y_ref[0, t], y_ref[1, t]], axis=0).astype(BF16)
        f = _dot(g_ref[t], y)
        fr, fi = f[:LANES], f[LANES:]
        kr, ki = kf_ref[0, t], kf_ref[1, t]
        p = jnp.concatenate([fr * kr - fi * ki, fr * ki + fi * kr], axis=0).astype(BF16)
        u = _dot(gi_ref[t], p)
        o_ref[0, t] = u[:LANES]
        o_ref[1, t] = u[LANES:]


def _fft_mid(y, kf, g_fwd, g_inv):
    P, _, n1, _, W = y.shape
    kt = _tile(n1, 8, 1)
    yspec = pl.BlockSpec((None, 2, kt, LANES, W), lambda i, p: (p, 0, i, 0, 0))
    gspec = pl.BlockSpec((kt, 2 * LANES, 2 * LANES), lambda i, p: (i, 0, 0))
    return pl.pallas_call(
        _fft_mid_kernel,
        out_shape=jax.ShapeDtypeStruct(y.shape, F32),
        grid=(n1 // kt, P),
        in_specs=[yspec, pl.BlockSpec((2, kt, LANES, W), lambda i, p: (0, i, 0, 0)), gspec, gspec],
        out_specs=yspec,
        compiler_params=_cp(("arbitrary", "arbitrary")),
        name="fft_mid",
    )(y, kf, g_fwd, g_inv)


def _fft_stage3_kernel(u_ref, m_ref, z_ref, x_ref, bias_ref, g_ref, o_ref, *, normalise):
    a_out = z_ref.shape[1]
    bias = bias_ref[...]
    for s in range(o_ref.shape[2]):
        u = jnp.concatenate([u_ref[0, :, s, :], u_ref[1, :, s, :]], axis=0).astype(BF16)
        conv = _dot(m_ref[...], u)
        for part in range(2):
            y = x_ref[part, :, s, :] * (
                conv[part * a_out:(part + 1) * a_out] + z_ref[part, :, s, :] * bias)
            if normalise:
                y = y * lax.rsqrt(jnp.mean(y * y, axis=-1, keepdims=True) + EPS) * g_ref[...]
            o_ref[part, :, s, :] = y


def _fft_stage3(u, m3, z, gate, bias_row, norm_g=None):
    P, _, n1, _, W = u.shape
    A = z.shape[2]
    pair = pl.BlockSpec((None, 2, A, FFT_ROWS, W), lambda p, i: (p, 0, 0, i, 0))
    row = pl.BlockSpec((1, W), lambda p, i: (0, 0))
    normalise = norm_g is not None
    return pl.pallas_call(
        functools.partial(_fft_stage3_kernel, normalise=normalise),
        out_shape=jax.ShapeDtypeStruct(z.shape, F32),
        grid=(P, LANES // FFT_ROWS),
        in_specs=[pl.BlockSpec((None, 2, n1, FFT_ROWS, W), lambda p, i: (p, 0, 0, i, 0)),
                  pl.BlockSpec(m3.shape, lambda p, i: (0, 0)),
                  pair, pair, row, row],
        out_specs=pair,
        compiler_params=_cp(("parallel", "arbitrary")),
        name="fft_stage3",
    )(u, m3, z, gate, bias_row, norm_g.reshape(1, W) if normalise else bias_row)


def _hyena_mixer(v, gates, kfs, d_bias, norm_g, mats, n1):
    B, L, W = v.shape
    pair_shape = (B // 2, 2, L // LANES, LANES, W)
    m1_complex, _, m3, g_fwd, g_inv = mats
    z = v.reshape(pair_shape)
    for o in range(HY_ORDER):
        y = _fft_stage1(z, m1_complex, n1, complex_in=True)
        u = _fft_mid(y, kfs[o], g_fwd, g_inv)
        z = _fft_stage3(u, m3, z, gates[o].reshape(pair_shape), d_bias[o].reshape(1, W),
                        norm_g if o == HY_ORDER - 1 else None)
    return z.reshape(B, L, W)


def kernel(x, c, ctx, c_ctx, w_mod, b_mod, norm1_g, w_in, attn_sink, pool_w, pool_scale,
           hy_conv_w, hy_conv_b, hy_f_w1, hy_f_b1, hy_f_w2, hy_f_b2, hy_f_w3, hy_f_b3,
           hy_f_freq, hy_bias, g_attn, g_pool, g_hyena, w_out, norm2_g,
           ff_w1, ff_w3, ff_w2, router_w, moe_w1, moe_w3, moe_w2, final_g):
    B, S, D = x.shape
    C = ctx.shape[1]
    depth = w_mod.shape[0]
    assert B % 2 == 0 and S % BLOCK == 0
    fft_mats = {}

    c_rows = jnp.concatenate([c, c_ctx[None], jnp.zeros((SUBLANES - B - 1, D), F32)], axis=0)
    xc = ctx

    def hyena_branch(l, proj, seq_len):
        rows = -(-seq_len // FFT_MIN_ROWS) * FFT_MIN_ROWS
        n1 = 2 * rows // LANES
        if rows not in fft_mats:
            fft_mats[rows] = _fft_matrices(n1, rows // LANES)
        mats = fft_mats[rows]
        taps = _hyena_filter_taps(seq_len, hy_f_w1[l], hy_f_b1[l], hy_f_w2[l], hy_f_b2[l],
                                  hy_f_w3[l], hy_f_b3[l], hy_f_freq[l])
        v, x1, x2 = _short_conv(proj, hy_conv_w[l], hy_conv_b[l])
        if seq_len < rows:
            pad = lambda t: jnp.pad(t, ((0, 0), (0, rows - seq_len), (0, 0)))
            taps = jnp.pad(taps, ((0, rows - seq_len), (0, 0)))
            v, x1, x2 = pad(v), pad(x1), pad(x2)
        kfs = _filter_spectrum(taps, mats, n1)
        z = _hyena_mixer(v, (x1, x2), kfs, hy_bias[l], g_hyena[l], mats, n1)
        return z[:, :seq_len]

    def mix(l, proj, y_att, res, gate, w_out_l, seq_len, norm=None):
        y_pool = _pool_mixer(proj, pool_w[l], pool_scale[l], g_pool[l])
        y_hy = hyena_branch(l, proj, seq_len)
        return _mix_out([y_att, y_pool, y_hy], w_out_l, res, gate, norm)

    for l in range(depth):
        last = l == depth - 1
        mod = _adaln(c_rows, w_mod[l], b_mod[l])
        sh1, sc1, g1, sh2, sc2, g2 = [mod[:B, j * D:(j + 1) * D] for j in range(6)]
        csh1, csc1, cg1, csh2, csc2, cg2 = [mod[B:B + 1, j * D:(j + 1) * D] for j in range(6)]
        w_in_l = w_in[l].astype(BF16)
        w_out_l = w_out[l].astype(BF16)

        hc = _norm_mod(xc, norm1_g[l], csh1, csc1)
        if last:
            kv_w = w_in_l[:, ATT_WIDTH:ATT_WIDTH + 2 * KV_WIDTH]
            kv_c = _matmul(hc, kv_w, out_dtype=BF16)
            k_c, v_c = kv_c[..., :KV_WIDTH], kv_c[..., KV_WIDTH:]
        else:
            proj_c = _matmul(hc, w_in_l)
            q_c, k_c, v_c = _qkv_prepare(proj_c, rope=False)
            y_att_c = _attention(q_c, None, None, k_c, v_c, attn_sink[l], g_attn[l])
            xc_new, hc2 = mix(l, proj_c, y_att_c, xc, cg1, w_out_l, C, (norm2_g[l], csh2, csc2))
            i = l // 2
            if l % 2 == 0:
                hid_c = _swiglu_hidden(hc2, ff_w1[i].astype(BF16), ff_w3[i].astype(BF16))
                xc_new = _matmul_gated_residual([hid_c], ff_w2[i].astype(BF16), xc_new, cg2)
            else:
                raise NotImplementedError("context tokens through an expert layer")

        h = _norm_mod(x, norm1_g[l], sh1, sc1)
        proj = _matmul(h, w_in_l, tm=2048)
        q, k, v = _qkv_prepare(proj, rope=True)
        y_att = _attention(q, k, v, k_c, v_c, attn_sink[l], g_attn[l])

        i = l // 2
        if l % 2 == 0:
            x, h2 = mix(l, proj, y_att, x, g1, w_out_l, S, (norm2_g[l], sh2, sc2))
            hid = _swiglu_hidden(h2, ff_w1[i].astype(BF16), ff_w3[i].astype(BF16), tm=2048)
            x = _matmul_gated_residual([hid], ff_w2[i].astype(BF16), x, g2)
            if last:
                x = _rmsnorm(x, final_g, F32)
        else:
            assert last
            x = mix(l, proj, y_att, x, g1, w_out_l, S)
            h2, gi, gw, counts = _norm_mod_router(x, norm2_g[l], sh2, sc2, router_w[i])
            x = _moe_layer(x, h2, gi, gw, counts, g2, final_g, moe_w1[i], moe_w3[i], moe_w2[i])
        if not last:
            xc = xc_new
    return x
```

```python
import functools
import math

import numpy as np
import jax
import jax.numpy as jnp
from jax import lax
from jax.experimental import pallas as pl
from jax.experimental.pallas import tpu as pltpu

F32 = jnp.float32
BF16 = jnp.bfloat16

EPS = 1e-6
NEG = -1e30
GRID_W = 64
ATT_HEADS = 8
ATT_KV_HEADS = 2
ATT_GROUP = ATT_HEADS // ATT_KV_HEADS
HEAD_DIM = 128
ATT_WIDTH = ATT_HEADS * HEAD_DIM
KV_WIDTH = ATT_KV_HEADS * HEAD_DIM
WINDOW = 128
BLOCK = 128
ROPE_BASE = 10000.0
POOL_WINDOWS = (2, 4, 8, 16)
POOL_GROUP = 128
POOL_WIDTH = POOL_GROUP * len(POOL_WINDOWS)
HY_WIDTH = 512
HY_ORDER = 2
HY_SHORT = 3
HY_EMB = 33
HY_BANDS = (HY_EMB - 1) // 2
HY_SHORT_DECAY_PCT = 0.3
HY_LONG_DECAY_PCT = 1.5
HY_TARGET = 1e-2
N_EXPERTS = 8
TOP_K = 2

LANES = 128
SUBLANES = 8
HALO = SUBLANES
FFT_MIN_ROWS = 2048
FFT_ROWS = SUBLANES
MOE_OUT_K_CHUNKS = 4
VMEM_LIMIT = 56 * 1024 * 1024


def _cp(sem):
    return pltpu.CompilerParams(dimension_semantics=sem, vmem_limit_bytes=VMEM_LIMIT)


def _tile(n, pref, mult=SUBLANES):
    if n <= pref:
        return n
    t = (pref // mult) * mult
    while t >= mult:
        if n % t == 0:
            return t
        t -= mult
    return n


def _split_bf16(x):
    hi = x.astype(BF16)
    lo = (x - hi.astype(F32)).astype(BF16)
    return hi, lo


def _dot(a, b):
    return jnp.dot(a, b, preferred_element_type=F32)


def _pack_bf16_pairs(x):
    n = x.shape[-1] // 2
    bits = lax.bitcast_convert_type(x.astype(BF16).astype(F32), jnp.uint32)
    return (bits[:, :n] >> 16) | (bits[:, n:] & jnp.uint32(0xFFFF0000))


def _unpack_bf16_pairs(u):
    lo = lax.bitcast_convert_type(u << 16, F32).astype(BF16)
    hi = lax.bitcast_convert_type(u & jnp.uint32(0xFFFF0000), F32).astype(BF16)
    return lo, hi


def _dot3(a, b):
    ah, al = _split_bf16(a)
    bh, bl = _split_bf16(b)
    return _dot(ah, bh) + (_dot(ah, bl) + _dot(al, bh))


def _adaln_kernel(c_ref, w_ref, b_ref, o_ref):
    c = c_ref[...]
    a = c * (1.0 / (1.0 + jnp.exp(-c)))
    o_ref[...] = _dot3(a, w_ref[...]) + b_ref[...]


def _adaln(c_rows, w, b):
    R, D = c_rows.shape
    N = w.shape[1]
    tn = _tile(N, 1536, LANES)
    return pl.pallas_call(
        _adaln_kernel,
        out_shape=jax.ShapeDtypeStruct((R, N), F32),
        grid=(N // tn,),
        in_specs=[pl.BlockSpec((R, D), lambda j: (0, 0)),
                  pl.BlockSpec((D, tn), lambda j: (0, j)),
                  pl.BlockSpec((1, tn), lambda j: (0, j))],
        out_specs=pl.BlockSpec((R, tn), lambda j: (0, j)),
        compiler_params=_cp(("arbitrary",)),
        name="adaln",
    )(c_rows, w, b.reshape(1, N))


def _norm_mod_kernel(x_ref, g_ref, sh_ref, sc_ref, o_ref):
    x = x_ref[...]
    y = x * lax.rsqrt(jnp.mean(x * x, axis=-1, keepdims=True) + EPS) * g_ref[...]
    o_ref[...] = (y * (1.0 + sc_ref[...]) + sh_ref[...]).astype(o_ref.dtype)


def _bcast_map(arr):
    if arr.shape[0] == 1:
        return lambda b, i: (0, 0, 0)
    return lambda b, i: (b, 0, 0)


def _norm_mod(x, g, shift, scale):
    B, L, D = x.shape
    ts = _tile(L, 512)
    shift = shift[:, None, :]
    scale = scale[:, None, :]
    return pl.pallas_call(
        _norm_mod_kernel,
        out_shape=jax.ShapeDtypeStruct((B, L, D), BF16),
        grid=(B, L // ts),
        in_specs=[pl.BlockSpec((None, ts, D), lambda b, i: (b, i, 0)),
                  pl.BlockSpec((1, D), lambda b, i: (0, 0)),
                  pl.BlockSpec((None, 1, D), _bcast_map(shift)),
                  pl.BlockSpec((None, 1, D), _bcast_map(scale))],
        out_specs=pl.BlockSpec((None, ts, D), lambda b, i: (b, i, 0)),
        compiler_params=_cp(("parallel", "parallel")),
        name="norm_mod",
    )(x, g.reshape(1, D), shift, scale)


def _route(h, rw_ref, tri_ref, run_ref, gi_ref, gw_ref, cnt_ref):
    logits = _dot3(h, rw_ref[...])
    lane = lax.broadcasted_iota(jnp.int32, logits.shape, 1)
    logits = jnp.where(lane < N_EXPERTS, logits, NEG)
    m1 = jnp.max(logits, axis=-1, keepdims=True)
    i1 = jnp.min(jnp.where(logits == m1, lane, LANES), axis=-1, keepdims=True)
    rest = jnp.where(lane == i1, NEG, logits)
    m2 = jnp.max(rest, axis=-1, keepdims=True)
    i2 = jnp.min(jnp.where(rest == m2, lane, LANES), axis=-1, keepdims=True)
    e2 = jnp.exp(m2 - m1)
    w1 = 1.0 / (1.0 + e2)
    w2 = e2 * w1
    pick1 = lane == i1
    pick2 = lane == i2
    both = jnp.where(pick1 | pick2, 1.0, 0.0)
    before = _dot(tri_ref[...], both.astype(BF16)) + run_ref[...]
    r1 = jnp.sum(jnp.where(pick1, before, 0.0), axis=-1, keepdims=True).astype(jnp.int32)
    r2 = jnp.sum(jnp.where(pick2, before, 0.0), axis=-1, keepdims=True).astype(jnp.int32)
    run_ref[...] = run_ref[...] + jnp.sum(both, axis=0, keepdims=True)
    cnt_ref[...] = run_ref[...]
    gi_ref[...] = jnp.where(lane == 0, i1, jnp.where(lane == 1, i2,
                            jnp.where(lane == 2, r1, jnp.where(lane == 3, r2, 0))))
    gw_ref[...] = jnp.where(lane == 0, w1, jnp.where(lane == 1, w2, 0.0))


def _norm_mod_router_kernel(x_ref, g_ref, sh_ref, sc_ref, rw_ref, tri_ref,
                            o_ref, gi_ref, gw_ref, cnt_ref, run_ref):
    @pl.when((pl.program_id(0) == 0) & (pl.program_id(1) == 0))
    def _():
        run_ref[...] = jnp.zeros(run_ref.shape, run_ref.dtype)

    x = x_ref[...]
    y = x * lax.rsqrt(jnp.mean(x * x, axis=-1, keepdims=True) + EPS) * g_ref[...]
    h = y * (1.0 + sc_ref[...]) + sh_ref[...]
    o_ref[...] = _pack_bf16_pairs(h)
    _route(h, rw_ref, tri_ref, run_ref, gi_ref, gw_ref, cnt_ref)


def _norm_mod_router(x, g, shift, scale, router_w):
    B, L, D = x.shape
    ts = _tile(L, 512)
    shift = shift[:, None, :]
    scale = scale[:, None, :]
    rw = jnp.pad(router_w, ((0, 0), (0, LANES - router_w.shape[1])))
    tri = jnp.asarray(np.tril(np.ones((ts, ts), np.float32), -1), BF16)
    row = pl.BlockSpec((None, ts, D), lambda b, i: (b, i, 0))
    small = pl.BlockSpec((None, ts, LANES), lambda b, i: (b, i, 0))
    return pl.pallas_call(
        _norm_mod_router_kernel,
        out_shape=(jax.ShapeDtypeStruct((B, L, D // 2), jnp.uint32),
                   jax.ShapeDtypeStruct((B, L, LANES), jnp.int32),
                   jax.ShapeDtypeStruct((B, L, LANES), F32),
                   jax.ShapeDtypeStruct((1, LANES), F32)),
        grid=(B, L // ts),
        in_specs=[row,
                  pl.BlockSpec((1, D), lambda b, i: (0, 0)),
                  pl.BlockSpec((None, 1, D), _bcast_map(shift)),
                  pl.BlockSpec((None, 1, D), _bcast_map(scale)),
                  pl.BlockSpec((D, LANES), lambda b, i: (0, 0)),
                  pl.BlockSpec((ts, ts), lambda b, i: (0, 0))],
        out_specs=(pl.BlockSpec((None, ts, D // 2), lambda b, i: (b, i, 0)), small, small,
                   pl.BlockSpec((1, LANES), lambda b, i: (0, 0))),
        scratch_shapes=[pltpu.VMEM((1, LANES), F32)],
        compiler_params=_cp(("arbitrary", "arbitrary")),
        name="norm_mod_router",
    )(x, g.reshape(1, D), shift, scale, rw, tri)


def _mix_out_kernel(*refs, n_a, emit_h):
    a_refs = refs[:n_a]
    w_refs = refs[n_a:2 * n_a]
    res_ref, gate_ref = refs[2 * n_a:2 * n_a + 2]
    acc = _dot(a_refs[0][...].astype(BF16), w_refs[0][...])
    for a_ref, w_ref in zip(a_refs[1:], w_refs[1:]):
        acc = acc + _dot(a_ref[...].astype(BF16), w_ref[...])
    x = res_ref[...] + gate_ref[...] * acc
    if emit_h:
        g_ref, sh_ref, sc_ref, x_ref, h_ref = refs[2 * n_a + 2:]
        y = x * lax.rsqrt(jnp.mean(x * x, axis=-1, keepdims=True) + EPS) * g_ref[...]
        h_ref[...] = (y * (1.0 + sc_ref[...]) + sh_ref[...]).astype(h_ref.dtype)
    else:
        x_ref, = refs[2 * n_a + 2:]
    x_ref[...] = x


def _mix_out(a_list, w, res, gate, norm=None):
    B, L, D = res.shape
    tm = _tile(L, 512)
    emit_h = norm is not None
    gate = gate[:, None, :]
    in_specs = [pl.BlockSpec((None, tm, a.shape[-1]), lambda b, i: (b, i, 0)) for a in a_list]
    off = 0
    for a in a_list:
        k = a.shape[-1]
        assert off % k == 0
        in_specs.append(pl.BlockSpec((k, D), functools.partial(lambda b, i, blk: (blk, 0), blk=off // k)))
        off += k
    row = pl.BlockSpec((None, tm, D), lambda b, i: (b, i, 0))
    in_specs += [row, pl.BlockSpec((None, 1, D), _bcast_map(gate))]
    args = [*a_list, *([w] * len(a_list)), res, gate]
    out_shape = [jax.ShapeDtypeStruct((B, L, D), F32)]
    out_specs = [row]
    if emit_h:
        g, shift, scale = norm
        shift, scale = shift[:, None, :], scale[:, None, :]
        in_specs += [pl.BlockSpec((1, D), lambda b, i: (0, 0)),
                     pl.BlockSpec((None, 1, D), _bcast_map(shift)),
                     pl.BlockSpec((None, 1, D), _bcast_map(scale))]
        args += [g.reshape(1, D), shift, scale]
        out_shape.append(jax.ShapeDtypeStruct((B, L, D), BF16))
        out_specs.append(row)
    out = pl.pallas_call(
        functools.partial(_mix_out_kernel, n_a=len(a_list), emit_h=emit_h),
        out_shape=tuple(out_shape),
        grid=(B, L // tm),
        in_specs=in_specs,
        out_specs=tuple(out_specs),
        compiler_params=_cp(("parallel", "parallel")),
        name="mix_out",
    )(*args)
    return out if emit_h else out[0]


def _rmsnorm_kernel(x_ref, g_ref, o_ref):
    x = x_ref[...]
    y = x * lax.rsqrt(jnp.mean(x * x, axis=-1, keepdims=True) + EPS) * g_ref[...]
    o_ref[...] = y.astype(o_ref.dtype)


def _rmsnorm(x, g, out_dtype):
    B, L, D = x.shape
    ts = _tile(L, 512)
    return pl.pallas_call(
        _rmsnorm_kernel,
        out_shape=jax.ShapeDtypeStruct((B, L, D), out_dtype),
        grid=(B, L // ts),
        in_specs=[pl.BlockSpec((None, ts, D), lambda b, i: (b, i, 0)),
                  pl.BlockSpec((1, D), lambda b, i: (0, 0))],
        out_specs=pl.BlockSpec((None, ts, D), lambda b, i: (b, i, 0)),
        compiler_params=_cp(("parallel", "parallel")),
        name="rmsnorm",
    )(x, g.reshape(1, D))


def _mm_kernel(a_ref, w_ref, o_ref):
    o_ref[...] = _dot(a_ref[...], w_ref[...]).astype(o_ref.dtype)


def _matmul(a, w, out_dtype=F32, tm=1024, tn=512):
    B, L, K = a.shape
    N = w.shape[1]
    tm = _tile(L, tm)
    tn = _tile(N, tn, LANES)
    return pl.pallas_call(
        _mm_kernel,
        out_shape=jax.ShapeDtypeStruct((B, L, N), out_dtype),
        grid=(B, L // tm, N // tn),
        in_specs=[pl.BlockSpec((None, tm, K), lambda b, i, j: (b, i, 0)),
                  pl.BlockSpec((K, tn), lambda b, i, j: (0, j))],
        out_specs=pl.BlockSpec((None, tm, tn), lambda b, i, j: (b, i, j)),
        compiler_params=_cp(("parallel", "parallel", "arbitrary")),
        name="matmul",
    )(a, w)


def _mm_res_kernel(*refs, n_a):
    a_refs = refs[:n_a]
    w_refs = refs[n_a:2 * n_a]
    res_ref, gate_ref, o_ref = refs[2 * n_a:]
    acc = _dot(a_refs[0][...].astype(BF16), w_refs[0][...])
    for a_ref, w_ref in zip(a_refs[1:], w_refs[1:]):
        acc = acc + _dot(a_ref[...].astype(BF16), w_ref[...])
    o_ref[...] = res_ref[...] + gate_ref[...] * acc


def _matmul_gated_residual(a_list, w, res, gate, tm=1024, tn=512):
    B, L, N = res.shape
    tm = _tile(L, tm)
    tn = _tile(N, tn, LANES)
    gate = gate[:, None, :]
    widths = [a.shape[-1] for a in a_list]
    unit = math.gcd(*widths) if len(widths) > 1 else widths[0]
    in_specs = [pl.BlockSpec((None, tm, k), lambda b, i, j: (b, i, 0)) for k in widths]
    off = 0
    for k in widths:
        assert off % k == 0 and k % unit == 0
        in_specs.append(pl.BlockSpec((k, tn), functools.partial(
            lambda b, i, j, blk: (blk, j), blk=off // k)))
        off += k
    gmap = (lambda b, i, j: (0, 0, j)) if gate.shape[0] == 1 else (lambda b, i, j: (b, 0, j))
    in_specs += [pl.BlockSpec((None, tm, tn), lambda b, i, j: (b, i, j)),
                 pl.BlockSpec((None, 1, tn), gmap)]
    return pl.pallas_call(
        functools.partial(_mm_res_kernel, n_a=len(a_list)),
        out_shape=jax.ShapeDtypeStruct((B, L, N), F32),
        grid=(B, L // tm, N // tn),
        in_specs=in_specs,
        out_specs=pl.BlockSpec((None, tm, tn), lambda b, i, j: (b, i, j)),
        compiler_params=_cp(("parallel", "parallel", "arbitrary")),
        name="matmul_gated_residual",
    )(*a_list, *([w] * len(a_list)), res, gate)


def _silu(x):
    return x * (1.0 / (1.0 + jnp.exp(-x)))


def _swiglu_kernel(a_ref, w1_ref, w3_ref, o_ref):
    a = a_ref[...]
    o_ref[...] = (_silu(_dot(a, w1_ref[...])) * _dot(a, w3_ref[...])).astype(o_ref.dtype)


def _swiglu_hidden(a, w1, w3, tm=1024, tf=512):
    B, L, D = a.shape
    F = w1.shape[1]
    tm = _tile(L, tm)
    tf = _tile(F, tf, LANES)
    wspec = pl.BlockSpec((D, tf), lambda b, i, j: (0, j))
    return pl.pallas_call(
        _swiglu_kernel,
        out_shape=jax.ShapeDtypeStruct((B, L, F), BF16),
        grid=(B, L // tm, F // tf),
        in_specs=[pl.BlockSpec((None, tm, D), lambda b, i, j: (b, i, 0)), wspec, wspec],
        out_specs=pl.BlockSpec((None, tm, tf), lambda b, i, j: (b, i, j)),
        compiler_params=_cp(("parallel", "parallel", "arbitrary")),
        name="swiglu_hidden",
    )(a, w1, w3)


def _moe_hidden_kernel(te_ref, nt_ref, a_ref, w1_ref, w3_ref, o_ref, w1b_ref, w3b_ref):
    i = pl.program_id(1)
    used = i < nt_ref[0]
    half = a_ref.shape[1]
    new_weights = (i == 0) | (te_ref[i] != te_ref[jnp.maximum(i - 1, 0)])

    @pl.when(used & new_weights)
    def _():
        w1b_ref[...] = w1_ref[...].astype(BF16)
        w3b_ref[...] = w3_ref[...].astype(BF16)

    @pl.when(used)
    def _():
        lo, hi = _unpack_bf16_pairs(a_ref[...])
        gate = _dot(lo, w1b_ref[:half, :]) + _dot(hi, w1b_ref[half:, :])
        up = _dot(lo, w3b_ref[:half, :]) + _dot(hi, w3b_ref[half:, :])
        o_ref[...] = (_silu(gate) * up).astype(o_ref.dtype)

    @pl.when(jnp.logical_not(used))
    def _():
        o_ref[...] = jnp.zeros(o_ref.shape, o_ref.dtype)


def _moe_out_kernel(te_ref, nt_ref, a_ref, w2_ref, o_ref):
    used = pl.program_id(0) < nt_ref[0]

    @pl.when(used)
    def _():
        kc = a_ref.shape[1] // MOE_OUT_K_CHUNKS
        acc = _dot(a_ref[:, :kc], w2_ref[:kc, :].astype(BF16))
        for c in range(1, MOE_OUT_K_CHUNKS):
            acc = acc + _dot(a_ref[:, c * kc:(c + 1) * kc], w2_ref[c * kc:(c + 1) * kc, :].astype(BF16))
        o_ref[...] = _pack_bf16_pairs(acc)

    @pl.when(jnp.logical_not(used))
    def _():
        o_ref[...] = jnp.zeros(o_ref.shape, o_ref.dtype)


def _moe_out_cols(d_model):
    return _tile(d_model, 512, 2 * LANES)


def _moe_experts(a_sorted, tile_expert, n_tiles_used, w1, w3, w2, tm, tf=512):
    R = a_sorted.shape[0]
    E, D, F = w1.shape
    tf = _tile(F, tf, LANES)
    tn = _moe_out_cols(D)
    nt = R // tm
    w13 = pl.BlockSpec((None, D, tf), lambda j, i, te, n: (te[i], 0, j))
    hidden = pl.pallas_call(
        _moe_hidden_kernel,
        out_shape=jax.ShapeDtypeStruct((R, F), BF16),
        grid_spec=pltpu.PrefetchScalarGridSpec(
            num_scalar_prefetch=2,
            grid=(F // tf, nt),
            in_specs=[pl.BlockSpec((tm, D // 2), lambda j, i, te, n: (i, 0)), w13, w13],
            out_specs=pl.BlockSpec((tm, tf), lambda j, i, te, n: (i, j)),
            scratch_shapes=[pltpu.VMEM((D, tf), BF16), pltpu.VMEM((D, tf), BF16)]),
        compiler_params=_cp(("arbitrary", "arbitrary")),
        name="moe_hidden",
    )(tile_expert, n_tiles_used, a_sorted, w1, w3)
    return pl.pallas_call(
        _moe_out_kernel,
        out_shape=jax.ShapeDtypeStruct((R, D // 2), jnp.uint32),
        grid_spec=pltpu.PrefetchScalarGridSpec(
            num_scalar_prefetch=2,
            grid=(nt, D // tn),
            in_specs=[pl.BlockSpec((tm, F), lambda i, j, te, n: (i, 0)),
                      pl.BlockSpec((None, F, tn), lambda i, j, te, n: (te[i], 0, j))],
            out_specs=pl.BlockSpec((tm, tn // 2), lambda i, j, te, n: (i, j))),
        compiler_params=_cp(("arbitrary", "arbitrary")),
        name="moe_out",
    )(tile_expert, n_tiles_used, hidden, w2)


def _unpack_column_blocks(u, block):
    parts = []
    for c in range(u.shape[1] // block):
        w = u[:, c * block:(c + 1) * block]
        parts.append(lax.bitcast_convert_type(w << 16, F32))
        parts.append(lax.bitcast_convert_type(w & jnp.uint32(0xFFFF0000), F32))
    return jnp.concatenate(parts, axis=1)


def _moe_combine_kernel(x_ref, ya_ref, yb_ref, gw_ref, gate_ref, g_ref, o_ref, *, block):
    gw = gw_ref[...]
    moe = (gw[:, 0:1] * _unpack_column_blocks(ya_ref[...], block)
           + gw[:, 1:2] * _unpack_column_blocks(yb_ref[...], block))
    x = x_ref[...] + gate_ref[...] * moe
    y = x * lax.rsqrt(jnp.mean(x * x, axis=-1, keepdims=True) + EPS) * g_ref[...]
    o_ref[...] = y


def _moe_combine_norm(x, ya, yb, gw, gate, g):
    B, L, D = x.shape
    ts = _tile(L, 512)
    row = pl.BlockSpec((None, ts, D), lambda b, i: (b, i, 0))
    packed = pl.BlockSpec((None, ts, D // 2), lambda b, i: (b, i, 0))
    return pl.pallas_call(
        functools.partial(_moe_combine_kernel, block=_moe_out_cols(D) // 2),
        out_shape=jax.ShapeDtypeStruct((B, L, D), F32),
        grid=(B, L // ts),
        in_specs=[row, packed, packed,
                  pl.BlockSpec((None, ts, LANES), lambda b, i: (b, i, 0)),
                  pl.BlockSpec((None, 1, D), lambda b, i: (b, 0, 0)),
                  pl.BlockSpec((1, D), lambda b, i: (0, 0))],
        out_specs=row,
        compiler_params=_cp(("parallel", "parallel")),
        name="moe_combine_norm",
    )(x, ya, yb, gw, gate[:, None, :], g.reshape(1, D))


def _moe_layer(x, h, gi, gw, counts, gate, final_g, w1, w3, w2):
    B, L, D = x.shape
    E = w1.shape[0]
    n_tok = B * L
    n_pair = n_tok * TOP_K
    tm = _tile(n_pair, 1024)
    nt = n_pair // tm + E
    e_pair = gi[..., :TOP_K].reshape(n_pair)
    rank = gi[..., TOP_K:2 * TOP_K].reshape(n_pair)
    counts = counts[0, :E].astype(jnp.int32)
    tiles_per = (counts + tm - 1) // tm
    tile_end = jnp.cumsum(tiles_per)
    tile_start = tile_end - tiles_per
    row_start = jnp.zeros((n_pair,), jnp.int32)
    for e in range(E):
        row_start = jnp.where(e_pair == e, tile_start[e] * tm, row_start)
    pos = row_start + rank
    tile_ids = jnp.arange(nt, dtype=jnp.int32)
    tile_expert = jnp.minimum(
        jnp.sum((tile_ids[:, None] >= tile_end[None, :]).astype(jnp.int32), axis=1), E - 1)
    n_used = tile_end[-1:].astype(jnp.int32)
    tok_pair = jnp.arange(n_pair, dtype=jnp.int32) // TOP_K
    src = (jnp.arange(nt * tm, dtype=jnp.int32) % n_tok).at[pos].set(
        tok_pair, unique_indices=True, mode="promise_in_bounds")
    gather_rows = lambda rows, idx: rows.at[idx].get(mode="promise_in_bounds")
    a_sorted = gather_rows(h.reshape(n_tok, h.shape[-1]), src)
    y_sorted = _moe_experts(a_sorted, tile_expert.astype(jnp.int32), n_used, w1, w3, w2, tm)
    pos2 = pos.reshape(n_tok, TOP_K)
    ya = gather_rows(y_sorted, pos2[:, 0]).reshape(B, L, D // 2)
    yb = gather_rows(y_sorted, pos2[:, 1]).reshape(B, L, D // 2)
    return _moe_combine_norm(x, ya, yb, gw, gate, final_g)


def _rope_tables(n_tokens):
    pos = np.arange(n_tokens)
    row = (pos // GRID_W).astype(np.float32)
    col = (pos % GRID_W).astype(np.float32)
    n_freq = HEAD_DIM // 4
    inv = (np.float32(ROPE_BASE) ** (-np.arange(n_freq, dtype=np.float32) / np.float32(n_freq)))
    ang_r = (row[:, None] * inv).astype(np.float64)
    ang_c = (col[:, None] * inv).astype(np.float64)
    cos = np.concatenate([np.cos(ang_r), np.cos(ang_r), np.cos(ang_c), np.cos(ang_c)], axis=1)
    sin = np.concatenate([-np.sin(ang_r), np.sin(ang_r), -np.sin(ang_c), np.sin(ang_c)], axis=1)
    return jnp.asarray(cos, F32), jnp.asarray(sin, F32)


def _swap_halves(x):
    n = x.shape[-1]
    quarter = HEAD_DIM // 4
    lane = lax.broadcasted_iota(jnp.int32, x.shape, 1)
    up = pltpu.roll(x, n - quarter, axis=1)
    down = pltpu.roll(x, quarter, axis=1)
    return jnp.where((lane & quarter) == 0, up, down)


IN_PROJ_COLS = 2 * KV_WIDTH
N_Q_BLOCKS = ATT_WIDTH // IN_PROJ_COLS


def _in_proj_kernel(x_ref, g_ref, sh_ref, sc_ref, w_ref, cos_ref, sin_ref,
                    q_ref, kv_ref, rest_ref, h_ref, *, rope):
    j = pl.program_id(2)

    @pl.when(j == 0)
    def _():
        x = x_ref[...]
        y = x * lax.rsqrt(jnp.mean(x * x, axis=-1, keepdims=True) + EPS) * g_ref[...]
        h_ref[...] = (y * (1.0 + sc_ref[...]) + sh_ref[...]).astype(h_ref.dtype)

    acc = _dot(h_ref[...], w_ref[...])

    def rotate(t):
        if not rope:
            return t
        heads = t.shape[1] // HEAD_DIM
        return (t * jnp.tile(cos_ref[...], (1, heads))
                + _swap_halves(t) * jnp.tile(sin_ref[...], (1, heads)))

    @pl.when(j < N_Q_BLOCKS)
    def _():
        q_ref[...] = (rotate(acc) * (HEAD_DIM ** -0.5)).astype(q_ref.dtype)

    @pl.when(j == N_Q_BLOCKS)
    def _():
        kv_ref[:, :KV_WIDTH] = rotate(acc[:, :KV_WIDTH]).astype(kv_ref.dtype)
        kv_ref[:, KV_WIDTH:] = acc[:, KV_WIDTH:].astype(kv_ref.dtype)

    @pl.when(j > N_Q_BLOCKS)
    def _():
        rest_ref[...] = acc


def _in_proj(x, g, shift, scale, w, rope):
    B, L, D = x.shape
    n_col = w.shape[1]
    tn = IN_PROJ_COLS
    n_rest = n_col - ATT_WIDTH - tn
    assert ATT_WIDTH % tn == 0 and n_rest % tn == 0
    tm = _tile(L, 1024)
    shift, scale = shift[:, None, :], scale[:, None, :]
    cos, sin = _rope_tables(L)
    tab = pl.BlockSpec((tm, HEAD_DIM), lambda b, i, j: (i, 0))
    bmap = lambda arr: (lambda b, i, j: (0, 0, 0)) if arr.shape[0] == 1 else (lambda b, i, j: (b, 0, 0))
    return pl.pallas_call(
        functools.partial(_in_proj_kernel, rope=rope),
        out_shape=(jax.ShapeDtypeStruct((B, L, ATT_WIDTH), BF16),
                   jax.ShapeDtypeStruct((B, L, tn), BF16),
                   jax.ShapeDtypeStruct((B, L, n_rest), F32)),
        grid=(B, L // tm, n_col // tn),
        in_specs=[pl.BlockSpec((None, tm, D), lambda b, i, j: (b, i, 0)),
                  pl.BlockSpec((1, D), lambda b, i, j: (0, 0)),
                  pl.BlockSpec((None, 1, D), bmap(shift)),
                  pl.BlockSpec((None, 1, D), bmap(scale)),
                  pl.BlockSpec((D, tn), lambda b, i, j: (0, j)),
                  tab, tab],
        out_specs=(pl.BlockSpec((None, tm, tn), lambda b, i, j: (b, i, jnp.minimum(j, N_Q_BLOCKS - 1))),
                   pl.BlockSpec((None, tm, tn), lambda b, i, j: (b, i, 0)),
                   pl.BlockSpec((None, tm, tn),
                                lambda b, i, j: (b, i, jnp.maximum(j - N_Q_BLOCKS - 1, 0)))),
        scratch_shapes=[pltpu.VMEM((tm, D), BF16)],
        compiler_params=_cp(("parallel", "parallel", "arbitrary")),
        name="in_proj",
    )(x, g.reshape(1, D), shift, scale, w, cos, sin)


def _nt_dot(a, b):
    return lax.dot_general(a, b, (((1,), (1,)), ((), ())), preferred_element_type=F32)


def _attn_bias_tables():
    T, G = BLOCK, ATT_GROUP
    qi = np.arange(G * T)[:, None] % T
    ki = np.arange(3 * T)[None, :]
    band = np.abs(ki - T - qi) <= WINDOW
    after_start = ki >= T
    before_end = ki < 2 * T
    masks = [band & after_start, band, band & before_end, band & after_start & before_end]
    return jnp.asarray(np.stack([np.where(m, 0.0, NEG) for m in masks]), F32)


def _attn_kernel(*refs, local, n_blocks):
    if local:
        (q_ref, kp_ref, kc_ref, kn_ref, vp_ref, vc_ref, vn_ref,
         kx_ref, vx_ref, bias_ref, sink_ref, g_ref, o_ref, acc_ref) = refs
    else:
        q_ref, kx_ref, vx_ref, sink_ref, g_ref, o_ref, acc_ref = refs
    T = BLOCK
    G = ATT_GROUP
    n_sub = q_ref.shape[0] // T
    for h in range(ATT_KV_HEADS):
        cols = slice(h * HEAD_DIM, (h + 1) * HEAD_DIM)
        sink = jnp.concatenate(
            [jnp.broadcast_to(sink_ref[:, (h * G + g) * HEAD_DIM:(h * G + g) * HEAD_DIM + 1], (T, 1))
             for g in range(G)], axis=0)
        kx = kx_ref[:, cols]
        vx = jnp.concatenate([vx_ref[:, cols], jnp.ones((kx_ref.shape[0], HEAD_DIM), BF16)], axis=1)
        if local:
            k_band = jnp.concatenate([kp_ref[:, cols], kc_ref[:, cols], kn_ref[:, cols]], axis=0)
            v_band = jnp.concatenate([vp_ref[:, cols], vc_ref[:, cols], vn_ref[:, cols]], axis=0)
            v_band = jnp.concatenate([v_band, jnp.ones(v_band.shape, BF16)], axis=1)
        for j in range(n_sub):
            rows = slice(j * T, (j + 1) * T)
            qs = jnp.concatenate(
                [q_ref[rows, (h * G + g) * HEAD_DIM:(h * G + g + 1) * HEAD_DIM] for g in range(G)],
                axis=0)
            s_ctx = _nt_dot(qs, kx)
            m = jnp.maximum(jnp.max(s_ctx, axis=-1, keepdims=True), sink)
            if local:
                blk = pl.program_id(1) * n_sub + j
                is_first = blk == 0
                is_last = blk == n_blocks - 1
                table = jnp.where(is_first, jnp.where(is_last, 3, 0), jnp.where(is_last, 2, 1))
                s_loc = _nt_dot(qs, k_band[j * T:(j + 3) * T]) + bias_ref[table]
                m = jnp.maximum(m, jnp.max(s_loc, axis=-1, keepdims=True))
            o = _dot(jnp.exp((s_ctx - m).astype(BF16)), vx)
            if local:
                o = o + _dot(jnp.exp((s_loc - m).astype(BF16)), v_band[j * T:(j + 3) * T])
            denom = o[:, HEAD_DIM:HEAD_DIM + 1] + jnp.exp(sink - m)
            o = o[:, :HEAD_DIM] * (1.0 / denom)
            for g in range(G):
                acc_ref[rows, (h * G + g) * HEAD_DIM:(h * G + g + 1) * HEAD_DIM] = o[g * T:(g + 1) * T]
    y = acc_ref[...]
    y = y * lax.rsqrt(jnp.mean(y * y, axis=-1, keepdims=True) + EPS) * g_ref[...]
    o_ref[...] = y.astype(o_ref.dtype)


def _attention(q, kv, kv_ctx, sink, g_attn):
    B, L, _ = q.shape
    C = kv_ctx.shape[1]
    local = kv is not None
    T = BLOCK
    nb = L // T
    n_sub = 2 if nb % 2 == 0 else 1
    sink_row = jnp.repeat(sink.astype(F32), HEAD_DIM).reshape(1, ATT_WIDTH)
    qspec = pl.BlockSpec((None, n_sub * T, ATT_WIDTH), lambda b, i: (b, i, 0))
    in_specs = [qspec]
    args = [q]
    cspecs = [pl.BlockSpec((None, C, KV_WIDTH), functools.partial(lambda b, i, part: (b, 0, part), part=part))
              for part in range(2)]
    row = pl.BlockSpec((1, ATT_WIDTH), lambda b, i: (0, 0))
    if local:
        for part in range(2):
            in_specs += [
                pl.BlockSpec((None, T, KV_WIDTH), functools.partial(
                    lambda b, i, part: (b, jnp.maximum(i * n_sub - 1, 0), part), part=part)),
                pl.BlockSpec((None, n_sub * T, KV_WIDTH), functools.partial(
                    lambda b, i, part: (b, i, part), part=part)),
                pl.BlockSpec((None, T, KV_WIDTH), functools.partial(
                    lambda b, i, part: (b, jnp.minimum((i + 1) * n_sub, nb - 1), part), part=part))]
        bias = _attn_bias_tables()
        in_specs += cspecs + [pl.BlockSpec(bias.shape, lambda b, i: (0, 0, 0))]
        args += [kv] * 6 + [kv_ctx, kv_ctx, bias]
    else:
        in_specs += cspecs
        args += [kv_ctx, kv_ctx]
    in_specs += [row, row]
    args += [sink_row, g_attn.reshape(1, ATT_WIDTH)]
    return pl.pallas_call(
        functools.partial(_attn_kernel, local=local, n_blocks=nb),
        out_shape=jax.ShapeDtypeStruct((B, L, ATT_WIDTH), BF16),
        grid=(B, nb // n_sub),
        in_specs=in_specs,
        out_specs=qspec,
        scratch_shapes=[pltpu.VMEM((n_sub * T, ATT_WIDTH), F32)],
        compiler_params=_cp(("parallel", "arbitrary")),
        name="attention_local" if local else "attention_context",
    )(*args)


def _halo_specs(ts, L, width, col_block):
    nb8 = L // HALO
    per = ts // HALO
    prev = pl.BlockSpec((None, HALO, width),
                        lambda b, i: (b, jnp.maximum(i * per - 1, 0), col_block))
    cur = pl.BlockSpec((None, ts, width), lambda b, i: (b, i, col_block))
    nxt = pl.BlockSpec((None, HALO, width),
                       lambda b, i: (b, jnp.minimum((i + 1) * per, nb8 - 1), col_block))
    return [prev, cur, nxt]


def _with_halo(prev_ref, cur_ref, next_ref):
    i = pl.program_id(1)
    last = pl.num_programs(1) - 1
    prev = jnp.where(i > 0, prev_ref[...], 0.0)
    nxt = jnp.where(i < last, next_ref[...], 0.0)
    return jnp.concatenate([prev, cur_ref[...], nxt], axis=0)


def _pool_kernel(prev_ref, cur_ref, next_ref, w_ref, scale_ref, g_ref, o_ref, *, seq_len):
    ts = cur_ref.shape[0]
    ext = _with_halo(prev_ref, cur_ref, next_ref)
    pos = pl.program_id(1) * ts + lax.broadcasted_iota(jnp.int32, (ts, 1), 0)
    outs = []
    for gidx, win in enumerate(POOL_WINDOWS):
        cols = slice(gidx * POOL_GROUP, (gidx + 1) * POOL_GROUP)
        run = ext[:, cols]
        step = 1
        while step < win:
            run = run[:run.shape[0] - step] + run[step:]
            step *= 2
        lo = HALO - win // 2
        total = run[lo:lo + ts]
        cnt = (jnp.minimum(pos + (win - win // 2), seq_len) - jnp.maximum(pos - win // 2, 0))
        mean = total * (1.0 / cnt.astype(F32))
        outs.append(_dot3(mean - cur_ref[:, cols], w_ref[gidx]))
    y = jnp.concatenate(outs, axis=-1) * scale_ref[...]
    y = y * lax.rsqrt(jnp.mean(y * y, axis=-1, keepdims=True) + EPS) * g_ref[...]
    o_ref[...] = y.astype(o_ref.dtype)


def _pool_mixer(proj, pool_w, pool_scale, g_pool):
    B, L, _ = proj.shape
    ts = _tile(L, 512)
    col_block = 0
    row = pl.BlockSpec((1, POOL_WIDTH), lambda b, i: (0, 0))
    return pl.pallas_call(
        functools.partial(_pool_kernel, seq_len=L),
        out_shape=jax.ShapeDtypeStruct((B, L, POOL_WIDTH), BF16),
        grid=(B, L // ts),
        in_specs=_halo_specs(ts, L, POOL_WIDTH, col_block) + [
            pl.BlockSpec(pool_w.shape, lambda b, i: (0, 0, 0)), row, row],
        out_specs=pl.BlockSpec((None, ts, POOL_WIDTH), lambda b, i: (b, i, 0)),
        compiler_params=_cp(("parallel", "arbitrary")),
        name="pool_mixer",
    )(proj, proj, proj, pool_w, pool_scale.reshape(1, POOL_WIDTH), g_pool.reshape(1, POOL_WIDTH))


def _short_conv_kernel(*refs):
    halo_refs = refs[:9]
    w_ref, b_ref = refs[9:11]
    out_refs = refs[11:]
    ts = out_refs[0].shape[0]
    for part in range(HY_ORDER + 1):
        ext = _with_halo(*halo_refs[3 * part:3 * part + 3])
        cols = slice(part * HY_WIDTH, (part + 1) * HY_WIDTH)
        acc = b_ref[:, cols] + ext[HALO - 1:HALO - 1 + ts] * w_ref[0:1, cols]
        acc = acc + ext[HALO:HALO + ts] * w_ref[1:2, cols]
        acc = acc + ext[HALO + 1:HALO + 1 + ts] * w_ref[2:3, cols]
        out_refs[part][...] = acc


def _short_conv(proj, conv_w, conv_b):
    B, L, _ = proj.shape
    ts = _tile(L, 512)
    first = POOL_WIDTH // HY_WIDTH
    assert first * HY_WIDTH == POOL_WIDTH
    specs = []
    for part in range(HY_ORDER + 1):
        specs += _halo_specs(ts, L, HY_WIDTH, first + part)
    n_col = (HY_ORDER + 1) * HY_WIDTH
    out = pl.BlockSpec((None, ts, HY_WIDTH), lambda b, i: (b, i, 0))
    return pl.pallas_call(
        _short_conv_kernel,
        out_shape=tuple(jax.ShapeDtypeStruct((B, L, HY_WIDTH), F32) for _ in range(HY_ORDER + 1)),
        grid=(B, L // ts),
        in_specs=specs + [pl.BlockSpec((HY_SHORT, n_col), lambda b, i: (0, 0)),
                          pl.BlockSpec((1, n_col), lambda b, i: (0, 0))],
        out_specs=(out,) * (HY_ORDER + 1),
        compiler_params=_cp(("parallel", "arbitrary")),
        name="hyena_short_conv",
    )(*([proj] * 9), conv_w, conv_b.reshape(1, n_col))


def _filter_tables(L):
    m = np.arange(L, dtype=np.float32)
    t = (m / np.float32(max(L - 1, 1))).astype(np.float32)
    w = (np.float32(2.0 * math.pi) * m / np.float32(L)).astype(np.float32)
    f = np.linspace(1e-4, HY_BANDS - 1, HY_BANDS, dtype=np.float32)
    ang = (w[:, None] * f).astype(np.float64)
    z = np.concatenate([t[:, None].astype(np.float64), np.cos(ang), -np.sin(ang)], axis=-1)
    z = np.pad(z, ((0, 0), (0, LANES - HY_EMB)))
    max_decay = math.log(HY_TARGET) / HY_SHORT_DECAY_PCT
    min_decay = math.log(HY_TARGET) / HY_LONG_DECAY_PCT
    deltas = np.linspace(min_decay, max_decay, HY_WIDTH, dtype=np.float32)
    decay = np.exp(-t[:, None].astype(np.float64) * np.abs(deltas)[None].astype(np.float64))
    return jnp.asarray(z, F32), jnp.asarray(decay, F32)


def _filter_kernel(z_ref, decay_ref, w1_ref, b1_ref, w2_ref, b2_ref, w3_ref, b3_ref, fr_ref, o_ref):
    tl = z_ref.shape[0]
    fr = fr_ref[...]
    a = jnp.sin(fr * (_dot3(z_ref[...], w1_ref[...]) + b1_ref[...]))
    a = jnp.sin(fr * (_dot3(a, w2_ref[...]) + b2_ref[...]))
    h = _dot3(a, w3_ref[...]) + b3_ref[...]
    h = h * jnp.tile(decay_ref[...], (1, 2 * HY_ORDER))
    row = pl.program_id(0) * tl + lax.broadcasted_iota(jnp.int32, h.shape, 0)
    col = lax.broadcasted_iota(jnp.int32, h.shape, 1)
    backward = (col // HY_WIDTH) % 2 == 1
    o_ref[...] = jnp.where(backward & (row == 0), 0.0, h)


def _hyena_filter_taps(L, w1, b1, w2, b2, w3, b3, freq):
    z, decay = _filter_tables(L)
    hid = w1.shape[1]
    n_out = w3.shape[1]
    tl = _tile(L, 512)
    w1p = jnp.pad(w1, ((0, LANES - HY_EMB), (0, 0)))
    full = lambda a: pl.BlockSpec(a.shape, lambda i: (0,) * a.ndim)
    args = [w1p, b1.reshape(1, hid), w2, b2.reshape(1, hid), w3, b3.reshape(1, n_out),
            freq.reshape(1, hid)]
    return pl.pallas_call(
        _filter_kernel,
        out_shape=jax.ShapeDtypeStruct((L, n_out), F32),
        grid=(L // tl,),
        in_specs=[pl.BlockSpec((tl, LANES), lambda i: (i, 0)),
                  pl.BlockSpec((tl, HY_WIDTH), lambda i: (i, 0))] + [full(a) for a in args],
        out_specs=pl.BlockSpec((tl, n_out), lambda i: (i, 0)),
        compiler_params=_cp(("arbitrary",)),
        name="hyena_filter_taps",
    )(z, decay, *args)


def _fft_matrices(n1, a_in):
    n = n1 * LANES
    k1 = np.arange(n1)
    a = np.arange(a_in)
    th1 = 2.0 * np.pi * np.outer(k1, a) / n1
    c1, s1 = np.cos(th1), np.sin(th1)
    m1_complex = np.block([[c1, s1], [-s1, c1]])
    m1_real = np.concatenate([c1, -s1], axis=0)
    m3 = np.block([[c1.T, -s1.T], [s1.T, c1.T]])
    b = np.arange(LANES)
    k2 = np.arange(LANES)
    k = k1[:, None, None] + n1 * k2[None, :, None]
    th2 = 2.0 * np.pi * (k * b[None, None, :] % n) / n
    c2, s2 = np.cos(th2), np.sin(th2)
    fwd = np.concatenate([np.concatenate([c2, s2], axis=2),
                          np.concatenate([-s2, c2], axis=2)], axis=1)
    c2t, s2t = np.swapaxes(c2, 1, 2), np.swapaxes(s2, 1, 2)
    inv = np.concatenate([np.concatenate([c2t, -s2t], axis=2),
                          np.concatenate([s2t, c2t], axis=2)], axis=1)
    as_bf16 = lambda m: jnp.asarray(m, F32).astype(BF16)
    return as_bf16(m1_complex), as_bf16(m1_real), as_bf16(m3), as_bf16(fwd), as_bf16(inv)


def _fft_stage1_kernel(x_ref, m_ref, o_ref, *, complex_in):
    n1 = o_ref.shape[1]
    for s in range(o_ref.shape[2]):
        if complex_in:
            x = jnp.concatenate([x_ref[0, :, s, :], x_ref[1, :, s, :]], axis=0)
        else:
            x = x_ref[:, s, :]
        y = _dot(m_ref[...], x.astype(BF16))
        o_ref[0, :, s, :] = y[:n1]
        o_ref[1, :, s, :] = y[n1:]


def _fft_stage1(x, m1, n1, complex_in):
    P, A, Wt = x.shape[0], x.shape[-3], x.shape[-1]
    tw = _tile(Wt, 512, LANES)
    if complex_in:
        xspec = pl.BlockSpec((None, 2, A, FFT_ROWS, tw), lambda p, i, j: (p, 0, 0, i, j))
    else:
        xspec = pl.BlockSpec((None, A, FFT_ROWS, tw), lambda p, i, j: (p, 0, i, j))
    return pl.pallas_call(
        functools.partial(_fft_stage1_kernel, complex_in=complex_in),
        out_shape=jax.ShapeDtypeStruct((P, 2, n1, LANES, Wt), F32),
        grid=(P, LANES // FFT_ROWS, Wt // tw),
        in_specs=[xspec, pl.BlockSpec(m1.shape, lambda p, i, j: (0, 0))],
        out_specs=pl.BlockSpec((None, 2, n1, FFT_ROWS, tw), lambda p, i, j: (p, 0, 0, i, j)),
        compiler_params=_cp(("parallel", "arbitrary", "arbitrary")),
        name="fft_stage1",
    )(x, m1)


def _filter_spectrum_kernel(y_ref, g_ref, o0_ref, o1_ref, *, scale):
    kt = y_ref.shape[1]
    for t in range(kt):
        y = jnp.concatenate([y_ref[0, t], y_ref[1, t]], axis=0).astype(BF16)
        f = _dot(g_ref[t], y)
        fr, fi = f[:LANES], f[LANES:]
        for o, o_ref in enumerate((o0_ref, o1_ref)):
            fw = slice((2 * o) * HY_WIDTH, (2 * o + 1) * HY_WIDTH)
            bw = slice((2 * o + 1) * HY_WIDTH, (2 * o + 2) * HY_WIDTH)
            o_ref[0, t] = (fr[:, fw] + fr[:, bw]) * scale
            o_ref[1, t] = (fi[:, fw] - fi[:, bw]) * scale


def _filter_spectrum(taps, mats, n1):
    L, n_col = taps.shape
    a_in = L // LANES
    _, m1_real, _, g_fwd, _ = mats
    y = _fft_stage1(taps.reshape(1, a_in, LANES, n_col), m1_real, n1, complex_in=False)
    kt = _tile(n1, 4, 1)
    out = jax.ShapeDtypeStruct((2, n1, LANES, HY_WIDTH), F32)
    ospec = pl.BlockSpec((2, kt, LANES, HY_WIDTH), lambda i: (0, i, 0, 0))
    return pl.pallas_call(
        functools.partial(_filter_spectrum_kernel, scale=1.0 / (n1 * LANES)),
        out_shape=(out, out),
        grid=(n1 // kt,),
        in_specs=[pl.BlockSpec((None, 2, kt, LANES, n_col), lambda i: (0, 0, i, 0, 0)),
                  pl.BlockSpec((kt, 2 * LANES, 2 * LANES), lambda i: (i, 0, 0))],
        out_specs=(ospec, ospec),
        compiler_params=_cp(("arbitrary",)),
        name="hyena_filter_spectrum",
    )(y, g_fwd)


def _fft_mid_kernel(y_ref, kf_ref, g_ref, gi_ref, o_ref):
    kt = y_ref.shape[1]
    for t in range(kt):
        y = jnp.concatenate([y_ref[0, t], y_ref[1, t]], axis=0).astype(BF16)
        f = _dot(g_ref[t], y)
        fr, fi = f[:LANES], f[LANES:]
        kr, ki = kf_ref[0, t], kf_ref[1, t]
        p = jnp.concatenate([fr * kr - fi * ki, fr * ki + fi * kr], axis=0).astype(BF16)
        u = _dot(gi_ref[t], p)
        o_ref[0, t] = u[:LANES]
        o_ref[1, t] = u[LANES:]


def _fft_mid(y, kf, g_fwd, g_inv):
    P, _, n1, _, W = y.shape
    kt = _tile(n1, 8, 1)
    yspec = pl.BlockSpec((None, 2, kt, LANES, W), lambda i, p: (p, 0, i, 0, 0))
    gspec = pl.BlockSpec((kt, 2 * LANES, 2 * LANES), lambda i, p: (i, 0, 0))
    return pl.pallas_call(
        _fft_mid_kernel,
        out_shape=jax.ShapeDtypeStruct(y.shape, F32),
        grid=(n1 // kt, P),
        in_specs=[yspec, pl.BlockSpec((2, kt, LANES, W), lambda i, p: (0, i, 0, 0)), gspec, gspec],
        out_specs=yspec,
        compiler_params=_cp(("arbitrary", "arbitrary")),
        name="fft_mid",
    )(y, kf, g_fwd, g_inv)


def _fft_stage3_kernel(u_ref, m_ref, z_ref, x_ref, bias_ref, g_ref, o_ref, *, normalise):
    a_out = z_ref.shape[1]
    bias = bias_ref[...]
    for s in range(o_ref.shape[2]):
        u = jnp.concatenate([u_ref[0, :, s, :], u_ref[1, :, s, :]], axis=0).astype(BF16)
        conv = _dot(m_ref[...], u)
        for part in range(2):
            y = x_ref[part, :, s, :] * (
                conv[part * a_out:(part + 1) * a_out] + z_ref[part, :, s, :] * bias)
            if normalise:
                y = y * lax.rsqrt(jnp.mean(y * y, axis=-1, keepdims=True) + EPS) * g_ref[...]
            o_ref[part, :, s, :] = y


def _fft_stage3(u, m3, z, gate, bias_row, norm_g=None):
    P, _, n1, _, W = u.shape
    A = z.shape[2]
    pair = pl.BlockSpec((None, 2, A, FFT_ROWS, W), lambda p, i: (p, 0, 0, i, 0))
    row = pl.BlockSpec((1, W), lambda p, i: (0, 0))
    normalise = norm_g is not None
    return pl.pallas_call(
        functools.partial(_fft_stage3_kernel, normalise=normalise),
        out_shape=jax.ShapeDtypeStruct(z.shape, F32),
        grid=(P, LANES // FFT_ROWS),
        in_specs=[pl.BlockSpec((None, 2, n1, FFT_ROWS, W), lambda p, i: (p, 0, 0, i, 0)),
                  pl.BlockSpec(m3.shape, lambda p, i: (0, 0)),
                  pair, pair, row, row],
        out_specs=pair,
        compiler_params=_cp(("parallel", "arbitrary")),
        name="fft_stage3",
    )(u, m3, z, gate, bias_row, norm_g.reshape(1, W) if normalise else bias_row)


def _hyena_mixer(v, gates, kfs, d_bias, norm_g, mats, n1):
    B, L, W = v.shape
    pair_shape = (B // 2, 2, L // LANES, LANES, W)
    m1_complex, _, m3, g_fwd, g_inv = mats
    z = v.reshape(pair_shape)
    for o in range(HY_ORDER):
        y = _fft_stage1(z, m1_complex, n1, complex_in=True)
        u = _fft_mid(y, kfs[o], g_fwd, g_inv)
        z = _fft_stage3(u, m3, z, gates[o].reshape(pair_shape), d_bias[o].reshape(1, W),
                        norm_g if o == HY_ORDER - 1 else None)
    return z.reshape(B, L, W)


def kernel(x, c, ctx, c_ctx, w_mod, b_mod, norm1_g, w_in, attn_sink, pool_w, pool_scale,
           hy_conv_w, hy_conv_b, hy_f_w1, hy_f_b1, hy_f_w2, hy_f_b2, hy_f_w3, hy_f_b3,
           hy_f_freq, hy_bias, g_attn, g_pool, g_hyena, w_out, norm2_g,
           ff_w1, ff_w3, ff_w2, router_w, moe_w1, moe_w3, moe_w2, final_g):
    B, S, D = x.shape
    C = ctx.shape[1]
    depth = w_mod.shape[0]
    assert B % 2 == 0 and S % BLOCK == 0
    fft_mats = {}

    c_rows = jnp.concatenate([c, c_ctx[None], jnp.zeros((SUBLANES - B - 1, D), F32)], axis=0)
    xc = ctx

    def hyena_branch(l, proj, seq_len):
        rows = -(-seq_len // FFT_MIN_ROWS) * FFT_MIN_ROWS
        n1 = 2 * rows // LANES
        if rows not in fft_mats:
            fft_mats[rows] = _fft_matrices(n1, rows // LANES)
        mats = fft_mats[rows]
        taps = _hyena_filter_taps(seq_len, hy_f_w1[l], hy_f_b1[l], hy_f_w2[l], hy_f_b2[l],
                                  hy_f_w3[l], hy_f_b3[l], hy_f_freq[l])
        v, x1, x2 = _short_conv(proj, hy_conv_w[l], hy_conv_b[l])
        if seq_len < rows:
            pad = lambda t: jnp.pad(t, ((0, 0), (0, rows - seq_len), (0, 0)))
            taps = jnp.pad(taps, ((0, rows - seq_len), (0, 0)))
            v, x1, x2 = pad(v), pad(x1), pad(x2)
        kfs = _filter_spectrum(taps, mats, n1)
        z = _hyena_mixer(v, (x1, x2), kfs, hy_bias[l], g_hyena[l], mats, n1)
        return z[:, :seq_len]

    def mix(l, proj, y_att, res, gate, w_out_l, seq_len, norm=None):
        y_pool = _pool_mixer(proj, pool_w[l], pool_scale[l], g_pool[l])
        y_hy = hyena_branch(l, proj, seq_len)
        return _mix_out([y_att, y_pool, y_hy], w_out_l, res, gate, norm)

    for l in range(depth):
        last = l == depth - 1
        mod = _adaln(c_rows, w_mod[l], b_mod[l])
        sh1, sc1, g1, sh2, sc2, g2 = [mod[:B, j * D:(j + 1) * D] for j in range(6)]
        csh1, csc1, cg1, csh2, csc2, cg2 = [mod[B:B + 1, j * D:(j + 1) * D] for j in range(6)]
        w_in_l = w_in[l].astype(BF16)
        w_out_l = w_out[l].astype(BF16)

        if last:
            hc = _norm_mod(xc, norm1_g[l], csh1, csc1)
            kv_w = w_in_l[:, ATT_WIDTH:ATT_WIDTH + 2 * KV_WIDTH]
            kv_c = _matmul(hc, kv_w, out_dtype=BF16)
        else:
            q_c, kv_c, proj_c = _in_proj(xc, norm1_g[l], csh1, csc1, w_in_l, rope=False)
            y_att_c = _attention(q_c, None, kv_c, attn_sink[l], g_attn[l])
            xc_new, hc2 = mix(l, proj_c, y_att_c, xc, cg1, w_out_l, C, (norm2_g[l], csh2, csc2))
            i = l // 2
            if l % 2 == 0:
                hid_c = _swiglu_hidden(hc2, ff_w1[i].astype(BF16), ff_w3[i].astype(BF16))
                xc_new = _matmul_gated_residual([hid_c], ff_w2[i].astype(BF16), xc_new, cg2)
            else:
                raise NotImplementedError("context tokens through an expert layer")

        q, kv, proj = _in_proj(x, norm1_g[l], sh1, sc1, w_in_l, rope=True)
        y_att = _attention(q, kv, kv_c, attn_sink[l], g_attn[l])

        i = l // 2
        if l % 2 == 0:
            x, h2 = mix(l, proj, y_att, x, g1, w_out_l, S, (norm2_g[l], sh2, sc2))
            hid = _swiglu_hidden(h2, ff_w1[i].astype(BF16), ff_w3[i].astype(BF16))
            x = _matmul_gated_residual([hid], ff_w2[i].astype(BF16), x, g2)
            if last:
                x = _rmsnorm(x, final_g, F32)
        else:
            assert last
            x = mix(l, proj, y_att, x, g1, w_out_l, S)
            h2, gi, gw, counts = _norm_mod_router(x, norm2_g[l], sh2, sc2, router_w[i])
            x = _moe_layer(x, h2, gi, gw, counts, g2, final_g, moe_w1[i], moe_w3[i], moe_w2[i])
        if not last:
            xc = xc_new
    return x
```

```python
import functools
import math

import numpy as np
import jax
import jax.numpy as jnp
from jax import lax
from jax.experimental import pallas as pl
from jax.experimental.pallas import tpu as pltpu

F32 = jnp.float32
BF16 = jnp.bfloat16

EPS = 1e-6
NEG = -1e30
GRID_W = 64
ATT_HEADS = 8
ATT_KV_HEADS = 2
ATT_GROUP = ATT_HEADS // ATT_KV_HEADS
HEAD_DIM = 128
ATT_WIDTH = ATT_HEADS * HEAD_DIM
KV_WIDTH = ATT_KV_HEADS * HEAD_DIM
WINDOW = 128
BLOCK = 128
ROPE_BASE = 10000.0
POOL_WINDOWS = (2, 4, 8, 16)
POOL_GROUP = 128
POOL_WIDTH = POOL_GROUP * len(POOL_WINDOWS)
HY_WIDTH = 512
HY_ORDER = 2
HY_SHORT = 3
HY_EMB = 33
HY_BANDS = (HY_EMB - 1) // 2
HY_SHORT_DECAY_PCT = 0.3
HY_LONG_DECAY_PCT = 1.5
HY_TARGET = 1e-2
N_EXPERTS = 8
TOP_K = 2

LANES = 128
SUBLANES = 8
HALO = SUBLANES
FFT_MIN_ROWS = 2048
FFT_ROWS = SUBLANES
MOE_OUT_K_CHUNKS = 4
VMEM_LIMIT = 56 * 1024 * 1024


def _cp(sem):
    return pltpu.CompilerParams(dimension_semantics=sem, vmem_limit_bytes=VMEM_LIMIT)


def _tile(n, pref, mult=SUBLANES):
    if n <= pref:
        return n
    t = (pref // mult) * mult
    while t >= mult:
        if n % t == 0:
            return t
        t -= mult
    return n


def _split_bf16(x):
    hi = x.astype(BF16)
    lo = (x - hi.astype(F32)).astype(BF16)
    return hi, lo


def _dot(a, b):
    return jnp.dot(a, b, preferred_element_type=F32)


def _pack_bf16(lo, hi):
    lo_bits = lax.bitcast_convert_type(lo.astype(BF16).astype(F32), jnp.uint32)
    hi_bits = lax.bitcast_convert_type(hi.astype(BF16).astype(F32), jnp.uint32)
    return (lo_bits >> 16) | (hi_bits & jnp.uint32(0xFFFF0000))


def _pack_bf16_pairs(x):
    n = x.shape[-1] // 2
    return _pack_bf16(x[:, :n], x[:, n:])


def _unpack_bf16_pairs(u):
    lo = lax.bitcast_convert_type(u << 16, F32).astype(BF16)
    hi = lax.bitcast_convert_type(u & jnp.uint32(0xFFFF0000), F32).astype(BF16)
    return lo, hi


def _dot3(a, b):
    ah, al = _split_bf16(a)
    bh, bl = _split_bf16(b)
    return _dot(ah, bh) + (_dot(ah, bl) + _dot(al, bh))


def _adaln_kernel(c_ref, w_ref, b_ref, o_ref):
    c = c_ref[...]
    a = c * (1.0 / (1.0 + jnp.exp(-c)))
    o_ref[...] = _dot3(a, w_ref[...]) + b_ref[...]


def _adaln(c_rows, w, b):
    R, D = c_rows.shape
    N = w.shape[1]
    tn = _tile(N, 1536, LANES)
    return pl.pallas_call(
        _adaln_kernel,
        out_shape=jax.ShapeDtypeStruct((R, N), F32),
        grid=(N // tn,),
        in_specs=[pl.BlockSpec((R, D), lambda j: (0, 0)),
                  pl.BlockSpec((D, tn), lambda j: (0, j)),
                  pl.BlockSpec((1, tn), lambda j: (0, j))],
        out_specs=pl.BlockSpec((R, tn), lambda j: (0, j)),
        compiler_params=_cp(("arbitrary",)),
        name="adaln",
    )(c_rows, w, b.reshape(1, N))


def _norm_mod_kernel(x_ref, g_ref, sh_ref, sc_ref, o_ref):
    x = x_ref[...]
    y = x * lax.rsqrt(jnp.mean(x * x, axis=-1, keepdims=True) + EPS) * g_ref[...]
    o_ref[...] = (y * (1.0 + sc_ref[...]) + sh_ref[...]).astype(o_ref.dtype)


def _bcast_map(arr):
    if arr.shape[0] == 1:
        return lambda b, i: (0, 0, 0)
    return lambda b, i: (b, 0, 0)


def _norm_mod(x, g, shift, scale):
    B, L, D = x.shape
    ts = _tile(L, 512)
    shift = shift[:, None, :]
    scale = scale[:, None, :]
    return pl.pallas_call(
        _norm_mod_kernel,
        out_shape=jax.ShapeDtypeStruct((B, L, D), BF16),
        grid=(B, L // ts),
        in_specs=[pl.BlockSpec((None, ts, D), lambda b, i: (b, i, 0)),
                  pl.BlockSpec((1, D), lambda b, i: (0, 0)),
                  pl.BlockSpec((None, 1, D), _bcast_map(shift)),
                  pl.BlockSpec((None, 1, D), _bcast_map(scale))],
        out_specs=pl.BlockSpec((None, ts, D), lambda b, i: (b, i, 0)),
        compiler_params=_cp(("parallel", "parallel")),
        name="norm_mod",
    )(x, g.reshape(1, D), shift, scale)


def _route(h, rw_ref, tri_ref, run_ref, gi_ref, gw_ref, cnt_ref):
    logits = _dot3(h, rw_ref[...])
    lane = lax.broadcasted_iota(jnp.int32, logits.shape, 1)
    logits = jnp.where(lane < N_EXPERTS, logits, NEG)
    m1 = jnp.max(logits, axis=-1, keepdims=True)
    i1 = jnp.min(jnp.where(logits == m1, lane, LANES), axis=-1, keepdims=True)
    rest = jnp.where(lane == i1, NEG, logits)
    m2 = jnp.max(rest, axis=-1, keepdims=True)
    i2 = jnp.min(jnp.where(rest == m2, lane, LANES), axis=-1, keepdims=True)
    e2 = jnp.exp(m2 - m1)
    w1 = 1.0 / (1.0 + e2)
    w2 = e2 * w1
    pick1 = lane == i1
    pick2 = lane == i2
    both = jnp.where(pick1 | pick2, 1.0, 0.0)
    before = _dot(tri_ref[...], both.astype(BF16)) + run_ref[...]
    r1 = jnp.sum(jnp.where(pick1, before, 0.0), axis=-1, keepdims=True).astype(jnp.int32)
    r2 = jnp.sum(jnp.where(pick2, before, 0.0), axis=-1, keepdims=True).astype(jnp.int32)
    run_ref[...] = run_ref[...] + jnp.sum(both, axis=0, keepdims=True)
    cnt_ref[...] = run_ref[...]
    gi_ref[...] = jnp.where(lane == 0, i1, jnp.where(lane == 1, i2,
                            jnp.where(lane == 2, r1, jnp.where(lane == 3, r2, 0))))
    gw_ref[...] = jnp.where(lane == 0, w1, jnp.where(lane == 1, w2, 0.0))


def _norm_mod_router_kernel(x_ref, g_ref, sh_ref, sc_ref, rw_ref, tri_ref,
                            o_ref, gi_ref, gw_ref, cnt_ref, run_ref):
    @pl.when((pl.program_id(0) == 0) & (pl.program_id(1) == 0))
    def _():
        run_ref[...] = jnp.zeros(run_ref.shape, run_ref.dtype)

    x = x_ref[...]
    y = x * lax.rsqrt(jnp.mean(x * x, axis=-1, keepdims=True) + EPS) * g_ref[...]
    h = y * (1.0 + sc_ref[...]) + sh_ref[...]
    o_ref[...] = _pack_bf16_pairs(h)
    _route(h, rw_ref, tri_ref, run_ref, gi_ref, gw_ref, cnt_ref)


def _norm_mod_router(x, g, shift, scale, router_w):
    B, L, D = x.shape
    ts = _tile(L, 512)
    shift = shift[:, None, :]
    scale = scale[:, None, :]
    rw = jnp.pad(router_w, ((0, 0), (0, LANES - router_w.shape[1])))
    tri = jnp.asarray(np.tril(np.ones((ts, ts), np.float32), -1), BF16)
    row = pl.BlockSpec((None, ts, D), lambda b, i: (b, i, 0))
    small = pl.BlockSpec((None, ts, LANES), lambda b, i: (b, i, 0))
    return pl.pallas_call(
        _norm_mod_router_kernel,
        out_shape=(jax.ShapeDtypeStruct((B, L, D // 2), jnp.uint32),
                   jax.ShapeDtypeStruct((B, L, LANES), jnp.int32),
                   jax.ShapeDtypeStruct((B, L, LANES), F32),
                   jax.ShapeDtypeStruct((1, LANES), F32)),
        grid=(B, L // ts),
        in_specs=[row,
                  pl.BlockSpec((1, D), lambda b, i: (0, 0)),
                  pl.BlockSpec((None, 1, D), _bcast_map(shift)),
                  pl.BlockSpec((None, 1, D), _bcast_map(scale)),
                  pl.BlockSpec((D, LANES), lambda b, i: (0, 0)),
                  pl.BlockSpec((ts, ts), lambda b, i: (0, 0))],
        out_specs=(pl.BlockSpec((None, ts, D // 2), lambda b, i: (b, i, 0)), small, small,
                   pl.BlockSpec((1, LANES), lambda b, i: (0, 0))),
        scratch_shapes=[pltpu.VMEM((1, LANES), F32)],
        compiler_params=_cp(("arbitrary", "arbitrary")),
        name="norm_mod_router",
    )(x, g.reshape(1, D), shift, scale, rw, tri)


def _mix_out_kernel(*refs, n_a, emit_h):
    a_refs = refs[:n_a]
    w_refs = refs[n_a:2 * n_a]
    res_ref, gate_ref = refs[2 * n_a:2 * n_a + 2]
    acc = _dot(a_refs[0][...].astype(BF16), w_refs[0][...])
    for a_ref, w_ref in zip(a_refs[1:], w_refs[1:]):
        acc = acc + _dot(a_ref[...].astype(BF16), w_ref[...])
    x = res_ref[...] + gate_ref[...] * acc
    if emit_h:
        g_ref, sh_ref, sc_ref, x_ref, h_ref = refs[2 * n_a + 2:]
        y = x * lax.rsqrt(jnp.mean(x * x, axis=-1, keepdims=True) + EPS) * g_ref[...]
        h_ref[...] = (y * (1.0 + sc_ref[...]) + sh_ref[...]).astype(h_ref.dtype)
    else:
        x_ref, = refs[2 * n_a + 2:]
    x_ref[...] = x


def _mix_out(a_list, w, res, gate, norm=None):
    B, L, D = res.shape
    tm = _tile(L, 512)
    emit_h = norm is not None
    gate = gate[:, None, :]
    in_specs = [pl.BlockSpec((None, tm, a.shape[-1]), lambda b, i: (b, i, 0)) for a in a_list]
    off = 0
    for a in a_list:
        k = a.shape[-1]
        assert off % k == 0
        in_specs.append(pl.BlockSpec((k, D), functools.partial(lambda b, i, blk: (blk, 0), blk=off // k)))
        off += k
    row = pl.BlockSpec((None, tm, D), lambda b, i: (b, i, 0))
    in_specs += [row, pl.BlockSpec((None, 1, D), _bcast_map(gate))]
    args = [*a_list, *([w] * len(a_list)), res, gate]
    out_shape = [jax.ShapeDtypeStruct((B, L, D), F32)]
    out_specs = [row]
    if emit_h:
        g, shift, scale = norm
        shift, scale = shift[:, None, :], scale[:, None, :]
        in_specs += [pl.BlockSpec((1, D), lambda b, i: (0, 0)),
                     pl.BlockSpec((None, 1, D), _bcast_map(shift)),
                     pl.BlockSpec((None, 1, D), _bcast_map(scale))]
        args += [g.reshape(1, D), shift, scale]
        out_shape.append(jax.ShapeDtypeStruct((B, L, D), BF16))
        out_specs.append(row)
    out = pl.pallas_call(
        functools.partial(_mix_out_kernel, n_a=len(a_list), emit_h=emit_h),
        out_shape=tuple(out_shape),
        grid=(B, L // tm),
        in_specs=in_specs,
        out_specs=tuple(out_specs),
        compiler_params=_cp(("parallel", "parallel")),
        name="mix_out",
    )(*args)
    return out if emit_h else out[0]


def _rmsnorm_kernel(x_ref, g_ref, o_ref):
    x = x_ref[...]
    y = x * lax.rsqrt(jnp.mean(x * x, axis=-1, keepdims=True) + EPS) * g_ref[...]
    o_ref[...] = y.astype(o_ref.dtype)


def _rmsnorm(x, g, out_dtype):
    B, L, D = x.shape
    ts = _tile(L, 512)
    return pl.pallas_call(
        _rmsnorm_kernel,
        out_shape=jax.ShapeDtypeStruct((B, L, D), out_dtype),
        grid=(B, L // ts),
        in_specs=[pl.BlockSpec((None, ts, D), lambda b, i: (b, i, 0)),
                  pl.BlockSpec((1, D), lambda b, i: (0, 0))],
        out_specs=pl.BlockSpec((None, ts, D), lambda b, i: (b, i, 0)),
        compiler_params=_cp(("parallel", "parallel")),
        name="rmsnorm",
    )(x, g.reshape(1, D))


def _mm_kernel(a_ref, w_ref, o_ref):
    o_ref[...] = _dot(a_ref[...], w_ref[...]).astype(o_ref.dtype)


def _matmul(a, w, out_dtype=F32, tm=1024, tn=512):
    B, L, K = a.shape
    N = w.shape[1]
    tm = _tile(L, tm)
    tn = _tile(N, tn, LANES)
    return pl.pallas_call(
        _mm_kernel,
        out_shape=jax.ShapeDtypeStruct((B, L, N), out_dtype),
        grid=(B, L // tm, N // tn),
        in_specs=[pl.BlockSpec((None, tm, K), lambda b, i, j: (b, i, 0)),
                  pl.BlockSpec((K, tn), lambda b, i, j: (0, j))],
        out_specs=pl.BlockSpec((None, tm, tn), lambda b, i, j: (b, i, j)),
        compiler_params=_cp(("parallel", "parallel", "arbitrary")),
        name="matmul",
    )(a, w)


def _mm_res_kernel(*refs, n_a):
    a_refs = refs[:n_a]
    w_refs = refs[n_a:2 * n_a]
    res_ref, gate_ref, o_ref = refs[2 * n_a:]
    acc = _dot(a_refs[0][...].astype(BF16), w_refs[0][...])
    for a_ref, w_ref in zip(a_refs[1:], w_refs[1:]):
        acc = acc + _dot(a_ref[...].astype(BF16), w_ref[...])
    o_ref[...] = res_ref[...] + gate_ref[...] * acc


def _matmul_gated_residual(a_list, w, res, gate, tm=1024, tn=512):
    B, L, N = res.shape
    tm = _tile(L, tm)
    tn = _tile(N, tn, LANES)
    gate = gate[:, None, :]
    widths = [a.shape[-1] for a in a_list]
    unit = math.gcd(*widths) if len(widths) > 1 else widths[0]
    in_specs = [pl.BlockSpec((None, tm, k), lambda b, i, j: (b, i, 0)) for k in widths]
    off = 0
    for k in widths:
        assert off % k == 0 and k % unit == 0
        in_specs.append(pl.BlockSpec((k, tn), functools.partial(
            lambda b, i, j, blk: (blk, j), blk=off // k)))
        off += k
    gmap = (lambda b, i, j: (0, 0, j)) if gate.shape[0] == 1 else (lambda b, i, j: (b, 0, j))
    in_specs += [pl.BlockSpec((None, tm, tn), lambda b, i, j: (b, i, j)),
                 pl.BlockSpec((None, 1, tn), gmap)]
    return pl.pallas_call(
        functools.partial(_mm_res_kernel, n_a=len(a_list)),
        out_shape=jax.ShapeDtypeStruct((B, L, N), F32),
        grid=(B, L // tm, N // tn),
        in_specs=in_specs,
        out_specs=pl.BlockSpec((None, tm, tn), lambda b, i, j: (b, i, j)),
        compiler_params=_cp(("parallel", "parallel", "arbitrary")),
        name="matmul_gated_residual",
    )(*a_list, *([w] * len(a_list)), res, gate)


def _silu(x):
    return x * (1.0 / (1.0 + jnp.exp(-x)))


def _swiglu_kernel(a_ref, w1_ref, w3_ref, o_ref):
    a = a_ref[...]
    o_ref[...] = (_silu(_dot(a, w1_ref[...])) * _dot(a, w3_ref[...])).astype(o_ref.dtype)


def _swiglu_hidden(a, w1, w3, tm=1024, tf=512):
    B, L, D = a.shape
    F = w1.shape[1]
    tm = _tile(L, tm)
    tf = _tile(F, tf, LANES)
    wspec = pl.BlockSpec((D, tf), lambda b, i, j: (0, j))
    return pl.pallas_call(
        _swiglu_kernel,
        out_shape=jax.ShapeDtypeStruct((B, L, F), BF16),
        grid=(B, L // tm, F // tf),
        in_specs=[pl.BlockSpec((None, tm, D), lambda b, i, j: (b, i, 0)), wspec, wspec],
        out_specs=pl.BlockSpec((None, tm, tf), lambda b, i, j: (b, i, j)),
        compiler_params=_cp(("parallel", "parallel", "arbitrary")),
        name="swiglu_hidden",
    )(a, w1, w3)


def _moe_hidden_kernel(te_ref, nt_ref, a_ref, w1_ref, w3_ref, o_ref, w1b_ref, w3b_ref):
    i = pl.program_id(1)
    used = i < nt_ref[0]
    half = a_ref.shape[1]
    new_weights = (i == 0) | (te_ref[i] != te_ref[jnp.maximum(i - 1, 0)])

    @pl.when(used & new_weights)
    def _():
        w1b_ref[...] = w1_ref[...].astype(BF16)
        w3b_ref[...] = w3_ref[...].astype(BF16)

    @pl.when(used)
    def _():
        lo, hi = _unpack_bf16_pairs(a_ref[...])
        gate = _dot(lo, w1b_ref[:half, :]) + _dot(hi, w1b_ref[half:, :])
        up = _dot(lo, w3b_ref[:half, :]) + _dot(hi, w3b_ref[half:, :])
        o_ref[...] = (_silu(gate) * up).astype(o_ref.dtype)

    @pl.when(jnp.logical_not(used))
    def _():
        o_ref[...] = jnp.zeros(o_ref.shape, o_ref.dtype)


def _moe_out_kernel(te_ref, nt_ref, a_ref, w2_ref, o_ref):
    used = pl.program_id(0) < nt_ref[0]

    @pl.when(used)
    def _():
        kc = a_ref.shape[1] // MOE_OUT_K_CHUNKS
        acc = _dot(a_ref[:, :kc], w2_ref[:kc, :].astype(BF16))
        for c in range(1, MOE_OUT_K_CHUNKS):
            acc = acc + _dot(a_ref[:, c * kc:(c + 1) * kc], w2_ref[c * kc:(c + 1) * kc, :].astype(BF16))
        o_ref[...] = _pack_bf16_pairs(acc)

    @pl.when(jnp.logical_not(used))
    def _():
        o_ref[...] = jnp.zeros(o_ref.shape, o_ref.dtype)


def _moe_out_cols(d_model):
    return _tile(d_model, 512, 2 * LANES)


def _moe_experts(a_sorted, tile_expert, n_tiles_used, w1, w3, w2, tm, tf=512):
    R = a_sorted.shape[0]
    E, D, F = w1.shape
    tf = _tile(F, tf, LANES)
    tn = _moe_out_cols(D)
    nt = R // tm
    w13 = pl.BlockSpec((None, D, tf), lambda j, i, te, n: (te[i], 0, j))
    hidden = pl.pallas_call(
        _moe_hidden_kernel,
        out_shape=jax.ShapeDtypeStruct((R, F), BF16),
        grid_spec=pltpu.PrefetchScalarGridSpec(
            num_scalar_prefetch=2,
            grid=(F // tf, nt),
            in_specs=[pl.BlockSpec((tm, D // 2), lambda j, i, te, n: (i, 0)), w13, w13],
            out_specs=pl.BlockSpec((tm, tf), lambda j, i, te, n: (i, j)),
            scratch_shapes=[pltpu.VMEM((D, tf), BF16), pltpu.VMEM((D, tf), BF16)]),
        compiler_params=_cp(("arbitrary", "arbitrary")),
        name="moe_hidden",
    )(tile_expert, n_tiles_used, a_sorted, w1, w3)
    return pl.pallas_call(
        _moe_out_kernel,
        out_shape=jax.ShapeDtypeStruct((R, D // 2), jnp.uint32),
        grid_spec=pltpu.PrefetchScalarGridSpec(
            num_scalar_prefetch=2,
            grid=(nt, D // tn),
            in_specs=[pl.BlockSpec((tm, F), lambda i, j, te, n: (i, 0)),
                      pl.BlockSpec((None, F, tn), lambda i, j, te, n: (te[i], 0, j))],
            out_specs=pl.BlockSpec((tm, tn // 2), lambda i, j, te, n: (i, j))),
        compiler_params=_cp(("arbitrary", "arbitrary")),
        name="moe_out",
    )(tile_expert, n_tiles_used, hidden, w2)


def _unpack_column_blocks(u, block):
    parts = []
    for c in range(u.shape[1] // block):
        w = u[:, c * block:(c + 1) * block]
        parts.append(lax.bitcast_convert_type(w << 16, F32))
        parts.append(lax.bitcast_convert_type(w & jnp.uint32(0xFFFF0000), F32))
    return jnp.concatenate(parts, axis=1)


def _moe_combine_kernel(x_ref, ya_ref, yb_ref, gw_ref, gate_ref, g_ref, o_ref, *, block):
    gw = gw_ref[...]
    moe = (gw[:, 0:1] * _unpack_column_blocks(ya_ref[...], block)
           + gw[:, 1:2] * _unpack_column_blocks(yb_ref[...], block))
    x = x_ref[...] + gate_ref[...] * moe
    y = x * lax.rsqrt(jnp.mean(x * x, axis=-1, keepdims=True) + EPS) * g_ref[...]
    o_ref[...] = y


def _moe_combine_norm(x, ya, yb, gw, gate, g):
    B, L, D = x.shape
    ts = _tile(L, 512)
    row = pl.BlockSpec((None, ts, D), lambda b, i: (b, i, 0))
    packed = pl.BlockSpec((None, ts, D // 2), lambda b, i: (b, i, 0))
    return pl.pallas_call(
        functools.partial(_moe_combine_kernel, block=_moe_out_cols(D) // 2),
        out_shape=jax.ShapeDtypeStruct((B, L, D), F32),
        grid=(B, L // ts),
        in_specs=[row, packed, packed,
                  pl.BlockSpec((None, ts, LANES), lambda b, i: (b, i, 0)),
                  pl.BlockSpec((None, 1, D), lambda b, i: (b, 0, 0)),
                  pl.BlockSpec((1, D), lambda b, i: (0, 0))],
        out_specs=row,
        compiler_params=_cp(("parallel", "parallel")),
        name="moe_combine_norm",
    )(x, ya, yb, gw, gate[:, None, :], g.reshape(1, D))


def _moe_layer(x, h, gi, gw, counts, gate, final_g, w1, w3, w2):
    B, L, D = x.shape
    E = w1.shape[0]
    n_tok = B * L
    n_pair = n_tok * TOP_K
    tm = _tile(n_pair, 1024)
    nt = n_pair // tm + E
    e_pair = gi[..., :TOP_K].reshape(n_pair)
    rank = gi[..., TOP_K:2 * TOP_K].reshape(n_pair)
    counts = counts[0, :E].astype(jnp.int32)
    tiles_per = (counts + tm - 1) // tm
    tile_end = jnp.cumsum(tiles_per)
    tile_start = tile_end - tiles_per
    row_start = jnp.zeros((n_pair,), jnp.int32)
    for e in range(E):
        row_start = jnp.where(e_pair == e, tile_start[e] * tm, row_start)
    pos = row_start + rank
    tile_ids = jnp.arange(nt, dtype=jnp.int32)
    tile_expert = jnp.minimum(
        jnp.sum((tile_ids[:, None] >= tile_end[None, :]).astype(jnp.int32), axis=1), E - 1)
    n_used = tile_end[-1:].astype(jnp.int32)
    tok_pair = jnp.arange(n_pair, dtype=jnp.int32) // TOP_K
    src = (jnp.arange(nt * tm, dtype=jnp.int32) % n_tok).at[pos].set(
        tok_pair, unique_indices=True, mode="promise_in_bounds")
    gather_rows = lambda rows, idx: rows.at[idx].get(mode="promise_in_bounds")
    a_sorted = gather_rows(h.reshape(n_tok, h.shape[-1]), src)
    y_sorted = _moe_experts(a_sorted, tile_expert.astype(jnp.int32), n_used, w1, w3, w2, tm)
    pos2 = pos.reshape(n_tok, TOP_K)
    ya = gather_rows(y_sorted, pos2[:, 0]).reshape(B, L, D // 2)
    yb = gather_rows(y_sorted, pos2[:, 1]).reshape(B, L, D // 2)
    return _moe_combine_norm(x, ya, yb, gw, gate, final_g)


def _rope_tables(n_tokens):
    pos = np.arange(n_tokens)
    row = (pos // GRID_W).astype(np.float32)
    col = (pos % GRID_W).astype(np.float32)
    n_freq = HEAD_DIM // 4
    inv = (np.float32(ROPE_BASE) ** (-np.arange(n_freq, dtype=np.float32) / np.float32(n_freq)))
    ang_r = (row[:, None] * inv).astype(np.float64)
    ang_c = (col[:, None] * inv).astype(np.float64)
    cos = np.concatenate([np.cos(ang_r), np.cos(ang_r), np.cos(ang_c), np.cos(ang_c)], axis=1)
    sin = np.concatenate([-np.sin(ang_r), np.sin(ang_r), -np.sin(ang_c), np.sin(ang_c)], axis=1)
    return jnp.asarray(cos, F32), jnp.asarray(sin, F32)


def _swap_halves(x):
    n = x.shape[-1]
    quarter = HEAD_DIM // 4
    lane = lax.broadcasted_iota(jnp.int32, x.shape, 1)
    up = pltpu.roll(x, n - quarter, axis=1)
    down = pltpu.roll(x, quarter, axis=1)
    return jnp.where((lane & quarter) == 0, up, down)


IN_PROJ_COLS = 2 * KV_WIDTH


def _proj_rope_kernel(h_ref, w_ref, cos_ref, sin_ref, o_ref, *, rope_cols, scale):
    acc = _dot(h_ref[...], w_ref[...])
    if rope_cols:
        heads = rope_cols // HEAD_DIM
        t = acc[:, :rope_cols]
        t = t * jnp.tile(cos_ref[...], (1, heads)) + _swap_halves(t) * jnp.tile(sin_ref[...], (1, heads))
        acc = t if rope_cols == acc.shape[1] else jnp.concatenate([t, acc[:, rope_cols:]], axis=1)
    if scale != 1.0:
        acc = acc * scale
    o_ref[...] = acc.astype(o_ref.dtype)


def _proj_rope(h, w, rope_cols, scale):
    B, L, D = h.shape
    N = w.shape[1]
    tn = IN_PROJ_COLS
    tm = _tile(L, 1024)
    cos, sin = _rope_tables(L)
    tab = pl.BlockSpec((tm, HEAD_DIM), lambda b, i, j: (i, 0))
    return pl.pallas_call(
        functools.partial(_proj_rope_kernel, rope_cols=rope_cols, scale=scale),
        out_shape=jax.ShapeDtypeStruct((B, L, N), BF16),
        grid=(B, L // tm, N // tn),
        in_specs=[pl.BlockSpec((None, tm, D), lambda b, i, j: (b, i, 0)),
                  pl.BlockSpec((D, tn), lambda b, i, j: (0, j)),
                  tab, tab],
        out_specs=pl.BlockSpec((None, tm, tn), lambda b, i, j: (b, i, j)),
        compiler_params=_cp(("parallel", "parallel", "arbitrary")),
        name="proj_rope",
    )(h, w, cos, sin)


def _in_proj(x, g, shift, scale, w, rope):
    h = _norm_mod(x, g, shift, scale)
    kv_end = ATT_WIDTH + IN_PROJ_COLS
    q = _proj_rope(h, w[:, :ATT_WIDTH], IN_PROJ_COLS if rope else 0, HEAD_DIM ** -0.5)
    kv = _proj_rope(h, w[:, ATT_WIDTH:kv_end], KV_WIDTH if rope else 0, 1.0)
    rest = _matmul(h, w[:, kv_end:], tm=2048)
    return q, kv, rest


def _nt_dot(a, b):
    return lax.dot_general(a, b, (((1,), (1,)), ((), ())), preferred_element_type=F32)


def _attn_bias_tables():
    T, G = BLOCK, ATT_GROUP
    qi = np.arange(G * T)[:, None] % T
    ki = np.arange(3 * T)[None, :]
    band = np.abs(ki - T - qi) <= WINDOW
    after_start = ki >= T
    before_end = ki < 2 * T
    masks = [band & after_start, band, band & before_end, band & after_start & before_end]
    return jnp.asarray(np.stack([np.where(m, 0.0, NEG) for m in masks]), F32)


def _attn_kernel(*refs, local, n_blocks):
    if local:
        (q_ref, kp_ref, kc_ref, kn_ref, vp_ref, vc_ref, vn_ref,
         kx_ref, vx_ref, bias_ref, sink_ref, g_ref, o_ref, acc_ref) = refs
    else:
        q_ref, kx_ref, vx_ref, sink_ref, g_ref, o_ref, acc_ref = refs
    T = BLOCK
    G = ATT_GROUP
    n_sub = q_ref.shape[0] // T
    for h in range(ATT_KV_HEADS):
        cols = slice(h * HEAD_DIM, (h + 1) * HEAD_DIM)
        sink = jnp.concatenate(
            [jnp.broadcast_to(sink_ref[:, (h * G + g) * HEAD_DIM:(h * G + g) * HEAD_DIM + 1], (T, 1))
             for g in range(G)], axis=0)
        kx = kx_ref[:, cols]
        vx = jnp.concatenate([vx_ref[:, cols], jnp.ones((kx_ref.shape[0], HEAD_DIM), BF16)], axis=1)
        if local:
            k_band = jnp.concatenate([kp_ref[:, cols], kc_ref[:, cols], kn_ref[:, cols]], axis=0)
            v_band = jnp.concatenate([vp_ref[:, cols], vc_ref[:, cols], vn_ref[:, cols]], axis=0)
            v_band = jnp.concatenate([v_band, jnp.ones(v_band.shape, BF16)], axis=1)
        for j in range(n_sub):
            rows = slice(j * T, (j + 1) * T)
            qs = jnp.concatenate(
                [q_ref[rows, (h * G + g) * HEAD_DIM:(h * G + g + 1) * HEAD_DIM] for g in range(G)],
                axis=0)
            s_ctx = _nt_dot(qs, kx)
            m = jnp.maximum(jnp.max(s_ctx, axis=-1, keepdims=True), sink)
            if local:
                blk = pl.program_id(1) * n_sub + j
                is_first = blk == 0
                is_last = blk == n_blocks - 1
                table = jnp.where(is_first, jnp.where(is_last, 3, 0), jnp.where(is_last, 2, 1))
                s_loc = _nt_dot(qs, k_band[j * T:(j + 3) * T]) + bias_ref[table]
                m = jnp.maximum(m, jnp.max(s_loc, axis=-1, keepdims=True))
            o = _dot(jnp.exp((s_ctx - m).astype(BF16)), vx)
            if local:
                o = o + _dot(jnp.exp((s_loc - m).astype(BF16)), v_band[j * T:(j + 3) * T])
            denom = o[:, HEAD_DIM:HEAD_DIM + 1] + jnp.exp(sink - m)
            o = o[:, :HEAD_DIM] * (1.0 / denom)
            for g in range(G):
                acc_ref[rows, (h * G + g) * HEAD_DIM:(h * G + g + 1) * HEAD_DIM] = o[g * T:(g + 1) * T]
    y = acc_ref[...]
    y = y * lax.rsqrt(jnp.mean(y * y, axis=-1, keepdims=True) + EPS) * g_ref[...]
    o_ref[...] = y.astype(o_ref.dtype)


def _attention(q, kv, kv_ctx, sink, g_attn):
    B, L, _ = q.shape
    C = kv_ctx.shape[1]
    local = kv is not None
    T = BLOCK
    nb = L // T
    n_sub = 2 if nb % 2 == 0 else 1
    sink_row = jnp.repeat(sink.astype(F32), HEAD_DIM).reshape(1, ATT_WIDTH)
    qspec = pl.BlockSpec((None, n_sub * T, ATT_WIDTH), lambda b, i: (b, i, 0))
    in_specs = [qspec]
    args = [q]
    cspecs = [pl.BlockSpec((None, C, KV_WIDTH), functools.partial(lambda b, i, part: (b, 0, part), part=part))
              for part in range(2)]
    row = pl.BlockSpec((1, ATT_WIDTH), lambda b, i: (0, 0))
    if local:
        for part in range(2):
            in_specs += [
                pl.BlockSpec((None, T, KV_WIDTH), functools.partial(
                    lambda b, i, part: (b, jnp.maximum(i * n_sub - 1, 0), part), part=part)),
                pl.BlockSpec((None, n_sub * T, KV_WIDTH), functools.partial(
                    lambda b, i, part: (b, i, part), part=part)),
                pl.BlockSpec((None, T, KV_WIDTH), functools.partial(
                    lambda b, i, part: (b, jnp.minimum((i + 1) * n_sub, nb - 1), part), part=part))]
        bias = _attn_bias_tables()
        in_specs += cspecs + [pl.BlockSpec(bias.shape, lambda b, i: (0, 0, 0))]
        args += [kv] * 6 + [kv_ctx, kv_ctx, bias]
    else:
        in_specs += cspecs
        args += [kv_ctx, kv_ctx]
    in_specs += [row, row]
    args += [sink_row, g_attn.reshape(1, ATT_WIDTH)]
    return pl.pallas_call(
        functools.partial(_attn_kernel, local=local, n_blocks=nb),
        out_shape=jax.ShapeDtypeStruct((B, L, ATT_WIDTH), BF16),
        grid=(B, nb // n_sub),
        in_specs=in_specs,
        out_specs=qspec,
        scratch_shapes=[pltpu.VMEM((n_sub * T, ATT_WIDTH), F32)],
        compiler_params=_cp(("parallel", "arbitrary")),
        name="attention_local" if local else "attention_context",
    )(*args)


def _halo_specs(ts, L, width, col_block):
    nb8 = L // HALO
    per = ts // HALO
    prev = pl.BlockSpec((None, HALO, width),
                        lambda b, i: (b, jnp.maximum(i * per - 1, 0), col_block))
    cur = pl.BlockSpec((None, ts, width), lambda b, i: (b, i, col_block))
    nxt = pl.BlockSpec((None, HALO, width),
                       lambda b, i: (b, jnp.minimum((i + 1) * per, nb8 - 1), col_block))
    return [prev, cur, nxt]


def _with_halo(prev_ref, cur_ref, next_ref):
    i = pl.program_id(1)
    last = pl.num_programs(1) - 1
    prev = jnp.where(i > 0, prev_ref[...], 0.0)
    nxt = jnp.where(i < last, next_ref[...], 0.0)
    return jnp.concatenate([prev, cur_ref[...], nxt], axis=0)


def _pool_kernel(prev_ref, cur_ref, next_ref, w_ref, scale_ref, g_ref, o_ref, *, seq_len):
    ts = cur_ref.shape[0]
    ext = _with_halo(prev_ref, cur_ref, next_ref)
    pos = pl.program_id(1) * ts + lax.broadcasted_iota(jnp.int32, (ts, 1), 0)
    outs = []
    for gidx, win in enumerate(POOL_WINDOWS):
        cols = slice(gidx * POOL_GROUP, (gidx + 1) * POOL_GROUP)
        run = ext[:, cols]
        step = 1
        while step < win:
            run = run[:run.shape[0] - step] + run[step:]
            step *= 2
        lo = HALO - win // 2
        total = run[lo:lo + ts]
        cnt = (jnp.minimum(pos + (win - win // 2), seq_len) - jnp.maximum(pos - win // 2, 0))
        mean = total * (1.0 / cnt.astype(F32))
        outs.append(_dot3(mean - cur_ref[:, cols], w_ref[gidx]))
    y = jnp.concatenate(outs, axis=-1) * scale_ref[...]
    y = y * lax.rsqrt(jnp.mean(y * y, axis=-1, keepdims=True) + EPS) * g_ref[...]
    o_ref[...] = y.astype(o_ref.dtype)


def _pool_mixer(proj, pool_w, pool_scale, g_pool):
    B, L, _ = proj.shape
    ts = _tile(L, 512)
    col_block = 0
    row = pl.BlockSpec((1, POOL_WIDTH), lambda b, i: (0, 0))
    return pl.pallas_call(
        functools.partial(_pool_kernel, seq_len=L),
        out_shape=jax.ShapeDtypeStruct((B, L, POOL_WIDTH), BF16),
        grid=(B, L // ts),
        in_specs=_halo_specs(ts, L, POOL_WIDTH, col_block) + [
            pl.BlockSpec(pool_w.shape, lambda b, i: (0, 0, 0)), row, row],
        out_specs=pl.BlockSpec((None, ts, POOL_WIDTH), lambda b, i: (b, i, 0)),
        compiler_params=_cp(("parallel", "arbitrary")),
        name="pool_mixer",
    )(proj, proj, proj, pool_w, pool_scale.reshape(1, POOL_WIDTH), g_pool.reshape(1, POOL_WIDTH))


def _short_conv_kernel(*refs):
    halo_refs = refs[:9]
    w_ref, b_ref = refs[9:11]
    out_refs = refs[11:]
    ts = out_refs[0].shape[0]
    for part in range(HY_ORDER + 1):
        ext = _with_halo(*halo_refs[3 * part:3 * part + 3])
        cols = slice(part * HY_WIDTH, (part + 1) * HY_WIDTH)
        acc = b_ref[:, cols] + ext[HALO - 1:HALO - 1 + ts] * w_ref[0:1, cols]
        acc = acc + ext[HALO:HALO + ts] * w_ref[1:2, cols]
        acc = acc + ext[HALO + 1:HALO + 1 + ts] * w_ref[2:3, cols]
        out_refs[part][...] = acc


def _short_conv(proj, conv_w, conv_b):
    B, L, _ = proj.shape
    ts = _tile(L, 512)
    first = POOL_WIDTH // HY_WIDTH
    assert first * HY_WIDTH == POOL_WIDTH
    specs = []
    for part in range(HY_ORDER + 1):
        specs += _halo_specs(ts, L, HY_WIDTH, first + part)
    n_col = (HY_ORDER + 1) * HY_WIDTH
    out = pl.BlockSpec((None, ts, HY_WIDTH), lambda b, i: (b, i, 0))
    return pl.pallas_call(
        _short_conv_kernel,
        out_shape=tuple(jax.ShapeDtypeStruct((B, L, HY_WIDTH), F32) for _ in range(HY_ORDER + 1)),
        grid=(B, L // ts),
        in_specs=specs + [pl.BlockSpec((HY_SHORT, n_col), lambda b, i: (0, 0)),
                          pl.BlockSpec((1, n_col), lambda b, i: (0, 0))],
        out_specs=(out,) * (HY_ORDER + 1),
        compiler_params=_cp(("parallel", "arbitrary")),
        name="hyena_short_conv",
    )(*([proj] * 9), conv_w, conv_b.reshape(1, n_col))


def _filter_tables(L):
    m = np.arange(L, dtype=np.float32)
    t = (m / np.float32(max(L - 1, 1))).astype(np.float32)
    w = (np.float32(2.0 * math.pi) * m / np.float32(L)).astype(np.float32)
    f = np.linspace(1e-4, HY_BANDS - 1, HY_BANDS, dtype=np.float32)
    ang = (w[:, None] * f).astype(np.float64)
    z = np.concatenate([t[:, None].astype(np.float64), np.cos(ang), -np.sin(ang)], axis=-1)
    z = np.pad(z, ((0, 0), (0, LANES - HY_EMB)))
    max_decay = math.log(HY_TARGET) / HY_SHORT_DECAY_PCT
    min_decay = math.log(HY_TARGET) / HY_LONG_DECAY_PCT
    deltas = np.linspace(min_decay, max_decay, HY_WIDTH, dtype=np.float32)
    decay = np.exp(-t[:, None].astype(np.float64) * np.abs(deltas)[None].astype(np.float64))
    return jnp.asarray(z, F32), jnp.asarray(decay, F32)


def _filter_kernel(z_ref, decay_ref, w1_ref, b1_ref, w2_ref, b2_ref, w3_ref, b3_ref, fr_ref, o_ref):
    tl = z_ref.shape[0]
    fr = fr_ref[...]
    a = jnp.sin(fr * (_dot3(z_ref[...], w1_ref[...]) + b1_ref[...]))
    a = jnp.sin(fr * (_dot3(a, w2_ref[...]) + b2_ref[...]))
    h = _dot3(a, w3_ref[...]) + b3_ref[...]
    h = h * jnp.tile(decay_ref[...], (1, 2 * HY_ORDER))
    row = pl.program_id(0) * tl + lax.broadcasted_iota(jnp.int32, h.shape, 0)
    col = lax.broadcasted_iota(jnp.int32, h.shape, 1)
    backward = (col // HY_WIDTH) % 2 == 1
    o_ref[...] = jnp.where(backward & (row == 0), 0.0, h)


def _hyena_filter_taps(L, w1, b1, w2, b2, w3, b3, freq):
    z, decay = _filter_tables(L)
    hid = w1.shape[1]
    n_out = w3.shape[1]
    tl = _tile(L, 512)
    w1p = jnp.pad(w1, ((0, LANES - HY_EMB), (0, 0)))
    full = lambda a: pl.BlockSpec(a.shape, lambda i: (0,) * a.ndim)
    args = [w1p, b1.reshape(1, hid), w2, b2.reshape(1, hid), w3, b3.reshape(1, n_out),
            freq.reshape(1, hid)]
    return pl.pallas_call(
        _filter_kernel,
        out_shape=jax.ShapeDtypeStruct((L, n_out), F32),
        grid=(L // tl,),
        in_specs=[pl.BlockSpec((tl, LANES), lambda i: (i, 0)),
                  pl.BlockSpec((tl, HY_WIDTH), lambda i: (i, 0))] + [full(a) for a in args],
        out_specs=pl.BlockSpec((tl, n_out), lambda i: (i, 0)),
        compiler_params=_cp(("arbitrary",)),
        name="hyena_filter_taps",
    )(z, decay, *args)


def _fft_matrices(n1, a_in):
    n = n1 * LANES
    k1 = np.arange(n1)
    a = np.arange(a_in)
    th1 = 2.0 * np.pi * np.outer(k1, a) / n1
    c1, s1 = np.cos(th1), np.sin(th1)
    m1_complex = np.block([[c1, s1], [-s1, c1]])
    m1_real = np.concatenate([c1, -s1], axis=0)
    m3 = np.block([[c1.T, -s1.T], [s1.T, c1.T]])
    b = np.arange(LANES)
    k2 = np.arange(LANES)
    k = k1[:, None, None] + n1 * k2[None, :, None]
    th2 = 2.0 * np.pi * (k * b[None, None, :] % n) / n
    c2, s2 = np.cos(th2), np.sin(th2)
    fwd = np.concatenate([np.concatenate([c2, s2], axis=2),
                          np.concatenate([-s2, c2], axis=2)], axis=1)
    c2t, s2t = np.swapaxes(c2, 1, 2), np.swapaxes(s2, 1, 2)
    inv = np.concatenate([np.concatenate([c2t, -s2t], axis=2),
                          np.concatenate([s2t, c2t], axis=2)], axis=1)
    as_bf16 = lambda m: jnp.asarray(m, F32).astype(BF16)
    return as_bf16(m1_complex), as_bf16(m1_real), as_bf16(m3), as_bf16(fwd), as_bf16(inv)


def _fft_stage1_kernel(x_ref, m_ref, o_ref, *, complex_in):
    n1 = o_ref.shape[0]
    for s in range(o_ref.shape[1]):
        if complex_in:
            x = jnp.concatenate([x_ref[0, :, s, :], x_ref[1, :, s, :]], axis=0)
        else:
            x = x_ref[:, s, :]
        y = _dot(m_ref[...], x.astype(BF16))
        o_ref[:, s, :] = _pack_bf16(y[:n1], y[n1:])


def _fft_stage1(x, m1, n1, complex_in):
    P, A, Wt = x.shape[0], x.shape[-3], x.shape[-1]
    tw = _tile(Wt, 512, LANES)
    if complex_in:
        xspec = pl.BlockSpec((None, 2, A, FFT_ROWS, tw), lambda p, i, j: (p, 0, 0, i, j))
    else:
        xspec = pl.BlockSpec((None, A, FFT_ROWS, tw), lambda p, i, j: (p, 0, i, j))
    return pl.pallas_call(
        functools.partial(_fft_stage1_kernel, complex_in=complex_in),
        out_shape=jax.ShapeDtypeStruct((P, n1, LANES, Wt), jnp.uint32),
        grid=(P, LANES // FFT_ROWS, Wt // tw),
        in_specs=[xspec, pl.BlockSpec(m1.shape, lambda p, i, j: (0, 0))],
        out_specs=pl.BlockSpec((None, n1, FFT_ROWS, tw), lambda p, i, j: (p, 0, i, j)),
        compiler_params=_cp(("parallel", "arbitrary", "arbitrary")),
        name="fft_stage1",
    )(x, m1)


def _filter_spectrum_kernel(y_ref, g_ref, o0_ref, o1_ref, *, scale):
    kt = y_ref.shape[0]
    for t in range(kt):
        y = jnp.concatenate(_unpack_bf16_pairs(y_ref[t]), axis=0)
        f = _dot(g_ref[t], y)
        fr, fi = f[:LANES], f[LANES:]
        for o, o_ref in enumerate((o0_ref, o1_ref)):
            fw = slice((2 * o) * HY_WIDTH, (2 * o + 1) * HY_WIDTH)
            bw = slice((2 * o + 1) * HY_WIDTH, (2 * o + 2) * HY_WIDTH)
            o_ref[0, t] = ((fr[:, fw] + fr[:, bw]) * scale).astype(o_ref.dtype)
            o_ref[1, t] = ((fi[:, fw] - fi[:, bw]) * scale).astype(o_ref.dtype)


def _filter_spectrum(taps, mats, n1):
    L, n_col = taps.shape
    a_in = L // LANES
    _, m1_real, _, g_fwd, _ = mats
    y = _fft_stage1(taps.reshape(1, a_in, LANES, n_col), m1_real, n1, complex_in=False)
    kt = _tile(n1, 4, 1)
    out = jax.ShapeDtypeStruct((2, n1, LANES, HY_WIDTH), BF16)
    ospec = pl.BlockSpec((2, kt, LANES, HY_WIDTH), lambda i: (0, i, 0, 0))
    return pl.pallas_call(
        functools.partial(_filter_spectrum_kernel, scale=1.0 / (n1 * LANES)),
        out_shape=(out, out),
        grid=(n1 // kt,),
        in_specs=[pl.BlockSpec((None, kt, LANES, n_col), lambda i: (0, i, 0, 0)),
                  pl.BlockSpec((kt, 2 * LANES, 2 * LANES), lambda i: (i, 0, 0))],
        out_specs=(ospec, ospec),
        compiler_params=_cp(("arbitrary",)),
        name="hyena_filter_spectrum",
    )(y, g_fwd)


def _fft_mid_kernel(y_ref, kf_ref, g_ref, gi_ref, o_ref):
    kt = y_ref.shape[0]
    for t in range(kt):
        y = jnp.concatenate(_unpack_bf16_pairs(y_ref[t]), axis=0)
        f = _dot(g_ref[t], y)
        fr, fi = f[:LANES], f[LANES:]
        kr, ki = kf_ref[0, t].astype(F32), kf_ref[1, t].astype(F32)
        p = jnp.concatenate([fr * kr - fi * ki, fr * ki + fi * kr], axis=0).astype(BF16)
        u = _dot(gi_ref[t], p)
        o_ref[t] = _pack_bf16(u[:LANES], u[LANES:])


def _fft_mid(y, kf, g_fwd, g_inv):
    P, n1, _, W = y.shape
    kt = _tile(n1, 8, 1)
    yspec = pl.BlockSpec((None, kt, LANES, W), lambda i, p: (p, i, 0, 0))
    gspec = pl.BlockSpec((kt, 2 * LANES, 2 * LANES), lambda i, p: (i, 0, 0))
    return pl.pallas_call(
        _fft_mid_kernel,
        out_shape=jax.ShapeDtypeStruct(y.shape, jnp.uint32),
        grid=(n1 // kt, P),
        in_specs=[yspec, pl.BlockSpec((2, kt, LANES, W), lambda i, p: (0, i, 0, 0)), gspec, gspec],
        out_specs=yspec,
        compiler_params=_cp(("arbitrary", "arbitrary")),
        name="fft_mid",
    )(y, kf, g_fwd, g_inv)


def _fft_stage3_kernel(u_ref, m_ref, z_ref, x_ref, bias_ref, g_ref, o_ref, ur_ref, ui_ref,
                       *, normalise):
    a_out = z_ref.shape[1]
    bias = bias_ref[...]
    packed = u_ref[...]
    ur_ref[...] = lax.bitcast_convert_type(packed << 16, F32)
    ui_ref[...] = lax.bitcast_convert_type(packed & jnp.uint32(0xFFFF0000), F32)
    for s in range(o_ref.shape[2]):
        u = jnp.concatenate([ur_ref[:, s, :], ui_ref[:, s, :]], axis=0).astype(BF16)
        conv = _dot(m_ref[...], u)
        for part in range(2):
            y = x_ref[part, :, s, :] * (
                conv[part * a_out:(part + 1) * a_out] + z_ref[part, :, s, :] * bias)
            if normalise:
                y = y * lax.rsqrt(jnp.mean(y * y, axis=-1, keepdims=True) + EPS) * g_ref[...]
            o_ref[part, :, s, :] = y


def _fft_stage3(u, m3, z, gate, bias_row, norm_g=None):
    P, n1, _, W = u.shape
    A = z.shape[2]
    pair = pl.BlockSpec((None, 2, A, FFT_ROWS, W), lambda p, i: (p, 0, 0, i, 0))
    row = pl.BlockSpec((1, W), lambda p, i: (0, 0))
    normalise = norm_g is not None
    return pl.pallas_call(
        functools.partial(_fft_stage3_kernel, normalise=normalise),
        out_shape=jax.ShapeDtypeStruct(z.shape, F32),
        grid=(P, LANES // FFT_ROWS),
        in_specs=[pl.BlockSpec((None, n1, FFT_ROWS, W), lambda p, i: (p, 0, i, 0)),
                  pl.BlockSpec(m3.shape, lambda p, i: (0, 0)),
                  pair, pair, row, row],
        out_specs=pair,
        scratch_shapes=[pltpu.VMEM((n1, FFT_ROWS, W), F32), pltpu.VMEM((n1, FFT_ROWS, W), F32)],
        compiler_params=_cp(("parallel", "arbitrary")),
        name="fft_stage3",
    )(u, m3, z, gate, bias_row, norm_g.reshape(1, W) if normalise else bias_row)


def _hyena_mixer(v, gates, kfs, d_bias, norm_g, mats, n1):
    B, L, W = v.shape
    pair_shape = (B // 2, 2, L // LANES, LANES, W)
    m1_complex, _, m3, g_fwd, g_inv = mats
    z = v.reshape(pair_shape)
    for o in range(HY_ORDER):
        y = _fft_stage1(z, m1_complex, n1, complex_in=True)
        u = _fft_mid(y, kfs[o], g_fwd, g_inv)
        z = _fft_stage3(u, m3, z, gates[o].reshape(pair_shape), d_bias[o].reshape(1, W),
                        norm_g if o == HY_ORDER - 1 else None)
    return z.reshape(B, L, W)


def kernel(x, c, ctx, c_ctx, w_mod, b_mod, norm1_g, w_in, attn_sink, pool_w, pool_scale,
           hy_conv_w, hy_conv_b, hy_f_w1, hy_f_b1, hy_f_w2, hy_f_b2, hy_f_w3, hy_f_b3,
           hy_f_freq, hy_bias, g_attn, g_pool, g_hyena, w_out, norm2_g,
           ff_w1, ff_w3, ff_w2, router_w, moe_w1, moe_w3, moe_w2, final_g):
    B, S, D = x.shape
    C = ctx.shape[1]
    depth = w_mod.shape[0]
    assert B % 2 == 0 and S % BLOCK == 0
    fft_mats = {}

    c_rows = jnp.concatenate([c, c_ctx[None], jnp.zeros((SUBLANES - B - 1, D), F32)], axis=0)
    xc = ctx

    def hyena_branch(l, proj, seq_len):
        rows = -(-seq_len // FFT_MIN_ROWS) * FFT_MIN_ROWS
        n1 = 2 * rows // LANES
        if rows not in fft_mats:
            fft_mats[rows] = _fft_matrices(n1, rows // LANES)
        mats = fft_mats[rows]
        taps = _hyena_filter_taps(seq_len, hy_f_w1[l], hy_f_b1[l], hy_f_w2[l], hy_f_b2[l],
                                  hy_f_w3[l], hy_f_b3[l], hy_f_freq[l])
        v, x1, x2 = _short_conv(proj, hy_conv_w[l], hy_conv_b[l])
        if seq_len < rows:
            pad = lambda t: jnp.pad(t, ((0, 0), (0, rows - seq_len), (0, 0)))
            taps = jnp.pad(taps, ((0, rows - seq_len), (0, 0)))
            v, x1, x2 = pad(v), pad(x1), pad(x2)
        kfs = _filter_spectrum(taps, mats, n1)
        z = _hyena_mixer(v, (x1, x2), kfs, hy_bias[l], g_hyena[l], mats, n1)
        return z[:, :seq_len]

    def mix(l, proj, y_att, res, gate, w_out_l, seq_len, norm=None):
        y_pool = _pool_mixer(proj, pool_w[l], pool_scale[l], g_pool[l])
        y_hy = hyena_branch(l, proj, seq_len)
        return _mix_out([y_att, y_pool, y_hy], w_out_l, res, gate, norm)

    for l in range(depth):
        last = l == depth - 1
        mod = _adaln(c_rows, w_mod[l], b_mod[l])
        sh1, sc1, g1, sh2, sc2, g2 = [mod[:B, j * D:(j + 1) * D] for j in range(6)]
        csh1, csc1, cg1, csh2, csc2, cg2 = [mod[B:B + 1, j * D:(j + 1) * D] for j in range(6)]
        w_in_l = w_in[l].astype(BF16)
        w_out_l = w_out[l].astype(BF16)

        if last:
            hc = _norm_mod(xc, norm1_g[l], csh1, csc1)
            kv_w = w_in_l[:, ATT_WIDTH:ATT_WIDTH + 2 * KV_WIDTH]
            kv_c = _matmul(hc, kv_w, out_dtype=BF16)
        else:
            q_c, kv_c, proj_c = _in_proj(xc, norm1_g[l], csh1, csc1, w_in_l, rope=False)
            y_att_c = _attention(q_c, None, kv_c, attn_sink[l], g_attn[l])
            xc_new, hc2 = mix(l, proj_c, y_att_c, xc, cg1, w_out_l, C, (norm2_g[l], csh2, csc2))
            i = l // 2
            if l % 2 == 0:
                hid_c = _swiglu_hidden(hc2, ff_w1[i].astype(BF16), ff_w3[i].astype(BF16))
                xc_new = _matmul_gated_residual([hid_c], ff_w2[i].astype(BF16), xc_new, cg2)
            else:
                raise NotImplementedError("context tokens through an expert layer")

        q, kv, proj = _in_proj(x, norm1_g[l], sh1, sc1, w_in_l, rope=True)
        y_att = _attention(q, kv, kv_c, attn_sink[l], g_attn[l])

        i = l // 2
        if l % 2 == 0:
            x, h2 = mix(l, proj, y_att, x, g1, w_out_l, S, (norm2_g[l], sh2, sc2))
            hid = _swiglu_hidden(h2, ff_w1[i].astype(BF16), ff_w3[i].astype(BF16))
            x = _matmul_gated_residual([hid], ff_w2[i].astype(BF16), x, g2)
            if last:
                x = _rmsnorm(x, final_g, F32)
        else:
            assert last
            x = mix(l, proj, y_att, x, g1, w_out_l, S)
            h2, gi, gw, counts = _norm_mod_router(x, norm2_g[l], sh2, sc2, router_w[i])
            x = _moe_layer(x, h2, gi, gw, counts, g2, final_g, moe_w1[i], moe_w3[i], moe_w2[i])
        if not last:
            xc = xc_new
    return x
```

```python
import functools
import math

import numpy as np
import jax
import jax.numpy as jnp
from jax import lax
from jax.experimental import pallas as pl
from jax.experimental.pallas import tpu as pltpu

F32 = jnp.float32
BF16 = jnp.bfloat16

EPS = 1e-6
NEG = -1e30
GRID_W = 64
ATT_HEADS = 8
ATT_KV_HEADS = 2
ATT_GROUP = ATT_HEADS // ATT_KV_HEADS
HEAD_DIM = 128
ATT_WIDTH = ATT_HEADS * HEAD_DIM
KV_WIDTH = ATT_KV_HEADS * HEAD_DIM
WINDOW = 128
BLOCK = 128
ROPE_BASE = 10000.0
POOL_WINDOWS = (2, 4, 8, 16)
POOL_GROUP = 128
POOL_WIDTH = POOL_GROUP * len(POOL_WINDOWS)
HY_WIDTH = 512
HY_ORDER = 2
HY_SHORT = 3
HY_EMB = 33
HY_BANDS = (HY_EMB - 1) // 2
HY_SHORT_DECAY_PCT = 0.3
HY_LONG_DECAY_PCT = 1.5
HY_TARGET = 1e-2
N_EXPERTS = 8
TOP_K = 2

LANES = 128
SUBLANES = 8
HALO = SUBLANES
FFT_MIN_ROWS = 2048
FFT_ROWS = SUBLANES
MOE_OUT_K_CHUNKS = 4
VMEM_LIMIT = 56 * 1024 * 1024


def _cp(sem):
    return pltpu.CompilerParams(dimension_semantics=sem, vmem_limit_bytes=VMEM_LIMIT)


def _tile(n, pref, mult=SUBLANES):
    if n <= pref:
        return n
    t = (pref // mult) * mult
    while t >= mult:
        if n % t == 0:
            return t
        t -= mult
    return n


def _split_bf16(x):
    hi = x.astype(BF16)
    lo = (x - hi.astype(F32)).astype(BF16)
    return hi, lo


def _dot(a, b):
    return jnp.dot(a, b, preferred_element_type=F32)


def _pack_bf16(lo, hi):
    lo_bits = lax.bitcast_convert_type(lo.astype(BF16).astype(F32), jnp.uint32)
    hi_bits = lax.bitcast_convert_type(hi.astype(BF16).astype(F32), jnp.uint32)
    return (lo_bits >> 16) | (hi_bits & jnp.uint32(0xFFFF0000))


def _pack_bf16_pairs(x):
    n = x.shape[-1] // 2
    return _pack_bf16(x[:, :n], x[:, n:])


def _unpack_bf16_pairs(u):
    lo = lax.bitcast_convert_type(u << 16, F32).astype(BF16)
    hi = lax.bitcast_convert_type(u & jnp.uint32(0xFFFF0000), F32).astype(BF16)
    return lo, hi


def _dot3(a, b):
    ah, al = _split_bf16(a)
    bh, bl = _split_bf16(b)
    return _dot(ah, bh) + (_dot(ah, bl) + _dot(al, bh))


def _adaln_kernel(c_ref, w_ref, b_ref, o_ref):
    c = c_ref[...]
    a = c * (1.0 / (1.0 + jnp.exp(-c)))
    o_ref[...] = _dot3(a, w_ref[...]) + b_ref[...]


def _adaln(c_rows, w, b):
    R, D = c_rows.shape
    N = w.shape[1]
    tn = _tile(N, 1536, LANES)
    return pl.pallas_call(
        _adaln_kernel,
        out_shape=jax.ShapeDtypeStruct((R, N), F32),
        grid=(N // tn,),
        in_specs=[pl.BlockSpec((R, D), lambda j: (0, 0)),
                  pl.BlockSpec((D, tn), lambda j: (0, j)),
                  pl.BlockSpec((1, tn), lambda j: (0, j))],
        out_specs=pl.BlockSpec((R, tn), lambda j: (0, j)),
        compiler_params=_cp(("arbitrary",)),
        name="adaln",
    )(c_rows, w, b.reshape(1, N))


def _norm_mod_kernel(x_ref, g_ref, sh_ref, sc_ref, o_ref):
    x = x_ref[...]
    y = x * lax.rsqrt(jnp.mean(x * x, axis=-1, keepdims=True) + EPS) * g_ref[...]
    o_ref[...] = (y * (1.0 + sc_ref[...]) + sh_ref[...]).astype(o_ref.dtype)


def _bcast_map(arr):
    if arr.shape[0] == 1:
        return lambda b, i: (0, 0, 0)
    return lambda b, i: (b, 0, 0)


def _norm_mod(x, g, shift, scale):
    B, L, D = x.shape
    ts = _tile(L, 512)
    shift = shift[:, None, :]
    scale = scale[:, None, :]
    return pl.pallas_call(
        _norm_mod_kernel,
        out_shape=jax.ShapeDtypeStruct((B, L, D), BF16),
        grid=(B, L // ts),
        in_specs=[pl.BlockSpec((None, ts, D), lambda b, i: (b, i, 0)),
                  pl.BlockSpec((1, D), lambda b, i: (0, 0)),
                  pl.BlockSpec((None, 1, D), _bcast_map(shift)),
                  pl.BlockSpec((None, 1, D), _bcast_map(scale))],
        out_specs=pl.BlockSpec((None, ts, D), lambda b, i: (b, i, 0)),
        compiler_params=_cp(("parallel", "parallel")),
        name="norm_mod",
    )(x, g.reshape(1, D), shift, scale)


def _route(h, rw_ref, tri_ref, run_ref, gi_ref, gw_ref, cnt_ref):
    logits = _dot3(h, rw_ref[...])
    lane = lax.broadcasted_iota(jnp.int32, logits.shape, 1)
    logits = jnp.where(lane < N_EXPERTS, logits, NEG)
    m1 = jnp.max(logits, axis=-1, keepdims=True)
    i1 = jnp.min(jnp.where(logits == m1, lane, LANES), axis=-1, keepdims=True)
    rest = jnp.where(lane == i1, NEG, logits)
    m2 = jnp.max(rest, axis=-1, keepdims=True)
    i2 = jnp.min(jnp.where(rest == m2, lane, LANES), axis=-1, keepdims=True)
    e2 = jnp.exp(m2 - m1)
    w1 = 1.0 / (1.0 + e2)
    w2 = e2 * w1
    pick1 = lane == i1
    pick2 = lane == i2
    both = jnp.where(pick1 | pick2, 1.0, 0.0)
    before = _dot(tri_ref[...], both.astype(BF16)) + run_ref[...]
    r1 = jnp.sum(jnp.where(pick1, before, 0.0), axis=-1, keepdims=True)
    r2 = jnp.sum(jnp.where(pick2, before, 0.0), axis=-1, keepdims=True)
    run_ref[...] = run_ref[...] + jnp.sum(both, axis=0, keepdims=True)
    cnt_ref[...] = run_ref[...]
    ids = jnp.where(lane == 0, i1.astype(F32), jnp.where(lane == 1, i2.astype(F32),
                    jnp.where(lane == 2, r1, jnp.where(lane == 3, r2, 0.0))))
    gi_ref[...] = jnp.transpose(ids)[:SUBLANES, :]
    gw_ref[...] = jnp.where(lane == 0, w1, jnp.where(lane == 1, w2, 0.0))


def _norm_mod_router_kernel(x_ref, g_ref, sh_ref, sc_ref, rw_ref, tri_ref,
                            o_ref, gi_ref, gw_ref, cnt_ref, run_ref):
    @pl.when((pl.program_id(0) == 0) & (pl.program_id(1) == 0))
    def _():
        run_ref[...] = jnp.zeros(run_ref.shape, run_ref.dtype)

    x = x_ref[...]
    y = x * lax.rsqrt(jnp.mean(x * x, axis=-1, keepdims=True) + EPS) * g_ref[...]
    h = y * (1.0 + sc_ref[...]) + sh_ref[...]
    o_ref[...] = _pack_bf16_pairs(h)
    _route(h, rw_ref, tri_ref, run_ref, gi_ref, gw_ref, cnt_ref)


def _norm_mod_router(x, g, shift, scale, router_w):
    B, L, D = x.shape
    ts = _tile(L, 512)
    shift = shift[:, None, :]
    scale = scale[:, None, :]
    rw = jnp.pad(router_w, ((0, 0), (0, LANES - router_w.shape[1])))
    tri = jnp.asarray(np.tril(np.ones((ts, ts), np.float32), -1), BF16)
    row = pl.BlockSpec((None, ts, D), lambda b, i: (b, i, 0))
    small = pl.BlockSpec((None, ts, LANES), lambda b, i: (b, i, 0))
    return pl.pallas_call(
        _norm_mod_router_kernel,
        out_shape=(jax.ShapeDtypeStruct((B, L, D // 2), jnp.uint32),
                   jax.ShapeDtypeStruct((B, SUBLANES, L), F32),
                   jax.ShapeDtypeStruct((B, L, LANES), F32),
                   jax.ShapeDtypeStruct((1, LANES), F32)),
        grid=(B, L // ts),
        in_specs=[row,
                  pl.BlockSpec((1, D), lambda b, i: (0, 0)),
                  pl.BlockSpec((None, 1, D), _bcast_map(shift)),
                  pl.BlockSpec((None, 1, D), _bcast_map(scale)),
                  pl.BlockSpec((D, LANES), lambda b, i: (0, 0)),
                  pl.BlockSpec((ts, ts), lambda b, i: (0, 0))],
        out_specs=(pl.BlockSpec((None, ts, D // 2), lambda b, i: (b, i, 0)),
                   pl.BlockSpec((None, SUBLANES, ts), lambda b, i: (b, 0, i)), small,
                   pl.BlockSpec((1, LANES), lambda b, i: (0, 0))),
        scratch_shapes=[pltpu.VMEM((1, LANES), F32)],
        compiler_params=_cp(("arbitrary", "arbitrary")),
        name="norm_mod_router",
    )(x, g.reshape(1, D), shift, scale, rw, tri)


def _mix_out_kernel(*refs, n_a, emit_h):
    a_refs = refs[:n_a]
    w_refs = refs[n_a:2 * n_a]
    res_ref, gate_ref = refs[2 * n_a:2 * n_a + 2]
    acc = _dot(a_refs[0][...].astype(BF16), w_refs[0][...])
    for a_ref, w_ref in zip(a_refs[1:], w_refs[1:]):
        acc = acc + _dot(a_ref[...].astype(BF16), w_ref[...])
    x = res_ref[...] + gate_ref[...] * acc
    if emit_h:
        g_ref, sh_ref, sc_ref, x_ref, h_ref = refs[2 * n_a + 2:]
        y = x * lax.rsqrt(jnp.mean(x * x, axis=-1, keepdims=True) + EPS) * g_ref[...]
        h_ref[...] = (y * (1.0 + sc_ref[...]) + sh_ref[...]).astype(h_ref.dtype)
    else:
        x_ref, = refs[2 * n_a + 2:]
    x_ref[...] = x


def _mix_out(a_list, w, res, gate, norm=None):
    B, L, D = res.shape
    tm = _tile(L, 512)
    emit_h = norm is not None
    gate = gate[:, None, :]
    in_specs = [pl.BlockSpec((None, tm, a.shape[-1]), lambda b, i: (b, i, 0)) for a in a_list]
    off = 0
    for a in a_list:
        k = a.shape[-1]
        assert off % k == 0
        in_specs.append(pl.BlockSpec((k, D), functools.partial(lambda b, i, blk: (blk, 0), blk=off // k)))
        off += k
    row = pl.BlockSpec((None, tm, D), lambda b, i: (b, i, 0))
    in_specs += [row, pl.BlockSpec((None, 1, D), _bcast_map(gate))]
    args = [*a_list, *([w] * len(a_list)), res, gate]
    out_shape = [jax.ShapeDtypeStruct((B, L, D), F32)]
    out_specs = [row]
    if emit_h:
        g, shift, scale = norm
        shift, scale = shift[:, None, :], scale[:, None, :]
        in_specs += [pl.BlockSpec((1, D), lambda b, i: (0, 0)),
                     pl.BlockSpec((None, 1, D), _bcast_map(shift)),
                     pl.BlockSpec((None, 1, D), _bcast_map(scale))]
        args += [g.reshape(1, D), shift, scale]
        out_shape.append(jax.ShapeDtypeStruct((B, L, D), BF16))
        out_specs.append(row)
    out = pl.pallas_call(
        functools.partial(_mix_out_kernel, n_a=len(a_list), emit_h=emit_h),
        out_shape=tuple(out_shape),
        grid=(B, L // tm),
        in_specs=in_specs,
        out_specs=tuple(out_specs),
        compiler_params=_cp(("parallel", "parallel")),
        name="mix_out",
    )(*args)
    return out if emit_h else out[0]


def _rmsnorm_kernel(x_ref, g_ref, o_ref):
    x = x_ref[...]
    y = x * lax.rsqrt(jnp.mean(x * x, axis=-1, keepdims=True) + EPS) * g_ref[...]
    o_ref[...] = y.astype(o_ref.dtype)


def _rmsnorm(x, g, out_dtype):
    B, L, D = x.shape
    ts = _tile(L, 512)
    return pl.pallas_call(
        _rmsnorm_kernel,
        out_shape=jax.ShapeDtypeStruct((B, L, D), out_dtype),
        grid=(B, L // ts),
        in_specs=[pl.BlockSpec((None, ts, D), lambda b, i: (b, i, 0)),
                  pl.BlockSpec((1, D), lambda b, i: (0, 0))],
        out_specs=pl.BlockSpec((None, ts, D), lambda b, i: (b, i, 0)),
        compiler_params=_cp(("parallel", "parallel")),
        name="rmsnorm",
    )(x, g.reshape(1, D))


def _mm_kernel(a_ref, w_ref, o_ref):
    o_ref[...] = _dot(a_ref[...], w_ref[...]).astype(o_ref.dtype)


def _matmul(a, w, out_dtype=F32, tm=1024, tn=512):
    B, L, K = a.shape
    N = w.shape[1]
    tm = _tile(L, tm)
    tn = _tile(N, tn, LANES)
    return pl.pallas_call(
        _mm_kernel,
        out_shape=jax.ShapeDtypeStruct((B, L, N), out_dtype),
        grid=(B, L // tm, N // tn),
        in_specs=[pl.BlockSpec((None, tm, K), lambda b, i, j: (b, i, 0)),
                  pl.BlockSpec((K, tn), lambda b, i, j: (0, j))],
        out_specs=pl.BlockSpec((None, tm, tn), lambda b, i, j: (b, i, j)),
        compiler_params=_cp(("parallel", "parallel", "arbitrary")),
        name="matmul",
    )(a, w)


def _mm_res_kernel(*refs, n_a):
    a_refs = refs[:n_a]
    w_refs = refs[n_a:2 * n_a]
    res_ref, gate_ref, o_ref = refs[2 * n_a:]
    acc = _dot(a_refs[0][...].astype(BF16), w_refs[0][...])
    for a_ref, w_ref in zip(a_refs[1:], w_refs[1:]):
        acc = acc + _dot(a_ref[...].astype(BF16), w_ref[...])
    o_ref[...] = res_ref[...] + gate_ref[...] * acc


def _matmul_gated_residual(a_list, w, res, gate, tm=1024, tn=512):
    B, L, N = res.shape
    tm = _tile(L, tm)
    tn = _tile(N, tn, LANES)
    gate = gate[:, None, :]
    widths = [a.shape[-1] for a in a_list]
    unit = math.gcd(*widths) if len(widths) > 1 else widths[0]
    in_specs = [pl.BlockSpec((None, tm, k), lambda b, i, j: (b, i, 0)) for k in widths]
    off = 0
    for k in widths:
        assert off % k == 0 and k % unit == 0
        in_specs.append(pl.BlockSpec((k, tn), functools.partial(
            lambda b, i, j, blk: (blk, j), blk=off // k)))
        off += k
    gmap = (lambda b, i, j: (0, 0, j)) if gate.shape[0] == 1 else (lambda b, i, j: (b, 0, j))
    in_specs += [pl.BlockSpec((None, tm, tn), lambda b, i, j: (b, i, j)),
                 pl.BlockSpec((None, 1, tn), gmap)]
    return pl.pallas_call(
        functools.partial(_mm_res_kernel, n_a=len(a_list)),
        out_shape=jax.ShapeDtypeStruct((B, L, N), F32),
        grid=(B, L // tm, N // tn),
        in_specs=in_specs,
        out_specs=pl.BlockSpec((None, tm, tn), lambda b, i, j: (b, i, j)),
        compiler_params=_cp(("parallel", "parallel", "arbitrary")),
        name="matmul_gated_residual",
    )(*a_list, *([w] * len(a_list)), res, gate)


def _silu(x):
    return x * (1.0 / (1.0 + jnp.exp(-x)))


def _swiglu_kernel(a_ref, w1_ref, w3_ref, o_ref):
    a = a_ref[...]
    o_ref[...] = (_silu(_dot(a, w1_ref[...])) * _dot(a, w3_ref[...])).astype(o_ref.dtype)


def _swiglu_hidden(a, w1, w3, tm=1024, tf=512):
    B, L, D = a.shape
    F = w1.shape[1]
    tm = _tile(L, tm)
    tf = _tile(F, tf, LANES)
    wspec = pl.BlockSpec((D, tf), lambda b, i, j: (0, j))
    return pl.pallas_call(
        _swiglu_kernel,
        out_shape=jax.ShapeDtypeStruct((B, L, F), BF16),
        grid=(B, L // tm, F // tf),
        in_specs=[pl.BlockSpec((None, tm, D), lambda b, i, j: (b, i, 0)), wspec, wspec],
        out_specs=pl.BlockSpec((None, tm, tf), lambda b, i, j: (b, i, j)),
        compiler_params=_cp(("parallel", "parallel", "arbitrary")),
        name="swiglu_hidden",
    )(a, w1, w3)


def _for_valid_rows(n_valid, tm, compute, o_ref):
    half_rows = tm // 2

    @pl.when(n_valid > half_rows)
    def _():
        compute(slice(0, tm))

    @pl.when((n_valid > 0) & (n_valid <= half_rows))
    def _():
        compute(slice(0, half_rows))
        o_ref[half_rows:, :] = jnp.zeros((tm - half_rows, o_ref.shape[1]), o_ref.dtype)

    @pl.when(n_valid == 0)
    def _():
        o_ref[...] = jnp.zeros(o_ref.shape, o_ref.dtype)


def _moe_hidden_kernel(te_ref, rows_ref, a_ref, w1_ref, w3_ref, o_ref, w1b_ref, w3b_ref):
    i = pl.program_id(1)
    n_valid = rows_ref[i]
    half = a_ref.shape[1]
    new_weights = (i == 0) | (te_ref[i] != te_ref[jnp.maximum(i - 1, 0)])

    @pl.when((n_valid > 0) & new_weights)
    def _():
        w1b_ref[...] = w1_ref[...].astype(BF16)
        w3b_ref[...] = w3_ref[...].astype(BF16)

    def compute(rows):
        lo, hi = _unpack_bf16_pairs(a_ref[rows, :])
        gate = _dot(lo, w1b_ref[:half, :]) + _dot(hi, w1b_ref[half:, :])
        up = _dot(lo, w3b_ref[:half, :]) + _dot(hi, w3b_ref[half:, :])
        o_ref[rows, :] = (_silu(gate) * up).astype(o_ref.dtype)

    _for_valid_rows(n_valid, a_ref.shape[0], compute, o_ref)


def _moe_out_kernel(te_ref, rows_ref, a_ref, w2_ref, o_ref):
    def compute(rows):
        kc = a_ref.shape[1] // MOE_OUT_K_CHUNKS
        acc = _dot(a_ref[rows, :kc], w2_ref[:kc, :].astype(BF16))
        for c in range(1, MOE_OUT_K_CHUNKS):
            acc = acc + _dot(a_ref[rows, c * kc:(c + 1) * kc],
                             w2_ref[c * kc:(c + 1) * kc, :].astype(BF16))
        o_ref[rows, :] = _pack_bf16_pairs(acc)

    _for_valid_rows(rows_ref[pl.program_id(0)], a_ref.shape[0], compute, o_ref)


def _moe_out_cols(d_model):
    return _tile(d_model, 512, 2 * LANES)


def _moe_experts(a_sorted, tile_expert, tile_rows, w1, w3, w2, tm, tf=512):
    R = a_sorted.shape[0]
    E, D, F = w1.shape
    tf = _tile(F, tf, LANES)
    tn = _moe_out_cols(D)
    nt = R // tm
    w13 = pl.BlockSpec((None, D, tf), lambda j, i, te, n: (te[i], 0, j))
    hidden = pl.pallas_call(
        _moe_hidden_kernel,
        out_shape=jax.ShapeDtypeStruct((R, F), BF16),
        grid_spec=pltpu.PrefetchScalarGridSpec(
            num_scalar_prefetch=2,
            grid=(F // tf, nt),
            in_specs=[pl.BlockSpec((tm, D // 2), lambda j, i, te, n: (i, 0)), w13, w13],
            out_specs=pl.BlockSpec((tm, tf), lambda j, i, te, n: (i, j)),
            scratch_shapes=[pltpu.VMEM((D, tf), BF16), pltpu.VMEM((D, tf), BF16)]),
        compiler_params=_cp(("arbitrary", "arbitrary")),
        name="moe_hidden",
    )(tile_expert, tile_rows, a_sorted, w1, w3)
    return pl.pallas_call(
        _moe_out_kernel,
        out_shape=jax.ShapeDtypeStruct((R, D // 2), jnp.uint32),
        grid_spec=pltpu.PrefetchScalarGridSpec(
            num_scalar_prefetch=2,
            grid=(nt, D // tn),
            in_specs=[pl.BlockSpec((tm, F), lambda i, j, te, n: (i, 0)),
                      pl.BlockSpec((None, F, tn), lambda i, j, te, n: (te[i], 0, j))],
            out_specs=pl.BlockSpec((tm, tn // 2), lambda i, j, te, n: (i, j))),
        compiler_params=_cp(("arbitrary", "arbitrary")),
        name="moe_out",
    )(tile_expert, tile_rows, hidden, w2)


def _unpack_column_blocks(u, block):
    parts = []
    for c in range(u.shape[1] // block):
        w = u[:, c * block:(c + 1) * block]
        parts.append(lax.bitcast_convert_type(w << 16, F32))
        parts.append(lax.bitcast_convert_type(w & jnp.uint32(0xFFFF0000), F32))
    return jnp.concatenate(parts, axis=1)


def _moe_combine_kernel(x_ref, ya_ref, yb_ref, gw_ref, gate_ref, g_ref, o_ref, *, block):
    gw = gw_ref[...]
    moe = (gw[:, 0:1] * _unpack_column_blocks(ya_ref[...], block)
           + gw[:, 1:2] * _unpack_column_blocks(yb_ref[...], block))
    x = x_ref[...] + gate_ref[...] * moe
    y = x * lax.rsqrt(jnp.mean(x * x, axis=-1, keepdims=True) + EPS) * g_ref[...]
    o_ref[...] = y


def _moe_combine_norm(x, ya, yb, gw, gate, g):
    B, L, D = x.shape
    ts = _tile(L, 512)
    row = pl.BlockSpec((None, ts, D), lambda b, i: (b, i, 0))
    packed = pl.BlockSpec((None, ts, D // 2), lambda b, i: (b, i, 0))
    return pl.pallas_call(
        functools.partial(_moe_combine_kernel, block=_moe_out_cols(D) // 2),
        out_shape=jax.ShapeDtypeStruct((B, L, D), F32),
        grid=(B, L // ts),
        in_specs=[row, packed, packed,
                  pl.BlockSpec((None, ts, LANES), lambda b, i: (b, i, 0)),
                  pl.BlockSpec((None, 1, D), lambda b, i: (b, 0, 0)),
                  pl.BlockSpec((1, D), lambda b, i: (0, 0))],
        out_specs=row,
        compiler_params=_cp(("parallel", "parallel")),
        name="moe_combine_norm",
    )(x, ya, yb, gw, gate[:, None, :], g.reshape(1, D))


def _moe_layer(x, h, gi, gw, counts, gate, final_g, w1, w3, w2):
    B, L, D = x.shape
    E = w1.shape[0]
    n_tok = B * L
    n_pair = n_tok * TOP_K
    tm = _tile(n_pair, 1024)
    nt = n_pair // tm + E
    rows_of = lambda r: jnp.concatenate(
        [gi[:, r + c, :].reshape(n_tok) for c in range(TOP_K)]).astype(jnp.int32)
    e_pair = rows_of(0)
    rank = rows_of(TOP_K)
    counts = counts[0, :E].astype(jnp.int32)
    tiles_per = (counts + tm - 1) // tm
    tile_end = jnp.cumsum(tiles_per)
    tile_start = tile_end - tiles_per
    row_start = jnp.zeros((n_pair,), jnp.int32)
    for e in range(E):
        row_start = jnp.where(e_pair == e, tile_start[e] * tm, row_start)
    pos = row_start + rank
    tile_ids = jnp.arange(nt, dtype=jnp.int32)
    tile_expert = jnp.minimum(
        jnp.sum((tile_ids[:, None] >= tile_end[None, :]).astype(jnp.int32), axis=1), E - 1)
    tile_rows = jnp.clip(counts[tile_expert] - (tile_ids - tile_start[tile_expert]) * tm, 0, tm)
    tile_rows = jnp.where(tile_ids < tile_end[-1], tile_rows, 0).astype(jnp.int32)
    tok_pair = jnp.arange(n_pair, dtype=jnp.int32) % n_tok
    src = (jnp.arange(nt * tm, dtype=jnp.int32) % n_tok).at[pos].set(
        tok_pair, unique_indices=True, mode="promise_in_bounds")
    gather_rows = lambda rows, idx: rows.at[idx].get(mode="promise_in_bounds")
    a_sorted = gather_rows(h.reshape(n_tok, h.shape[-1]), src)
    y_sorted = _moe_experts(a_sorted, tile_expert.astype(jnp.int32), tile_rows, w1, w3, w2, tm)
    ya = gather_rows(y_sorted, pos[:n_tok]).reshape(B, L, D // 2)
    yb = gather_rows(y_sorted, pos[n_tok:]).reshape(B, L, D // 2)
    return _moe_combine_norm(x, ya, yb, gw, gate, final_g)


def _rope_tables(n_tokens):
    pos = np.arange(n_tokens)
    row = (pos // GRID_W).astype(np.float32)
    col = (pos % GRID_W).astype(np.float32)
    n_freq = HEAD_DIM // 4
    inv = (np.float32(ROPE_BASE) ** (-np.arange(n_freq, dtype=np.float32) / np.float32(n_freq)))
    ang_r = (row[:, None] * inv).astype(np.float64)
    ang_c = (col[:, None] * inv).astype(np.float64)
    cos = np.concatenate([np.cos(ang_r), np.cos(ang_r), np.cos(ang_c), np.cos(ang_c)], axis=1)
    sin = np.concatenate([-np.sin(ang_r), np.sin(ang_r), -np.sin(ang_c), np.sin(ang_c)], axis=1)
    return jnp.asarray(cos, F32), jnp.asarray(sin, F32)


def _swap_halves(x):
    n = x.shape[-1]
    quarter = HEAD_DIM // 4
    lane = lax.broadcasted_iota(jnp.int32, x.shape, 1)
    up = pltpu.roll(x, n - quarter, axis=1)
    down = pltpu.roll(x, quarter, axis=1)
    return jnp.where((lane & quarter) == 0, up, down)


IN_PROJ_COLS = 2 * KV_WIDTH


def _proj_rope_kernel(h_ref, w_ref, cos_ref, sin_ref, o_ref, *, rope_cols, scale):
    acc = _dot(h_ref[...], w_ref[...])
    if rope_cols:
        heads = rope_cols // HEAD_DIM
        t = acc[:, :rope_cols]
        t = t * jnp.tile(cos_ref[...], (1, heads)) + _swap_halves(t) * jnp.tile(sin_ref[...], (1, heads))
        acc = t if rope_cols == acc.shape[1] else jnp.concatenate([t, acc[:, rope_cols:]], axis=1)
    if scale != 1.0:
        acc = acc * scale
    o_ref[...] = acc.astype(o_ref.dtype)


def _proj_rope(h, w, rope_cols, scale):
    B, L, D = h.shape
    N = w.shape[1]
    tn = IN_PROJ_COLS
    tm = _tile(L, 1024)
    cos, sin = _rope_tables(L)
    tab = pl.BlockSpec((tm, HEAD_DIM), lambda b, i, j: (i, 0))
    return pl.pallas_call(
        functools.partial(_proj_rope_kernel, rope_cols=rope_cols, scale=scale),
        out_shape=jax.ShapeDtypeStruct((B, L, N), BF16),
        grid=(B, L // tm, N // tn),
        in_specs=[pl.BlockSpec((None, tm, D), lambda b, i, j: (b, i, 0)),
                  pl.BlockSpec((D, tn), lambda b, i, j: (0, j)),
                  tab, tab],
        out_specs=pl.BlockSpec((None, tm, tn), lambda b, i, j: (b, i, j)),
        compiler_params=_cp(("parallel", "parallel", "arbitrary")),
        name="proj_rope",
    )(h, w, cos, sin)


def _in_proj(x, g, shift, scale, w, rope):
    h = _norm_mod(x, g, shift, scale)
    kv_end = ATT_WIDTH + IN_PROJ_COLS
    q = _proj_rope(h, w[:, :ATT_WIDTH], IN_PROJ_COLS if rope else 0, HEAD_DIM ** -0.5)
    kv = _proj_rope(h, w[:, ATT_WIDTH:kv_end], KV_WIDTH if rope else 0, 1.0)
    rest = _matmul(h, w[:, kv_end:], tm=2048)
    return q, kv, rest


def _nt_dot(a, b):
    return lax.dot_general(a, b, (((1,), (1,)), ((), ())), preferred_element_type=F32)


def _attn_bias_tables():
    T, G = BLOCK, ATT_GROUP
    qi = np.arange(G * T)[:, None] % T
    ki = np.arange(3 * T)[None, :]
    band = np.abs(ki - T - qi) <= WINDOW
    after_start = ki >= T
    before_end = ki < 2 * T
    masks = [band & after_start, band, band & before_end, band & after_start & before_end]
    return jnp.asarray(np.stack([np.where(m, 0.0, NEG) for m in masks]), F32)


def _attn_kernel(*refs, local, n_blocks):
    if local:
        (q_ref, kp_ref, kc_ref, kn_ref, vp_ref, vc_ref, vn_ref,
         kx_ref, vx_ref, bias_ref, sink_ref, g_ref, o_ref, acc_ref) = refs
    else:
        q_ref, kx_ref, vx_ref, sink_ref, g_ref, o_ref, acc_ref = refs
    T = BLOCK
    G = ATT_GROUP
    n_sub = q_ref.shape[0] // T
    for h in range(ATT_KV_HEADS):
        cols = slice(h * HEAD_DIM, (h + 1) * HEAD_DIM)
        sink = jnp.concatenate(
            [jnp.broadcast_to(sink_ref[:, (h * G + g) * HEAD_DIM:(h * G + g) * HEAD_DIM + 1], (T, 1))
             for g in range(G)], axis=0)
        kx = kx_ref[:, cols]
        vx = jnp.concatenate([vx_ref[:, cols], jnp.ones((kx_ref.shape[0], HEAD_DIM), BF16)], axis=1)
        if local:
            k_band = jnp.concatenate([kp_ref[:, cols], kc_ref[:, cols], kn_ref[:, cols]], axis=0)
            v_band = jnp.concatenate([vp_ref[:, cols], vc_ref[:, cols], vn_ref[:, cols]], axis=0)
            v_band = jnp.concatenate([v_band, jnp.ones(v_band.shape, BF16)], axis=1)
        for j in range(n_sub):
            rows = slice(j * T, (j + 1) * T)
            qs = jnp.concatenate(
                [q_ref[rows, (h * G + g) * HEAD_DIM:(h * G + g + 1) * HEAD_DIM] for g in range(G)],
                axis=0)
            s_ctx = _nt_dot(qs, kx)
            m = jnp.maximum(jnp.max(s_ctx, axis=-1, keepdims=True), sink)
            if local:
                blk = pl.program_id(1) * n_sub + j
                is_first = blk == 0
                is_last = blk == n_blocks - 1
                table = jnp.where(is_first, jnp.where(is_last, 3, 0), jnp.where(is_last, 2, 1))
                s_loc = _nt_dot(qs, k_band[j * T:(j + 3) * T]) + bias_ref[table]
                m = jnp.maximum(m, jnp.max(s_loc, axis=-1, keepdims=True))
            o = _dot(jnp.exp((s_ctx - m).astype(BF16)), vx)
            if local:
                o = o + _dot(jnp.exp((s_loc - m).astype(BF16)), v_band[j * T:(j + 3) * T])
            denom = o[:, HEAD_DIM:HEAD_DIM + 1] + jnp.exp(sink - m)
            o = o[:, :HEAD_DIM] * (1.0 / denom)
            for g in range(G):
                acc_ref[rows, (h * G + g) * HEAD_DIM:(h * G + g + 1) * HEAD_DIM] = o[g * T:(g + 1) * T]
    y = acc_ref[...]
    y = y * lax.rsqrt(jnp.mean(y * y, axis=-1, keepdims=True) + EPS) * g_ref[...]
    o_ref[...] = y.astype(o_ref.dtype)


def _attention(q, kv, kv_ctx, sink, g_attn):
    B, L, _ = q.shape
    C = kv_ctx.shape[1]
    local = kv is not None
    T = BLOCK
    nb = L // T
    n_sub = 2 if nb % 2 == 0 else 1
    sink_row = jnp.repeat(sink.astype(F32), HEAD_DIM).reshape(1, ATT_WIDTH)
    qspec = pl.BlockSpec((None, n_sub * T, ATT_WIDTH), lambda b, i: (b, i, 0))
    in_specs = [qspec]
    args = [q]
    cspecs = [pl.BlockSpec((None, C, KV_WIDTH), functools.partial(lambda b, i, part: (b, 0, part), part=part))
              for part in range(2)]
    row = pl.BlockSpec((1, ATT_WIDTH), lambda b, i: (0, 0))
    if local:
        for part in range(2):
            in_specs += [
                pl.BlockSpec((None, T, KV_WIDTH), functools.partial(
                    lambda b, i, part: (b, jnp.maximum(i * n_sub - 1, 0), part), part=part)),
                pl.BlockSpec((None, n_sub * T, KV_WIDTH), functools.partial(
                    lambda b, i, part: (b, i, part), part=part)),
                pl.BlockSpec((None, T, KV_WIDTH), functools.partial(
                    lambda b, i, part: (b, jnp.minimum((i + 1) * n_sub, nb - 1), part), part=part))]
        bias = _attn_bias_tables()
        in_specs += cspecs + [pl.BlockSpec(bias.shape, lambda b, i: (0, 0, 0))]
        args += [kv] * 6 + [kv_ctx, kv_ctx, bias]
    else:
        in_specs += cspecs
        args += [kv_ctx, kv_ctx]
    in_specs += [row, row]
    args += [sink_row, g_attn.reshape(1, ATT_WIDTH)]
    return pl.pallas_call(
        functools.partial(_attn_kernel, local=local, n_blocks=nb),
        out_shape=jax.ShapeDtypeStruct((B, L, ATT_WIDTH), BF16),
        grid=(B, nb // n_sub),
        in_specs=in_specs,
        out_specs=qspec,
        scratch_shapes=[pltpu.VMEM((n_sub * T, ATT_WIDTH), F32)],
        compiler_params=_cp(("parallel", "arbitrary")),
        name="attention_local" if local else "attention_context",
    )(*args)


def _halo_specs(ts, L, width, col_block):
    nb8 = L // HALO
    per = ts // HALO
    prev = pl.BlockSpec((None, HALO, width),
                        lambda b, i: (b, jnp.maximum(i * per - 1, 0), col_block))
    cur = pl.BlockSpec((None, ts, width), lambda b, i: (b, i, col_block))
    nxt = pl.BlockSpec((None, HALO, width),
                       lambda b, i: (b, jnp.minimum((i + 1) * per, nb8 - 1), col_block))
    return [prev, cur, nxt]


def _with_halo(prev_ref, cur_ref, next_ref):
    i = pl.program_id(1)
    last = pl.num_programs(1) - 1
    prev = jnp.where(i > 0, prev_ref[...], 0.0)
    nxt = jnp.where(i < last, next_ref[...], 0.0)
    return jnp.concatenate([prev, cur_ref[...], nxt], axis=0)


def _pool_kernel(prev_ref, cur_ref, next_ref, w_ref, scale_ref, g_ref, o_ref, *, seq_len):
    ts = cur_ref.shape[0]
    ext = _with_halo(prev_ref, cur_ref, next_ref)
    pos = pl.program_id(1) * ts + lax.broadcasted_iota(jnp.int32, (ts, 1), 0)
    outs = []
    for gidx, win in enumerate(POOL_WINDOWS):
        cols = slice(gidx * POOL_GROUP, (gidx + 1) * POOL_GROUP)
        run = ext[:, cols]
        step = 1
        while step < win:
            run = run[:run.shape[0] - step] + run[step:]
            step *= 2
        lo = HALO - win // 2
        total = run[lo:lo + ts]
        cnt = (jnp.minimum(pos + (win - win // 2), seq_len) - jnp.maximum(pos - win // 2, 0))
        mean = total * (1.0 / cnt.astype(F32))
        outs.append(_dot3(mean - cur_ref[:, cols], w_ref[gidx]))
    y = jnp.concatenate(outs, axis=-1) * scale_ref[...]
    y = y * lax.rsqrt(jnp.mean(y * y, axis=-1, keepdims=True) + EPS) * g_ref[...]
    o_ref[...] = y.astype(o_ref.dtype)


def _pool_mixer(proj, pool_w, pool_scale, g_pool):
    B, L, _ = proj.shape
    ts = _tile(L, 512)
    col_block = 0
    row = pl.BlockSpec((1, POOL_WIDTH), lambda b, i: (0, 0))
    return pl.pallas_call(
        functools.partial(_pool_kernel, seq_len=L),
        out_shape=jax.ShapeDtypeStruct((B, L, POOL_WIDTH), BF16),
        grid=(B, L // ts),
        in_specs=_halo_specs(ts, L, POOL_WIDTH, col_block) + [
            pl.BlockSpec(pool_w.shape, lambda b, i: (0, 0, 0)), row, row],
        out_specs=pl.BlockSpec((None, ts, POOL_WIDTH), lambda b, i: (b, i, 0)),
        compiler_params=_cp(("parallel", "arbitrary")),
        name="pool_mixer",
    )(proj, proj, proj, pool_w, pool_scale.reshape(1, POOL_WIDTH), g_pool.reshape(1, POOL_WIDTH))


def _short_conv_kernel(*refs):
    halo_refs = refs[:9]
    w_ref, b_ref = refs[9:11]
    out_refs = refs[11:]
    ts = out_refs[0].shape[0]
    for part in range(HY_ORDER + 1):
        ext = _with_halo(*halo_refs[3 * part:3 * part + 3])
        cols = slice(part * HY_WIDTH, (part + 1) * HY_WIDTH)
        acc = b_ref[:, cols] + ext[HALO - 1:HALO - 1 + ts] * w_ref[0:1, cols]
        acc = acc + ext[HALO:HALO + ts] * w_ref[1:2, cols]
        acc = acc + ext[HALO + 1:HALO + 1 + ts] * w_ref[2:3, cols]
        out_refs[part][...] = acc


def _short_conv(proj, conv_w, conv_b):
    B, L, _ = proj.shape
    ts = _tile(L, 512)
    first = POOL_WIDTH // HY_WIDTH
    assert first * HY_WIDTH == POOL_WIDTH
    specs = []
    for part in range(HY_ORDER + 1):
        specs += _halo_specs(ts, L, HY_WIDTH, first + part)
    n_col = (HY_ORDER + 1) * HY_WIDTH
    out = pl.BlockSpec((None, ts, HY_WIDTH), lambda b, i: (b, i, 0))
    return pl.pallas_call(
        _short_conv_kernel,
        out_shape=tuple(jax.ShapeDtypeStruct((B, L, HY_WIDTH), F32) for _ in range(HY_ORDER + 1)),
        grid=(B, L // ts),
        in_specs=specs + [pl.BlockSpec((HY_SHORT, n_col), lambda b, i: (0, 0)),
                          pl.BlockSpec((1, n_col), lambda b, i: (0, 0))],
        out_specs=(out,) * (HY_ORDER + 1),
        compiler_params=_cp(("parallel", "arbitrary")),
        name="hyena_short_conv",
    )(*([proj] * 9), conv_w, conv_b.reshape(1, n_col))


def _filter_tables(L):
    m = np.arange(L, dtype=np.float32)
    t = (m / np.float32(max(L - 1, 1))).astype(np.float32)
    w = (np.float32(2.0 * math.pi) * m / np.float32(L)).astype(np.float32)
    f = np.linspace(1e-4, HY_BANDS - 1, HY_BANDS, dtype=np.float32)
    ang = (w[:, None] * f).astype(np.float64)
    z = np.concatenate([t[:, None].astype(np.float64), np.cos(ang), -np.sin(ang)], axis=-1)
    z = np.pad(z, ((0, 0), (0, LANES - HY_EMB)))
    max_decay = math.log(HY_TARGET) / HY_SHORT_DECAY_PCT
    min_decay = math.log(HY_TARGET) / HY_LONG_DECAY_PCT
    deltas = np.linspace(min_decay, max_decay, HY_WIDTH, dtype=np.float32)
    decay = np.exp(-t[:, None].astype(np.float64) * np.abs(deltas)[None].astype(np.float64))
    return jnp.asarray(z, F32), jnp.asarray(decay, F32)


def _filter_kernel(z_ref, decay_ref, w1_ref, b1_ref, w2_ref, b2_ref, w3_ref, b3_ref, fr_ref, o_ref):
    tl = z_ref.shape[0]
    fr = fr_ref[...]
    a = jnp.sin(fr * (_dot3(z_ref[...], w1_ref[...]) + b1_ref[...]))
    a = jnp.sin(fr * (_dot3(a, w2_ref[...]) + b2_ref[...]))
    h = _dot3(a, w3_ref[...]) + b3_ref[...]
    h = h * jnp.tile(decay_ref[...], (1, 2 * HY_ORDER))
    row = pl.program_id(0) * tl + lax.broadcasted_iota(jnp.int32, h.shape, 0)
    col = lax.broadcasted_iota(jnp.int32, h.shape, 1)
    backward = (col // HY_WIDTH) % 2 == 1
    o_ref[...] = jnp.where(backward & (row == 0), 0.0, h)


def _hyena_filter_taps(L, w1, b1, w2, b2, w3, b3, freq):
    z, decay = _filter_tables(L)
    hid = w1.shape[1]
    n_out = w3.shape[1]
    tl = _tile(L, 512)
    w1p = jnp.pad(w1, ((0, LANES - HY_EMB), (0, 0)))
    full = lambda a: pl.BlockSpec(a.shape, lambda i: (0,) * a.ndim)
    args = [w1p, b1.reshape(1, hid), w2, b2.reshape(1, hid), w3, b3.reshape(1, n_out),
            freq.reshape(1, hid)]
    return pl.pallas_call(
        _filter_kernel,
        out_shape=jax.ShapeDtypeStruct((L, n_out), F32),
        grid=(L // tl,),
        in_specs=[pl.BlockSpec((tl, LANES), lambda i: (i, 0)),
                  pl.BlockSpec((tl, HY_WIDTH), lambda i: (i, 0))] + [full(a) for a in args],
        out_specs=pl.BlockSpec((tl, n_out), lambda i: (i, 0)),
        compiler_params=_cp(("arbitrary",)),
        name="hyena_filter_taps",
    )(z, decay, *args)


def _fft_matrices(n1, a_in):
    n = n1 * LANES
    k1 = np.arange(n1)
    a = np.arange(a_in)
    th1 = 2.0 * np.pi * np.outer(k1, a) / n1
    c1, s1 = np.cos(th1), np.sin(th1)
    m1_complex = np.block([[c1, s1], [-s1, c1]])
    m1_real = np.concatenate([c1, -s1], axis=0)
    m3 = np.block([[c1.T, -s1.T], [s1.T, c1.T]])
    b = np.arange(LANES)
    k2 = np.arange(LANES)
    k = k1[:, None, None] + n1 * k2[None, :, None]
    th2 = 2.0 * np.pi * (k * b[None, None, :] % n) / n
    c2, s2 = np.cos(th2), np.sin(th2)
    fwd = np.concatenate([np.concatenate([c2, s2], axis=2),
                          np.concatenate([-s2, c2], axis=2)], axis=1)
    c2t, s2t = np.swapaxes(c2, 1, 2), np.swapaxes(s2, 1, 2)
    inv = np.concatenate([np.concatenate([c2t, -s2t], axis=2),
                          np.concatenate([s2t, c2t], axis=2)], axis=1)
    as_bf16 = lambda m: jnp.asarray(m, F32).astype(BF16)
    return as_bf16(m1_complex), as_bf16(m1_real), as_bf16(m3), as_bf16(fwd), as_bf16(inv)


def _fft_stage1_kernel(x_ref, m_ref, o_ref, *, complex_in):
    n1 = o_ref.shape[0]
    for s in range(o_ref.shape[1]):
        if complex_in:
            x = jnp.concatenate([x_ref[0, :, s, :], x_ref[1, :, s, :]], axis=0)
        else:
            x = x_ref[:, s, :]
        y = _dot(m_ref[...], x.astype(BF16))
        o_ref[:, s, :] = _pack_bf16(y[:n1], y[n1:])


def _fft_stage1(x, m1, n1, complex_in):
    P, A, Wt = x.shape[0], x.shape[-3], x.shape[-1]
    tw = _tile(Wt, 512, LANES)
    if complex_in:
        xspec = pl.BlockSpec((None, 2, A, FFT_ROWS, tw), lambda p, i, j: (p, 0, 0, i, j))
    else:
        xspec = pl.BlockSpec((None, A, FFT_ROWS, tw), lambda p, i, j: (p, 0, i, j))
    return pl.pallas_call(
        functools.partial(_fft_stage1_kernel, complex_in=complex_in),
        out_shape=jax.ShapeDtypeStruct((P, n1, LANES, Wt), jnp.uint32),
        grid=(P, LANES // FFT_ROWS, Wt // tw),
        in_specs=[xspec, pl.BlockSpec(m1.shape, lambda p, i, j: (0, 0))],
        out_specs=pl.BlockSpec((None, n1, FFT_ROWS, tw), lambda p, i, j: (p, 0, i, j)),
        compiler_params=_cp(("parallel", "arbitrary", "arbitrary")),
        name="fft_stage1",
    )(x, m1)


def _filter_spectrum_kernel(y_ref, g_ref, o0_ref, o1_ref, *, scale):
    kt = y_ref.shape[0]
    for t in range(kt):
        y = jnp.concatenate(_unpack_bf16_pairs(y_ref[t]), axis=0)
        f = _dot(g_ref[t], y)
        fr, fi = f[:LANES], f[LANES:]
        for o, o_ref in enumerate((o0_ref, o1_ref)):
            fw = slice((2 * o) * HY_WIDTH, (2 * o + 1) * HY_WIDTH)
            bw = slice((2 * o + 1) * HY_WIDTH, (2 * o + 2) * HY_WIDTH)
            o_ref[0, t] = ((fr[:, fw] + fr[:, bw]) * scale).astype(o_ref.dtype)
            o_ref[1, t] = ((fi[:, fw] - fi[:, bw]) * scale).astype(o_ref.dtype)


def _filter_spectrum(taps, mats, n1):
    L, n_col = taps.shape
    a_in = L // LANES
    _, m1_real, _, g_fwd, _ = mats
    y = _fft_stage1(taps.reshape(1, a_in, LANES, n_col), m1_real, n1, complex_in=False)
    kt = _tile(n1, 4, 1)
    out = jax.ShapeDtypeStruct((2, n1, LANES, HY_WIDTH), BF16)
    ospec = pl.BlockSpec((2, kt, LANES, HY_WIDTH), lambda i: (0, i, 0, 0))
    return pl.pallas_call(
        functools.partial(_filter_spectrum_kernel, scale=1.0 / (n1 * LANES)),
        out_shape=(out, out),
        grid=(n1 // kt,),
        in_specs=[pl.BlockSpec((None, kt, LANES, n_col), lambda i: (0, i, 0, 0)),
                  pl.BlockSpec((kt, 2 * LANES, 2 * LANES), lambda i: (i, 0, 0))],
        out_specs=(ospec, ospec),
        compiler_params=_cp(("arbitrary",)),
        name="hyena_filter_spectrum",
    )(y, g_fwd)


def _fft_mid_kernel(y_ref, kf_ref, g_ref, gi_ref, o_ref):
    kt = y_ref.shape[0]
    for t in range(kt):
        y = jnp.concatenate(_unpack_bf16_pairs(y_ref[t]), axis=0)
        f = _dot(g_ref[t], y)
        fr, fi = f[:LANES], f[LANES:]
        kr, ki = kf_ref[0, t].astype(F32), kf_ref[1, t].astype(F32)
        p = jnp.concatenate([fr * kr - fi * ki, fr * ki + fi * kr], axis=0).astype(BF16)
        u = _dot(gi_ref[t], p)
        o_ref[t] = _pack_bf16(u[:LANES], u[LANES:])


def _fft_mid(y, kf, g_fwd, g_inv):
    P, n1, _, W = y.shape
    kt = _tile(n1, 8, 1)
    yspec = pl.BlockSpec((None, kt, LANES, W), lambda i, p: (p, i, 0, 0))
    gspec = pl.BlockSpec((kt, 2 * LANES, 2 * LANES), lambda i, p: (i, 0, 0))
    return pl.pallas_call(
        _fft_mid_kernel,
        out_shape=jax.ShapeDtypeStruct(y.shape, jnp.uint32),
        grid=(n1 // kt, P),
        in_specs=[yspec, pl.BlockSpec((2, kt, LANES, W), lambda i, p: (0, i, 0, 0)), gspec, gspec],
        out_specs=yspec,
        compiler_params=_cp(("arbitrary", "arbitrary")),
        name="fft_mid",
    )(y, kf, g_fwd, g_inv)


def _fft_stage3_kernel(u_ref, m_ref, z_ref, x_ref, bias_ref, g_ref, o_ref, ur_ref, ui_ref,
                       *, normalise):
    a_out = z_ref.shape[1]
    bias = bias_ref[...]
    packed = u_ref[...]
    ur_ref[...] = lax.bitcast_convert_type(packed << 16, F32)
    ui_ref[...] = lax.bitcast_convert_type(packed & jnp.uint32(0xFFFF0000), F32)
    for s in range(o_ref.shape[2]):
        u = jnp.concatenate([ur_ref[:, s, :], ui_ref[:, s, :]], axis=0).astype(BF16)
        conv = _dot(m_ref[...], u)
        for part in range(2):
            y = x_ref[part, :, s, :] * (
                conv[part * a_out:(part + 1) * a_out] + z_ref[part, :, s, :] * bias)
            if normalise:
                y = y * lax.rsqrt(jnp.mean(y * y, axis=-1, keepdims=True) + EPS) * g_ref[...]
            o_ref[part, :, s, :] = y


def _fft_stage3(u, m3, z, gate, bias_row, norm_g=None):
    P, n1, _, W = u.shape
    A = z.shape[2]
    pair = pl.BlockSpec((None, 2, A, FFT_ROWS, W), lambda p, i: (p, 0, 0, i, 0))
    row = pl.BlockSpec((1, W), lambda p, i: (0, 0))
    normalise = norm_g is not None
    return pl.pallas_call(
        functools.partial(_fft_stage3_kernel, normalise=normalise),
        out_shape=jax.ShapeDtypeStruct(z.shape, F32),
        grid=(P, LANES // FFT_ROWS),
        in_specs=[pl.BlockSpec((None, n1, FFT_ROWS, W), lambda p, i: (p, 0, i, 0)),
                  pl.BlockSpec(m3.shape, lambda p, i: (0, 0)),
                  pair, pair, row, row],
        out_specs=pair,
        scratch_shapes=[pltpu.VMEM((n1, FFT_ROWS, W), F32), pltpu.VMEM((n1, FFT_ROWS, W), F32)],
        compiler_params=_cp(("parallel", "arbitrary")),
        name="fft_stage3",
    )(u, m3, z, gate, bias_row, norm_g.reshape(1, W) if normalise else bias_row)


def _hyena_mixer(v, gates, kfs, d_bias, norm_g, mats, n1):
    B, L, W = v.shape
    pair_shape = (B // 2, 2, L // LANES, LANES, W)
    m1_complex, _, m3, g_fwd, g_inv = mats
    z = v.reshape(pair_shape)
    for o in range(HY_ORDER):
        y = _fft_stage1(z, m1_complex, n1, complex_in=True)
        u = _fft_mid(y, kfs[o], g_fwd, g_inv)
        z = _fft_stage3(u, m3, z, gates[o].reshape(pair_shape), d_bias[o].reshape(1, W),
                        norm_g if o == HY_ORDER - 1 else None)
    return z.reshape(B, L, W)


def kernel(x, c, ctx, c_ctx, w_mod, b_mod, norm1_g, w_in, attn_sink, pool_w, pool_scale,
           hy_conv_w, hy_conv_b, hy_f_w1, hy_f_b1, hy_f_w2, hy_f_b2, hy_f_w3, hy_f_b3,
           hy_f_freq, hy_bias, g_attn, g_pool, g_hyena, w_out, norm2_g,
           ff_w1, ff_w3, ff_w2, router_w, moe_w1, moe_w3, moe_w2, final_g):
    B, S, D = x.shape
    C = ctx.shape[1]
    depth = w_mod.shape[0]
    assert B % 2 == 0 and S % BLOCK == 0
    fft_mats = {}

    c_rows = jnp.concatenate([c, c_ctx[None], jnp.zeros((SUBLANES - B - 1, D), F32)], axis=0)
    xc = ctx

    def hyena_branch(l, proj, seq_len):
        rows = -(-seq_len // FFT_MIN_ROWS) * FFT_MIN_ROWS
        n1 = 2 * rows // LANES
        if rows not in fft_mats:
            fft_mats[rows] = _fft_matrices(n1, rows // LANES)
        mats = fft_mats[rows]
        taps = _hyena_filter_taps(seq_len, hy_f_w1[l], hy_f_b1[l], hy_f_w2[l], hy_f_b2[l],
                                  hy_f_w3[l], hy_f_b3[l], hy_f_freq[l])
        v, x1, x2 = _short_conv(proj, hy_conv_w[l], hy_conv_b[l])
        if seq_len < rows:
            pad = lambda t: jnp.pad(t, ((0, 0), (0, rows - seq_len), (0, 0)))
            taps = jnp.pad(taps, ((0, rows - seq_len), (0, 0)))
            v, x1, x2 = pad(v), pad(x1), pad(x2)
        kfs = _filter_spectrum(taps, mats, n1)
        z = _hyena_mixer(v, (x1, x2), kfs, hy_bias[l], g_hyena[l], mats, n1)
        return z[:, :seq_len]

    def mix(l, proj, y_att, res, gate, w_out_l, seq_len, norm=None):
        y_pool = _pool_mixer(proj, pool_w[l], pool_scale[l], g_pool[l])
        y_hy = hyena_branch(l, proj, seq_len)
        return _mix_out([y_att, y_pool, y_hy], w_out_l, res, gate, norm)

    for l in range(depth):
        last = l == depth - 1
        mod = _adaln(c_rows, w_mod[l], b_mod[l])
        sh1, sc1, g1, sh2, sc2, g2 = [mod[:B, j * D:(j + 1) * D] for j in range(6)]
        csh1, csc1, cg1, csh2, csc2, cg2 = [mod[B:B + 1, j * D:(j + 1) * D] for j in range(6)]
        w_in_l = w_in[l].astype(BF16)
        w_out_l = w_out[l].astype(BF16)

        if last:
            hc = _norm_mod(xc, norm1_g[l], csh1, csc1)
            kv_w = w_in_l[:, ATT_WIDTH:ATT_WIDTH + 2 * KV_WIDTH]
            kv_c = _matmul(hc, kv_w, out_dtype=BF16)
        else:
            q_c, kv_c, proj_c = _in_proj(xc, norm1_g[l], csh1, csc1, w_in_l, rope=False)
            y_att_c = _attention(q_c, None, kv_c, attn_sink[l], g_attn[l])
            xc_new, hc2 = mix(l, proj_c, y_att_c, xc, cg1, w_out_l, C, (norm2_g[l], csh2, csc2))
            i = l // 2
            if l % 2 == 0:
                hid_c = _swiglu_hidden(hc2, ff_w1[i].astype(BF16), ff_w3[i].astype(BF16))
                xc_new = _matmul_gated_residual([hid_c], ff_w2[i].astype(BF16), xc_new, cg2)
            else:
                raise NotImplementedError("context tokens through an expert layer")

        q, kv, proj = _in_proj(x, norm1_g[l], sh1, sc1, w_in_l, rope=True)
        y_att = _attention(q, kv, kv_c, attn_sink[l], g_attn[l])

        i = l // 2
        if l % 2 == 0:
            x, h2 = mix(l, proj, y_att, x, g1, w_out_l, S, (norm2_g[l], sh2, sc2))
            hid = _swiglu_hidden(h2, ff_w1[i].astype(BF16), ff_w3[i].astype(BF16))
            x = _matmul_gated_residual([hid], ff_w2[i].astype(BF16), x, g2)
            if last:
                x = _rmsnorm(x, final_g, F32)
        else:
            assert last
            x = mix(l, proj, y_att, x, g1, w_out_l, S)
            h2, gi, gw, counts = _norm_mod_router(x, norm2_g[l], sh2, sc2, router_w[i])
            x = _moe_layer(x, h2, gi, gw, counts, g2, final_g, moe_w1[i], moe_w3[i], moe_w2[i])
        if not last:
            xc = xc_new
    return x
```

```python
import functools
import math

import numpy as np
import jax
import jax.numpy as jnp
from jax import lax
from jax.experimental import pallas as pl
from jax.experimental.pallas import tpu as pltpu

F32 = jnp.float32
BF16 = jnp.bfloat16

EPS = 1e-6
NEG = -1e30
GRID_W = 64
ATT_HEADS = 8
ATT_KV_HEADS = 2
ATT_GROUP = ATT_HEADS // ATT_KV_HEADS
HEAD_DIM = 128
ATT_WIDTH = ATT_HEADS * HEAD_DIM
KV_WIDTH = ATT_KV_HEADS * HEAD_DIM
WINDOW = 128
BLOCK = 128
ROPE_BASE = 10000.0
POOL_WINDOWS = (2, 4, 8, 16)
POOL_GROUP = 128
POOL_WIDTH = POOL_GROUP * len(POOL_WINDOWS)
HY_WIDTH = 512
HY_ORDER = 2
HY_SHORT = 3
HY_EMB = 33
HY_BANDS = (HY_EMB - 1) // 2
HY_SHORT_DECAY_PCT = 0.3
HY_LONG_DECAY_PCT = 1.5
HY_TARGET = 1e-2
N_EXPERTS = 8
TOP_K = 2

LANES = 128
SUBLANES = 8
HALO = SUBLANES
FFT_MIN_ROWS = 2048
FFT_ROWS = SUBLANES
MOE_OUT_K_CHUNKS = 4
VMEM_LIMIT = 56 * 1024 * 1024


def _cp(sem):
    return pltpu.CompilerParams(dimension_semantics=sem, vmem_limit_bytes=VMEM_LIMIT)


def _tile(n, pref, mult=SUBLANES):
    if n <= pref:
        return n
    t = (pref // mult) * mult
    while t >= mult:
        if n % t == 0:
            return t
        t -= mult
    return n


def _split_bf16(x):
    hi = x.astype(BF16)
    lo = (x - hi.astype(F32)).astype(BF16)
    return hi, lo


def _dot(a, b):
    return jnp.dot(a, b, preferred_element_type=F32)


def _pack_bf16(lo, hi):
    lo_bits = lax.bitcast_convert_type(lo.astype(BF16).astype(F32), jnp.uint32)
    hi_bits = lax.bitcast_convert_type(hi.astype(BF16).astype(F32), jnp.uint32)
    return (lo_bits >> 16) | (hi_bits & jnp.uint32(0xFFFF0000))


def _pack_bf16_pairs(x):
    n = x.shape[-1] // 2
    return _pack_bf16(x[:, :n], x[:, n:])


def _unpack_bf16_pairs(u):
    lo = lax.bitcast_convert_type(u << 16, F32).astype(BF16)
    hi = lax.bitcast_convert_type(u & jnp.uint32(0xFFFF0000), F32).astype(BF16)
    return lo, hi


def _dot3(a, b):
    ah, al = _split_bf16(a)
    bh, bl = _split_bf16(b)
    return _dot(ah, bh) + (_dot(ah, bl) + _dot(al, bh))


def _adaln_kernel(c_ref, w_ref, b_ref, o_ref):
    c = c_ref[...]
    a = c * (1.0 / (1.0 + jnp.exp(-c)))
    o_ref[...] = _dot3(a, w_ref[...]) + b_ref[...]


def _adaln(c_rows, w_all, b_all, layer):
    R, D = c_rows.shape
    N = w_all.shape[2]
    tn = _tile(N, 1536, LANES)
    return pl.pallas_call(
        _adaln_kernel,
        out_shape=jax.ShapeDtypeStruct((R, N), F32),
        grid=(N // tn,),
        in_specs=[pl.BlockSpec((R, D), lambda j: (0, 0)),
                  pl.BlockSpec((None, D, tn), lambda j: (layer, 0, j)),
                  pl.BlockSpec((None, 1, tn), lambda j: (layer, 0, j))],
        out_specs=pl.BlockSpec((R, tn), lambda j: (0, j)),
        compiler_params=_cp(("arbitrary",)),
        name="adaln",
    )(c_rows, w_all, b_all[:, None, :])


def _norm_mod_kernel(x_ref, g_ref, sh_ref, sc_ref, o_ref):
    x = x_ref[...]
    y = x * lax.rsqrt(jnp.mean(x * x, axis=-1, keepdims=True) + EPS) * g_ref[...]
    o_ref[...] = (y * (1.0 + sc_ref[...]) + sh_ref[...]).astype(o_ref.dtype)


def _bcast_map(arr):
    if arr.shape[0] == 1:
        return lambda b, i: (0, 0, 0)
    return lambda b, i: (b, 0, 0)


def _norm_mod(x, g, shift, scale):
    B, L, D = x.shape
    ts = _tile(L, 512)
    shift = shift[:, None, :]
    scale = scale[:, None, :]
    return pl.pallas_call(
        _norm_mod_kernel,
        out_shape=jax.ShapeDtypeStruct((B, L, D), BF16),
        grid=(B, L // ts),
        in_specs=[pl.BlockSpec((None, ts, D), lambda b, i: (b, i, 0)),
                  pl.BlockSpec((1, D), lambda b, i: (0, 0)),
                  pl.BlockSpec((None, 1, D), _bcast_map(shift)),
                  pl.BlockSpec((None, 1, D), _bcast_map(scale))],
        out_specs=pl.BlockSpec((None, ts, D), lambda b, i: (b, i, 0)),
        compiler_params=_cp(("parallel", "parallel")),
        name="norm_mod",
    )(x, g.reshape(1, D), shift, scale)


def _route(h, rw_ref, tri_ref, run_ref, gi_ref, gw_ref, cnt_ref):
    logits = _dot3(h, rw_ref[...])
    lane = lax.broadcasted_iota(jnp.int32, logits.shape, 1)
    logits = jnp.where(lane < N_EXPERTS, logits, NEG)
    m1 = jnp.max(logits, axis=-1, keepdims=True)
    i1 = jnp.min(jnp.where(logits == m1, lane, LANES), axis=-1, keepdims=True)
    rest = jnp.where(lane == i1, NEG, logits)
    m2 = jnp.max(rest, axis=-1, keepdims=True)
    i2 = jnp.min(jnp.where(rest == m2, lane, LANES), axis=-1, keepdims=True)
    e2 = jnp.exp(m2 - m1)
    w1 = 1.0 / (1.0 + e2)
    w2 = e2 * w1
    pick1 = lane == i1
    pick2 = lane == i2
    both = jnp.where(pick1 | pick2, 1.0, 0.0)
    before = _dot(tri_ref[...], both.astype(BF16)) + run_ref[...]
    r1 = jnp.sum(jnp.where(pick1, before, 0.0), axis=-1, keepdims=True)
    r2 = jnp.sum(jnp.where(pick2, before, 0.0), axis=-1, keepdims=True)
    run_ref[...] = run_ref[...] + jnp.sum(both, axis=0, keepdims=True)
    cnt_ref[...] = run_ref[...]
    ids = jnp.where(lane == 0, i1.astype(F32), jnp.where(lane == 1, i2.astype(F32),
                    jnp.where(lane == 2, r1, jnp.where(lane == 3, r2, 0.0))))
    gi_ref[...] = jnp.transpose(ids)[:SUBLANES, :]
    gw_ref[...] = jnp.where(lane == 0, w1, jnp.where(lane == 1, w2, 0.0))


def _norm_mod_router_kernel(x_ref, g_ref, sh_ref, sc_ref, rw_ref, tri_ref,
                            o_ref, gi_ref, gw_ref, cnt_ref, run_ref):
    @pl.when((pl.program_id(0) == 0) & (pl.program_id(1) == 0))
    def _():
        run_ref[...] = jnp.zeros(run_ref.shape, run_ref.dtype)

    x = x_ref[...]
    y = x * lax.rsqrt(jnp.mean(x * x, axis=-1, keepdims=True) + EPS) * g_ref[...]
    h = y * (1.0 + sc_ref[...]) + sh_ref[...]
    o_ref[...] = _pack_bf16_pairs(h)
    _route(h, rw_ref, tri_ref, run_ref, gi_ref, gw_ref, cnt_ref)


def _norm_mod_router(x, g, shift, scale, router_w):
    B, L, D = x.shape
    ts = _tile(L, 512)
    shift = shift[:, None, :]
    scale = scale[:, None, :]
    rw = jnp.pad(router_w, ((0, 0), (0, LANES - router_w.shape[1])))
    tri = jnp.asarray(np.tril(np.ones((ts, ts), np.float32), -1), BF16)
    row = pl.BlockSpec((None, ts, D), lambda b, i: (b, i, 0))
    small = pl.BlockSpec((None, ts, LANES), lambda b, i: (b, i, 0))
    return pl.pallas_call(
        _norm_mod_router_kernel,
        out_shape=(jax.ShapeDtypeStruct((B, L, D // 2), jnp.uint32),
                   jax.ShapeDtypeStruct((B, SUBLANES, L), F32),
                   jax.ShapeDtypeStruct((B, L, LANES), F32),
                   jax.ShapeDtypeStruct((1, LANES), F32)),
        grid=(B, L // ts),
        in_specs=[row,
                  pl.BlockSpec((1, D), lambda b, i: (0, 0)),
                  pl.BlockSpec((None, 1, D), _bcast_map(shift)),
                  pl.BlockSpec((None, 1, D), _bcast_map(scale)),
                  pl.BlockSpec((D, LANES), lambda b, i: (0, 0)),
                  pl.BlockSpec((ts, ts), lambda b, i: (0, 0))],
        out_specs=(pl.BlockSpec((None, ts, D // 2), lambda b, i: (b, i, 0)),
                   pl.BlockSpec((None, SUBLANES, ts), lambda b, i: (b, 0, i)), small,
                   pl.BlockSpec((1, LANES), lambda b, i: (0, 0))),
        scratch_shapes=[pltpu.VMEM((1, LANES), F32)],
        compiler_params=_cp(("arbitrary", "arbitrary")),
        name="norm_mod_router",
    )(x, g.reshape(1, D), shift, scale, rw, tri)


def _mix_out_kernel(*refs, n_a, emit_h):
    a_refs = refs[:n_a]
    w_refs = refs[n_a:2 * n_a]
    res_ref, gate_ref = refs[2 * n_a:2 * n_a + 2]
    acc = _dot(a_refs[0][...].astype(BF16), w_refs[0][...])
    for a_ref, w_ref in zip(a_refs[1:], w_refs[1:]):
        acc = acc + _dot(a_ref[...].astype(BF16), w_ref[...])
    x = res_ref[...] + gate_ref[...] * acc
    if emit_h:
        g_ref, sh_ref, sc_ref, x_ref, h_ref = refs[2 * n_a + 2:]
        y = x * lax.rsqrt(jnp.mean(x * x, axis=-1, keepdims=True) + EPS) * g_ref[...]
        h_ref[...] = (y * (1.0 + sc_ref[...]) + sh_ref[...]).astype(h_ref.dtype)
    else:
        x_ref, = refs[2 * n_a + 2:]
    x_ref[...] = x


def _mix_out(a_list, w, res, gate, norm=None):
    B, L, D = res.shape
    tm = _tile(L, 512)
    emit_h = norm is not None
    gate = gate[:, None, :]
    in_specs = [pl.BlockSpec((None, tm, a.shape[-1]), lambda b, i: (b, i, 0)) for a in a_list]
    off = 0
    for a in a_list:
        k = a.shape[-1]
        assert off % k == 0
        in_specs.append(pl.BlockSpec((k, D), functools.partial(lambda b, i, blk: (blk, 0), blk=off // k)))
        off += k
    row = pl.BlockSpec((None, tm, D), lambda b, i: (b, i, 0))
    in_specs += [row, pl.BlockSpec((None, 1, D), _bcast_map(gate))]
    args = [*a_list, *([w] * len(a_list)), res, gate]
    out_shape = [jax.ShapeDtypeStruct((B, L, D), F32)]
    out_specs = [row]
    if emit_h:
        g, shift, scale = norm
        shift, scale = shift[:, None, :], scale[:, None, :]
        in_specs += [pl.BlockSpec((1, D), lambda b, i: (0, 0)),
                     pl.BlockSpec((None, 1, D), _bcast_map(shift)),
                     pl.BlockSpec((None, 1, D), _bcast_map(scale))]
        args += [g.reshape(1, D), shift, scale]
        out_shape.append(jax.ShapeDtypeStruct((B, L, D), BF16))
        out_specs.append(row)
    out = pl.pallas_call(
        functools.partial(_mix_out_kernel, n_a=len(a_list), emit_h=emit_h),
        out_shape=tuple(out_shape),
        grid=(B, L // tm),
        in_specs=in_specs,
        out_specs=tuple(out_specs),
        compiler_params=_cp(("parallel", "parallel")),
        name="mix_out",
    )(*args)
    return out if emit_h else out[0]


def _rmsnorm_kernel(x_ref, g_ref, o_ref):
    x = x_ref[...]
    y = x * lax.rsqrt(jnp.mean(x * x, axis=-1, keepdims=True) + EPS) * g_ref[...]
    o_ref[...] = y.astype(o_ref.dtype)


def _rmsnorm(x, g, out_dtype):
    B, L, D = x.shape
    ts = _tile(L, 512)
    return pl.pallas_call(
        _rmsnorm_kernel,
        out_shape=jax.ShapeDtypeStruct((B, L, D), out_dtype),
        grid=(B, L // ts),
        in_specs=[pl.BlockSpec((None, ts, D), lambda b, i: (b, i, 0)),
                  pl.BlockSpec((1, D), lambda b, i: (0, 0))],
        out_specs=pl.BlockSpec((None, ts, D), lambda b, i: (b, i, 0)),
        compiler_params=_cp(("parallel", "parallel")),
        name="rmsnorm",
    )(x, g.reshape(1, D))


def _mm_kernel(a_ref, w_ref, o_ref):
    o_ref[...] = _dot(a_ref[...], w_ref[...]).astype(o_ref.dtype)


def _matmul(a, w, out_dtype=F32, tm=1024, tn=512):
    B, L, K = a.shape
    N = w.shape[1]
    tm = _tile(L, tm)
    tn = _tile(N, tn, LANES)
    return pl.pallas_call(
        _mm_kernel,
        out_shape=jax.ShapeDtypeStruct((B, L, N), out_dtype),
        grid=(B, L // tm, N // tn),
        in_specs=[pl.BlockSpec((None, tm, K), lambda b, i, j: (b, i, 0)),
                  pl.BlockSpec((K, tn), lambda b, i, j: (0, j))],
        out_specs=pl.BlockSpec((None, tm, tn), lambda b, i, j: (b, i, j)),
        compiler_params=_cp(("parallel", "parallel", "arbitrary")),
        name="matmul",
    )(a, w)


def _mm_res_kernel(*refs, n_a):
    a_refs = refs[:n_a]
    w_refs = refs[n_a:2 * n_a]
    res_ref, gate_ref, o_ref = refs[2 * n_a:]
    acc = _dot(a_refs[0][...].astype(BF16), w_refs[0][...])
    for a_ref, w_ref in zip(a_refs[1:], w_refs[1:]):
        acc = acc + _dot(a_ref[...].astype(BF16), w_ref[...])
    o_ref[...] = res_ref[...] + gate_ref[...] * acc


def _matmul_gated_residual(a_list, w, res, gate, tm=1024, tn=512):
    B, L, N = res.shape
    tm = _tile(L, tm)
    tn = _tile(N, tn, LANES)
    gate = gate[:, None, :]
    widths = [a.shape[-1] for a in a_list]
    unit = math.gcd(*widths) if len(widths) > 1 else widths[0]
    in_specs = [pl.BlockSpec((None, tm, k), lambda b, i, j: (b, i, 0)) for k in widths]
    off = 0
    for k in widths:
        assert off % k == 0 and k % unit == 0
        in_specs.append(pl.BlockSpec((k, tn), functools.partial(
            lambda b, i, j, blk: (blk, j), blk=off // k)))
        off += k
    gmap = (lambda b, i, j: (0, 0, j)) if gate.shape[0] == 1 else (lambda b, i, j: (b, 0, j))
    in_specs += [pl.BlockSpec((None, tm, tn), lambda b, i, j: (b, i, j)),
                 pl.BlockSpec((None, 1, tn), gmap)]
    return pl.pallas_call(
        functools.partial(_mm_res_kernel, n_a=len(a_list)),
        out_shape=jax.ShapeDtypeStruct((B, L, N), F32),
        grid=(B, L // tm, N // tn),
        in_specs=in_specs,
        out_specs=pl.BlockSpec((None, tm, tn), lambda b, i, j: (b, i, j)),
        compiler_params=_cp(("parallel", "parallel", "arbitrary")),
        name="matmul_gated_residual",
    )(*a_list, *([w] * len(a_list)), res, gate)


def _silu(x):
    return x * (1.0 / (1.0 + jnp.exp(-x)))


def _swiglu_kernel(a_ref, w1_ref, w3_ref, o_ref):
    a = a_ref[...]
    o_ref[...] = (_silu(_dot(a, w1_ref[...])) * _dot(a, w3_ref[...])).astype(o_ref.dtype)


def _swiglu_hidden(a, w1, w3, tm=1024, tf=512):
    B, L, D = a.shape
    F = w1.shape[1]
    tm = _tile(L, tm)
    tf = _tile(F, tf, LANES)
    wspec = pl.BlockSpec((D, tf), lambda b, i, j: (0, j))
    return pl.pallas_call(
        _swiglu_kernel,
        out_shape=jax.ShapeDtypeStruct((B, L, F), BF16),
        grid=(B, L // tm, F // tf),
        in_specs=[pl.BlockSpec((None, tm, D), lambda b, i, j: (b, i, 0)), wspec, wspec],
        out_specs=pl.BlockSpec((None, tm, tf), lambda b, i, j: (b, i, j)),
        compiler_params=_cp(("parallel", "parallel", "arbitrary")),
        name="swiglu_hidden",
    )(a, w1, w3)


def _for_valid_rows(n_valid, tm, compute, o_ref):
    half_rows = tm // 2

    @pl.when(n_valid > half_rows)
    def _():
        compute(slice(0, tm))

    @pl.when((n_valid > 0) & (n_valid <= half_rows))
    def _():
        compute(slice(0, half_rows))
        o_ref[half_rows:, :] = jnp.zeros((tm - half_rows, o_ref.shape[1]), o_ref.dtype)

    @pl.when(n_valid == 0)
    def _():
        o_ref[...] = jnp.zeros(o_ref.shape, o_ref.dtype)


def _moe_hidden_kernel(te_ref, rows_ref, a_ref, w1_ref, w3_ref, o_ref, w1b_ref, w3b_ref):
    i = pl.program_id(1)
    n_valid = rows_ref[i]
    half = a_ref.shape[1]
    new_weights = (i == 0) | (te_ref[i] != te_ref[jnp.maximum(i - 1, 0)])

    @pl.when((n_valid > 0) & new_weights)
    def _():
        w1b_ref[...] = w1_ref[...].astype(BF16)
        w3b_ref[...] = w3_ref[...].astype(BF16)

    def compute(rows):
        lo, hi = _unpack_bf16_pairs(a_ref[rows, :])
        gate = _dot(lo, w1b_ref[:half, :]) + _dot(hi, w1b_ref[half:, :])
        up = _dot(lo, w3b_ref[:half, :]) + _dot(hi, w3b_ref[half:, :])
        o_ref[rows, :] = (_silu(gate) * up).astype(o_ref.dtype)

    _for_valid_rows(n_valid, a_ref.shape[0], compute, o_ref)


def _moe_out_kernel(te_ref, rows_ref, a_ref, w2_ref, o_ref):
    def compute(rows):
        kc = a_ref.shape[1] // MOE_OUT_K_CHUNKS
        acc = _dot(a_ref[rows, :kc], w2_ref[:kc, :].astype(BF16))
        for c in range(1, MOE_OUT_K_CHUNKS):
            acc = acc + _dot(a_ref[rows, c * kc:(c + 1) * kc],
                             w2_ref[c * kc:(c + 1) * kc, :].astype(BF16))
        o_ref[rows, :] = _pack_bf16_pairs(acc)

    _for_valid_rows(rows_ref[pl.program_id(0)], a_ref.shape[0], compute, o_ref)


def _moe_out_cols(d_model):
    return _tile(d_model, 512, 2 * LANES)


def _moe_experts(a_sorted, tile_expert, tile_rows, w1, w3, w2, tm, tf=512):
    R = a_sorted.shape[0]
    E, D, F = w1.shape
    tf = _tile(F, tf, LANES)
    tn = _moe_out_cols(D)
    nt = R // tm
    w13 = pl.BlockSpec((None, D, tf), lambda j, i, te, n: (te[i], 0, j))
    hidden = pl.pallas_call(
        _moe_hidden_kernel,
        out_shape=jax.ShapeDtypeStruct((R, F), BF16),
        grid_spec=pltpu.PrefetchScalarGridSpec(
            num_scalar_prefetch=2,
            grid=(F // tf, nt),
            in_specs=[pl.BlockSpec((tm, D // 2), lambda j, i, te, n: (i, 0)), w13, w13],
            out_specs=pl.BlockSpec((tm, tf), lambda j, i, te, n: (i, j)),
            scratch_shapes=[pltpu.VMEM((D, tf), BF16), pltpu.VMEM((D, tf), BF16)]),
        compiler_params=_cp(("arbitrary", "arbitrary")),
        name="moe_hidden",
    )(tile_expert, tile_rows, a_sorted, w1, w3)
    return pl.pallas_call(
        _moe_out_kernel,
        out_shape=jax.ShapeDtypeStruct((R, D // 2), jnp.uint32),
        grid_spec=pltpu.PrefetchScalarGridSpec(
            num_scalar_prefetch=2,
            grid=(nt, D // tn),
            in_specs=[pl.BlockSpec((tm, F), lambda i, j, te, n: (i, 0)),
                      pl.BlockSpec((None, F, tn), lambda i, j, te, n: (te[i], 0, j))],
            out_specs=pl.BlockSpec((tm, tn // 2), lambda i, j, te, n: (i, j))),
        compiler_params=_cp(("arbitrary", "arbitrary")),
        name="moe_out",
    )(tile_expert, tile_rows, hidden, w2)


def _unpack_column_blocks(u, block):
    parts = []
    for c in range(u.shape[1] // block):
        w = u[:, c * block:(c + 1) * block]
        parts.append(lax.bitcast_convert_type(w << 16, F32))
        parts.append(lax.bitcast_convert_type(w & jnp.uint32(0xFFFF0000), F32))
    return jnp.concatenate(parts, axis=1)


def _moe_combine_kernel(x_ref, ya_ref, yb_ref, gw_ref, gate_ref, g_ref, o_ref, *, block):
    gw = gw_ref[...]
    moe = (gw[:, 0:1] * _unpack_column_blocks(ya_ref[...], block)
           + gw[:, 1:2] * _unpack_column_blocks(yb_ref[...], block))
    x = x_ref[...] + gate_ref[...] * moe
    y = x * lax.rsqrt(jnp.mean(x * x, axis=-1, keepdims=True) + EPS) * g_ref[...]
    o_ref[...] = y


def _moe_combine_norm(x, ya, yb, gw, gate, g):
    B, L, D = x.shape
    ts = _tile(L, 512)
    row = pl.BlockSpec((None, ts, D), lambda b, i: (b, i, 0))
    packed = pl.BlockSpec((None, ts, D // 2), lambda b, i: (b, i, 0))
    return pl.pallas_call(
        functools.partial(_moe_combine_kernel, block=_moe_out_cols(D) // 2),
        out_shape=jax.ShapeDtypeStruct((B, L, D), F32),
        grid=(B, L // ts),
        in_specs=[row, packed, packed,
                  pl.BlockSpec((None, ts, LANES), lambda b, i: (b, i, 0)),
                  pl.BlockSpec((None, 1, D), lambda b, i: (b, 0, 0)),
                  pl.BlockSpec((1, D), lambda b, i: (0, 0))],
        out_specs=row,
        compiler_params=_cp(("parallel", "parallel")),
        name="moe_combine_norm",
    )(x, ya, yb, gw, gate[:, None, :], g.reshape(1, D))


def _moe_layer(x, h, gi, gw, counts, gate, final_g, w1, w3, w2):
    B, L, D = x.shape
    E = w1.shape[0]
    n_tok = B * L
    n_pair = n_tok * TOP_K
    tm = _tile(n_pair, 1024)
    nt = n_pair // tm + E
    rows_of = lambda r: jnp.concatenate(
        [gi[:, r + c, :].reshape(n_tok) for c in range(TOP_K)]).astype(jnp.int32)
    e_pair = rows_of(0)
    rank = rows_of(TOP_K)
    counts = counts[0, :E].astype(jnp.int32)
    tiles_per = (counts + tm - 1) // tm
    tile_end = jnp.cumsum(tiles_per)
    tile_start = tile_end - tiles_per
    row_start = jnp.zeros((n_pair,), jnp.int32)
    for e in range(E):
        row_start = jnp.where(e_pair == e, tile_start[e] * tm, row_start)
    pos = row_start + rank
    tile_ids = jnp.arange(nt, dtype=jnp.int32)
    tile_expert = jnp.minimum(
        jnp.sum((tile_ids[:, None] >= tile_end[None, :]).astype(jnp.int32), axis=1), E - 1)
    tile_rows = jnp.clip(counts[tile_expert] - (tile_ids - tile_start[tile_expert]) * tm, 0, tm)
    tile_rows = jnp.where(tile_ids < tile_end[-1], tile_rows, 0).astype(jnp.int32)
    tok_pair = jnp.arange(n_pair, dtype=jnp.int32) % n_tok
    src = (jnp.arange(nt * tm, dtype=jnp.int32) % n_tok).at[pos].set(
        tok_pair, unique_indices=True, mode="promise_in_bounds")
    gather_rows = lambda rows, idx: rows.at[idx].get(mode="promise_in_bounds")
    a_sorted = gather_rows(h.reshape(n_tok, h.shape[-1]), src)
    y_sorted = _moe_experts(a_sorted, tile_expert.astype(jnp.int32), tile_rows, w1, w3, w2, tm)
    ya = gather_rows(y_sorted, pos[:n_tok]).reshape(B, L, D // 2)
    yb = gather_rows(y_sorted, pos[n_tok:]).reshape(B, L, D // 2)
    return _moe_combine_norm(x, ya, yb, gw, gate, final_g)


def _rope_tables(n_tokens):
    pos = np.arange(n_tokens)
    row = (pos // GRID_W).astype(np.float32)
    col = (pos % GRID_W).astype(np.float32)
    n_freq = HEAD_DIM // 4
    inv = (np.float32(ROPE_BASE) ** (-np.arange(n_freq, dtype=np.float32) / np.float32(n_freq)))
    ang_r = (row[:, None] * inv).astype(np.float64)
    ang_c = (col[:, None] * inv).astype(np.float64)
    cos = np.concatenate([np.cos(ang_r), np.cos(ang_r), np.cos(ang_c), np.cos(ang_c)], axis=1)
    sin = np.concatenate([-np.sin(ang_r), np.sin(ang_r), -np.sin(ang_c), np.sin(ang_c)], axis=1)
    return jnp.asarray(cos, F32), jnp.asarray(sin, F32)


def _swap_halves(x):
    n = x.shape[-1]
    quarter = HEAD_DIM // 4
    lane = lax.broadcasted_iota(jnp.int32, x.shape, 1)
    up = pltpu.roll(x, n - quarter, axis=1)
    down = pltpu.roll(x, quarter, axis=1)
    return jnp.where((lane & quarter) == 0, up, down)


IN_PROJ_COLS = 2 * KV_WIDTH


def _proj_rope_kernel(h_ref, w_ref, cos_ref, sin_ref, o_ref, *, rope_cols, scale):
    acc = _dot(h_ref[...], w_ref[...])
    if rope_cols:
        heads = rope_cols // HEAD_DIM
        t = acc[:, :rope_cols]
        t = t * jnp.tile(cos_ref[...], (1, heads)) + _swap_halves(t) * jnp.tile(sin_ref[...], (1, heads))
        acc = t if rope_cols == acc.shape[1] else jnp.concatenate([t, acc[:, rope_cols:]], axis=1)
    if scale != 1.0:
        acc = acc * scale
    o_ref[...] = acc.astype(o_ref.dtype)


def _proj_rope(h, w, rope_cols, scale):
    B, L, D = h.shape
    N = w.shape[1]
    tn = IN_PROJ_COLS
    tm = _tile(L, 1024)
    cos, sin = _rope_tables(L)
    tab = pl.BlockSpec((tm, HEAD_DIM), lambda b, i, j: (i, 0))
    return pl.pallas_call(
        functools.partial(_proj_rope_kernel, rope_cols=rope_cols, scale=scale),
        out_shape=jax.ShapeDtypeStruct((B, L, N), BF16),
        grid=(B, L // tm, N // tn),
        in_specs=[pl.BlockSpec((None, tm, D), lambda b, i, j: (b, i, 0)),
                  pl.BlockSpec((D, tn), lambda b, i, j: (0, j)),
                  tab, tab],
        out_specs=pl.BlockSpec((None, tm, tn), lambda b, i, j: (b, i, j)),
        compiler_params=_cp(("parallel", "parallel", "arbitrary")),
        name="proj_rope",
    )(h, w, cos, sin)


def _in_proj(x, g, shift, scale, w, rope):
    h = _norm_mod(x, g, shift, scale)
    kv_end = ATT_WIDTH + IN_PROJ_COLS
    q = _proj_rope(h, w[:, :ATT_WIDTH], IN_PROJ_COLS if rope else 0, HEAD_DIM ** -0.5)
    kv = _proj_rope(h, w[:, ATT_WIDTH:kv_end], KV_WIDTH if rope else 0, 1.0)
    rest = _matmul(h, w[:, kv_end:], tm=2048)
    return q, kv, rest


def _nt_dot(a, b):
    return lax.dot_general(a, b, (((1,), (1,)), ((), ())), preferred_element_type=F32)


def _attn_bias_tables():
    T, G = BLOCK, ATT_GROUP
    qi = np.arange(G * T)[:, None] % T
    ki = np.arange(3 * T)[None, :]
    band = np.abs(ki - T - qi) <= WINDOW
    after_start = ki >= T
    before_end = ki < 2 * T
    masks = [band & after_start, band, band & before_end, band & after_start & before_end]
    return jnp.asarray(np.stack([np.where(m, 0.0, NEG) for m in masks]), F32)


def _attn_kernel(*refs, local, n_blocks):
    if local:
        (q_ref, kp_ref, kc_ref, kn_ref, vp_ref, vc_ref, vn_ref,
         kx_ref, vx_ref, bias_ref, sink_ref, g_ref, o_ref, acc_ref) = refs
    else:
        q_ref, kx_ref, vx_ref, sink_ref, g_ref, o_ref, acc_ref = refs
    T = BLOCK
    G = ATT_GROUP
    n_sub = q_ref.shape[0] // T
    for h in range(ATT_KV_HEADS):
        cols = slice(h * HEAD_DIM, (h + 1) * HEAD_DIM)
        sink = jnp.concatenate(
            [jnp.broadcast_to(sink_ref[:, (h * G + g) * HEAD_DIM:(h * G + g) * HEAD_DIM + 1], (T, 1))
             for g in range(G)], axis=0)
        kx = kx_ref[:, cols]
        vx = jnp.concatenate([vx_ref[:, cols], jnp.ones((kx_ref.shape[0], HEAD_DIM), BF16)], axis=1)
        if local:
            k_band = jnp.concatenate([kp_ref[:, cols], kc_ref[:, cols], kn_ref[:, cols]], axis=0)
            v_band = jnp.concatenate([vp_ref[:, cols], vc_ref[:, cols], vn_ref[:, cols]], axis=0)
            v_band = jnp.concatenate([v_band, jnp.ones(v_band.shape, BF16)], axis=1)
        for j in range(n_sub):
            rows = slice(j * T, (j + 1) * T)
            qs = jnp.concatenate(
                [q_ref[rows, (h * G + g) * HEAD_DIM:(h * G + g + 1) * HEAD_DIM] for g in range(G)],
                axis=0)
            s_ctx = _nt_dot(qs, kx)
            m = jnp.maximum(jnp.max(s_ctx, axis=-1, keepdims=True), sink)
            if local:
                blk = pl.program_id(1) * n_sub + j
                is_first = blk == 0
                is_last = blk == n_blocks - 1
                table = jnp.where(is_first, jnp.where(is_last, 3, 0), jnp.where(is_last, 2, 1))
                s_loc = _nt_dot(qs, k_band[j * T:(j + 3) * T]) + bias_ref[table]
                m = jnp.maximum(m, jnp.max(s_loc, axis=-1, keepdims=True))
            o = _dot(jnp.exp((s_ctx - m).astype(BF16)), vx)
            if local:
                o = o + _dot(jnp.exp((s_loc - m).astype(BF16)), v_band[j * T:(j + 3) * T])
            denom = o[:, HEAD_DIM:HEAD_DIM + 1] + jnp.exp(sink - m)
            o = o[:, :HEAD_DIM] * (1.0 / denom)
            for g in range(G):
                acc_ref[rows, (h * G + g) * HEAD_DIM:(h * G + g + 1) * HEAD_DIM] = o[g * T:(g + 1) * T]
    y = acc_ref[...]
    y = y * lax.rsqrt(jnp.mean(y * y, axis=-1, keepdims=True) + EPS) * g_ref[...]
    o_ref[...] = y.astype(o_ref.dtype)


def _attention(q, kv, kv_ctx, sink, g_attn):
    B, L, _ = q.shape
    C = kv_ctx.shape[1]
    local = kv is not None
    T = BLOCK
    nb = L // T
    n_sub = 2 if nb % 2 == 0 else 1
    sink_row = jnp.repeat(sink.astype(F32), HEAD_DIM).reshape(1, ATT_WIDTH)
    qspec = pl.BlockSpec((None, n_sub * T, ATT_WIDTH), lambda b, i: (b, i, 0))
    in_specs = [qspec]
    args = [q]
    cspecs = [pl.BlockSpec((None, C, KV_WIDTH), functools.partial(lambda b, i, part: (b, 0, part), part=part))
              for part in range(2)]
    row = pl.BlockSpec((1, ATT_WIDTH), lambda b, i: (0, 0))
    if local:
        for part in range(2):
            in_specs += [
                pl.BlockSpec((None, T, KV_WIDTH), functools.partial(
                    lambda b, i, part: (b, jnp.maximum(i * n_sub - 1, 0), part), part=part)),
                pl.BlockSpec((None, n_sub * T, KV_WIDTH), functools.partial(
                    lambda b, i, part: (b, i, part), part=part)),
                pl.BlockSpec((None, T, KV_WIDTH), functools.partial(
                    lambda b, i, part: (b, jnp.minimum((i + 1) * n_sub, nb - 1), part), part=part))]
        bias = _attn_bias_tables()
        in_specs += cspecs + [pl.BlockSpec(bias.shape, lambda b, i: (0, 0, 0))]
        args += [kv] * 6 + [kv_ctx, kv_ctx, bias]
    else:
        in_specs += cspecs
        args += [kv_ctx, kv_ctx]
    in_specs += [row, row]
    args += [sink_row, g_attn.reshape(1, ATT_WIDTH)]
    return pl.pallas_call(
        functools.partial(_attn_kernel, local=local, n_blocks=nb),
        out_shape=jax.ShapeDtypeStruct((B, L, ATT_WIDTH), BF16),
        grid=(B, nb // n_sub),
        in_specs=in_specs,
        out_specs=qspec,
        scratch_shapes=[pltpu.VMEM((n_sub * T, ATT_WIDTH), F32)],
        compiler_params=_cp(("parallel", "arbitrary")),
        name="attention_local" if local else "attention_context",
    )(*args)


def _halo_specs(ts, L, width, col_block):
    nb8 = L // HALO
    per = ts // HALO
    prev = pl.BlockSpec((None, HALO, width),
                        lambda b, i: (b, jnp.maximum(i * per - 1, 0), col_block))
    cur = pl.BlockSpec((None, ts, width), lambda b, i: (b, i, col_block))
    nxt = pl.BlockSpec((None, HALO, width),
                       lambda b, i: (b, jnp.minimum((i + 1) * per, nb8 - 1), col_block))
    return [prev, cur, nxt]


def _with_halo(prev_ref, cur_ref, next_ref):
    i = pl.program_id(1)
    last = pl.num_programs(1) - 1
    prev = jnp.where(i > 0, prev_ref[...], 0.0)
    nxt = jnp.where(i < last, next_ref[...], 0.0)
    return jnp.concatenate([prev, cur_ref[...], nxt], axis=0)


def _pool_kernel(prev_ref, cur_ref, next_ref, w_ref, scale_ref, g_ref, o_ref, *, seq_len):
    ts = cur_ref.shape[0]
    ext = _with_halo(prev_ref, cur_ref, next_ref)
    pos = pl.program_id(1) * ts + lax.broadcasted_iota(jnp.int32, (ts, 1), 0)
    outs = []
    for gidx, win in enumerate(POOL_WINDOWS):
        cols = slice(gidx * POOL_GROUP, (gidx + 1) * POOL_GROUP)
        run = ext[:, cols]
        step = 1
        while step < win:
            run = run[:run.shape[0] - step] + run[step:]
            step *= 2
        lo = HALO - win // 2
        total = run[lo:lo + ts]
        cnt = (jnp.minimum(pos + (win - win // 2), seq_len) - jnp.maximum(pos - win // 2, 0))
        mean = total * (1.0 / cnt.astype(F32))
        outs.append(_dot3(mean - cur_ref[:, cols], w_ref[gidx]))
    y = jnp.concatenate(outs, axis=-1) * scale_ref[...]
    y = y * lax.rsqrt(jnp.mean(y * y, axis=-1, keepdims=True) + EPS) * g_ref[...]
    o_ref[...] = y.astype(o_ref.dtype)


def _pool_mixer(proj, pool_w, pool_scale, g_pool):
    B, L, _ = proj.shape
    ts = _tile(L, 512)
    col_block = 0
    row = pl.BlockSpec((1, POOL_WIDTH), lambda b, i: (0, 0))
    return pl.pallas_call(
        functools.partial(_pool_kernel, seq_len=L),
        out_shape=jax.ShapeDtypeStruct((B, L, POOL_WIDTH), BF16),
        grid=(B, L // ts),
        in_specs=_halo_specs(ts, L, POOL_WIDTH, col_block) + [
            pl.BlockSpec(pool_w.shape, lambda b, i: (0, 0, 0)), row, row],
        out_specs=pl.BlockSpec((None, ts, POOL_WIDTH), lambda b, i: (b, i, 0)),
        compiler_params=_cp(("parallel", "arbitrary")),
        name="pool_mixer",
    )(proj, proj, proj, pool_w, pool_scale.reshape(1, POOL_WIDTH), g_pool.reshape(1, POOL_WIDTH))


def _short_conv_kernel(*refs):
    halo_refs = refs[:9]
    w_ref, b_ref = refs[9:11]
    out_refs = refs[11:]
    ts = out_refs[0].shape[0]
    for part in range(HY_ORDER + 1):
        ext = _with_halo(*halo_refs[3 * part:3 * part + 3])
        cols = slice(part * HY_WIDTH, (part + 1) * HY_WIDTH)
        acc = b_ref[:, cols] + ext[HALO - 1:HALO - 1 + ts] * w_ref[0:1, cols]
        acc = acc + ext[HALO:HALO + ts] * w_ref[1:2, cols]
        acc = acc + ext[HALO + 1:HALO + 1 + ts] * w_ref[2:3, cols]
        out_refs[part][...] = acc


def _short_conv(proj, conv_w, conv_b):
    B, L, _ = proj.shape
    ts = _tile(L, 512)
    first = POOL_WIDTH // HY_WIDTH
    assert first * HY_WIDTH == POOL_WIDTH
    specs = []
    for part in range(HY_ORDER + 1):
        specs += _halo_specs(ts, L, HY_WIDTH, first + part)
    n_col = (HY_ORDER + 1) * HY_WIDTH
    out = pl.BlockSpec((None, ts, HY_WIDTH), lambda b, i: (b, i, 0))
    return pl.pallas_call(
        _short_conv_kernel,
        out_shape=tuple(jax.ShapeDtypeStruct((B, L, HY_WIDTH), F32) for _ in range(HY_ORDER + 1)),
        grid=(B, L // ts),
        in_specs=specs + [pl.BlockSpec((HY_SHORT, n_col), lambda b, i: (0, 0)),
                          pl.BlockSpec((1, n_col), lambda b, i: (0, 0))],
        out_specs=(out,) * (HY_ORDER + 1),
        compiler_params=_cp(("parallel", "arbitrary")),
        name="hyena_short_conv",
    )(*([proj] * 9), conv_w, conv_b.reshape(1, n_col))


def _filter_tables(L):
    m = np.arange(L, dtype=np.float32)
    t = (m / np.float32(max(L - 1, 1))).astype(np.float32)
    w = (np.float32(2.0 * math.pi) * m / np.float32(L)).astype(np.float32)
    f = np.linspace(1e-4, HY_BANDS - 1, HY_BANDS, dtype=np.float32)
    ang = (w[:, None] * f).astype(np.float64)
    z = np.concatenate([t[:, None].astype(np.float64), np.cos(ang), -np.sin(ang)], axis=-1)
    z = np.pad(z, ((0, 0), (0, LANES - HY_EMB)))
    backward = z[(L - np.arange(L)) % L]
    max_decay = math.log(HY_TARGET) / HY_SHORT_DECAY_PCT
    min_decay = math.log(HY_TARGET) / HY_LONG_DECAY_PCT
    deltas = np.abs(np.linspace(min_decay, max_decay, HY_WIDTH, dtype=np.float32))
    return jnp.asarray(np.concatenate([z, backward]), F32), jnp.asarray(deltas[None], F32)


def _filter_kernel(z_ref, rate_ref, w1_ref, b1_ref, w2_ref, b2_ref, w3a_ref, w3b_ref,
                   b3a_ref, b3b_ref, fr_ref, o_ref, *, seq_len, period):
    tl = z_ref.shape[0]
    n0 = pl.program_id(0) * tl
    active = (n0 < seq_len) | (n0 >= period - seq_len)

    @pl.when(active)
    def _():
        n = n0 + lax.broadcasted_iota(jnp.int32, (tl, 1), 0)
        lag = jnp.where(n < seq_len, n, period - n)
        t = lag.astype(F32) / float(max(seq_len - 1, 1))
        decay = jnp.exp(-t * rate_ref[...])
        fr = fr_ref[...]
        a = jnp.sin(fr * (_dot3(z_ref[...], w1_ref[...]) + b1_ref[...]))
        a = jnp.sin(fr * (_dot3(a, w2_ref[...]) + b2_ref[...]))
        taps = [(_dot3(a, w3_ref[...]) + b3_ref[...]) * decay
                for w3_ref, b3_ref in ((w3a_ref, b3a_ref), (w3b_ref, b3b_ref))]
        o_ref[...] = jnp.where(n == period - seq_len, 0.0, jnp.concatenate(taps, axis=1))

    @pl.when(jnp.logical_not(active))
    def _():
        o_ref[...] = jnp.zeros(o_ref.shape, o_ref.dtype)


def _hyena_filter_taps(L, period, w1, b1, w2, b2, w3, b3, freq):
    z, rates = _filter_tables(L)
    hid = w1.shape[1]
    tl = _tile(L, 512)
    nf = L // tl
    nt = period // tl
    assert HY_ORDER == 2 and period % tl == 0 and period >= 2 * L
    w1p = jnp.pad(w1, ((0, LANES - HY_EMB), (0, 0)))
    b3 = b3.reshape(1, -1)
    full = lambda a: pl.BlockSpec(a.shape, lambda i: (0,) * a.ndim)
    is_bwd = lambda i: jnp.where(i >= nt - nf, 1, 0)
    ztile = lambda i: (jnp.where(i < nf, i, nf + jnp.clip(i - (nt - nf), 0, nf - 1)), 0)
    w3spec = lambda o: pl.BlockSpec((hid, HY_WIDTH), lambda i: (0, 2 * o + is_bwd(i)))
    b3spec = lambda o: pl.BlockSpec((1, HY_WIDTH), lambda i: (0, 2 * o + is_bwd(i)))
    small = [w1p, b1.reshape(1, hid), w2, b2.reshape(1, hid)]
    return pl.pallas_call(
        functools.partial(_filter_kernel, seq_len=L, period=period),
        out_shape=jax.ShapeDtypeStruct((period, HY_ORDER * HY_WIDTH), F32),
        grid=(nt,),
        in_specs=[pl.BlockSpec((tl, LANES), ztile), full(rates)] + [full(a) for a in small]
                 + [w3spec(0), w3spec(1), b3spec(0), b3spec(1), pl.BlockSpec((1, hid), lambda i: (0, 0))],
        out_specs=pl.BlockSpec((tl, HY_ORDER * HY_WIDTH), lambda i: (i, 0)),
        compiler_params=_cp(("arbitrary",)),
        name="hyena_filter_taps",
    )(z, rates, *small, w3, w3, b3, b3, freq.reshape(1, hid))


def _fft_matrices(n1, a_in):
    n = n1 * LANES
    k1 = np.arange(n1)
    a = np.arange(a_in)
    th1 = 2.0 * np.pi * np.outer(k1, a) / n1
    c1, s1 = np.cos(th1), np.sin(th1)
    m1_complex = np.block([[c1, s1], [-s1, c1]])
    th_full = 2.0 * np.pi * np.outer(k1, k1) / n1
    m1_real = np.concatenate([np.cos(th_full), -np.sin(th_full)], axis=0)
    m3 = np.block([[c1.T, -s1.T], [s1.T, c1.T]])
    b = np.arange(LANES)
    k2 = np.arange(LANES)
    k = k1[:, None, None] + n1 * k2[None, :, None]
    th2 = 2.0 * np.pi * (k * b[None, None, :] % n) / n
    c2, s2 = np.cos(th2), np.sin(th2)
    fwd = np.concatenate([np.concatenate([c2, s2], axis=2),
                          np.concatenate([-s2, c2], axis=2)], axis=1)
    c2t, s2t = np.swapaxes(c2, 1, 2), np.swapaxes(s2, 1, 2)
    inv = np.concatenate([np.concatenate([c2t, -s2t], axis=2),
                          np.concatenate([s2t, c2t], axis=2)], axis=1)
    as_bf16 = lambda m: jnp.asarray(m, F32).astype(BF16)
    return as_bf16(m1_complex), as_bf16(m1_real), as_bf16(m3), as_bf16(fwd), as_bf16(inv)


def _fft_stage1_kernel(x_ref, m_ref, o_ref, *, complex_in):
    n1 = o_ref.shape[0]
    for s in range(o_ref.shape[1]):
        if complex_in:
            x = jnp.concatenate([x_ref[0, :, s, :], x_ref[1, :, s, :]], axis=0)
        else:
            x = x_ref[:, s, :]
        y = _dot(m_ref[...], x.astype(BF16))
        o_ref[:, s, :] = _pack_bf16(y[:n1], y[n1:])


def _fft_stage1(x, m1, n1, complex_in):
    P, A, Wt = x.shape[0], x.shape[-3], x.shape[-1]
    tw = _tile(Wt, 512, LANES)
    if complex_in:
        xspec = pl.BlockSpec((None, 2, A, FFT_ROWS, tw), lambda p, i, j: (p, 0, 0, i, j))
    else:
        xspec = pl.BlockSpec((None, A, FFT_ROWS, tw), lambda p, i, j: (p, 0, i, j))
    return pl.pallas_call(
        functools.partial(_fft_stage1_kernel, complex_in=complex_in),
        out_shape=jax.ShapeDtypeStruct((P, n1, LANES, Wt), jnp.uint32),
        grid=(P, LANES // FFT_ROWS, Wt // tw),
        in_specs=[xspec, pl.BlockSpec(m1.shape, lambda p, i, j: (0, 0))],
        out_specs=pl.BlockSpec((None, n1, FFT_ROWS, tw), lambda p, i, j: (p, 0, i, j)),
        compiler_params=_cp(("parallel", "arbitrary", "arbitrary")),
        name="fft_stage1",
    )(x, m1)


def _filter_spectrum_kernel(y_ref, g_ref, o0_ref, o1_ref, *, scale):
    kt = y_ref.shape[0]
    for t in range(kt):
        y = jnp.concatenate(_unpack_bf16_pairs(y_ref[t]), axis=0)
        f = _dot(g_ref[t], y)
        for o, o_ref in enumerate((o0_ref, o1_ref)):
            cols = slice(o * HY_WIDTH, (o + 1) * HY_WIDTH)
            o_ref[0, t] = (f[:LANES, cols] * scale).astype(o_ref.dtype)
            o_ref[1, t] = (f[LANES:, cols] * scale).astype(o_ref.dtype)


def _filter_spectrum(taps, mats, n1):
    n_col = taps.shape[1]
    _, m1_real, _, g_fwd, _ = mats
    y = _fft_stage1(taps.reshape(1, n1, LANES, n_col), m1_real, n1, complex_in=False)
    kt = _tile(n1, 8, 1)
    out = jax.ShapeDtypeStruct((2, n1, LANES, HY_WIDTH), BF16)
    ospec = pl.BlockSpec((2, kt, LANES, HY_WIDTH), lambda i: (0, i, 0, 0))
    return pl.pallas_call(
        functools.partial(_filter_spectrum_kernel, scale=1.0 / (n1 * LANES)),
        out_shape=(out, out),
        grid=(n1 // kt,),
        in_specs=[pl.BlockSpec((None, kt, LANES, n_col), lambda i: (0, i, 0, 0)),
                  pl.BlockSpec((kt, 2 * LANES, 2 * LANES), lambda i: (i, 0, 0))],
        out_specs=(ospec, ospec),
        compiler_params=_cp(("arbitrary",)),
        name="hyena_filter_spectrum",
    )(y, g_fwd)


def _fft_mid_kernel(y_ref, kf_ref, g_ref, gi_ref, o_ref):
    kt = y_ref.shape[0]
    for t in range(kt):
        y = jnp.concatenate(_unpack_bf16_pairs(y_ref[t]), axis=0)
        f = _dot(g_ref[t], y)
        fr, fi = f[:LANES], f[LANES:]
        kr, ki = kf_ref[0, t].astype(F32), kf_ref[1, t].astype(F32)
        p = jnp.concatenate([fr * kr - fi * ki, fr * ki + fi * kr], axis=0).astype(BF16)
        u = _dot(gi_ref[t], p)
        o_ref[t] = _pack_bf16(u[:LANES], u[LANES:])


def _fft_mid(y, kf, g_fwd, g_inv):
    P, n1, _, W = y.shape
    kt = _tile(n1, 8, 1)
    yspec = pl.BlockSpec((None, kt, LANES, W), lambda i, p: (p, i, 0, 0))
    gspec = pl.BlockSpec((kt, 2 * LANES, 2 * LANES), lambda i, p: (i, 0, 0))
    return pl.pallas_call(
        _fft_mid_kernel,
        out_shape=jax.ShapeDtypeStruct(y.shape, jnp.uint32),
        grid=(n1 // kt, P),
        in_specs=[yspec, pl.BlockSpec((2, kt, LANES, W), lambda i, p: (0, i, 0, 0)), gspec, gspec],
        out_specs=yspec,
        compiler_params=_cp(("arbitrary", "arbitrary")),
        name="fft_mid",
    )(y, kf, g_fwd, g_inv)


def _fft_stage3_kernel(u_ref, m_ref, z_ref, x_ref, bias_ref, g_ref, o_ref, ur_ref, ui_ref,
                       *, normalise):
    a_out = z_ref.shape[1]
    bias = bias_ref[...]
    packed = u_ref[...]
    ur_ref[...] = lax.bitcast_convert_type(packed << 16, F32)
    ui_ref[...] = lax.bitcast_convert_type(packed & jnp.uint32(0xFFFF0000), F32)
    for s in range(o_ref.shape[2]):
        u = jnp.concatenate([ur_ref[:, s, :], ui_ref[:, s, :]], axis=0).astype(BF16)
        conv = _dot(m_ref[...], u)
        for part in range(2):
            y = x_ref[part, :, s, :] * (
                conv[part * a_out:(part + 1) * a_out] + z_ref[part, :, s, :] * bias)
            if normalise:
                y = y * lax.rsqrt(jnp.mean(y * y, axis=-1, keepdims=True) + EPS) * g_ref[...]
            o_ref[part, :, s, :] = y


def _fft_stage3(u, m3, z, gate, bias_row, norm_g=None):
    P, n1, _, W = u.shape
    A = z.shape[2]
    pair = pl.BlockSpec((None, 2, A, FFT_ROWS, W), lambda p, i: (p, 0, 0, i, 0))
    row = pl.BlockSpec((1, W), lambda p, i: (0, 0))
    normalise = norm_g is not None
    return pl.pallas_call(
        functools.partial(_fft_stage3_kernel, normalise=normalise),
        out_shape=jax.ShapeDtypeStruct(z.shape, F32),
        grid=(P, LANES // FFT_ROWS),
        in_specs=[pl.BlockSpec((None, n1, FFT_ROWS, W), lambda p, i: (p, 0, i, 0)),
                  pl.BlockSpec(m3.shape, lambda p, i: (0, 0)),
                  pair, pair, row, row],
        out_specs=pair,
        scratch_shapes=[pltpu.VMEM((n1, FFT_ROWS, W), F32), pltpu.VMEM((n1, FFT_ROWS, W), F32)],
        compiler_params=_cp(("parallel", "arbitrary")),
        name="fft_stage3",
    )(u, m3, z, gate, bias_row, norm_g.reshape(1, W) if normalise else bias_row)


def _hyena_mixer(v, gates, kfs, d_bias, norm_g, mats, n1):
    B, L, W = v.shape
    pair_shape = (B // 2, 2, L // LANES, LANES, W)
    m1_complex, _, m3, g_fwd, g_inv = mats
    z = v.reshape(pair_shape)
    for o in range(HY_ORDER):
        y = _fft_stage1(z, m1_complex, n1, complex_in=True)
        u = _fft_mid(y, kfs[o], g_fwd, g_inv)
        z = _fft_stage3(u, m3, z, gates[o].reshape(pair_shape), d_bias[o].reshape(1, W),
                        norm_g if o == HY_ORDER - 1 else None)
    return z.reshape(B, L, W)


def kernel(x, c, ctx, c_ctx, w_mod, b_mod, norm1_g, w_in, attn_sink, pool_w, pool_scale,
           hy_conv_w, hy_conv_b, hy_f_w1, hy_f_b1, hy_f_w2, hy_f_b2, hy_f_w3, hy_f_b3,
           hy_f_freq, hy_bias, g_attn, g_pool, g_hyena, w_out, norm2_g,
           ff_w1, ff_w3, ff_w2, router_w, moe_w1, moe_w3, moe_w2, final_g):
    B, S, D = x.shape
    C = ctx.shape[1]
    depth = w_mod.shape[0]
    assert B % 2 == 0 and S % BLOCK == 0
    fft_mats = {}

    c_rows = jnp.concatenate([c, c_ctx[None], jnp.zeros((SUBLANES - B - 1, D), F32)], axis=0)
    xc = ctx

    def hyena_branch(l, proj, seq_len):
        rows = -(-seq_len // FFT_MIN_ROWS) * FFT_MIN_ROWS
        n1 = 2 * rows // LANES
        if rows not in fft_mats:
            fft_mats[rows] = _fft_matrices(n1, rows // LANES)
        mats = fft_mats[rows]
        taps = _hyena_filter_taps(seq_len, 2 * rows, hy_f_w1[l], hy_f_b1[l], hy_f_w2[l], hy_f_b2[l],
                                  hy_f_w3[l], hy_f_b3[l], hy_f_freq[l])
        v, x1, x2 = _short_conv(proj, hy_conv_w[l], hy_conv_b[l])
        if seq_len < rows:
            pad = lambda t: jnp.pad(t, ((0, 0), (0, rows - seq_len), (0, 0)))
            v, x1, x2 = pad(v), pad(x1), pad(x2)
        kfs = _filter_spectrum(taps, mats, n1)
        z = _hyena_mixer(v, (x1, x2), kfs, hy_bias[l], g_hyena[l], mats, n1)
        return z[:, :seq_len]

    def mix(l, proj, y_att, res, gate, w_out_l, seq_len, norm=None):
        y_pool = _pool_mixer(proj, pool_w[l], pool_scale[l], g_pool[l])
        y_hy = hyena_branch(l, proj, seq_len)
        return _mix_out([y_att, y_pool, y_hy], w_out_l, res, gate, norm)

    for l in range(depth):
        last = l == depth - 1
        mod = _adaln(c_rows, w_mod, b_mod, l)
        sh1, sc1, g1, sh2, sc2, g2 = [mod[:B, j * D:(j + 1) * D] for j in range(6)]
        csh1, csc1, cg1, csh2, csc2, cg2 = [mod[B:B + 1, j * D:(j + 1) * D] for j in range(6)]
        w_in_l = w_in[l].astype(BF16)
        w_out_l = w_out[l].astype(BF16)

        if last:
            hc = _norm_mod(xc, norm1_g[l], csh1, csc1)
            kv_w = w_in_l[:, ATT_WIDTH:ATT_WIDTH + 2 * KV_WIDTH]
            kv_c = _matmul(hc, kv_w, out_dtype=BF16)
        else:
            q_c, kv_c, proj_c = _in_proj(xc, norm1_g[l], csh1, csc1, w_in_l, rope=False)
            y_att_c = _attention(q_c, None, kv_c, attn_sink[l], g_attn[l])
            xc_new, hc2 = mix(l, proj_c, y_att_c, xc, cg1, w_out_l, C, (norm2_g[l], csh2, csc2))
            i = l // 2
            if l % 2 == 0:
                hid_c = _swiglu_hidden(hc2, ff_w1[i].astype(BF16), ff_w3[i].astype(BF16))
                xc_new = _matmul_gated_residual([hid_c], ff_w2[i].astype(BF16), xc_new, cg2)
            else:
                raise NotImplementedError("context tokens through an expert layer")

        q, kv, proj = _in_proj(x, norm1_g[l], sh1, sc1, w_in_l, rope=True)
        y_att = _attention(q, kv, kv_c, attn_sink[l], g_attn[l])

        i = l // 2
        if l % 2 == 0:
            x, h2 = mix(l, proj, y_att, x, g1, w_out_l, S, (norm2_g[l], sh2, sc2))
            hid = _swiglu_hidden(h2, ff_w1[i].astype(BF16), ff_w3[i].astype(BF16))
            x = _matmul_gated_residual([hid], ff_w2[i].astype(BF16), x, g2)
            if last:
                x = _rmsnorm(x, final_g, F32)
        else:
            assert last
            x = mix(l, proj, y_att, x, g1, w_out_l, S)
            h2, gi, gw, counts = _norm_mod_router(x, norm2_g[l], sh2, sc2, router_w[i])
            x = _moe_layer(x, h2, gi, gw, counts, g2, final_g, moe_w1[i], moe_w3[i], moe_w2[i])
        if not last:
            xc = xc_new
    return x
```

```python
import functools
import math

import numpy as np
import jax
import jax.numpy as jnp
from jax import lax
from jax.experimental import pallas as pl
from jax.experimental.pallas import tpu as pltpu

F32 = jnp.float32
BF16 = jnp.bfloat16

EPS = 1e-6
NEG = -1e30
GRID_W = 64
ATT_HEADS = 8
ATT_KV_HEADS = 2
ATT_GROUP = ATT_HEADS // ATT_KV_HEADS
HEAD_DIM = 128
ATT_WIDTH = ATT_HEADS * HEAD_DIM
KV_WIDTH = ATT_KV_HEADS * HEAD_DIM
WINDOW = 128
BLOCK = 128
ROPE_BASE = 10000.0
POOL_WINDOWS = (2, 4, 8, 16)
POOL_GROUP = 128
POOL_WIDTH = POOL_GROUP * len(POOL_WINDOWS)
HY_WIDTH = 512
HY_ORDER = 2
HY_SHORT = 3
HY_EMB = 33
HY_BANDS = (HY_EMB - 1) // 2
HY_SHORT_DECAY_PCT = 0.3
HY_LONG_DECAY_PCT = 1.5
HY_TARGET = 1e-2
N_EXPERTS = 8
TOP_K = 2

LANES = 128
SUBLANES = 8
HALO = SUBLANES
FFT_MIN_ROWS = 2048
FFT_ROWS = SUBLANES
MOE_OUT_K_CHUNKS = 4
VMEM_LIMIT = 56 * 1024 * 1024


def _cp(sem):
    return pltpu.CompilerParams(dimension_semantics=sem, vmem_limit_bytes=VMEM_LIMIT)


def _tile(n, pref, mult=SUBLANES):
    if n <= pref:
        return n
    t = (pref // mult) * mult
    while t >= mult:
        if n % t == 0:
            return t
        t -= mult
    return n


def _split_bf16(x):
    hi = x.astype(BF16)
    lo = (x - hi.astype(F32)).astype(BF16)
    return hi, lo


def _dot(a, b):
    return jnp.dot(a, b, preferred_element_type=F32)


def _pack_bf16(lo, hi):
    lo_bits = lax.bitcast_convert_type(lo.astype(BF16).astype(F32), jnp.uint32)
    hi_bits = lax.bitcast_convert_type(hi.astype(BF16).astype(F32), jnp.uint32)
    return (lo_bits >> 16) | (hi_bits & jnp.uint32(0xFFFF0000))


def _pack_bf16_pairs(x):
    n = x.shape[-1] // 2
    return _pack_bf16(x[:, :n], x[:, n:])


def _unpack_bf16_pairs(u):
    lo = lax.bitcast_convert_type(u << 16, F32).astype(BF16)
    hi = lax.bitcast_convert_type(u & jnp.uint32(0xFFFF0000), F32).astype(BF16)
    return lo, hi


def _dot3(a, b):
    ah, al = _split_bf16(a)
    bh, bl = _split_bf16(b)
    return _dot(ah, bh) + (_dot(ah, bl) + _dot(al, bh))


def _adaln_kernel(c_ref, w_ref, b_ref, o_ref):
    c = c_ref[...]
    a = c * (1.0 / (1.0 + jnp.exp(-c)))
    o_ref[...] = _dot3(a, w_ref[...]) + b_ref[...]


def _adaln(c_rows, w_all, b_all, layer):
    R, D = c_rows.shape
    N = w_all.shape[2]
    tn = _tile(N, 1536, LANES)
    return pl.pallas_call(
        _adaln_kernel,
        out_shape=jax.ShapeDtypeStruct((R, N), F32),
        grid=(N // tn,),
        in_specs=[pl.BlockSpec((R, D), lambda j: (0, 0)),
                  pl.BlockSpec((None, D, tn), lambda j: (layer, 0, j)),
                  pl.BlockSpec((None, 1, tn), lambda j: (layer, 0, j))],
        out_specs=pl.BlockSpec((R, tn), lambda j: (0, j)),
        compiler_params=_cp(("arbitrary",)),
        name="adaln",
    )(c_rows, w_all, b_all[:, None, :])


def _norm_mod_kernel(x_ref, g_ref, sh_ref, sc_ref, o_ref):
    x = x_ref[...]
    y = x * lax.rsqrt(jnp.mean(x * x, axis=-1, keepdims=True) + EPS) * g_ref[...]
    o_ref[...] = (y * (1.0 + sc_ref[...]) + sh_ref[...]).astype(o_ref.dtype)


def _bcast_map(arr):
    if arr.shape[0] == 1:
        return lambda b, i: (0, 0, 0)
    return lambda b, i: (b, 0, 0)


def _norm_mod(x, g, shift, scale):
    B, L, D = x.shape
    ts = _tile(L, 512)
    shift = shift[:, None, :]
    scale = scale[:, None, :]
    return pl.pallas_call(
        _norm_mod_kernel,
        out_shape=jax.ShapeDtypeStruct((B, L, D), BF16),
        grid=(B, L // ts),
        in_specs=[pl.BlockSpec((None, ts, D), lambda b, i: (b, i, 0)),
                  pl.BlockSpec((1, D), lambda b, i: (0, 0)),
                  pl.BlockSpec((None, 1, D), _bcast_map(shift)),
                  pl.BlockSpec((None, 1, D), _bcast_map(scale))],
        out_specs=pl.BlockSpec((None, ts, D), lambda b, i: (b, i, 0)),
        compiler_params=_cp(("parallel", "parallel")),
        name="norm_mod",
    )(x, g.reshape(1, D), shift, scale)


def _route(h, rw_ref, tri_ref, run_ref, gi_ref, gw_ref, cnt_ref):
    logits = _dot3(h, rw_ref[...])
    lane = lax.broadcasted_iota(jnp.int32, logits.shape, 1)
    logits = jnp.where(lane < N_EXPERTS, logits, NEG)
    m1 = jnp.max(logits, axis=-1, keepdims=True)
    i1 = jnp.min(jnp.where(logits == m1, lane, LANES), axis=-1, keepdims=True)
    rest = jnp.where(lane == i1, NEG, logits)
    m2 = jnp.max(rest, axis=-1, keepdims=True)
    i2 = jnp.min(jnp.where(rest == m2, lane, LANES), axis=-1, keepdims=True)
    e2 = jnp.exp(m2 - m1)
    w1 = 1.0 / (1.0 + e2)
    w2 = e2 * w1
    pick1 = lane == i1
    pick2 = lane == i2
    both = jnp.where(pick1 | pick2, 1.0, 0.0)
    before = _dot(tri_ref[...], both.astype(BF16)) + run_ref[...]
    r1 = jnp.sum(jnp.where(pick1, before, 0.0), axis=-1, keepdims=True)
    r2 = jnp.sum(jnp.where(pick2, before, 0.0), axis=-1, keepdims=True)
    run_ref[...] = run_ref[...] + jnp.sum(both, axis=0, keepdims=True)
    cnt_ref[...] = run_ref[...]
    ids = jnp.where(lane == 0, i1.astype(F32), jnp.where(lane == 1, i2.astype(F32),
                    jnp.where(lane == 2, r1, jnp.where(lane == 3, r2, 0.0))))
    gi_ref[...] = jnp.transpose(ids)[:SUBLANES, :]
    gw_ref[...] = jnp.where(lane == 0, w1, jnp.where(lane == 1, w2, 0.0))


def _norm_mod_router_kernel(x_ref, g_ref, sh_ref, sc_ref, rw_ref, tri_ref,
                            o_ref, gi_ref, gw_ref, cnt_ref, run_ref):
    @pl.when((pl.program_id(0) == 0) & (pl.program_id(1) == 0))
    def _():
        run_ref[...] = jnp.zeros(run_ref.shape, run_ref.dtype)

    x = x_ref[...]
    y = x * lax.rsqrt(jnp.mean(x * x, axis=-1, keepdims=True) + EPS) * g_ref[...]
    h = y * (1.0 + sc_ref[...]) + sh_ref[...]
    o_ref[...] = _pack_bf16_pairs(h)
    _route(h, rw_ref, tri_ref, run_ref, gi_ref, gw_ref, cnt_ref)


def _norm_mod_router(x, g, shift, scale, router_w):
    B, L, D = x.shape
    ts = _tile(L, 512)
    shift = shift[:, None, :]
    scale = scale[:, None, :]
    rw = jnp.pad(router_w, ((0, 0), (0, LANES - router_w.shape[1])))
    tri = jnp.asarray(np.tril(np.ones((ts, ts), np.float32), -1), BF16)
    row = pl.BlockSpec((None, ts, D), lambda b, i: (b, i, 0))
    small = pl.BlockSpec((None, ts, LANES), lambda b, i: (b, i, 0))
    return pl.pallas_call(
        _norm_mod_router_kernel,
        out_shape=(jax.ShapeDtypeStruct((B, L, D // 2), jnp.uint32),
                   jax.ShapeDtypeStruct((B, SUBLANES, L), F32),
                   jax.ShapeDtypeStruct((B, L, LANES), F32),
                   jax.ShapeDtypeStruct((1, LANES), F32)),
        grid=(B, L // ts),
        in_specs=[row,
                  pl.BlockSpec((1, D), lambda b, i: (0, 0)),
                  pl.BlockSpec((None, 1, D), _bcast_map(shift)),
                  pl.BlockSpec((None, 1, D), _bcast_map(scale)),
                  pl.BlockSpec((D, LANES), lambda b, i: (0, 0)),
                  pl.BlockSpec((ts, ts), lambda b, i: (0, 0))],
        out_specs=(pl.BlockSpec((None, ts, D // 2), lambda b, i: (b, i, 0)),
                   pl.BlockSpec((None, SUBLANES, ts), lambda b, i: (b, 0, i)), small,
                   pl.BlockSpec((1, LANES), lambda b, i: (0, 0))),
        scratch_shapes=[pltpu.VMEM((1, LANES), F32)],
        compiler_params=_cp(("arbitrary", "arbitrary")),
        name="norm_mod_router",
    )(x, g.reshape(1, D), shift, scale, rw, tri)


def _mix_out_kernel(*refs, n_a, emit_h):
    a_refs = refs[:n_a]
    w_refs = refs[n_a:2 * n_a]
    res_ref, gate_ref = refs[2 * n_a:2 * n_a + 2]
    acc = _dot(a_refs[0][...].astype(BF16), w_refs[0][...])
    for a_ref, w_ref in zip(a_refs[1:], w_refs[1:]):
        acc = acc + _dot(a_ref[...].astype(BF16), w_ref[...])
    x = res_ref[...] + gate_ref[...] * acc
    if emit_h:
        g_ref, sh_ref, sc_ref, x_ref, h_ref = refs[2 * n_a + 2:]
        y = x * lax.rsqrt(jnp.mean(x * x, axis=-1, keepdims=True) + EPS) * g_ref[...]
        h_ref[...] = (y * (1.0 + sc_ref[...]) + sh_ref[...]).astype(h_ref.dtype)
    else:
        x_ref, = refs[2 * n_a + 2:]
    x_ref[...] = x


def _mix_out(a_list, w, res, gate, norm=None):
    B, L, D = res.shape
    tm = _tile(L, 512)
    emit_h = norm is not None
    gate = gate[:, None, :]
    in_specs = [pl.BlockSpec((None, tm, a.shape[-1]), lambda b, i: (b, i, 0)) for a in a_list]
    off = 0
    for a in a_list:
        k = a.shape[-1]
        assert off % k == 0
        in_specs.append(pl.BlockSpec((k, D), functools.partial(lambda b, i, blk: (blk, 0), blk=off // k)))
        off += k
    row = pl.BlockSpec((None, tm, D), lambda b, i: (b, i, 0))
    in_specs += [row, pl.BlockSpec((None, 1, D), _bcast_map(gate))]
    args = [*a_list, *([w] * len(a_list)), res, gate]
    out_shape = [jax.ShapeDtypeStruct((B, L, D), F32)]
    out_specs = [row]
    if emit_h:
        g, shift, scale = norm
        shift, scale = shift[:, None, :], scale[:, None, :]
        in_specs += [pl.BlockSpec((1, D), lambda b, i: (0, 0)),
                     pl.BlockSpec((None, 1, D), _bcast_map(shift)),
                     pl.BlockSpec((None, 1, D), _bcast_map(scale))]
        args += [g.reshape(1, D), shift, scale]
        out_shape.append(jax.ShapeDtypeStruct((B, L, D), BF16))
        out_specs.append(row)
    out = pl.pallas_call(
        functools.partial(_mix_out_kernel, n_a=len(a_list), emit_h=emit_h),
        out_shape=tuple(out_shape),
        grid=(B, L // tm),
        in_specs=in_specs,
        out_specs=tuple(out_specs),
        compiler_params=_cp(("parallel", "parallel")),
        name="mix_out",
    )(*args)
    return out if emit_h else out[0]


def _rmsnorm_kernel(x_ref, g_ref, o_ref):
    x = x_ref[...]
    y = x * lax.rsqrt(jnp.mean(x * x, axis=-1, keepdims=True) + EPS) * g_ref[...]
    o_ref[...] = y.astype(o_ref.dtype)


def _rmsnorm(x, g, out_dtype):
    B, L, D = x.shape
    ts = _tile(L, 512)
    return pl.pallas_call(
        _rmsnorm_kernel,
        out_shape=jax.ShapeDtypeStruct((B, L, D), out_dtype),
        grid=(B, L // ts),
        in_specs=[pl.BlockSpec((None, ts, D), lambda b, i: (b, i, 0)),
                  pl.BlockSpec((1, D), lambda b, i: (0, 0))],
        out_specs=pl.BlockSpec((None, ts, D), lambda b, i: (b, i, 0)),
        compiler_params=_cp(("parallel", "parallel")),
        name="rmsnorm",
    )(x, g.reshape(1, D))


def _mm_kernel(a_ref, w_ref, o_ref):
    o_ref[...] = _dot(a_ref[...], w_ref[...]).astype(o_ref.dtype)


def _matmul(a, w, out_dtype=F32, tm=1024, tn=512):
    B, L, K = a.shape
    N = w.shape[1]
    tm = _tile(L, tm)
    tn = _tile(N, tn, LANES)
    return pl.pallas_call(
        _mm_kernel,
        out_shape=jax.ShapeDtypeStruct((B, L, N), out_dtype),
        grid=(B, L // tm, N // tn),
        in_specs=[pl.BlockSpec((None, tm, K), lambda b, i, j: (b, i, 0)),
                  pl.BlockSpec((K, tn), lambda b, i, j: (0, j))],
        out_specs=pl.BlockSpec((None, tm, tn), lambda b, i, j: (b, i, j)),
        compiler_params=_cp(("parallel", "parallel", "arbitrary")),
        name="matmul",
    )(a, w)


def _mm_res_kernel(*refs, n_a):
    a_refs = refs[:n_a]
    w_refs = refs[n_a:2 * n_a]
    res_ref, gate_ref, o_ref = refs[2 * n_a:]
    acc = _dot(a_refs[0][...].astype(BF16), w_refs[0][...])
    for a_ref, w_ref in zip(a_refs[1:], w_refs[1:]):
        acc = acc + _dot(a_ref[...].astype(BF16), w_ref[...])
    o_ref[...] = res_ref[...] + gate_ref[...] * acc


def _matmul_gated_residual(a_list, w, res, gate, tm=1024, tn=512):
    B, L, N = res.shape
    tm = _tile(L, tm)
    tn = _tile(N, tn, LANES)
    gate = gate[:, None, :]
    widths = [a.shape[-1] for a in a_list]
    unit = math.gcd(*widths) if len(widths) > 1 else widths[0]
    in_specs = [pl.BlockSpec((None, tm, k), lambda b, i, j: (b, i, 0)) for k in widths]
    off = 0
    for k in widths:
        assert off % k == 0 and k % unit == 0
        in_specs.append(pl.BlockSpec((k, tn), functools.partial(
            lambda b, i, j, blk: (blk, j), blk=off // k)))
        off += k
    gmap = (lambda b, i, j: (0, 0, j)) if gate.shape[0] == 1 else (lambda b, i, j: (b, 0, j))
    in_specs += [pl.BlockSpec((None, tm, tn), lambda b, i, j: (b, i, j)),
                 pl.BlockSpec((None, 1, tn), gmap)]
    return pl.pallas_call(
        functools.partial(_mm_res_kernel, n_a=len(a_list)),
        out_shape=jax.ShapeDtypeStruct((B, L, N), F32),
        grid=(B, L // tm, N // tn),
        in_specs=in_specs,
        out_specs=pl.BlockSpec((None, tm, tn), lambda b, i, j: (b, i, j)),
        compiler_params=_cp(("parallel", "parallel", "arbitrary")),
        name="matmul_gated_residual",
    )(*a_list, *([w] * len(a_list)), res, gate)


def _silu(x):
    return x * (1.0 / (1.0 + jnp.exp(-x)))


def _swiglu_kernel(a_ref, w1_ref, w3_ref, o_ref):
    a = a_ref[...]
    o_ref[...] = (_silu(_dot(a, w1_ref[...])) * _dot(a, w3_ref[...])).astype(o_ref.dtype)


def _swiglu_hidden(a, w1, w3, tm=1024, tf=512):
    B, L, D = a.shape
    F = w1.shape[1]
    tm = _tile(L, tm)
    tf = _tile(F, tf, LANES)
    wspec = pl.BlockSpec((D, tf), lambda b, i, j: (0, j))
    return pl.pallas_call(
        _swiglu_kernel,
        out_shape=jax.ShapeDtypeStruct((B, L, F), BF16),
        grid=(B, L // tm, F // tf),
        in_specs=[pl.BlockSpec((None, tm, D), lambda b, i, j: (b, i, 0)), wspec, wspec],
        out_specs=pl.BlockSpec((None, tm, tf), lambda b, i, j: (b, i, j)),
        compiler_params=_cp(("parallel", "parallel", "arbitrary")),
        name="swiglu_hidden",
    )(a, w1, w3)


def _for_valid_rows(n_valid, tm, compute, o_ref):
    half_rows = tm // 2

    @pl.when(n_valid > half_rows)
    def _():
        compute(slice(0, tm))

    @pl.when((n_valid > 0) & (n_valid <= half_rows))
    def _():
        compute(slice(0, half_rows))
        o_ref[half_rows:, :] = jnp.zeros((tm - half_rows, o_ref.shape[1]), o_ref.dtype)

    @pl.when(n_valid == 0)
    def _():
        o_ref[...] = jnp.zeros(o_ref.shape, o_ref.dtype)


def _moe_hidden_kernel(te_ref, rows_ref, a_ref, w1_ref, w3_ref, o_ref, w1b_ref, w3b_ref):
    i = pl.program_id(1)
    n_valid = rows_ref[i]
    half = a_ref.shape[1]
    new_weights = (i == 0) | (te_ref[i] != te_ref[jnp.maximum(i - 1, 0)])

    @pl.when((n_valid > 0) & new_weights)
    def _():
        w1b_ref[...] = w1_ref[...].astype(BF16)
        w3b_ref[...] = w3_ref[...].astype(BF16)

    def compute(rows):
        lo, hi = _unpack_bf16_pairs(a_ref[rows, :])
        gate = _dot(lo, w1b_ref[:half, :]) + _dot(hi, w1b_ref[half:, :])
        up = _dot(lo, w3b_ref[:half, :]) + _dot(hi, w3b_ref[half:, :])
        o_ref[rows, :] = (_silu(gate) * up).astype(o_ref.dtype)

    _for_valid_rows(n_valid, a_ref.shape[0], compute, o_ref)


def _moe_out_kernel(te_ref, rows_ref, a_ref, w2_ref, o_ref):
    def compute(rows):
        kc = a_ref.shape[1] // MOE_OUT_K_CHUNKS
        acc = _dot(a_ref[rows, :kc], w2_ref[:kc, :].astype(BF16))
        for c in range(1, MOE_OUT_K_CHUNKS):
            acc = acc + _dot(a_ref[rows, c * kc:(c + 1) * kc],
                             w2_ref[c * kc:(c + 1) * kc, :].astype(BF16))
        o_ref[rows, :] = _pack_bf16_pairs(acc)

    _for_valid_rows(rows_ref[pl.program_id(0)], a_ref.shape[0], compute, o_ref)


def _moe_out_cols(d_model):
    return _tile(d_model, 512, 2 * LANES)


def _moe_experts(a_sorted, tile_expert, tile_rows, w1, w3, w2, tm, tf=512):
    R = a_sorted.shape[0]
    E, D, F = w1.shape
    tf = _tile(F, tf, LANES)
    tn = _moe_out_cols(D)
    nt = R // tm
    w13 = pl.BlockSpec((None, D, tf), lambda j, i, te, n: (te[i], 0, j))
    hidden = pl.pallas_call(
        _moe_hidden_kernel,
        out_shape=jax.ShapeDtypeStruct((R, F), BF16),
        grid_spec=pltpu.PrefetchScalarGridSpec(
            num_scalar_prefetch=2,
            grid=(F // tf, nt),
            in_specs=[pl.BlockSpec((tm, D // 2), lambda j, i, te, n: (i, 0)), w13, w13],
            out_specs=pl.BlockSpec((tm, tf), lambda j, i, te, n: (i, j)),
            scratch_shapes=[pltpu.VMEM((D, tf), BF16), pltpu.VMEM((D, tf), BF16)]),
        compiler_params=_cp(("arbitrary", "arbitrary")),
        name="moe_hidden",
    )(tile_expert, tile_rows, a_sorted, w1, w3)
    return pl.pallas_call(
        _moe_out_kernel,
        out_shape=jax.ShapeDtypeStruct((R, D // 2), jnp.uint32),
        grid_spec=pltpu.PrefetchScalarGridSpec(
            num_scalar_prefetch=2,
            grid=(nt, D // tn),
            in_specs=[pl.BlockSpec((tm, F), lambda i, j, te, n: (i, 0)),
                      pl.BlockSpec((None, F, tn), lambda i, j, te, n: (te[i], 0, j))],
            out_specs=pl.BlockSpec((tm, tn // 2), lambda i, j, te, n: (i, j))),
        compiler_params=_cp(("arbitrary", "arbitrary")),
        name="moe_out",
    )(tile_expert, tile_rows, hidden, w2)


def _unpack_column_blocks(u, block):
    parts = []
    for c in range(u.shape[1] // block):
        w = u[:, c * block:(c + 1) * block]
        parts.append(lax.bitcast_convert_type(w << 16, F32))
        parts.append(lax.bitcast_convert_type(w & jnp.uint32(0xFFFF0000), F32))
    return jnp.concatenate(parts, axis=1)


def _moe_combine_kernel(x_ref, ya_ref, yb_ref, gw_ref, gate_ref, g_ref, o_ref, *, block):
    gw = gw_ref[...]
    moe = (gw[:, 0:1] * _unpack_column_blocks(ya_ref[...], block)
           + gw[:, 1:2] * _unpack_column_blocks(yb_ref[...], block))
    x = x_ref[...] + gate_ref[...] * moe
    y = x * lax.rsqrt(jnp.mean(x * x, axis=-1, keepdims=True) + EPS) * g_ref[...]
    o_ref[...] = y


def _moe_combine_norm(x, ya, yb, gw, gate, g):
    B, L, D = x.shape
    ts = _tile(L, 512)
    row = pl.BlockSpec((None, ts, D), lambda b, i: (b, i, 0))
    packed = pl.BlockSpec((None, ts, D // 2), lambda b, i: (b, i, 0))
    return pl.pallas_call(
        functools.partial(_moe_combine_kernel, block=_moe_out_cols(D) // 2),
        out_shape=jax.ShapeDtypeStruct((B, L, D), F32),
        grid=(B, L // ts),
        in_specs=[row, packed, packed,
                  pl.BlockSpec((None, ts, LANES), lambda b, i: (b, i, 0)),
                  pl.BlockSpec((None, 1, D), lambda b, i: (b, 0, 0)),
                  pl.BlockSpec((1, D), lambda b, i: (0, 0))],
        out_specs=row,
        compiler_params=_cp(("parallel", "parallel")),
        name="moe_combine_norm",
    )(x, ya, yb, gw, gate[:, None, :], g.reshape(1, D))


def _moe_layer(x, h, gi, gw, counts, gate, final_g, w1, w3, w2):
    B, L, D = x.shape
    E = w1.shape[0]
    n_tok = B * L
    n_pair = n_tok * TOP_K
    tm = _tile(n_pair, 1024)
    nt = n_pair // tm + E
    rows_of = lambda r: jnp.concatenate(
        [gi[:, r + c, :].reshape(n_tok) for c in range(TOP_K)]).astype(jnp.int32)
    e_pair = rows_of(0)
    rank = rows_of(TOP_K)
    counts = counts[0, :E].astype(jnp.int32)
    tiles_per = (counts + tm - 1) // tm
    tile_end = jnp.cumsum(tiles_per)
    tile_start = tile_end - tiles_per
    row_start = jnp.zeros((n_pair,), jnp.int32)
    for e in range(E):
        row_start = jnp.where(e_pair == e, tile_start[e] * tm, row_start)
    pos = row_start + rank
    tile_ids = jnp.arange(nt, dtype=jnp.int32)
    tile_expert = jnp.minimum(
        jnp.sum((tile_ids[:, None] >= tile_end[None, :]).astype(jnp.int32), axis=1), E - 1)
    tile_rows = jnp.clip(counts[tile_expert] - (tile_ids - tile_start[tile_expert]) * tm, 0, tm)
    tile_rows = jnp.where(tile_ids < tile_end[-1], tile_rows, 0).astype(jnp.int32)
    tok_pair = jnp.arange(n_pair, dtype=jnp.int32) % n_tok
    src = (jnp.arange(nt * tm, dtype=jnp.int32) % n_tok).at[pos].set(
        tok_pair, unique_indices=True, mode="promise_in_bounds")
    gather_rows = lambda rows, idx: rows.at[idx].get(mode="promise_in_bounds")
    a_sorted = gather_rows(h.reshape(n_tok, h.shape[-1]), src)
    y_sorted = _moe_experts(a_sorted, tile_expert.astype(jnp.int32), tile_rows, w1, w3, w2, tm)
    ya = gather_rows(y_sorted, pos[:n_tok]).reshape(B, L, D // 2)
    yb = gather_rows(y_sorted, pos[n_tok:]).reshape(B, L, D // 2)
    return _moe_combine_norm(x, ya, yb, gw, gate, final_g)


def _rope_tables(n_tokens):
    pos = np.arange(n_tokens)
    row = (pos // GRID_W).astype(np.float32)
    col = (pos % GRID_W).astype(np.float32)
    n_freq = HEAD_DIM // 4
    inv = (np.float32(ROPE_BASE) ** (-np.arange(n_freq, dtype=np.float32) / np.float32(n_freq)))
    ang_r = (row[:, None] * inv).astype(np.float64)
    ang_c = (col[:, None] * inv).astype(np.float64)
    cos = np.concatenate([np.cos(ang_r), np.cos(ang_r), np.cos(ang_c), np.cos(ang_c)], axis=1)
    sin = np.concatenate([-np.sin(ang_r), np.sin(ang_r), -np.sin(ang_c), np.sin(ang_c)], axis=1)
    return jnp.asarray(cos, F32), jnp.asarray(sin, F32)


def _swap_halves(x):
    n = x.shape[-1]
    quarter = HEAD_DIM // 4
    lane = lax.broadcasted_iota(jnp.int32, x.shape, 1)
    up = pltpu.roll(x, n - quarter, axis=1)
    down = pltpu.roll(x, quarter, axis=1)
    return jnp.where((lane & quarter) == 0, up, down)


IN_PROJ_COLS = 2 * KV_WIDTH


def _proj_rope_kernel(h_ref, w_ref, cos_ref, sin_ref, o_ref, *, rope_cols, scale):
    acc = _dot(h_ref[...], w_ref[...])
    if rope_cols:
        heads = rope_cols // HEAD_DIM
        t = acc[:, :rope_cols]
        t = t * jnp.tile(cos_ref[...], (1, heads)) + _swap_halves(t) * jnp.tile(sin_ref[...], (1, heads))
        acc = t if rope_cols == acc.shape[1] else jnp.concatenate([t, acc[:, rope_cols:]], axis=1)
    if scale != 1.0:
        acc = acc * scale
    o_ref[...] = acc.astype(o_ref.dtype)


def _proj_rope(h, w, rope_cols, scale):
    B, L, D = h.shape
    N = w.shape[1]
    tn = IN_PROJ_COLS
    tm = _tile(L, 1024)
    cos, sin = _rope_tables(L)
    tab = pl.BlockSpec((tm, HEAD_DIM), lambda b, i, j: (i, 0))
    return pl.pallas_call(
        functools.partial(_proj_rope_kernel, rope_cols=rope_cols, scale=scale),
        out_shape=jax.ShapeDtypeStruct((B, L, N), BF16),
        grid=(B, L // tm, N // tn),
        in_specs=[pl.BlockSpec((None, tm, D), lambda b, i, j: (b, i, 0)),
                  pl.BlockSpec((D, tn), lambda b, i, j: (0, j)),
                  tab, tab],
        out_specs=pl.BlockSpec((None, tm, tn), lambda b, i, j: (b, i, j)),
        compiler_params=_cp(("parallel", "parallel", "arbitrary")),
        name="proj_rope",
    )(h, w, cos, sin)


def _in_proj(x, g, shift, scale, w, rope):
    h = _norm_mod(x, g, shift, scale)
    kv_end = ATT_WIDTH + IN_PROJ_COLS
    q = _proj_rope(h, w[:, :ATT_WIDTH], IN_PROJ_COLS if rope else 0, HEAD_DIM ** -0.5)
    kv = _proj_rope(h, w[:, ATT_WIDTH:kv_end], KV_WIDTH if rope else 0, 1.0)
    rest = _matmul(h, w[:, kv_end:], tm=2048)
    return q, kv, rest


def _nt_dot(a, b):
    return lax.dot_general(a, b, (((1,), (1,)), ((), ())), preferred_element_type=F32)


def _attn_bias_tables():
    T, G = BLOCK, ATT_GROUP
    qi = np.arange(G * T)[:, None] % T
    ki = np.arange(3 * T)[None, :]
    band = np.abs(ki - T - qi) <= WINDOW
    after_start = ki >= T
    before_end = ki < 2 * T
    masks = [band & after_start, band, band & before_end, band & after_start & before_end]
    return jnp.asarray(np.stack([np.where(m, 0.0, NEG) for m in masks]), F32)


def _attn_kernel(*refs, local, n_blocks):
    if local:
        (q_ref, kp_ref, kc_ref, kn_ref, vp_ref, vc_ref, vn_ref,
         kx_ref, vx_ref, bias_ref, sink_ref, g_ref, o_ref, acc_ref) = refs
    else:
        q_ref, kx_ref, vx_ref, sink_ref, g_ref, o_ref, acc_ref = refs
    T = BLOCK
    G = ATT_GROUP
    n_sub = q_ref.shape[0] // T
    for h in range(ATT_KV_HEADS):
        cols = slice(h * HEAD_DIM, (h + 1) * HEAD_DIM)
        sink = jnp.concatenate(
            [jnp.broadcast_to(sink_ref[:, (h * G + g) * HEAD_DIM:(h * G + g) * HEAD_DIM + 1], (T, 1))
             for g in range(G)], axis=0)
        kx = kx_ref[:, cols]
        vx = jnp.concatenate([vx_ref[:, cols], jnp.ones((kx_ref.shape[0], HEAD_DIM), BF16)], axis=1)
        if local:
            k_band = jnp.concatenate([kp_ref[:, cols], kc_ref[:, cols], kn_ref[:, cols]], axis=0)
            v_band = jnp.concatenate([vp_ref[:, cols], vc_ref[:, cols], vn_ref[:, cols]], axis=0)
            v_band = jnp.concatenate([v_band, jnp.ones(v_band.shape, BF16)], axis=1)
        for j in range(n_sub):
            rows = slice(j * T, (j + 1) * T)
            qs = jnp.concatenate(
                [q_ref[rows, (h * G + g) * HEAD_DIM:(h * G + g + 1) * HEAD_DIM] for g in range(G)],
                axis=0)
            s_ctx = _nt_dot(qs, kx)
            m = jnp.maximum(jnp.max(s_ctx, axis=-1, keepdims=True), sink)
            if local:
                blk = pl.program_id(1) * n_sub + j
                is_first = blk == 0
                is_last = blk == n_blocks - 1
                table = jnp.where(is_first, jnp.where(is_last, 3, 0), jnp.where(is_last, 2, 1))
                s_loc = _nt_dot(qs, k_band[j * T:(j + 3) * T]) + bias_ref[table]
                m = jnp.maximum(m, jnp.max(s_loc, axis=-1, keepdims=True))
            o = _dot(jnp.exp((s_ctx - m).astype(BF16)), vx)
            if local:
                o = o + _dot(jnp.exp((s_loc - m).astype(BF16)), v_band[j * T:(j + 3) * T])
            denom = o[:, HEAD_DIM:HEAD_DIM + 1] + jnp.exp(sink - m)
            o = o[:, :HEAD_DIM] * (1.0 / denom)
            for g in range(G):
                acc_ref[rows, (h * G + g) * HEAD_DIM:(h * G + g + 1) * HEAD_DIM] = o[g * T:(g + 1) * T]
    y = acc_ref[...]
    y = y * lax.rsqrt(jnp.mean(y * y, axis=-1, keepdims=True) + EPS) * g_ref[...]
    o_ref[...] = y.astype(o_ref.dtype)


def _attention(q, kv, kv_ctx, sink, g_attn):
    B, L, _ = q.shape
    C = kv_ctx.shape[1]
    local = kv is not None
    T = BLOCK
    nb = L // T
    n_sub = 4 if nb % 4 == 0 else (2 if nb % 2 == 0 else 1)
    sink_row = jnp.repeat(sink.astype(F32), HEAD_DIM).reshape(1, ATT_WIDTH)
    qspec = pl.BlockSpec((None, n_sub * T, ATT_WIDTH), lambda b, i: (b, i, 0))
    in_specs = [qspec]
    args = [q]
    cspecs = [pl.BlockSpec((None, C, KV_WIDTH), functools.partial(lambda b, i, part: (b, 0, part), part=part))
              for part in range(2)]
    row = pl.BlockSpec((1, ATT_WIDTH), lambda b, i: (0, 0))
    if local:
        for part in range(2):
            in_specs += [
                pl.BlockSpec((None, T, KV_WIDTH), functools.partial(
                    lambda b, i, part: (b, jnp.maximum(i * n_sub - 1, 0), part), part=part)),
                pl.BlockSpec((None, n_sub * T, KV_WIDTH), functools.partial(
                    lambda b, i, part: (b, i, part), part=part)),
                pl.BlockSpec((None, T, KV_WIDTH), functools.partial(
                    lambda b, i, part: (b, jnp.minimum((i + 1) * n_sub, nb - 1), part), part=part))]
        bias = _attn_bias_tables()
        in_specs += cspecs + [pl.BlockSpec(bias.shape, lambda b, i: (0, 0, 0))]
        args += [kv] * 6 + [kv_ctx, kv_ctx, bias]
    else:
        in_specs += cspecs
        args += [kv_ctx, kv_ctx]
    in_specs += [row, row]
    args += [sink_row, g_attn.reshape(1, ATT_WIDTH)]
    return pl.pallas_call(
        functools.partial(_attn_kernel, local=local, n_blocks=nb),
        out_shape=jax.ShapeDtypeStruct((B, L, ATT_WIDTH), BF16),
        grid=(B, nb // n_sub),
        in_specs=in_specs,
        out_specs=qspec,
        scratch_shapes=[pltpu.VMEM((n_sub * T, ATT_WIDTH), F32)],
        compiler_params=_cp(("parallel", "arbitrary")),
        name="attention_local" if local else "attention_context",
    )(*args)


def _halo_specs(ts, L, width, col_block):
    nb8 = L // HALO
    per = ts // HALO
    prev = pl.BlockSpec((None, HALO, width),
                        lambda b, i: (b, jnp.maximum(i * per - 1, 0), col_block))
    cur = pl.BlockSpec((None, ts, width), lambda b, i: (b, i, col_block))
    nxt = pl.BlockSpec((None, HALO, width),
                       lambda b, i: (b, jnp.minimum((i + 1) * per, nb8 - 1), col_block))
    return [prev, cur, nxt]


def _with_halo(prev_ref, cur_ref, next_ref):
    i = pl.program_id(1)
    last = pl.num_programs(1) - 1
    prev = jnp.where(i > 0, prev_ref[...], 0.0)
    nxt = jnp.where(i < last, next_ref[...], 0.0)
    return jnp.concatenate([prev, cur_ref[...], nxt], axis=0)


def _pool_kernel(prev_ref, cur_ref, next_ref, w_ref, scale_ref, g_ref, o_ref, *, seq_len):
    ts = cur_ref.shape[0]
    ext = _with_halo(prev_ref, cur_ref, next_ref)
    pos = pl.program_id(1) * ts + lax.broadcasted_iota(jnp.int32, (ts, 1), 0)
    outs = []
    for gidx, win in enumerate(POOL_WINDOWS):
        cols = slice(gidx * POOL_GROUP, (gidx + 1) * POOL_GROUP)
        run = ext[:, cols]
        step = 1
        while step < win:
            run = run[:run.shape[0] - step] + run[step:]
            step *= 2
        lo = HALO - win // 2
        total = run[lo:lo + ts]
        cnt = (jnp.minimum(pos + (win - win // 2), seq_len) - jnp.maximum(pos - win // 2, 0))
        mean = total * (1.0 / cnt.astype(F32))
        outs.append(_dot3(mean - cur_ref[:, cols], w_ref[gidx]))
    y = jnp.concatenate(outs, axis=-1) * scale_ref[...]
    y = y * lax.rsqrt(jnp.mean(y * y, axis=-1, keepdims=True) + EPS) * g_ref[...]
    o_ref[...] = y.astype(o_ref.dtype)


def _pool_mixer(proj, pool_w, pool_scale, g_pool):
    B, L, _ = proj.shape
    ts = _tile(L, 512)
    col_block = 0
    row = pl.BlockSpec((1, POOL_WIDTH), lambda b, i: (0, 0))
    return pl.pallas_call(
        functools.partial(_pool_kernel, seq_len=L),
        out_shape=jax.ShapeDtypeStruct((B, L, POOL_WIDTH), BF16),
        grid=(B, L // ts),
        in_specs=_halo_specs(ts, L, POOL_WIDTH, col_block) + [
            pl.BlockSpec(pool_w.shape, lambda b, i: (0, 0, 0)), row, row],
        out_specs=pl.BlockSpec((None, ts, POOL_WIDTH), lambda b, i: (b, i, 0)),
        compiler_params=_cp(("parallel", "arbitrary")),
        name="pool_mixer",
    )(proj, proj, proj, pool_w, pool_scale.reshape(1, POOL_WIDTH), g_pool.reshape(1, POOL_WIDTH))


def _short_conv_kernel(*refs):
    halo_refs = refs[:9]
    w_ref, b_ref = refs[9:11]
    out_refs = refs[11:]
    ts = out_refs[0].shape[0]
    for part in range(HY_ORDER + 1):
        ext = _with_halo(*halo_refs[3 * part:3 * part + 3])
        cols = slice(part * HY_WIDTH, (part + 1) * HY_WIDTH)
        acc = b_ref[:, cols] + ext[HALO - 1:HALO - 1 + ts] * w_ref[0:1, cols]
        acc = acc + ext[HALO:HALO + ts] * w_ref[1:2, cols]
        acc = acc + ext[HALO + 1:HALO + 1 + ts] * w_ref[2:3, cols]
        out_refs[part][...] = acc


def _short_conv(proj, conv_w, conv_b):
    B, L, _ = proj.shape
    ts = _tile(L, 512)
    first = POOL_WIDTH // HY_WIDTH
    assert first * HY_WIDTH == POOL_WIDTH
    specs = []
    for part in range(HY_ORDER + 1):
        specs += _halo_specs(ts, L, HY_WIDTH, first + part)
    n_col = (HY_ORDER + 1) * HY_WIDTH
    out = pl.BlockSpec((None, ts, HY_WIDTH), lambda b, i: (b, i, 0))
    return pl.pallas_call(
        _short_conv_kernel,
        out_shape=tuple(jax.ShapeDtypeStruct((B, L, HY_WIDTH), F32) for _ in range(HY_ORDER + 1)),
        grid=(B, L // ts),
        in_specs=specs + [pl.BlockSpec((HY_SHORT, n_col), lambda b, i: (0, 0)),
                          pl.BlockSpec((1, n_col), lambda b, i: (0, 0))],
        out_specs=(out,) * (HY_ORDER + 1),
        compiler_params=_cp(("parallel", "arbitrary")),
        name="hyena_short_conv",
    )(*([proj] * 9), conv_w, conv_b.reshape(1, n_col))


def _filter_tables(L):
    m = np.arange(L, dtype=np.float32)
    t = (m / np.float32(max(L - 1, 1))).astype(np.float32)
    w = (np.float32(2.0 * math.pi) * m / np.float32(L)).astype(np.float32)
    f = np.linspace(1e-4, HY_BANDS - 1, HY_BANDS, dtype=np.float32)
    ang = (w[:, None] * f).astype(np.float64)
    z = np.concatenate([t[:, None].astype(np.float64), np.cos(ang), -np.sin(ang)], axis=-1)
    z = np.pad(z, ((0, 0), (0, LANES - HY_EMB)))
    backward = z[(L - np.arange(L)) % L]
    max_decay = math.log(HY_TARGET) / HY_SHORT_DECAY_PCT
    min_decay = math.log(HY_TARGET) / HY_LONG_DECAY_PCT
    deltas = np.abs(np.linspace(min_decay, max_decay, HY_WIDTH, dtype=np.float32))
    return jnp.asarray(np.concatenate([z, backward]), F32), jnp.asarray(deltas[None], F32)


def _filter_kernel(z_ref, rate_ref, w1_ref, b1_ref, w2_ref, b2_ref, w3a_ref, w3b_ref,
                   b3a_ref, b3b_ref, fr_ref, o_ref, *, seq_len, period):
    tl = z_ref.shape[0]
    n0 = pl.program_id(0) * tl
    active = (n0 < seq_len) | (n0 >= period - seq_len)

    @pl.when(active)
    def _():
        n = n0 + lax.broadcasted_iota(jnp.int32, (tl, 1), 0)
        lag = jnp.where(n < seq_len, n, period - n)
        t = lag.astype(F32) / float(max(seq_len - 1, 1))
        decay = jnp.exp(-t * rate_ref[...])
        fr = fr_ref[...]
        a = jnp.sin(fr * (_dot3(z_ref[...], w1_ref[...]) + b1_ref[...]))
        a = jnp.sin(fr * (_dot3(a, w2_ref[...]) + b2_ref[...]))
        taps = [(_dot3(a, w3_ref[...]) + b3_ref[...]) * decay
                for w3_ref, b3_ref in ((w3a_ref, b3a_ref), (w3b_ref, b3b_ref))]
        o_ref[...] = jnp.where(n == period - seq_len, 0.0, jnp.concatenate(taps, axis=1))

    @pl.when(jnp.logical_not(active))
    def _():
        o_ref[...] = jnp.zeros(o_ref.shape, o_ref.dtype)


def _hyena_filter_taps(L, period, w1, b1, w2, b2, w3, b3, freq):
    z, rates = _filter_tables(L)
    hid = w1.shape[1]
    tl = _tile(L, 512)
    nf = L // tl
    nt = period // tl
    assert HY_ORDER == 2 and period % tl == 0 and period >= 2 * L
    w1p = jnp.pad(w1, ((0, LANES - HY_EMB), (0, 0)))
    b3 = b3.reshape(1, -1)
    full = lambda a: pl.BlockSpec(a.shape, lambda i: (0,) * a.ndim)
    is_bwd = lambda i: jnp.where(i >= nt - nf, 1, 0)
    ztile = lambda i: (jnp.where(i < nf, i, nf + jnp.clip(i - (nt - nf), 0, nf - 1)), 0)
    w3spec = lambda o: pl.BlockSpec((hid, HY_WIDTH), lambda i: (0, 2 * o + is_bwd(i)))
    b3spec = lambda o: pl.BlockSpec((1, HY_WIDTH), lambda i: (0, 2 * o + is_bwd(i)))
    small = [w1p, b1.reshape(1, hid), w2, b2.reshape(1, hid)]
    return pl.pallas_call(
        functools.partial(_filter_kernel, seq_len=L, period=period),
        out_shape=jax.ShapeDtypeStruct((period, HY_ORDER * HY_WIDTH), F32),
        grid=(nt,),
        in_specs=[pl.BlockSpec((tl, LANES), ztile), full(rates)] + [full(a) for a in small]
                 + [w3spec(0), w3spec(1), b3spec(0), b3spec(1), pl.BlockSpec((1, hid), lambda i: (0, 0))],
        out_specs=pl.BlockSpec((tl, HY_ORDER * HY_WIDTH), lambda i: (i, 0)),
        compiler_params=_cp(("arbitrary",)),
        name="hyena_filter_taps",
    )(z, rates, *small, w3, w3, b3, b3, freq.reshape(1, hid))


def _fft_matrices(n1, a_in):
    n = n1 * LANES
    k1 = np.arange(n1)
    a = np.arange(a_in)
    th1 = 2.0 * np.pi * np.outer(k1, a) / n1
    c1, s1 = np.cos(th1), np.sin(th1)
    m1_complex = np.block([[c1, s1], [-s1, c1]])
    th_full = 2.0 * np.pi * np.outer(k1, k1) / n1
    m1_real = np.concatenate([np.cos(th_full), -np.sin(th_full)], axis=0)
    m3 = np.block([[c1.T, -s1.T], [s1.T, c1.T]])
    b = np.arange(LANES)
    k2 = np.arange(LANES)
    k = k1[:, None, None] + n1 * k2[None, :, None]
    th2 = 2.0 * np.pi * (k * b[None, None, :] % n) / n
    c2, s2 = np.cos(th2), np.sin(th2)
    fwd = np.concatenate([np.concatenate([c2, s2], axis=2),
                          np.concatenate([-s2, c2], axis=2)], axis=1)
    c2t, s2t = np.swapaxes(c2, 1, 2), np.swapaxes(s2, 1, 2)
    inv = np.concatenate([np.concatenate([c2t, -s2t], axis=2),
                          np.concatenate([s2t, c2t], axis=2)], axis=1)
    as_bf16 = lambda m: jnp.asarray(m, F32).astype(BF16)
    return as_bf16(m1_complex), as_bf16(m1_real), as_bf16(m3), as_bf16(fwd), as_bf16(inv)


def _fft_stage1_kernel(x_ref, m_ref, o_ref, *, complex_in):
    n1 = o_ref.shape[0]
    for s in range(o_ref.shape[1]):
        if complex_in:
            x = jnp.concatenate([x_ref[0, :, s, :], x_ref[1, :, s, :]], axis=0)
        else:
            x = x_ref[:, s, :]
        y = _dot(m_ref[...], x.astype(BF16))
        o_ref[:, s, :] = _pack_bf16(y[:n1], y[n1:])


def _fft_stage1(x, m1, n1, complex_in):
    P, A, Wt = x.shape[0], x.shape[-3], x.shape[-1]
    tw = _tile(Wt, 512, LANES)
    if complex_in:
        xspec = pl.BlockSpec((None, 2, A, FFT_ROWS, tw), lambda p, i, j: (p, 0, 0, i, j))
    else:
        xspec = pl.BlockSpec((None, A, FFT_ROWS, tw), lambda p, i, j: (p, 0, i, j))
    return pl.pallas_call(
        functools.partial(_fft_stage1_kernel, complex_in=complex_in),
        out_shape=jax.ShapeDtypeStruct((P, n1, LANES, Wt), jnp.uint32),
        grid=(P, LANES // FFT_ROWS, Wt // tw),
        in_specs=[xspec, pl.BlockSpec(m1.shape, lambda p, i, j: (0, 0))],
        out_specs=pl.BlockSpec((None, n1, FFT_ROWS, tw), lambda p, i, j: (p, 0, i, j)),
        compiler_params=_cp(("parallel", "arbitrary", "arbitrary")),
        name="fft_stage1",
    )(x, m1)


def _filter_spectrum_kernel(y_ref, g_ref, o0_ref, o1_ref, *, scale):
    kt = y_ref.shape[0]
    for t in range(kt):
        y = jnp.concatenate(_unpack_bf16_pairs(y_ref[t]), axis=0)
        f = _dot(g_ref[t], y)
        for o, o_ref in enumerate((o0_ref, o1_ref)):
            cols = slice(o * HY_WIDTH, (o + 1) * HY_WIDTH)
            o_ref[0, t] = (f[:LANES, cols] * scale).astype(o_ref.dtype)
            o_ref[1, t] = (f[LANES:, cols] * scale).astype(o_ref.dtype)


def _filter_spectrum(taps, mats, n1):
    n_col = taps.shape[1]
    _, m1_real, _, g_fwd, _ = mats
    y = _fft_stage1(taps.reshape(1, n1, LANES, n_col), m1_real, n1, complex_in=False)
    kt = _tile(n1, 8, 1)
    out = jax.ShapeDtypeStruct((2, n1, LANES, HY_WIDTH), BF16)
    ospec = pl.BlockSpec((2, kt, LANES, HY_WIDTH), lambda i: (0, i, 0, 0))
    return pl.pallas_call(
        functools.partial(_filter_spectrum_kernel, scale=1.0 / (n1 * LANES)),
        out_shape=(out, out),
        grid=(n1 // kt,),
        in_specs=[pl.BlockSpec((None, kt, LANES, n_col), lambda i: (0, i, 0, 0)),
                  pl.BlockSpec((kt, 2 * LANES, 2 * LANES), lambda i: (i, 0, 0))],
        out_specs=(ospec, ospec),
        compiler_params=_cp(("arbitrary",)),
        name="hyena_filter_spectrum",
    )(y, g_fwd)


def _fft_mid_kernel(y_ref, kf_ref, g_ref, gi_ref, o_ref):
    kt = y_ref.shape[0]
    for t in range(kt):
        y = jnp.concatenate(_unpack_bf16_pairs(y_ref[t]), axis=0)
        f = _dot(g_ref[t], y)
        fr, fi = f[:LANES], f[LANES:]
        kr, ki = kf_ref[0, t].astype(F32), kf_ref[1, t].astype(F32)
        p = jnp.concatenate([fr * kr - fi * ki, fr * ki + fi * kr], axis=0).astype(BF16)
        u = _dot(gi_ref[t], p)
        o_ref[t] = _pack_bf16(u[:LANES], u[LANES:])


def _fft_mid(y, kf, g_fwd, g_inv):
    P, n1, _, W = y.shape
    kt = _tile(n1, 8, 1)
    yspec = pl.BlockSpec((None, kt, LANES, W), lambda i, p: (p, i, 0, 0))
    gspec = pl.BlockSpec((kt, 2 * LANES, 2 * LANES), lambda i, p: (i, 0, 0))
    return pl.pallas_call(
        _fft_mid_kernel,
        out_shape=jax.ShapeDtypeStruct(y.shape, jnp.uint32),
        grid=(n1 // kt, P),
        in_specs=[yspec, pl.BlockSpec((2, kt, LANES, W), lambda i, p: (0, i, 0, 0)), gspec, gspec],
        out_specs=yspec,
        compiler_params=_cp(("arbitrary", "arbitrary")),
        name="fft_mid",
    )(y, kf, g_fwd, g_inv)


def _fft_stage3_kernel(u_ref, m_ref, z_ref, x_ref, bias_ref, g_ref, o_ref, ur_ref, ui_ref,
                       *, normalise):
    a_out = z_ref.shape[1]
    bias = bias_ref[...]
    packed = u_ref[...]
    ur_ref[...] = lax.bitcast_convert_type(packed << 16, F32)
    ui_ref[...] = lax.bitcast_convert_type(packed & jnp.uint32(0xFFFF0000), F32)
    for s in range(o_ref.shape[2]):
        u = jnp.concatenate([ur_ref[:, s, :], ui_ref[:, s, :]], axis=0).astype(BF16)
        conv = _dot(m_ref[...], u)
        for part in range(2):
            y = x_ref[part, :, s, :] * (
                conv[part * a_out:(part + 1) * a_out] + z_ref[part, :, s, :] * bias)
            if normalise:
                y = y * lax.rsqrt(jnp.mean(y * y, axis=-1, keepdims=True) + EPS) * g_ref[...]
            o_ref[part, :, s, :] = y


def _fft_stage3(u, m3, z, gate, bias_row, norm_g=None):
    P, n1, _, W = u.shape
    A = z.shape[2]
    pair = pl.BlockSpec((None, 2, A, FFT_ROWS, W), lambda p, i: (p, 0, 0, i, 0))
    row = pl.BlockSpec((1, W), lambda p, i: (0, 0))
    normalise = norm_g is not None
    return pl.pallas_call(
        functools.partial(_fft_stage3_kernel, normalise=normalise),
        out_shape=jax.ShapeDtypeStruct(z.shape, F32),
        grid=(P, LANES // FFT_ROWS),
        in_specs=[pl.BlockSpec((None, n1, FFT_ROWS, W), lambda p, i: (p, 0, i, 0)),
                  pl.BlockSpec(m3.shape, lambda p, i: (0, 0)),
                  pair, pair, row, row],
        out_specs=pair,
        scratch_shapes=[pltpu.VMEM((n1, FFT_ROWS, W), F32), pltpu.VMEM((n1, FFT_ROWS, W), F32)],
        compiler_params=_cp(("parallel", "arbitrary")),
        name="fft_stage3",
    )(u, m3, z, gate, bias_row, norm_g.reshape(1, W) if normalise else bias_row)


def _hyena_mixer(v, gates, kfs, d_bias, norm_g, mats, n1):
    B, L, W = v.shape
    pair_shape = (B // 2, 2, L // LANES, LANES, W)
    m1_complex, _, m3, g_fwd, g_inv = mats
    z = v.reshape(pair_shape)
    for o in range(HY_ORDER):
        y = _fft_stage1(z, m1_complex, n1, complex_in=True)
        u = _fft_mid(y, kfs[o], g_fwd, g_inv)
        z = _fft_stage3(u, m3, z, gates[o].reshape(pair_shape), d_bias[o].reshape(1, W),
                        norm_g if o == HY_ORDER - 1 else None)
    return z.reshape(B, L, W)


def kernel(x, c, ctx, c_ctx, w_mod, b_mod, norm1_g, w_in, attn_sink, pool_w, pool_scale,
           hy_conv_w, hy_conv_b, hy_f_w1, hy_f_b1, hy_f_w2, hy_f_b2, hy_f_w3, hy_f_b3,
           hy_f_freq, hy_bias, g_attn, g_pool, g_hyena, w_out, norm2_g,
           ff_w1, ff_w3, ff_w2, router_w, moe_w1, moe_w3, moe_w2, final_g):
    B, S, D = x.shape
    C = ctx.shape[1]
    depth = w_mod.shape[0]
    assert B % 2 == 0 and S % BLOCK == 0
    fft_mats = {}

    c_rows = jnp.concatenate([c, c_ctx[None], jnp.zeros((SUBLANES - B - 1, D), F32)], axis=0)
    xc = ctx

    def hyena_branch(l, proj, seq_len):
        rows = -(-seq_len // FFT_MIN_ROWS) * FFT_MIN_ROWS
        n1 = 2 * rows // LANES
        if rows not in fft_mats:
            fft_mats[rows] = _fft_matrices(n1, rows // LANES)
        mats = fft_mats[rows]
        taps = _hyena_filter_taps(seq_len, 2 * rows, hy_f_w1[l], hy_f_b1[l], hy_f_w2[l], hy_f_b2[l],
                                  hy_f_w3[l], hy_f_b3[l], hy_f_freq[l])
        v, x1, x2 = _short_conv(proj, hy_conv_w[l], hy_conv_b[l])
        if seq_len < rows:
            pad = lambda t: jnp.pad(t, ((0, 0), (0, rows - seq_len), (0, 0)))
            v, x1, x2 = pad(v), pad(x1), pad(x2)
        kfs = _filter_spectrum(taps, mats, n1)
        z = _hyena_mixer(v, (x1, x2), kfs, hy_bias[l], g_hyena[l], mats, n1)
        return z[:, :seq_len]

    def mix(l, proj, y_att, res, gate, w_out_l, seq_len, norm=None):
        y_pool = _pool_mixer(proj, pool_w[l], pool_scale[l], g_pool[l])
        y_hy = hyena_branch(l, proj, seq_len)
        return _mix_out([y_att, y_pool, y_hy], w_out_l, res, gate, norm)

    for l in range(depth):
        last = l == depth - 1
        mod = _adaln(c_rows, w_mod, b_mod, l)
        sh1, sc1, g1, sh2, sc2, g2 = [mod[:B, j * D:(j + 1) * D] for j in range(6)]
        csh1, csc1, cg1, csh2, csc2, cg2 = [mod[B:B + 1, j * D:(j + 1) * D] for j in range(6)]
        w_in_l = w_in[l].astype(BF16)
        w_out_l = w_out[l].astype(BF16)

        if last:
            hc = _norm_mod(xc, norm1_g[l], csh1, csc1)
            kv_w = w_in_l[:, ATT_WIDTH:ATT_WIDTH + 2 * KV_WIDTH]
            kv_c = _matmul(hc, kv_w, out_dtype=BF16)
        else:
            q_c, kv_c, proj_c = _in_proj(xc, norm1_g[l], csh1, csc1, w_in_l, rope=False)
            y_att_c = _attention(q_c, None, kv_c, attn_sink[l], g_attn[l])
            xc_new, hc2 = mix(l, proj_c, y_att_c, xc, cg1, w_out_l, C, (norm2_g[l], csh2, csc2))
            i = l // 2
            if l % 2 == 0:
                hid_c = _swiglu_hidden(hc2, ff_w1[i].astype(BF16), ff_w3[i].astype(BF16))
                xc_new = _matmul_gated_residual([hid_c], ff_w2[i].astype(BF16), xc_new, cg2)
            else:
                raise NotImplementedError("context tokens through an expert layer")

        q, kv, proj = _in_proj(x, norm1_g[l], sh1, sc1, w_in_l, rope=True)
        y_att = _attention(q, kv, kv_c, attn_sink[l], g_attn[l])

        i = l // 2
        if l % 2 == 0:
            x, h2 = mix(l, proj, y_att, x, g1, w_out_l, S, (norm2_g[l], sh2, sc2))
            hid = _swiglu_hidden(h2, ff_w1[i].astype(BF16), ff_w3[i].astype(BF16))
            x = _matmul_gated_residual([hid], ff_w2[i].astype(BF16), x, g2)
            if last:
                x = _rmsnorm(x, final_g, F32)
        else:
            assert last
            x = mix(l, proj, y_att, x, g1, w_out_l, S)
            h2, gi, gw, counts = _norm_mod_router(x, norm2_g[l], sh2, sc2, router_w[i])
            x = _moe_layer(x, h2, gi, gw, counts, g2, final_g, moe_w1[i], moe_w3[i], moe_w2[i])
        if not last:
            xc = xc_new
    return x
```

```python
import functools
import math

import numpy as np
import jax
import jax.numpy as jnp
from jax import lax
from jax.experimental import pallas as pl
from jax.experimental.pallas import tpu as pltpu

F32 = jnp.float32
BF16 = jnp.bfloat16

EPS = 1e-6
NEG = -1e30
GRID_W = 64
ATT_HEADS = 8
ATT_KV_HEADS = 2
ATT_GROUP = ATT_HEADS // ATT_KV_HEADS
HEAD_DIM = 128
ATT_WIDTH = ATT_HEADS * HEAD_DIM
KV_WIDTH = ATT_KV_HEADS * HEAD_DIM
WINDOW = 128
BLOCK = 128
ROPE_BASE = 10000.0
POOL_WINDOWS = (2, 4, 8, 16)
POOL_GROUP = 128
POOL_WIDTH = POOL_GROUP * len(POOL_WINDOWS)
HY_WIDTH = 512
HY_ORDER = 2
HY_SHORT = 3
HY_EMB = 33
HY_BANDS = (HY_EMB - 1) // 2
HY_SHORT_DECAY_PCT = 0.3
HY_LONG_DECAY_PCT = 1.5
HY_TARGET = 1e-2
N_EXPERTS = 8
TOP_K = 2

LANES = 128
SUBLANES = 8
HALO = SUBLANES
FFT_MIN_ROWS = 2048
FFT_ROWS = SUBLANES
MOE_OUT_K_CHUNKS = 4
VMEM_LIMIT = 56 * 1024 * 1024


def _cp(sem):
    return pltpu.CompilerParams(dimension_semantics=sem, vmem_limit_bytes=VMEM_LIMIT)


def _tile(n, pref, mult=SUBLANES):
    if n <= pref:
        return n
    t = (pref // mult) * mult
    while t >= mult:
        if n % t == 0:
            return t
        t -= mult
    return n


def _split_bf16(x):
    hi = x.astype(BF16)
    lo = (x - hi.astype(F32)).astype(BF16)
    return hi, lo


def _dot(a, b):
    return jnp.dot(a, b, preferred_element_type=F32)


def _pack_bf16(lo, hi):
    lo_bits = lax.bitcast_convert_type(lo.astype(BF16).astype(F32), jnp.uint32)
    hi_bits = lax.bitcast_convert_type(hi.astype(BF16).astype(F32), jnp.uint32)
    return (lo_bits >> 16) | (hi_bits & jnp.uint32(0xFFFF0000))


def _pack_bf16_pairs(x):
    n = x.shape[-1] // 2
    return _pack_bf16(x[:, :n], x[:, n:])


def _unpack_bf16_pairs(u):
    lo = lax.bitcast_convert_type(u << 16, F32).astype(BF16)
    hi = lax.bitcast_convert_type(u & jnp.uint32(0xFFFF0000), F32).astype(BF16)
    return lo, hi


def _dot3(a, b):
    ah, al = _split_bf16(a)
    bh, bl = _split_bf16(b)
    return _dot(ah, bh) + (_dot(ah, bl) + _dot(al, bh))


def _adaln_kernel(c_ref, w_ref, b_ref, o_ref):
    c = c_ref[...]
    a = c * (1.0 / (1.0 + jnp.exp(-c)))
    o_ref[...] = _dot3(a, w_ref[...]) + b_ref[...]


def _adaln(c_rows, w_all, b_all, layer):
    R, D = c_rows.shape
    N = w_all.shape[2]
    tn = _tile(N, 1536, LANES)
    return pl.pallas_call(
        _adaln_kernel,
        out_shape=jax.ShapeDtypeStruct((R, N), F32),
        grid=(N // tn,),
        in_specs=[pl.BlockSpec((R, D), lambda j: (0, 0)),
                  pl.BlockSpec((None, D, tn), lambda j: (layer, 0, j)),
                  pl.BlockSpec((None, 1, tn), lambda j: (layer, 0, j))],
        out_specs=pl.BlockSpec((R, tn), lambda j: (0, j)),
        compiler_params=_cp(("arbitrary",)),
        name="adaln",
    )(c_rows, w_all, b_all[:, None, :])


def _norm_mod_kernel(x_ref, g_ref, sh_ref, sc_ref, o_ref):
    x = x_ref[...]
    y = x * lax.rsqrt(jnp.mean(x * x, axis=-1, keepdims=True) + EPS) * g_ref[...]
    o_ref[...] = (y * (1.0 + sc_ref[...]) + sh_ref[...]).astype(o_ref.dtype)


def _bcast_map(arr):
    if arr.shape[0] == 1:
        return lambda b, i: (0, 0, 0)
    return lambda b, i: (b, 0, 0)


def _norm_mod(x, g, shift, scale):
    B, L, D = x.shape
    ts = _tile(L, 512)
    shift = shift[:, None, :]
    scale = scale[:, None, :]
    return pl.pallas_call(
        _norm_mod_kernel,
        out_shape=jax.ShapeDtypeStruct((B, L, D), BF16),
        grid=(B, L // ts),
        in_specs=[pl.BlockSpec((None, ts, D), lambda b, i: (b, i, 0)),
                  pl.BlockSpec((1, D), lambda b, i: (0, 0)),
                  pl.BlockSpec((None, 1, D), _bcast_map(shift)),
                  pl.BlockSpec((None, 1, D), _bcast_map(scale))],
        out_specs=pl.BlockSpec((None, ts, D), lambda b, i: (b, i, 0)),
        compiler_params=_cp(("parallel", "parallel")),
        name="norm_mod",
    )(x, g.reshape(1, D), shift, scale)


def _route(h, rw_ref, tri_ref, run_ref, gi_ref, gw_ref, cnt_ref):
    logits = _dot3(h, rw_ref[...])
    lane = lax.broadcasted_iota(jnp.int32, logits.shape, 1)
    logits = jnp.where(lane < N_EXPERTS, logits, NEG)
    m1 = jnp.max(logits, axis=-1, keepdims=True)
    i1 = jnp.min(jnp.where(logits == m1, lane, LANES), axis=-1, keepdims=True)
    rest = jnp.where(lane == i1, NEG, logits)
    m2 = jnp.max(rest, axis=-1, keepdims=True)
    i2 = jnp.min(jnp.where(rest == m2, lane, LANES), axis=-1, keepdims=True)
    e2 = jnp.exp(m2 - m1)
    w1 = 1.0 / (1.0 + e2)
    w2 = e2 * w1
    pick1 = lane == i1
    pick2 = lane == i2
    both = jnp.where(pick1 | pick2, 1.0, 0.0)
    before = _dot(tri_ref[...], both.astype(BF16)) + run_ref[...]
    r1 = jnp.sum(jnp.where(pick1, before, 0.0), axis=-1, keepdims=True)
    r2 = jnp.sum(jnp.where(pick2, before, 0.0), axis=-1, keepdims=True)
    run_ref[...] = run_ref[...] + jnp.sum(both, axis=0, keepdims=True)
    cnt_ref[...] = run_ref[...]
    ids = jnp.where(lane == 0, i1.astype(F32), jnp.where(lane == 1, i2.astype(F32),
                    jnp.where(lane == 2, r1, jnp.where(lane == 3, r2, 0.0))))
    gi_ref[...] = jnp.transpose(ids)[:SUBLANES, :]
    gw_ref[...] = jnp.where(lane == 0, w1, jnp.where(lane == 1, w2, 0.0))


def _norm_mod_router_kernel(x_ref, g_ref, sh_ref, sc_ref, rw_ref, tri_ref,
                            o_ref, gi_ref, gw_ref, cnt_ref, run_ref):
    @pl.when((pl.program_id(0) == 0) & (pl.program_id(1) == 0))
    def _():
        run_ref[...] = jnp.zeros(run_ref.shape, run_ref.dtype)

    x = x_ref[...]
    y = x * lax.rsqrt(jnp.mean(x * x, axis=-1, keepdims=True) + EPS) * g_ref[...]
    h = y * (1.0 + sc_ref[...]) + sh_ref[...]
    o_ref[...] = _pack_bf16_pairs(h)
    _route(h, rw_ref, tri_ref, run_ref, gi_ref, gw_ref, cnt_ref)


def _norm_mod_router(x, g, shift, scale, router_w):
    B, L, D = x.shape
    ts = _tile(L, 512)
    shift = shift[:, None, :]
    scale = scale[:, None, :]
    rw = jnp.pad(router_w, ((0, 0), (0, LANES - router_w.shape[1])))
    tri = jnp.asarray(np.tril(np.ones((ts, ts), np.float32), -1), BF16)
    row = pl.BlockSpec((None, ts, D), lambda b, i: (b, i, 0))
    small = pl.BlockSpec((None, ts, LANES), lambda b, i: (b, i, 0))
    return pl.pallas_call(
        _norm_mod_router_kernel,
        out_shape=(jax.ShapeDtypeStruct((B, L, D // 2), jnp.uint32),
                   jax.ShapeDtypeStruct((B, SUBLANES, L), F32),
                   jax.ShapeDtypeStruct((B, L, LANES), F32),
                   jax.ShapeDtypeStruct((1, LANES), F32)),
        grid=(B, L // ts),
        in_specs=[row,
                  pl.BlockSpec((1, D), lambda b, i: (0, 0)),
                  pl.BlockSpec((None, 1, D), _bcast_map(shift)),
                  pl.BlockSpec((None, 1, D), _bcast_map(scale)),
                  pl.BlockSpec((D, LANES), lambda b, i: (0, 0)),
                  pl.BlockSpec((ts, ts), lambda b, i: (0, 0))],
        out_specs=(pl.BlockSpec((None, ts, D // 2), lambda b, i: (b, i, 0)),
                   pl.BlockSpec((None, SUBLANES, ts), lambda b, i: (b, 0, i)), small,
                   pl.BlockSpec((1, LANES), lambda b, i: (0, 0))),
        scratch_shapes=[pltpu.VMEM((1, LANES), F32)],
        compiler_params=_cp(("arbitrary", "arbitrary")),
        name="norm_mod_router",
    )(x, g.reshape(1, D), shift, scale, rw, tri)


def _mix_out_kernel(*refs, n_a, emit_h):
    a_refs = refs[:n_a]
    w_refs = refs[n_a:2 * n_a]
    res_ref, gate_ref = refs[2 * n_a:2 * n_a + 2]
    acc = _dot(a_refs[0][...].astype(BF16), w_refs[0][...])
    for a_ref, w_ref in zip(a_refs[1:], w_refs[1:]):
        acc = acc + _dot(a_ref[...].astype(BF16), w_ref[...])
    x = res_ref[...] + gate_ref[...] * acc
    if emit_h:
        g_ref, sh_ref, sc_ref, x_ref, h_ref = refs[2 * n_a + 2:]
        y = x * lax.rsqrt(jnp.mean(x * x, axis=-1, keepdims=True) + EPS) * g_ref[...]
        h_ref[...] = (y * (1.0 + sc_ref[...]) + sh_ref[...]).astype(h_ref.dtype)
    else:
        x_ref, = refs[2 * n_a + 2:]
    x_ref[...] = x


def _mix_out(a_list, w, res, gate, norm=None):
    B, L, D = res.shape
    tm = _tile(L, 512)
    emit_h = norm is not None
    gate = gate[:, None, :]
    in_specs = [pl.BlockSpec((None, tm, a.shape[-1]), lambda b, i: (b, i, 0)) for a in a_list]
    off = 0
    for a in a_list:
        k = a.shape[-1]
        assert off % k == 0
        in_specs.append(pl.BlockSpec((k, D), functools.partial(lambda b, i, blk: (blk, 0), blk=off // k)))
        off += k
    row = pl.BlockSpec((None, tm, D), lambda b, i: (b, i, 0))
    in_specs += [row, pl.BlockSpec((None, 1, D), _bcast_map(gate))]
    args = [*a_list, *([w] * len(a_list)), res, gate]
    out_shape = [jax.ShapeDtypeStruct((B, L, D), F32)]
    out_specs = [row]
    if emit_h:
        g, shift, scale = norm
        shift, scale = shift[:, None, :], scale[:, None, :]
        in_specs += [pl.BlockSpec((1, D), lambda b, i: (0, 0)),
                     pl.BlockSpec((None, 1, D), _bcast_map(shift)),
                     pl.BlockSpec((None, 1, D), _bcast_map(scale))]
        args += [g.reshape(1, D), shift, scale]
        out_shape.append(jax.ShapeDtypeStruct((B, L, D), BF16))
        out_specs.append(row)
    out = pl.pallas_call(
        functools.partial(_mix_out_kernel, n_a=len(a_list), emit_h=emit_h),
        out_shape=tuple(out_shape),
        grid=(B, L // tm),
        in_specs=in_specs,
        out_specs=tuple(out_specs),
        compiler_params=_cp(("parallel", "parallel")),
        name="mix_out",
    )(*args)
    return out if emit_h else out[0]


def _rmsnorm_kernel(x_ref, g_ref, o_ref):
    x = x_ref[...]
    y = x * lax.rsqrt(jnp.mean(x * x, axis=-1, keepdims=True) + EPS) * g_ref[...]
    o_ref[...] = y.astype(o_ref.dtype)


def _rmsnorm(x, g, out_dtype):
    B, L, D = x.shape
    ts = _tile(L, 512)
    return pl.pallas_call(
        _rmsnorm_kernel,
        out_shape=jax.ShapeDtypeStruct((B, L, D), out_dtype),
        grid=(B, L // ts),
        in_specs=[pl.BlockSpec((None, ts, D), lambda b, i: (b, i, 0)),
                  pl.BlockSpec((1, D), lambda b, i: (0, 0))],
        out_specs=pl.BlockSpec((None, ts, D), lambda b, i: (b, i, 0)),
        compiler_params=_cp(("parallel", "parallel")),
        name="rmsnorm",
    )(x, g.reshape(1, D))


def _mm_kernel(a_ref, w_ref, o_ref):
    o_ref[...] = _dot(a_ref[...], w_ref[...]).astype(o_ref.dtype)


def _matmul(a, w, out_dtype=F32, tm=1024, tn=512):
    B, L, K = a.shape
    N = w.shape[1]
    tm = _tile(L, tm)
    tn = _tile(N, tn, LANES)
    return pl.pallas_call(
        _mm_kernel,
        out_shape=jax.ShapeDtypeStruct((B, L, N), out_dtype),
        grid=(B, L // tm, N // tn),
        in_specs=[pl.BlockSpec((None, tm, K), lambda b, i, j: (b, i, 0)),
                  pl.BlockSpec((K, tn), lambda b, i, j: (0, j))],
        out_specs=pl.BlockSpec((None, tm, tn), lambda b, i, j: (b, i, j)),
        compiler_params=_cp(("parallel", "parallel", "arbitrary")),
        name="matmul",
    )(a, w)


def _mm_res_kernel(*refs, n_a):
    a_refs = refs[:n_a]
    w_refs = refs[n_a:2 * n_a]
    res_ref, gate_ref, o_ref = refs[2 * n_a:]
    acc = _dot(a_refs[0][...].astype(BF16), w_refs[0][...])
    for a_ref, w_ref in zip(a_refs[1:], w_refs[1:]):
        acc = acc + _dot(a_ref[...].astype(BF16), w_ref[...])
    o_ref[...] = res_ref[...] + gate_ref[...] * acc


def _matmul_gated_residual(a_list, w, res, gate, tm=1024, tn=512):
    B, L, N = res.shape
    tm = _tile(L, tm)
    tn = _tile(N, tn, LANES)
    gate = gate[:, None, :]
    widths = [a.shape[-1] for a in a_list]
    unit = math.gcd(*widths) if len(widths) > 1 else widths[0]
    in_specs = [pl.BlockSpec((None, tm, k), lambda b, i, j: (b, i, 0)) for k in widths]
    off = 0
    for k in widths:
        assert off % k == 0 and k % unit == 0
        in_specs.append(pl.BlockSpec((k, tn), functools.partial(
            lambda b, i, j, blk: (blk, j), blk=off // k)))
        off += k
    gmap = (lambda b, i, j: (0, 0, j)) if gate.shape[0] == 1 else (lambda b, i, j: (b, 0, j))
    in_specs += [pl.BlockSpec((None, tm, tn), lambda b, i, j: (b, i, j)),
                 pl.BlockSpec((None, 1, tn), gmap)]
    return pl.pallas_call(
        functools.partial(_mm_res_kernel, n_a=len(a_list)),
        out_shape=jax.ShapeDtypeStruct((B, L, N), F32),
        grid=(B, L // tm, N // tn),
        in_specs=in_specs,
        out_specs=pl.BlockSpec((None, tm, tn), lambda b, i, j: (b, i, j)),
        compiler_params=_cp(("parallel", "parallel", "arbitrary")),
        name="matmul_gated_residual",
    )(*a_list, *([w] * len(a_list)), res, gate)


def _silu(x):
    return x * (1.0 / (1.0 + jnp.exp(-x)))


def _swiglu_kernel(a_ref, w1_ref, w3_ref, o_ref):
    a = a_ref[...]
    o_ref[...] = (_silu(_dot(a, w1_ref[...])) * _dot(a, w3_ref[...])).astype(o_ref.dtype)


def _swiglu_hidden(a, w1, w3, tm=1024, tf=512):
    B, L, D = a.shape
    F = w1.shape[1]
    tm = _tile(L, tm)
    tf = _tile(F, tf, LANES)
    wspec = pl.BlockSpec((D, tf), lambda b, i, j: (0, j))
    return pl.pallas_call(
        _swiglu_kernel,
        out_shape=jax.ShapeDtypeStruct((B, L, F), BF16),
        grid=(B, L // tm, F // tf),
        in_specs=[pl.BlockSpec((None, tm, D), lambda b, i, j: (b, i, 0)), wspec, wspec],
        out_specs=pl.BlockSpec((None, tm, tf), lambda b, i, j: (b, i, j)),
        compiler_params=_cp(("parallel", "parallel", "arbitrary")),
        name="swiglu_hidden",
    )(a, w1, w3)


def _for_valid_rows(n_valid, tm, compute, o_ref):
    half_rows = tm // 2

    @pl.when(n_valid > half_rows)
    def _():
        compute(slice(0, tm))

    @pl.when((n_valid > 0) & (n_valid <= half_rows))
    def _():
        compute(slice(0, half_rows))
        o_ref[half_rows:, :] = jnp.zeros((tm - half_rows, o_ref.shape[1]), o_ref.dtype)

    @pl.when(n_valid == 0)
    def _():
        o_ref[...] = jnp.zeros(o_ref.shape, o_ref.dtype)


def _moe_hidden_kernel(te_ref, rows_ref, a_ref, w1_ref, w3_ref, o_ref, w1b_ref, w3b_ref):
    i = pl.program_id(1)
    n_valid = rows_ref[i]
    half = a_ref.shape[1]
    new_weights = (i == 0) | (te_ref[i] != te_ref[jnp.maximum(i - 1, 0)])

    @pl.when((n_valid > 0) & new_weights)
    def _():
        w1b_ref[...] = w1_ref[...].astype(BF16)
        w3b_ref[...] = w3_ref[...].astype(BF16)

    def compute(rows):
        lo, hi = _unpack_bf16_pairs(a_ref[rows, :])
        gate = _dot(lo, w1b_ref[:half, :]) + _dot(hi, w1b_ref[half:, :])
        up = _dot(lo, w3b_ref[:half, :]) + _dot(hi, w3b_ref[half:, :])
        o_ref[rows, :] = (_silu(gate) * up).astype(o_ref.dtype)

    _for_valid_rows(n_valid, a_ref.shape[0], compute, o_ref)


def _moe_out_kernel(te_ref, rows_ref, a_ref, w2_ref, o_ref):
    def compute(rows):
        kc = a_ref.shape[1] // MOE_OUT_K_CHUNKS
        acc = _dot(a_ref[rows, :kc], w2_ref[:kc, :].astype(BF16))
        for c in range(1, MOE_OUT_K_CHUNKS):
            acc = acc + _dot(a_ref[rows, c * kc:(c + 1) * kc],
                             w2_ref[c * kc:(c + 1) * kc, :].astype(BF16))
        o_ref[rows, :] = _pack_bf16_pairs(acc)

    _for_valid_rows(rows_ref[pl.program_id(0)], a_ref.shape[0], compute, o_ref)


def _moe_out_cols(d_model):
    return _tile(d_model, 512, 2 * LANES)


def _moe_experts(a_sorted, tile_expert, tile_rows, w1, w3, w2, tm, tf=512):
    R = a_sorted.shape[0]
    E, D, F = w1.shape
    tf = _tile(F, tf, LANES)
    tn = _moe_out_cols(D)
    nt = R // tm
    w13 = pl.BlockSpec((None, D, tf), lambda j, i, te, n: (te[i], 0, j))
    hidden = pl.pallas_call(
        _moe_hidden_kernel,
        out_shape=jax.ShapeDtypeStruct((R, F), BF16),
        grid_spec=pltpu.PrefetchScalarGridSpec(
            num_scalar_prefetch=2,
            grid=(F // tf, nt),
            in_specs=[pl.BlockSpec((tm, D // 2), lambda j, i, te, n: (i, 0)), w13, w13],
            out_specs=pl.BlockSpec((tm, tf), lambda j, i, te, n: (i, j)),
            scratch_shapes=[pltpu.VMEM((D, tf), BF16), pltpu.VMEM((D, tf), BF16)]),
        compiler_params=_cp(("arbitrary", "arbitrary")),
        name="moe_hidden",
    )(tile_expert, tile_rows, a_sorted, w1, w3)
    return pl.pallas_call(
        _moe_out_kernel,
        out_shape=jax.ShapeDtypeStruct((R, D // 2), jnp.uint32),
        grid_spec=pltpu.PrefetchScalarGridSpec(
            num_scalar_prefetch=2,
            grid=(nt, D // tn),
            in_specs=[pl.BlockSpec((tm, F), lambda i, j, te, n: (i, 0)),
                      pl.BlockSpec((None, F, tn), lambda i, j, te, n: (te[i], 0, j))],
            out_specs=pl.BlockSpec((tm, tn // 2), lambda i, j, te, n: (i, j))),
        compiler_params=_cp(("arbitrary", "arbitrary")),
        name="moe_out",
    )(tile_expert, tile_rows, hidden, w2)


def _unpack_column_blocks(u, block):
    parts = []
    for c in range(u.shape[1] // block):
        w = u[:, c * block:(c + 1) * block]
        parts.append(lax.bitcast_convert_type(w << 16, F32))
        parts.append(lax.bitcast_convert_type(w & jnp.uint32(0xFFFF0000), F32))
    return jnp.concatenate(parts, axis=1)


def _moe_combine_kernel(x_ref, ya_ref, yb_ref, gw_ref, gate_ref, g_ref, o_ref, *, block):
    gw = gw_ref[...]
    moe = (gw[:, 0:1] * _unpack_column_blocks(ya_ref[...], block)
           + gw[:, 1:2] * _unpack_column_blocks(yb_ref[...], block))
    x = x_ref[...] + gate_ref[...] * moe
    y = x * lax.rsqrt(jnp.mean(x * x, axis=-1, keepdims=True) + EPS) * g_ref[...]
    o_ref[...] = y


def _moe_combine_norm(x, ya, yb, gw, gate, g):
    B, L, D = x.shape
    ts = _tile(L, 512)
    row = pl.BlockSpec((None, ts, D), lambda b, i: (b, i, 0))
    packed = pl.BlockSpec((None, ts, D // 2), lambda b, i: (b, i, 0))
    return pl.pallas_call(
        functools.partial(_moe_combine_kernel, block=_moe_out_cols(D) // 2),
        out_shape=jax.ShapeDtypeStruct((B, L, D), F32),
        grid=(B, L // ts),
        in_specs=[row, packed, packed,
                  pl.BlockSpec((None, ts, LANES), lambda b, i: (b, i, 0)),
                  pl.BlockSpec((None, 1, D), lambda b, i: (b, 0, 0)),
                  pl.BlockSpec((1, D), lambda b, i: (0, 0))],
        out_specs=row,
        compiler_params=_cp(("parallel", "parallel")),
        name="moe_combine_norm",
    )(x, ya, yb, gw, gate[:, None, :], g.reshape(1, D))


def _moe_layer(x, h, gi, gw, counts, gate, final_g, w1, w3, w2):
    B, L, D = x.shape
    E = w1.shape[0]
    n_tok = B * L
    n_pair = n_tok * TOP_K
    tm = _tile(n_pair, 1024)
    nt = n_pair // tm + E
    rows_of = lambda r: jnp.concatenate(
        [gi[:, r + c, :].reshape(n_tok) for c in range(TOP_K)]).astype(jnp.int32)
    e_pair = rows_of(0)
    rank = rows_of(TOP_K)
    counts = counts[0, :E].astype(jnp.int32)
    tiles_per = (counts + tm - 1) // tm
    tile_end = jnp.cumsum(tiles_per)
    tile_start = tile_end - tiles_per
    row_start = jnp.zeros((n_pair,), jnp.int32)
    for e in range(E):
        row_start = jnp.where(e_pair == e, tile_start[e] * tm, row_start)
    pos = row_start + rank
    tile_ids = jnp.arange(nt, dtype=jnp.int32)
    tile_expert = jnp.minimum(
        jnp.sum((tile_ids[:, None] >= tile_end[None, :]).astype(jnp.int32), axis=1), E - 1)
    tile_rows = jnp.clip(counts[tile_expert] - (tile_ids - tile_start[tile_expert]) * tm, 0, tm)
    tile_rows = jnp.where(tile_ids < tile_end[-1], tile_rows, 0).astype(jnp.int32)
    tok_pair = jnp.arange(n_pair, dtype=jnp.int32) % n_tok
    src = (jnp.arange(nt * tm, dtype=jnp.int32) % n_tok).at[pos].set(
        tok_pair, unique_indices=True, mode="promise_in_bounds")
    gather_rows = lambda rows, idx: rows.at[idx].get(mode="promise_in_bounds")
    a_sorted = gather_rows(h.reshape(n_tok, h.shape[-1]), src)
    y_sorted = _moe_experts(a_sorted, tile_expert.astype(jnp.int32), tile_rows, w1, w3, w2, tm)
    ya = gather_rows(y_sorted, pos[:n_tok]).reshape(B, L, D // 2)
    yb = gather_rows(y_sorted, pos[n_tok:]).reshape(B, L, D // 2)
    return _moe_combine_norm(x, ya, yb, gw, gate, final_g)


def _rope_tables(n_tokens):
    pos = np.arange(n_tokens)
    row = (pos // GRID_W).astype(np.float32)
    col = (pos % GRID_W).astype(np.float32)
    n_freq = HEAD_DIM // 4
    inv = (np.float32(ROPE_BASE) ** (-np.arange(n_freq, dtype=np.float32) / np.float32(n_freq)))
    ang_r = (row[:, None] * inv).astype(np.float64)
    ang_c = (col[:, None] * inv).astype(np.float64)
    cos = np.concatenate([np.cos(ang_r), np.cos(ang_r), np.cos(ang_c), np.cos(ang_c)], axis=1)
    sin = np.concatenate([-np.sin(ang_r), np.sin(ang_r), -np.sin(ang_c), np.sin(ang_c)], axis=1)
    return jnp.asarray(cos, F32), jnp.asarray(sin, F32)


def _swap_halves(x):
    n = x.shape[-1]
    quarter = HEAD_DIM // 4
    lane = lax.broadcasted_iota(jnp.int32, x.shape, 1)
    up = pltpu.roll(x, n - quarter, axis=1)
    down = pltpu.roll(x, quarter, axis=1)
    return jnp.where((lane & quarter) == 0, up, down)


IN_PROJ_COLS = 2 * KV_WIDTH
PROJ_ROW_GROUPS = 4


def _proj_rope_kernel(h_ref, w_ref, cos_ref, sin_ref, o_ref, *, rope_cols, scale):
    rows_per = h_ref.shape[0] // PROJ_ROW_GROUPS
    for r in range(PROJ_ROW_GROUPS):
        rows = slice(r * rows_per, (r + 1) * rows_per)
        acc = _dot(h_ref[rows, :], w_ref[...])
        if rope_cols:
            heads = rope_cols // HEAD_DIM
            t = acc[:, :rope_cols]
            t = (t * jnp.tile(cos_ref[rows, :], (1, heads))
                 + _swap_halves(t) * jnp.tile(sin_ref[rows, :], (1, heads)))
            acc = t if rope_cols == acc.shape[1] else jnp.concatenate([t, acc[:, rope_cols:]], axis=1)
        if scale != 1.0:
            acc = acc * scale
        o_ref[rows, :] = acc.astype(o_ref.dtype)


def _proj_rope(h, w, rope_cols, scale):
    B, L, D = h.shape
    N = w.shape[1]
    tn = IN_PROJ_COLS
    tm = _tile(L, 1024)
    cos, sin = _rope_tables(L)
    tab = pl.BlockSpec((tm, HEAD_DIM), lambda b, i, j: (i, 0))
    return pl.pallas_call(
        functools.partial(_proj_rope_kernel, rope_cols=rope_cols, scale=scale),
        out_shape=jax.ShapeDtypeStruct((B, L, N), BF16),
        grid=(B, L // tm, N // tn),
        in_specs=[pl.BlockSpec((None, tm, D), lambda b, i, j: (b, i, 0)),
                  pl.BlockSpec((D, tn), lambda b, i, j: (0, j)),
                  tab, tab],
        out_specs=pl.BlockSpec((None, tm, tn), lambda b, i, j: (b, i, j)),
        compiler_params=_cp(("parallel", "parallel", "arbitrary")),
        name="proj_rope",
    )(h, w, cos, sin)


def _in_proj(x, g, shift, scale, w, rope):
    h = _norm_mod(x, g, shift, scale)
    kv_end = ATT_WIDTH + IN_PROJ_COLS
    q = _proj_rope(h, w[:, :ATT_WIDTH], IN_PROJ_COLS if rope else 0, HEAD_DIM ** -0.5)
    kv = _proj_rope(h, w[:, ATT_WIDTH:kv_end], KV_WIDTH if rope else 0, 1.0)
    rest = _matmul(h, w[:, kv_end:], tm=2048)
    return q, kv, rest


def _nt_dot(a, b):
    return lax.dot_general(a, b, (((1,), (1,)), ((), ())), preferred_element_type=F32)


def _attn_bias_tables():
    T, G = BLOCK, ATT_GROUP
    qi = np.arange(G * T)[:, None] % T
    ki = np.arange(3 * T)[None, :]
    band = np.abs(ki - T - qi) <= WINDOW
    after_start = ki >= T
    before_end = ki < 2 * T
    masks = [band & after_start, band, band & before_end, band & after_start & before_end]
    return jnp.asarray(np.stack([np.where(m, 0.0, NEG) for m in masks]), F32)


def _attn_kernel(*refs, local, n_blocks):
    if local:
        (q_ref, kp_ref, kc_ref, kn_ref, vp_ref, vc_ref, vn_ref,
         kx_ref, vx_ref, bias_ref, sink_ref, g_ref, o_ref, acc_ref) = refs
    else:
        q_ref, kx_ref, vx_ref, sink_ref, g_ref, o_ref, acc_ref = refs
    T = BLOCK
    G = ATT_GROUP
    n_sub = q_ref.shape[0] // T
    for h in range(ATT_KV_HEADS):
        cols = slice(h * HEAD_DIM, (h + 1) * HEAD_DIM)
        sink = jnp.concatenate(
            [jnp.broadcast_to(sink_ref[:, (h * G + g) * HEAD_DIM:(h * G + g) * HEAD_DIM + 1], (T, 1))
             for g in range(G)], axis=0)
        kx = kx_ref[:, cols]
        vx = jnp.concatenate([vx_ref[:, cols], jnp.ones((kx_ref.shape[0], HEAD_DIM), BF16)], axis=1)
        if local:
            k_band = jnp.concatenate([kp_ref[:, cols], kc_ref[:, cols], kn_ref[:, cols]], axis=0)
            v_band = jnp.concatenate([vp_ref[:, cols], vc_ref[:, cols], vn_ref[:, cols]], axis=0)
            v_band = jnp.concatenate([v_band, jnp.ones(v_band.shape, BF16)], axis=1)
        for j in range(n_sub):
            rows = slice(j * T, (j + 1) * T)
            qs = jnp.concatenate(
                [q_ref[rows, (h * G + g) * HEAD_DIM:(h * G + g + 1) * HEAD_DIM] for g in range(G)],
                axis=0)
            s_ctx = _nt_dot(qs, kx)
            m = jnp.maximum(jnp.max(s_ctx, axis=-1, keepdims=True), sink)
            if local:
                blk = pl.program_id(1) * n_sub + j
                is_first = blk == 0
                is_last = blk == n_blocks - 1
                table = jnp.where(is_first, jnp.where(is_last, 3, 0), jnp.where(is_last, 2, 1))
                s_loc = _nt_dot(qs, k_band[j * T:(j + 3) * T]) + bias_ref[table]
                m = jnp.maximum(m, jnp.max(s_loc, axis=-1, keepdims=True))
            o = _dot(jnp.exp((s_ctx - m).astype(BF16)), vx)
            if local:
                o = o + _dot(jnp.exp((s_loc - m).astype(BF16)), v_band[j * T:(j + 3) * T])
            denom = o[:, HEAD_DIM:HEAD_DIM + 1] + jnp.exp(sink - m)
            o = o[:, :HEAD_DIM] * (1.0 / denom)
            for g in range(G):
                acc_ref[rows, (h * G + g) * HEAD_DIM:(h * G + g + 1) * HEAD_DIM] = o[g * T:(g + 1) * T]
    y = acc_ref[...]
    y = y * lax.rsqrt(jnp.mean(y * y, axis=-1, keepdims=True) + EPS) * g_ref[...]
    o_ref[...] = y.astype(o_ref.dtype)


def _attention(q, kv, kv_ctx, sink, g_attn):
    B, L, _ = q.shape
    C = kv_ctx.shape[1]
    local = kv is not None
    T = BLOCK
    nb = L // T
    n_sub = 4 if nb % 4 == 0 else (2 if nb % 2 == 0 else 1)
    sink_row = jnp.repeat(sink.astype(F32), HEAD_DIM).reshape(1, ATT_WIDTH)
    qspec = pl.BlockSpec((None, n_sub * T, ATT_WIDTH), lambda b, i: (b, i, 0))
    in_specs = [qspec]
    args = [q]
    cspecs = [pl.BlockSpec((None, C, KV_WIDTH), functools.partial(lambda b, i, part: (b, 0, part), part=part))
              for part in range(2)]
    row = pl.BlockSpec((1, ATT_WIDTH), lambda b, i: (0, 0))
    if local:
        for part in range(2):
            in_specs += [
                pl.BlockSpec((None, T, KV_WIDTH), functools.partial(
                    lambda b, i, part: (b, jnp.maximum(i * n_sub - 1, 0), part), part=part)),
                pl.BlockSpec((None, n_sub * T, KV_WIDTH), functools.partial(
                    lambda b, i, part: (b, i, part), part=part)),
                pl.BlockSpec((None, T, KV_WIDTH), functools.partial(
                    lambda b, i, part: (b, jnp.minimum((i + 1) * n_sub, nb - 1), part), part=part))]
        bias = _attn_bias_tables()
        in_specs += cspecs + [pl.BlockSpec(bias.shape, lambda b, i: (0, 0, 0))]
        args += [kv] * 6 + [kv_ctx, kv_ctx, bias]
    else:
        in_specs += cspecs
        args += [kv_ctx, kv_ctx]
    in_specs += [row, row]
    args += [sink_row, g_attn.reshape(1, ATT_WIDTH)]
    return pl.pallas_call(
        functools.partial(_attn_kernel, local=local, n_blocks=nb),
        out_shape=jax.ShapeDtypeStruct((B, L, ATT_WIDTH), BF16),
        grid=(B, nb // n_sub),
        in_specs=in_specs,
        out_specs=qspec,
        scratch_shapes=[pltpu.VMEM((n_sub * T, ATT_WIDTH), F32)],
        compiler_params=_cp(("parallel", "arbitrary")),
        name="attention_local" if local else "attention_context",
    )(*args)


def _halo_specs(ts, L, width, col_block):
    nb8 = L // HALO
    per = ts // HALO
    prev = pl.BlockSpec((None, HALO, width),
                        lambda b, i: (b, jnp.maximum(i * per - 1, 0), col_block))
    cur = pl.BlockSpec((None, ts, width), lambda b, i: (b, i, col_block))
    nxt = pl.BlockSpec((None, HALO, width),
                       lambda b, i: (b, jnp.minimum((i + 1) * per, nb8 - 1), col_block))
    return [prev, cur, nxt]


def _with_halo(prev_ref, cur_ref, next_ref):
    i = pl.program_id(1)
    last = pl.num_programs(1) - 1
    prev = jnp.where(i > 0, prev_ref[...], 0.0)
    nxt = jnp.where(i < last, next_ref[...], 0.0)
    return jnp.concatenate([prev, cur_ref[...], nxt], axis=0)


def _pool_kernel(prev_ref, cur_ref, next_ref, w_ref, scale_ref, g_ref, o_ref, *, seq_len):
    ts = cur_ref.shape[0]
    ext = _with_halo(prev_ref, cur_ref, next_ref)
    pos = pl.program_id(1) * ts + lax.broadcasted_iota(jnp.int32, (ts, 1), 0)
    outs = []
    for gidx, win in enumerate(POOL_WINDOWS):
        cols = slice(gidx * POOL_GROUP, (gidx + 1) * POOL_GROUP)
        run = ext[:, cols]
        step = 1
        while step < win:
            run = run[:run.shape[0] - step] + run[step:]
            step *= 2
        lo = HALO - win // 2
        total = run[lo:lo + ts]
        cnt = (jnp.minimum(pos + (win - win // 2), seq_len) - jnp.maximum(pos - win // 2, 0))
        mean = total * (1.0 / cnt.astype(F32))
        outs.append(_dot3(mean - cur_ref[:, cols], w_ref[gidx]))
    y = jnp.concatenate(outs, axis=-1) * scale_ref[...]
    y = y * lax.rsqrt(jnp.mean(y * y, axis=-1, keepdims=True) + EPS) * g_ref[...]
    o_ref[...] = y.astype(o_ref.dtype)


def _pool_mixer(proj, pool_w, pool_scale, g_pool):
    B, L, _ = proj.shape
    ts = _tile(L, 512)
    col_block = 0
    row = pl.BlockSpec((1, POOL_WIDTH), lambda b, i: (0, 0))
    return pl.pallas_call(
        functools.partial(_pool_kernel, seq_len=L),
        out_shape=jax.ShapeDtypeStruct((B, L, POOL_WIDTH), BF16),
        grid=(B, L // ts),
        in_specs=_halo_specs(ts, L, POOL_WIDTH, col_block) + [
            pl.BlockSpec(pool_w.shape, lambda b, i: (0, 0, 0)), row, row],
        out_specs=pl.BlockSpec((None, ts, POOL_WIDTH), lambda b, i: (b, i, 0)),
        compiler_params=_cp(("parallel", "arbitrary")),
        name="pool_mixer",
    )(proj, proj, proj, pool_w, pool_scale.reshape(1, POOL_WIDTH), g_pool.reshape(1, POOL_WIDTH))


def _short_conv_kernel(*refs):
    halo_refs = refs[:9]
    w_ref, b_ref = refs[9:11]
    out_refs = refs[11:]
    ts = out_refs[0].shape[0]
    for part in range(HY_ORDER + 1):
        ext = _with_halo(*halo_refs[3 * part:3 * part + 3])
        cols = slice(part * HY_WIDTH, (part + 1) * HY_WIDTH)
        acc = b_ref[:, cols] + ext[HALO - 1:HALO - 1 + ts] * w_ref[0:1, cols]
        acc = acc + ext[HALO:HALO + ts] * w_ref[1:2, cols]
        acc = acc + ext[HALO + 1:HALO + 1 + ts] * w_ref[2:3, cols]
        out_refs[part][...] = acc


def _short_conv(proj, conv_w, conv_b):
    B, L, _ = proj.shape
    ts = _tile(L, 512)
    first = POOL_WIDTH // HY_WIDTH
    assert first * HY_WIDTH == POOL_WIDTH
    specs = []
    for part in range(HY_ORDER + 1):
        specs += _halo_specs(ts, L, HY_WIDTH, first + part)
    n_col = (HY_ORDER + 1) * HY_WIDTH
    out = pl.BlockSpec((None, ts, HY_WIDTH), lambda b, i: (b, i, 0))
    return pl.pallas_call(
        _short_conv_kernel,
        out_shape=tuple(jax.ShapeDtypeStruct((B, L, HY_WIDTH), F32) for _ in range(HY_ORDER + 1)),
        grid=(B, L // ts),
        in_specs=specs + [pl.BlockSpec((HY_SHORT, n_col), lambda b, i: (0, 0)),
                          pl.BlockSpec((1, n_col), lambda b, i: (0, 0))],
        out_specs=(out,) * (HY_ORDER + 1),
        compiler_params=_cp(("parallel", "arbitrary")),
        name="hyena_short_conv",
    )(*([proj] * 9), conv_w, conv_b.reshape(1, n_col))


def _filter_tables(L):
    m = np.arange(L, dtype=np.float32)
    t = (m / np.float32(max(L - 1, 1))).astype(np.float32)
    w = (np.float32(2.0 * math.pi) * m / np.float32(L)).astype(np.float32)
    f = np.linspace(1e-4, HY_BANDS - 1, HY_BANDS, dtype=np.float32)
    ang = (w[:, None] * f).astype(np.float64)
    z = np.concatenate([t[:, None].astype(np.float64), np.cos(ang), -np.sin(ang)], axis=-1)
    z = np.pad(z, ((0, 0), (0, LANES - HY_EMB)))
    backward = z[(L - np.arange(L)) % L]
    max_decay = math.log(HY_TARGET) / HY_SHORT_DECAY_PCT
    min_decay = math.log(HY_TARGET) / HY_LONG_DECAY_PCT
    deltas = np.abs(np.linspace(min_decay, max_decay, HY_WIDTH, dtype=np.float32))
    return jnp.asarray(np.concatenate([z, backward]), F32), jnp.asarray(deltas[None], F32)


def _filter_kernel(z_ref, rate_ref, w1_ref, b1_ref, w2_ref, b2_ref, w3a_ref, w3b_ref,
                   b3a_ref, b3b_ref, fr_ref, o_ref, *, seq_len, period):
    tl = z_ref.shape[0]
    n0 = pl.program_id(0) * tl
    active = (n0 < seq_len) | (n0 >= period - seq_len)

    @pl.when(active)
    def _():
        n = n0 + lax.broadcasted_iota(jnp.int32, (tl, 1), 0)
        lag = jnp.where(n < seq_len, n, period - n)
        t = lag.astype(F32) / float(max(seq_len - 1, 1))
        decay = jnp.exp(-t * rate_ref[...])
        fr = fr_ref[...]
        a = jnp.sin(fr * (_dot3(z_ref[...], w1_ref[...]) + b1_ref[...]))
        a = jnp.sin(fr * (_dot3(a, w2_ref[...]) + b2_ref[...]))
        taps = [(_dot3(a, w3_ref[...]) + b3_ref[...]) * decay
                for w3_ref, b3_ref in ((w3a_ref, b3a_ref), (w3b_ref, b3b_ref))]
        o_ref[...] = jnp.where(n == period - seq_len, 0.0, jnp.concatenate(taps, axis=1))

    @pl.when(jnp.logical_not(active))
    def _():
        o_ref[...] = jnp.zeros(o_ref.shape, o_ref.dtype)


def _hyena_filter_taps(L, period, w1, b1, w2, b2, w3, b3, freq):
    z, rates = _filter_tables(L)
    hid = w1.shape[1]
    tl = _tile(L, 512)
    nf = L // tl
    nt = period // tl
    assert HY_ORDER == 2 and period % tl == 0 and period >= 2 * L
    w1p = jnp.pad(w1, ((0, LANES - HY_EMB), (0, 0)))
    b3 = b3.reshape(1, -1)
    full = lambda a: pl.BlockSpec(a.shape, lambda i: (0,) * a.ndim)
    is_bwd = lambda i: jnp.where(i >= nt - nf, 1, 0)
    ztile = lambda i: (jnp.where(i < nf, i, nf + jnp.clip(i - (nt - nf), 0, nf - 1)), 0)
    w3spec = lambda o: pl.BlockSpec((hid, HY_WIDTH), lambda i: (0, 2 * o + is_bwd(i)))
    b3spec = lambda o: pl.BlockSpec((1, HY_WIDTH), lambda i: (0, 2 * o + is_bwd(i)))
    small = [w1p, b1.reshape(1, hid), w2, b2.reshape(1, hid)]
    return pl.pallas_call(
        functools.partial(_filter_kernel, seq_len=L, period=period),
        out_shape=jax.ShapeDtypeStruct((period, HY_ORDER * HY_WIDTH), F32),
        grid=(nt,),
        in_specs=[pl.BlockSpec((tl, LANES), ztile), full(rates)] + [full(a) for a in small]
                 + [w3spec(0), w3spec(1), b3spec(0), b3spec(1), pl.BlockSpec((1, hid), lambda i: (0, 0))],
        out_specs=pl.BlockSpec((tl, HY_ORDER * HY_WIDTH), lambda i: (i, 0)),
        compiler_params=_cp(("arbitrary",)),
        name="hyena_filter_taps",
    )(z, rates, *small, w3, w3, b3, b3, freq.reshape(1, hid))


def _fft_matrices(n1, a_in):
    n = n1 * LANES
    k1 = np.arange(n1)
    a = np.arange(a_in)
    th1 = 2.0 * np.pi * np.outer(k1, a) / n1
    c1, s1 = np.cos(th1), np.sin(th1)
    m1_complex = np.block([[c1, s1], [-s1, c1]])
    th_full = 2.0 * np.pi * np.outer(k1, k1) / n1
    m1_real = np.concatenate([np.cos(th_full), -np.sin(th_full)], axis=0)
    m3 = np.block([[c1.T, -s1.T], [s1.T, c1.T]])
    b = np.arange(LANES)
    k2 = np.arange(LANES)
    k = k1[:, None, None] + n1 * k2[None, :, None]
    th2 = 2.0 * np.pi * (k * b[None, None, :] % n) / n
    c2, s2 = np.cos(th2), np.sin(th2)
    fwd = np.concatenate([np.concatenate([c2, s2], axis=2),
                          np.concatenate([-s2, c2], axis=2)], axis=1)
    c2t, s2t = np.swapaxes(c2, 1, 2), np.swapaxes(s2, 1, 2)
    inv = np.concatenate([np.concatenate([c2t, -s2t], axis=2),
                          np.concatenate([s2t, c2t], axis=2)], axis=1)
    as_bf16 = lambda m: jnp.asarray(m, F32).astype(BF16)
    return as_bf16(m1_complex), as_bf16(m1_real), as_bf16(m3), as_bf16(fwd), as_bf16(inv)


def _fft_stage1_kernel(x_ref, m_ref, o_ref, *, complex_in):
    n1 = o_ref.shape[0]
    for s in range(o_ref.shape[1]):
        if complex_in:
            x = jnp.concatenate([x_ref[0, :, s, :], x_ref[1, :, s, :]], axis=0)
        else:
            x = x_ref[:, s, :]
        y = _dot(m_ref[...], x.astype(BF16))
        o_ref[:, s, :] = _pack_bf16(y[:n1], y[n1:])


def _fft_stage1(x, m1, n1, complex_in):
    P, A, Wt = x.shape[0], x.shape[-3], x.shape[-1]
    tw = _tile(Wt, 512, LANES)
    if complex_in:
        xspec = pl.BlockSpec((None, 2, A, FFT_ROWS, tw), lambda p, i, j: (p, 0, 0, i, j))
    else:
        xspec = pl.BlockSpec((None, A, FFT_ROWS, tw), lambda p, i, j: (p, 0, i, j))
    return pl.pallas_call(
        functools.partial(_fft_stage1_kernel, complex_in=complex_in),
        out_shape=jax.ShapeDtypeStruct((P, n1, LANES, Wt), jnp.uint32),
        grid=(P, LANES // FFT_ROWS, Wt // tw),
        in_specs=[xspec, pl.BlockSpec(m1.shape, lambda p, i, j: (0, 0))],
        out_specs=pl.BlockSpec((None, n1, FFT_ROWS, tw), lambda p, i, j: (p, 0, i, j)),
        compiler_params=_cp(("parallel", "arbitrary", "arbitrary")),
        name="fft_stage1",
    )(x, m1)


def _filter_spectrum_kernel(y_ref, g_ref, o0_ref, o1_ref, *, scale):
    kt = y_ref.shape[0]
    for t in range(kt):
        y = jnp.concatenate(_unpack_bf16_pairs(y_ref[t]), axis=0)
        f = _dot(g_ref[t], y)
        for o, o_ref in enumerate((o0_ref, o1_ref)):
            cols = slice(o * HY_WIDTH, (o + 1) * HY_WIDTH)
            o_ref[0, t] = (f[:LANES, cols] * scale).astype(o_ref.dtype)
            o_ref[1, t] = (f[LANES:, cols] * scale).astype(o_ref.dtype)


def _filter_spectrum(taps, mats, n1):
    n_col = taps.shape[1]
    _, m1_real, _, g_fwd, _ = mats
    y = _fft_stage1(taps.reshape(1, n1, LANES, n_col), m1_real, n1, complex_in=False)
    kt = _tile(n1, 8, 1)
    out = jax.ShapeDtypeStruct((2, n1, LANES, HY_WIDTH), BF16)
    ospec = pl.BlockSpec((2, kt, LANES, HY_WIDTH), lambda i: (0, i, 0, 0))
    return pl.pallas_call(
        functools.partial(_filter_spectrum_kernel, scale=1.0 / (n1 * LANES)),
        out_shape=(out, out),
        grid=(n1 // kt,),
        in_specs=[pl.BlockSpec((None, kt, LANES, n_col), lambda i: (0, i, 0, 0)),
                  pl.BlockSpec((kt, 2 * LANES, 2 * LANES), lambda i: (i, 0, 0))],
        out_specs=(ospec, ospec),
        compiler_params=_cp(("arbitrary",)),
        name="hyena_filter_spectrum",
    )(y, g_fwd)


def _fft_mid_kernel(y_ref, kf_ref, g_ref, gi_ref, o_ref):
    kt = y_ref.shape[0]
    for t in range(kt):
        y = jnp.concatenate(_unpack_bf16_pairs(y_ref[t]), axis=0)
        f = _dot(g_ref[t], y)
        fr, fi = f[:LANES], f[LANES:]
        kr, ki = kf_ref[0, t].astype(F32), kf_ref[1, t].astype(F32)
        p = jnp.concatenate([fr * kr - fi * ki, fr * ki + fi * kr], axis=0).astype(BF16)
        u = _dot(gi_ref[t], p)
        o_ref[t] = _pack_bf16(u[:LANES], u[LANES:])


def _fft_mid(y, kf, g_fwd, g_inv):
    P, n1, _, W = y.shape
    kt = _tile(n1, 8, 1)
    yspec = pl.BlockSpec((None, kt, LANES, W), lambda i, p: (p, i, 0, 0))
    gspec = pl.BlockSpec((kt, 2 * LANES, 2 * LANES), lambda i, p: (i, 0, 0))
    return pl.pallas_call(
        _fft_mid_kernel,
        out_shape=jax.ShapeDtypeStruct(y.shape, jnp.uint32),
        grid=(n1 // kt, P),
        in_specs=[yspec, pl.BlockSpec((2, kt, LANES, W), lambda i, p: (0, i, 0, 0)), gspec, gspec],
        out_specs=yspec,
        compiler_params=_cp(("arbitrary", "arbitrary")),
        name="fft_mid",
    )(y, kf, g_fwd, g_inv)


def _fft_stage3_kernel(u_ref, m_ref, z_ref, x_ref, bias_ref, g_ref, o_ref, ur_ref, ui_ref,
                       *, normalise):
    a_out = z_ref.shape[1]
    bias = bias_ref[...]
    packed = u_ref[...]
    ur_ref[...] = lax.bitcast_convert_type(packed << 16, F32)
    ui_ref[...] = lax.bitcast_convert_type(packed & jnp.uint32(0xFFFF0000), F32)
    for s in range(o_ref.shape[2]):
        u = jnp.concatenate([ur_ref[:, s, :], ui_ref[:, s, :]], axis=0).astype(BF16)
        conv = _dot(m_ref[...], u)
        for part in range(2):
            y = x_ref[part, :, s, :] * (
                conv[part * a_out:(part + 1) * a_out] + z_ref[part, :, s, :] * bias)
            if normalise:
                y = y * lax.rsqrt(jnp.mean(y * y, axis=-1, keepdims=True) + EPS) * g_ref[...]
            o_ref[part, :, s, :] = y


def _fft_stage3(u, m3, z, gate, bias_row, norm_g=None):
    P, n1, _, W = u.shape
    A = z.shape[2]
    pair = pl.BlockSpec((None, 2, A, FFT_ROWS, W), lambda p, i: (p, 0, 0, i, 0))
    row = pl.BlockSpec((1, W), lambda p, i: (0, 0))
    normalise = norm_g is not None
    return pl.pallas_call(
        functools.partial(_fft_stage3_kernel, normalise=normalise),
        out_shape=jax.ShapeDtypeStruct(z.shape, F32),
        grid=(P, LANES // FFT_ROWS),
        in_specs=[pl.BlockSpec((None, n1, FFT_ROWS, W), lambda p, i: (p, 0, i, 0)),
                  pl.BlockSpec(m3.shape, lambda p, i: (0, 0)),
                  pair, pair, row, row],
        out_specs=pair,
        scratch_shapes=[pltpu.VMEM((n1, FFT_ROWS, W), F32), pltpu.VMEM((n1, FFT_ROWS, W), F32)],
        compiler_params=_cp(("parallel", "arbitrary")),
        name="fft_stage3",
    )(u, m3, z, gate, bias_row, norm_g.reshape(1, W) if normalise else bias_row)


def _hyena_mixer(v, gates, kfs, d_bias, norm_g, mats, n1):
    B, L, W = v.shape
    pair_shape = (B // 2, 2, L // LANES, LANES, W)
    m1_complex, _, m3, g_fwd, g_inv = mats
    z = v.reshape(pair_shape)
    for o in range(HY_ORDER):
        y = _fft_stage1(z, m1_complex, n1, complex_in=True)
        u = _fft_mid(y, kfs[o], g_fwd, g_inv)
        z = _fft_stage3(u, m3, z, gates[o].reshape(pair_shape), d_bias[o].reshape(1, W),
                        norm_g if o == HY_ORDER - 1 else None)
    return z.reshape(B, L, W)


def kernel(x, c, ctx, c_ctx, w_mod, b_mod, norm1_g, w_in, attn_sink, pool_w, pool_scale,
           hy_conv_w, hy_conv_b, hy_f_w1, hy_f_b1, hy_f_w2, hy_f_b2, hy_f_w3, hy_f_b3,
           hy_f_freq, hy_bias, g_attn, g_pool, g_hyena, w_out, norm2_g,
           ff_w1, ff_w3, ff_w2, router_w, moe_w1, moe_w3, moe_w2, final_g):
    B, S, D = x.shape
    C = ctx.shape[1]
    depth = w_mod.shape[0]
    assert B % 2 == 0 and S % BLOCK == 0
    fft_mats = {}

    c_rows = jnp.concatenate([c, c_ctx[None], jnp.zeros((SUBLANES - B - 1, D), F32)], axis=0)
    xc = ctx

    def hyena_branch(l, proj, seq_len):
        rows = -(-seq_len // FFT_MIN_ROWS) * FFT_MIN_ROWS
        n1 = 2 * rows // LANES
        if rows not in fft_mats:
            fft_mats[rows] = _fft_matrices(n1, rows // LANES)
        mats = fft_mats[rows]
        taps = _hyena_filter_taps(seq_len, 2 * rows, hy_f_w1[l], hy_f_b1[l], hy_f_w2[l], hy_f_b2[l],
                                  hy_f_w3[l], hy_f_b3[l], hy_f_freq[l])
        v, x1, x2 = _short_conv(proj, hy_conv_w[l], hy_conv_b[l])
        if seq_len < rows:
            pad = lambda t: jnp.pad(t, ((0, 0), (0, rows - seq_len), (0, 0)))
            v, x1, x2 = pad(v), pad(x1), pad(x2)
        kfs = _filter_spectrum(taps, mats, n1)
        z = _hyena_mixer(v, (x1, x2), kfs, hy_bias[l], g_hyena[l], mats, n1)
        return z[:, :seq_len]

    def mix(l, proj, y_att, res, gate, w_out_l, seq_len, norm=None):
        y_pool = _pool_mixer(proj, pool_w[l], pool_scale[l], g_pool[l])
        y_hy = hyena_branch(l, proj, seq_len)
        return _mix_out([y_att, y_pool, y_hy], w_out_l, res, gate, norm)

    for l in range(depth):
        last = l == depth - 1
        mod = _adaln(c_rows, w_mod, b_mod, l)
        sh1, sc1, g1, sh2, sc2, g2 = [mod[:B, j * D:(j + 1) * D] for j in range(6)]
        csh1, csc1, cg1, csh2, csc2, cg2 = [mod[B:B + 1, j * D:(j + 1) * D] for j in range(6)]
        w_in_l = w_in[l].astype(BF16)
        w_out_l = w_out[l].astype(BF16)

        if last:
            hc = _norm_mod(xc, norm1_g[l], csh1, csc1)
            kv_w = w_in_l[:, ATT_WIDTH:ATT_WIDTH + 2 * KV_WIDTH]
            kv_c = _matmul(hc, kv_w, out_dtype=BF16)
        else:
            q_c, kv_c, proj_c = _in_proj(xc, norm1_g[l], csh1, csc1, w_in_l, rope=False)
            y_att_c = _attention(q_c, None, kv_c, attn_sink[l], g_attn[l])
            xc_new, hc2 = mix(l, proj_c, y_att_c, xc, cg1, w_out_l, C, (norm2_g[l], csh2, csc2))
            i = l // 2
            if l % 2 == 0:
                hid_c = _swiglu_hidden(hc2, ff_w1[i].astype(BF16), ff_w3[i].astype(BF16))
                xc_new = _matmul_gated_residual([hid_c], ff_w2[i].astype(BF16), xc_new, cg2)
            else:
                raise NotImplementedError("context tokens through an expert layer")

        q, kv, proj = _in_proj(x, norm1_g[l], sh1, sc1, w_in_l, rope=True)
        y_att = _attention(q, kv, kv_c, attn_sink[l], g_attn[l])

        i = l // 2
        if l % 2 == 0:
            x, h2 = mix(l, proj, y_att, x, g1, w_out_l, S, (norm2_g[l], sh2, sc2))
            hid = _swiglu_hidden(h2, ff_w1[i].astype(BF16), ff_w3[i].astype(BF16))
            x = _matmul_gated_residual([hid], ff_w2[i].astype(BF16), x, g2)
            if last:
                x = _rmsnorm(x, final_g, F32)
        else:
            assert last
            x = mix(l, proj, y_att, x, g1, w_out_l, S)
            h2, gi, gw, counts = _norm_mod_router(x, norm2_g[l], sh2, sc2, router_w[i])
            x = _moe_layer(x, h2, gi, gw, counts, g2, final_g, moe_w1[i], moe_w3[i], moe_w2[i])
        if not last:
            xc = xc_new
    return x
```

```python
import functools
import math

import numpy as np
import jax
import jax.numpy as jnp
from jax import lax
from jax.experimental import pallas as pl
from jax.experimental.pallas import tpu as pltpu

F32 = jnp.float32
BF16 = jnp.bfloat16

EPS = 1e-6
NEG = -1e30
GRID_W = 64
ATT_HEADS = 8
ATT_KV_HEADS = 2
ATT_GROUP = ATT_HEADS // ATT_KV_HEADS
HEAD_DIM = 128
ATT_WIDTH = ATT_HEADS * HEAD_DIM
KV_WIDTH = ATT_KV_HEADS * HEAD_DIM
WINDOW = 128
BLOCK = 128
ROPE_BASE = 10000.0
POOL_WINDOWS = (2, 4, 8, 16)
POOL_GROUP = 128
POOL_WIDTH = POOL_GROUP * len(POOL_WINDOWS)
HY_WIDTH = 512
HY_ORDER = 2
HY_SHORT = 3
HY_EMB = 33
HY_BANDS = (HY_EMB - 1) // 2
HY_SHORT_DECAY_PCT = 0.3
HY_LONG_DECAY_PCT = 1.5
HY_TARGET = 1e-2
N_EXPERTS = 8
TOP_K = 2

LANES = 128
SUBLANES = 8
HALO = SUBLANES
FFT_MIN_ROWS = 2048
FFT_ROWS = 2 * SUBLANES
MOE_OUT_K_CHUNKS = 4
VMEM_LIMIT = 56 * 1024 * 1024


def _cp(sem):
    return pltpu.CompilerParams(dimension_semantics=sem, vmem_limit_bytes=VMEM_LIMIT)


def _tile(n, pref, mult=SUBLANES):
    if n <= pref:
        return n
    t = (pref // mult) * mult
    while t >= mult:
        if n % t == 0:
            return t
        t -= mult
    return n


def _split_bf16(x):
    hi = x.astype(BF16)
    lo = (x - hi.astype(F32)).astype(BF16)
    return hi, lo


def _dot(a, b):
    return jnp.dot(a, b, preferred_element_type=F32)


def _pack_bf16(lo, hi):
    lo_bits = lax.bitcast_convert_type(lo.astype(BF16).astype(F32), jnp.uint32)
    hi_bits = lax.bitcast_convert_type(hi.astype(BF16).astype(F32), jnp.uint32)
    return (lo_bits >> 16) | (hi_bits & jnp.uint32(0xFFFF0000))


def _pack_bf16_pairs(x):
    n = x.shape[-1] // 2
    return _pack_bf16(x[:, :n], x[:, n:])


def _unpack_bf16_pairs(u):
    lo = lax.bitcast_convert_type(u << 16, F32).astype(BF16)
    hi = lax.bitcast_convert_type(u & jnp.uint32(0xFFFF0000), F32).astype(BF16)
    return lo, hi


def _dot3(a, b):
    ah, al = _split_bf16(a)
    bh, bl = _split_bf16(b)
    return _dot(ah, bh) + (_dot(ah, bl) + _dot(al, bh))


def _adaln_kernel(c_ref, w_ref, b_ref, o_ref):
    c = c_ref[...]
    a = c * (1.0 / (1.0 + jnp.exp(-c)))
    o_ref[...] = _dot3(a, w_ref[...]) + b_ref[...]


def _adaln(c_rows, w_all, b_all, layer):
    R, D = c_rows.shape
    N = w_all.shape[2]
    tn = _tile(N, 1536, LANES)
    return pl.pallas_call(
        _adaln_kernel,
        out_shape=jax.ShapeDtypeStruct((R, N), F32),
        grid=(N // tn,),
        in_specs=[pl.BlockSpec((R, D), lambda j: (0, 0)),
                  pl.BlockSpec((None, D, tn), lambda j: (layer, 0, j)),
                  pl.BlockSpec((None, 1, tn), lambda j: (layer, 0, j))],
        out_specs=pl.BlockSpec((R, tn), lambda j: (0, j)),
        compiler_params=_cp(("arbitrary",)),
        name="adaln",
    )(c_rows, w_all, b_all[:, None, :])


def _norm_mod_kernel(x_ref, g_ref, sh_ref, sc_ref, o_ref):
    x = x_ref[...]
    y = x * lax.rsqrt(jnp.mean(x * x, axis=-1, keepdims=True) + EPS) * g_ref[...]
    o_ref[...] = (y * (1.0 + sc_ref[...]) + sh_ref[...]).astype(o_ref.dtype)


def _bcast_map(arr):
    if arr.shape[0] == 1:
        return lambda b, i: (0, 0, 0)
    return lambda b, i: (b, 0, 0)


def _norm_mod(x, g, shift, scale):
    B, L, D = x.shape
    ts = _tile(L, 1024)
    shift = shift[:, None, :]
    scale = scale[:, None, :]
    return pl.pallas_call(
        _norm_mod_kernel,
        out_shape=jax.ShapeDtypeStruct((B, L, D), BF16),
        grid=(B, L // ts),
        in_specs=[pl.BlockSpec((None, ts, D), lambda b, i: (b, i, 0)),
                  pl.BlockSpec((1, D), lambda b, i: (0, 0)),
                  pl.BlockSpec((None, 1, D), _bcast_map(shift)),
                  pl.BlockSpec((None, 1, D), _bcast_map(scale))],
        out_specs=pl.BlockSpec((None, ts, D), lambda b, i: (b, i, 0)),
        compiler_params=_cp(("parallel", "parallel")),
        name="norm_mod",
    )(x, g.reshape(1, D), shift, scale)


def _route(h, rw_ref, tri_ref, run_ref, gi_ref, gw_ref, cnt_ref):
    logits = _dot3(h, rw_ref[...])
    lane = lax.broadcasted_iota(jnp.int32, logits.shape, 1)
    logits = jnp.where(lane < N_EXPERTS, logits, NEG)
    m1 = jnp.max(logits, axis=-1, keepdims=True)
    i1 = jnp.min(jnp.where(logits == m1, lane, LANES), axis=-1, keepdims=True)
    rest = jnp.where(lane == i1, NEG, logits)
    m2 = jnp.max(rest, axis=-1, keepdims=True)
    i2 = jnp.min(jnp.where(rest == m2, lane, LANES), axis=-1, keepdims=True)
    e2 = jnp.exp(m2 - m1)
    w1 = 1.0 / (1.0 + e2)
    w2 = e2 * w1
    pick1 = lane == i1
    pick2 = lane == i2
    both = jnp.where(pick1 | pick2, 1.0, 0.0)
    before = _dot(tri_ref[...], both.astype(BF16)) + run_ref[...]
    r1 = jnp.sum(jnp.where(pick1, before, 0.0), axis=-1, keepdims=True)
    r2 = jnp.sum(jnp.where(pick2, before, 0.0), axis=-1, keepdims=True)
    run_ref[...] = run_ref[...] + jnp.sum(both, axis=0, keepdims=True)
    cnt_ref[...] = run_ref[...]
    ids = jnp.where(lane == 0, i1.astype(F32), jnp.where(lane == 1, i2.astype(F32),
                    jnp.where(lane == 2, r1, jnp.where(lane == 3, r2, 0.0))))
    gi_ref[...] = jnp.transpose(ids)[:SUBLANES, :]
    gw_ref[...] = jnp.where(lane == 0, w1, jnp.where(lane == 1, w2, 0.0))


def _norm_mod_router_kernel(x_ref, g_ref, sh_ref, sc_ref, rw_ref, tri_ref,
                            o_ref, gi_ref, gw_ref, cnt_ref, run_ref):
    @pl.when((pl.program_id(0) == 0) & (pl.program_id(1) == 0))
    def _():
        run_ref[...] = jnp.zeros(run_ref.shape, run_ref.dtype)

    x = x_ref[...]
    y = x * lax.rsqrt(jnp.mean(x * x, axis=-1, keepdims=True) + EPS) * g_ref[...]
    h = y * (1.0 + sc_ref[...]) + sh_ref[...]
    o_ref[...] = _pack_bf16_pairs(h)
    _route(h, rw_ref, tri_ref, run_ref, gi_ref, gw_ref, cnt_ref)


def _norm_mod_router(x, g, shift, scale, router_w):
    B, L, D = x.shape
    ts = _tile(L, 512)
    shift = shift[:, None, :]
    scale = scale[:, None, :]
    rw = jnp.pad(router_w, ((0, 0), (0, LANES - router_w.shape[1])))
    tri = jnp.asarray(np.tril(np.ones((ts, ts), np.float32), -1), BF16)
    row = pl.BlockSpec((None, ts, D), lambda b, i: (b, i, 0))
    small = pl.BlockSpec((None, ts, LANES), lambda b, i: (b, i, 0))
    return pl.pallas_call(
        _norm_mod_router_kernel,
        out_shape=(jax.ShapeDtypeStruct((B, L, D // 2), jnp.uint32),
                   jax.ShapeDtypeStruct((B, SUBLANES, L), F32),
                   jax.ShapeDtypeStruct((B, L, LANES), F32),
                   jax.ShapeDtypeStruct((1, LANES), F32)),
        grid=(B, L // ts),
        in_specs=[row,
                  pl.BlockSpec((1, D), lambda b, i: (0, 0)),
                  pl.BlockSpec((None, 1, D), _bcast_map(shift)),
                  pl.BlockSpec((None, 1, D), _bcast_map(scale)),
                  pl.BlockSpec((D, LANES), lambda b, i: (0, 0)),
                  pl.BlockSpec((ts, ts), lambda b, i: (0, 0))],
        out_specs=(pl.BlockSpec((None, ts, D // 2), lambda b, i: (b, i, 0)),
                   pl.BlockSpec((None, SUBLANES, ts), lambda b, i: (b, 0, i)), small,
                   pl.BlockSpec((1, LANES), lambda b, i: (0, 0))),
        scratch_shapes=[pltpu.VMEM((1, LANES), F32)],
        compiler_params=_cp(("arbitrary", "arbitrary")),
        name="norm_mod_router",
    )(x, g.reshape(1, D), shift, scale, rw, tri)


def _mix_out_kernel(*refs, n_a, emit_h):
    a_refs = refs[:n_a]
    w_refs = refs[n_a:2 * n_a]
    res_ref, gate_ref = refs[2 * n_a:2 * n_a + 2]
    acc = _dot(a_refs[0][...].astype(BF16), w_refs[0][...])
    for a_ref, w_ref in zip(a_refs[1:], w_refs[1:]):
        acc = acc + _dot(a_ref[...].astype(BF16), w_ref[...])
    x = res_ref[...] + gate_ref[...] * acc
    if emit_h:
        g_ref, sh_ref, sc_ref, x_ref, h_ref = refs[2 * n_a + 2:]
        y = x * lax.rsqrt(jnp.mean(x * x, axis=-1, keepdims=True) + EPS) * g_ref[...]
        h_ref[...] = (y * (1.0 + sc_ref[...]) + sh_ref[...]).astype(h_ref.dtype)
    else:
        x_ref, = refs[2 * n_a + 2:]
    x_ref[...] = x


def _mix_out(a_list, w, res, gate, norm=None):
    B, L, D = res.shape
    tm = _tile(L, 512)
    emit_h = norm is not None
    gate = gate[:, None, :]
    in_specs = [pl.BlockSpec((None, tm, a.shape[-1]), lambda b, i: (b, i, 0)) for a in a_list]
    off = 0
    for a in a_list:
        k = a.shape[-1]
        assert off % k == 0
        in_specs.append(pl.BlockSpec((k, D), functools.partial(lambda b, i, blk: (blk, 0), blk=off // k)))
        off += k
    row = pl.BlockSpec((None, tm, D), lambda b, i: (b, i, 0))
    in_specs += [row, pl.BlockSpec((None, 1, D), _bcast_map(gate))]
    args = [*a_list, *([w] * len(a_list)), res, gate]
    out_shape = [jax.ShapeDtypeStruct((B, L, D), F32)]
    out_specs = [row]
    if emit_h:
        g, shift, scale = norm
        shift, scale = shift[:, None, :], scale[:, None, :]
        in_specs += [pl.BlockSpec((1, D), lambda b, i: (0, 0)),
                     pl.BlockSpec((None, 1, D), _bcast_map(shift)),
                     pl.BlockSpec((None, 1, D), _bcast_map(scale))]
        args += [g.reshape(1, D), shift, scale]
        out_shape.append(jax.ShapeDtypeStruct((B, L, D), BF16))
        out_specs.append(row)
    out = pl.pallas_call(
        functools.partial(_mix_out_kernel, n_a=len(a_list), emit_h=emit_h),
        out_shape=tuple(out_shape),
        grid=(B, L // tm),
        in_specs=in_specs,
        out_specs=tuple(out_specs),
        compiler_params=_cp(("parallel", "parallel")),
        name="mix_out",
    )(*args)
    return out if emit_h else out[0]


def _rmsnorm_kernel(x_ref, g_ref, o_ref):
    x = x_ref[...]
    y = x * lax.rsqrt(jnp.mean(x * x, axis=-1, keepdims=True) + EPS) * g_ref[...]
    o_ref[...] = y.astype(o_ref.dtype)


def _rmsnorm(x, g, out_dtype):
    B, L, D = x.shape
    ts = _tile(L, 512)
    return pl.pallas_call(
        _rmsnorm_kernel,
        out_shape=jax.ShapeDtypeStruct((B, L, D), out_dtype),
        grid=(B, L // ts),
        in_specs=[pl.BlockSpec((None, ts, D), lambda b, i: (b, i, 0)),
                  pl.BlockSpec((1, D), lambda b, i: (0, 0))],
        out_specs=pl.BlockSpec((None, ts, D), lambda b, i: (b, i, 0)),
        compiler_params=_cp(("parallel", "parallel")),
        name="rmsnorm",
    )(x, g.reshape(1, D))


def _mm_kernel(a_ref, w_ref, o_ref):
    o_ref[...] = _dot(a_ref[...], w_ref[...]).astype(o_ref.dtype)


def _matmul(a, w, out_dtype=F32, tm=1024, tn=512):
    B, L, K = a.shape
    N = w.shape[1]
    tm = _tile(L, tm)
    tn = _tile(N, tn, LANES)
    return pl.pallas_call(
        _mm_kernel,
        out_shape=jax.ShapeDtypeStruct((B, L, N), out_dtype),
        grid=(B, L // tm, N // tn),
        in_specs=[pl.BlockSpec((None, tm, K), lambda b, i, j: (b, i, 0)),
                  pl.BlockSpec((K, tn), lambda b, i, j: (0, j))],
        out_specs=pl.BlockSpec((None, tm, tn), lambda b, i, j: (b, i, j)),
        compiler_params=_cp(("parallel", "parallel", "arbitrary")),
        name="matmul",
    )(a, w)


def _mm_res_kernel(*refs, n_a):
    a_refs = refs[:n_a]
    w_refs = refs[n_a:2 * n_a]
    res_ref, gate_ref, o_ref = refs[2 * n_a:]
    acc = _dot(a_refs[0][...].astype(BF16), w_refs[0][...])
    for a_ref, w_ref in zip(a_refs[1:], w_refs[1:]):
        acc = acc + _dot(a_ref[...].astype(BF16), w_ref[...])
    o_ref[...] = res_ref[...] + gate_ref[...] * acc


def _matmul_gated_residual(a_list, w, res, gate, tm=1024, tn=512):
    B, L, N = res.shape
    tm = _tile(L, tm)
    tn = _tile(N, tn, LANES)
    gate = gate[:, None, :]
    widths = [a.shape[-1] for a in a_list]
    unit = math.gcd(*widths) if len(widths) > 1 else widths[0]
    in_specs = [pl.BlockSpec((None, tm, k), lambda b, i, j: (b, i, 0)) for k in widths]
    off = 0
    for k in widths:
        assert off % k == 0 and k % unit == 0
        in_specs.append(pl.BlockSpec((k, tn), functools.partial(
            lambda b, i, j, blk: (blk, j), blk=off // k)))
        off += k
    gmap = (lambda b, i, j: (0, 0, j)) if gate.shape[0] == 1 else (lambda b, i, j: (b, 0, j))
    in_specs += [pl.BlockSpec((None, tm, tn), lambda b, i, j: (b, i, j)),
                 pl.BlockSpec((None, 1, tn), gmap)]
    return pl.pallas_call(
        functools.partial(_mm_res_kernel, n_a=len(a_list)),
        out_shape=jax.ShapeDtypeStruct((B, L, N), F32),
        grid=(B, L // tm, N // tn),
        in_specs=in_specs,
        out_specs=pl.BlockSpec((None, tm, tn), lambda b, i, j: (b, i, j)),
        compiler_params=_cp(("parallel", "parallel", "arbitrary")),
        name="matmul_gated_residual",
    )(*a_list, *([w] * len(a_list)), res, gate)


def _silu(x):
    return x * (1.0 / (1.0 + jnp.exp(-x)))


def _swiglu_kernel(a_ref, w1_ref, w3_ref, o_ref):
    a = a_ref[...]
    o_ref[...] = (_silu(_dot(a, w1_ref[...])) * _dot(a, w3_ref[...])).astype(o_ref.dtype)


def _swiglu_hidden(a, w1, w3, tm=1024, tf=512):
    B, L, D = a.shape
    F = w1.shape[1]
    tm = _tile(L, tm)
    tf = _tile(F, tf, LANES)
    wspec = pl.BlockSpec((D, tf), lambda b, i, j: (0, j))
    return pl.pallas_call(
        _swiglu_kernel,
        out_shape=jax.ShapeDtypeStruct((B, L, F), BF16),
        grid=(B, L // tm, F // tf),
        in_specs=[pl.BlockSpec((None, tm, D), lambda b, i, j: (b, i, 0)), wspec, wspec],
        out_specs=pl.BlockSpec((None, tm, tf), lambda b, i, j: (b, i, j)),
        compiler_params=_cp(("parallel", "parallel", "arbitrary")),
        name="swiglu_hidden",
    )(a, w1, w3)


def _for_valid_rows(n_valid, tm, compute, o_ref):
    half_rows = tm // 2

    @pl.when(n_valid > half_rows)
    def _():
        compute(slice(0, tm))

    @pl.when((n_valid > 0) & (n_valid <= half_rows))
    def _():
        compute(slice(0, half_rows))
        o_ref[half_rows:, :] = jnp.zeros((tm - half_rows, o_ref.shape[1]), o_ref.dtype)

    @pl.when(n_valid == 0)
    def _():
        o_ref[...] = jnp.zeros(o_ref.shape, o_ref.dtype)


def _moe_hidden_kernel(te_ref, rows_ref, a_ref, w1_ref, w3_ref, o_ref, w1b_ref, w3b_ref):
    i = pl.program_id(1)
    n_valid = rows_ref[i]
    half = a_ref.shape[1]
    new_weights = (i == 0) | (te_ref[i] != te_ref[jnp.maximum(i - 1, 0)])

    @pl.when((n_valid > 0) & new_weights)
    def _():
        w1b_ref[...] = w1_ref[...].astype(BF16)
        w3b_ref[...] = w3_ref[...].astype(BF16)

    def compute(rows):
        lo, hi = _unpack_bf16_pairs(a_ref[rows, :])
        gate = _dot(lo, w1b_ref[:half, :]) + _dot(hi, w1b_ref[half:, :])
        up = _dot(lo, w3b_ref[:half, :]) + _dot(hi, w3b_ref[half:, :])
        o_ref[rows, :] = (_silu(gate) * up).astype(o_ref.dtype)

    _for_valid_rows(n_valid, a_ref.shape[0], compute, o_ref)


def _moe_out_kernel(te_ref, rows_ref, a_ref, w2_ref, o_ref):
    def compute(rows):
        kc = a_ref.shape[1] // MOE_OUT_K_CHUNKS
        acc = _dot(a_ref[rows, :kc], w2_ref[:kc, :].astype(BF16))
        for c in range(1, MOE_OUT_K_CHUNKS):
            acc = acc + _dot(a_ref[rows, c * kc:(c + 1) * kc],
                             w2_ref[c * kc:(c + 1) * kc, :].astype(BF16))
        o_ref[rows, :] = _pack_bf16_pairs(acc)

    _for_valid_rows(rows_ref[pl.program_id(0)], a_ref.shape[0], compute, o_ref)


def _moe_out_cols(d_model):
    return _tile(d_model, 512, 2 * LANES)


def _moe_experts(a_sorted, tile_expert, tile_rows, w1, w3, w2, tm, tf=512):
    R = a_sorted.shape[0]
    E, D, F = w1.shape
    tf = _tile(F, tf, LANES)
    tn = _moe_out_cols(D)
    nt = R // tm
    w13 = pl.BlockSpec((None, D, tf), lambda j, i, te, n: (te[i], 0, j))
    hidden = pl.pallas_call(
        _moe_hidden_kernel,
        out_shape=jax.ShapeDtypeStruct((R, F), BF16),
        grid_spec=pltpu.PrefetchScalarGridSpec(
            num_scalar_prefetch=2,
            grid=(F // tf, nt),
            in_specs=[pl.BlockSpec((tm, D // 2), lambda j, i, te, n: (i, 0)), w13, w13],
            out_specs=pl.BlockSpec((tm, tf), lambda j, i, te, n: (i, j)),
            scratch_shapes=[pltpu.VMEM((D, tf), BF16), pltpu.VMEM((D, tf), BF16)]),
        compiler_params=_cp(("arbitrary", "arbitrary")),
        name="moe_hidden",
    )(tile_expert, tile_rows, a_sorted, w1, w3)
    return pl.pallas_call(
        _moe_out_kernel,
        out_shape=jax.ShapeDtypeStruct((R, D // 2), jnp.uint32),
        grid_spec=pltpu.PrefetchScalarGridSpec(
            num_scalar_prefetch=2,
            grid=(nt, D // tn),
            in_specs=[pl.BlockSpec((tm, F), lambda i, j, te, n: (i, 0)),
                      pl.BlockSpec((None, F, tn), lambda i, j, te, n: (te[i], 0, j))],
            out_specs=pl.BlockSpec((tm, tn // 2), lambda i, j, te, n: (i, j))),
        compiler_params=_cp(("arbitrary", "arbitrary")),
        name="moe_out",
    )(tile_expert, tile_rows, hidden, w2)


def _unpack_column_blocks(u, block):
    parts = []
    for c in range(u.shape[1] // block):
        w = u[:, c * block:(c + 1) * block]
        parts.append(lax.bitcast_convert_type(w << 16, F32))
        parts.append(lax.bitcast_convert_type(w & jnp.uint32(0xFFFF0000), F32))
    return jnp.concatenate(parts, axis=1)


def _moe_combine_kernel(x_ref, ya_ref, yb_ref, gw_ref, gate_ref, g_ref, o_ref, *, block):
    gw = gw_ref[...]
    moe = (gw[:, 0:1] * _unpack_column_blocks(ya_ref[...], block)
           + gw[:, 1:2] * _unpack_column_blocks(yb_ref[...], block))
    x = x_ref[...] + gate_ref[...] * moe
    y = x * lax.rsqrt(jnp.mean(x * x, axis=-1, keepdims=True) + EPS) * g_ref[...]
    o_ref[...] = y


def _moe_combine_norm(x, ya, yb, gw, gate, g):
    B, L, D = x.shape
    ts = _tile(L, 512)
    row = pl.BlockSpec((None, ts, D), lambda b, i: (b, i, 0))
    packed = pl.BlockSpec((None, ts, D // 2), lambda b, i: (b, i, 0))
    return pl.pallas_call(
        functools.partial(_moe_combine_kernel, block=_moe_out_cols(D) // 2),
        out_shape=jax.ShapeDtypeStruct((B, L, D), F32),
        grid=(B, L // ts),
        in_specs=[row, packed, packed,
                  pl.BlockSpec((None, ts, LANES), lambda b, i: (b, i, 0)),
                  pl.BlockSpec((None, 1, D), lambda b, i: (b, 0, 0)),
                  pl.BlockSpec((1, D), lambda b, i: (0, 0))],
        out_specs=row,
        compiler_params=_cp(("parallel", "parallel")),
        name="moe_combine_norm",
    )(x, ya, yb, gw, gate[:, None, :], g.reshape(1, D))


def _moe_layer(x, h, gi, gw, counts, gate, final_g, w1, w3, w2):
    B, L, D = x.shape
    E = w1.shape[0]
    n_tok = B * L
    n_pair = n_tok * TOP_K
    tm = _tile(n_pair, 1024)
    nt = n_pair // tm + E
    rows_of = lambda r: jnp.concatenate(
        [gi[:, r + c, :].reshape(n_tok) for c in range(TOP_K)]).astype(jnp.int32)
    e_pair = rows_of(0)
    rank = rows_of(TOP_K)
    counts = counts[0, :E].astype(jnp.int32)
    tiles_per = (counts + tm - 1) // tm
    tile_end = jnp.cumsum(tiles_per)
    tile_start = tile_end - tiles_per
    row_start = jnp.zeros((n_pair,), jnp.int32)
    for e in range(E):
        row_start = jnp.where(e_pair == e, tile_start[e] * tm, row_start)
    pos = row_start + rank
    tile_ids = jnp.arange(nt, dtype=jnp.int32)
    tile_expert = jnp.minimum(
        jnp.sum((tile_ids[:, None] >= tile_end[None, :]).astype(jnp.int32), axis=1), E - 1)
    tile_rows = jnp.clip(counts[tile_expert] - (tile_ids - tile_start[tile_expert]) * tm, 0, tm)
    tile_rows = jnp.where(tile_ids < tile_end[-1], tile_rows, 0).astype(jnp.int32)
    tok_pair = jnp.arange(n_pair, dtype=jnp.int32) % n_tok
    src = (jnp.arange(nt * tm, dtype=jnp.int32) % n_tok).at[pos].set(
        tok_pair, unique_indices=True, mode="promise_in_bounds")
    gather_rows = lambda rows, idx: rows.at[idx].get(mode="promise_in_bounds")
    a_sorted = gather_rows(h.reshape(n_tok, h.shape[-1]), src)
    y_sorted = _moe_experts(a_sorted, tile_expert.astype(jnp.int32), tile_rows, w1, w3, w2, tm)
    ya = gather_rows(y_sorted, pos[:n_tok]).reshape(B, L, D // 2)
    yb = gather_rows(y_sorted, pos[n_tok:]).reshape(B, L, D // 2)
    return _moe_combine_norm(x, ya, yb, gw, gate, final_g)


def _rope_tables(n_tokens):
    pos = np.arange(n_tokens)
    row = (pos // GRID_W).astype(np.float32)
    col = (pos % GRID_W).astype(np.float32)
    n_freq = HEAD_DIM // 4
    inv = (np.float32(ROPE_BASE) ** (-np.arange(n_freq, dtype=np.float32) / np.float32(n_freq)))
    ang_r = (row[:, None] * inv).astype(np.float64)
    ang_c = (col[:, None] * inv).astype(np.float64)
    cos = np.concatenate([np.cos(ang_r), np.cos(ang_r), np.cos(ang_c), np.cos(ang_c)], axis=1)
    sin = np.concatenate([-np.sin(ang_r), np.sin(ang_r), -np.sin(ang_c), np.sin(ang_c)], axis=1)
    return jnp.asarray(cos, F32), jnp.asarray(sin, F32)


def _swap_halves(x):
    n = x.shape[-1]
    quarter = HEAD_DIM // 4
    lane = lax.broadcasted_iota(jnp.int32, x.shape, 1)
    up = pltpu.roll(x, n - quarter, axis=1)
    down = pltpu.roll(x, quarter, axis=1)
    return jnp.where((lane & quarter) == 0, up, down)


IN_PROJ_COLS = 2 * KV_WIDTH
PROJ_ROW_GROUPS = 4


def _proj_rope_kernel(h_ref, w_ref, cos_ref, sin_ref, o_ref, *, rope_cols, scale):
    rows_per = h_ref.shape[0] // PROJ_ROW_GROUPS
    for r in range(PROJ_ROW_GROUPS):
        rows = slice(r * rows_per, (r + 1) * rows_per)
        acc = _dot(h_ref[rows, :], w_ref[...])
        if rope_cols:
            heads = rope_cols // HEAD_DIM
            t = acc[:, :rope_cols]
            t = (t * jnp.tile(cos_ref[rows, :], (1, heads))
                 + _swap_halves(t) * jnp.tile(sin_ref[rows, :], (1, heads)))
            acc = t if rope_cols == acc.shape[1] else jnp.concatenate([t, acc[:, rope_cols:]], axis=1)
        if scale != 1.0:
            acc = acc * scale
        o_ref[rows, :] = acc.astype(o_ref.dtype)


def _proj_rope(h, w, rope_cols, scale):
    B, L, D = h.shape
    N = w.shape[1]
    tn = IN_PROJ_COLS
    tm = _tile(L, 1024)
    cos, sin = _rope_tables(L)
    tab = pl.BlockSpec((tm, HEAD_DIM), lambda b, i, j: (i, 0))
    return pl.pallas_call(
        functools.partial(_proj_rope_kernel, rope_cols=rope_cols, scale=scale),
        out_shape=jax.ShapeDtypeStruct((B, L, N), BF16),
        grid=(B, L // tm, N // tn),
        in_specs=[pl.BlockSpec((None, tm, D), lambda b, i, j: (b, i, 0)),
                  pl.BlockSpec((D, tn), lambda b, i, j: (0, j)),
                  tab, tab],
        out_specs=pl.BlockSpec((None, tm, tn), lambda b, i, j: (b, i, j)),
        compiler_params=_cp(("parallel", "parallel", "arbitrary")),
        name="proj_rope",
    )(h, w, cos, sin)


def _in_proj(x, g, shift, scale, w, rope):
    h = _norm_mod(x, g, shift, scale)
    kv_end = ATT_WIDTH + IN_PROJ_COLS
    q = _proj_rope(h, w[:, :ATT_WIDTH], IN_PROJ_COLS if rope else 0, HEAD_DIM ** -0.5)
    kv = _proj_rope(h, w[:, ATT_WIDTH:kv_end], KV_WIDTH if rope else 0, 1.0)
    rest = _matmul(h, w[:, kv_end:], tm=2048)
    return q, kv, rest


def _nt_dot(a, b):
    return lax.dot_general(a, b, (((1,), (1,)), ((), ())), preferred_element_type=F32)


def _attn_bias_tables():
    T, G = BLOCK, ATT_GROUP
    qi = np.arange(G * T)[:, None] % T
    ki = np.arange(3 * T)[None, :]
    band = np.abs(ki - T - qi) <= WINDOW
    after_start = ki >= T
    before_end = ki < 2 * T
    masks = [band & after_start, band, band & before_end, band & after_start & before_end]
    return jnp.asarray(np.stack([np.where(m, 0.0, NEG) for m in masks]), F32)


def _attn_kernel(*refs, local, n_blocks):
    if local:
        (q_ref, kp_ref, kc_ref, kn_ref, vp_ref, vc_ref, vn_ref,
         kx_ref, vx_ref, bias_ref, sink_ref, g_ref, o_ref, acc_ref) = refs
    else:
        q_ref, kx_ref, vx_ref, sink_ref, g_ref, o_ref, acc_ref = refs
    T = BLOCK
    G = ATT_GROUP
    n_sub = q_ref.shape[0] // T
    for h in range(ATT_KV_HEADS):
        cols = slice(h * HEAD_DIM, (h + 1) * HEAD_DIM)
        sink = jnp.concatenate(
            [jnp.broadcast_to(sink_ref[:, (h * G + g) * HEAD_DIM:(h * G + g) * HEAD_DIM + 1], (T, 1))
             for g in range(G)], axis=0)
        kx = kx_ref[:, cols]
        vx = jnp.concatenate([vx_ref[:, cols], jnp.ones((kx_ref.shape[0], HEAD_DIM), BF16)], axis=1)
        if local:
            k_band = jnp.concatenate([kp_ref[:, cols], kc_ref[:, cols], kn_ref[:, cols]], axis=0)
            v_band = jnp.concatenate([vp_ref[:, cols], vc_ref[:, cols], vn_ref[:, cols]], axis=0)
            v_band = jnp.concatenate([v_band, jnp.ones(v_band.shape, BF16)], axis=1)
        for j in range(n_sub):
            rows = slice(j * T, (j + 1) * T)
            qs = jnp.concatenate(
                [q_ref[rows, (h * G + g) * HEAD_DIM:(h * G + g + 1) * HEAD_DIM] for g in range(G)],
                axis=0)
            s_ctx = _nt_dot(qs, kx)
            m = jnp.maximum(jnp.max(s_ctx, axis=-1, keepdims=True), sink)
            if local:
                blk = pl.program_id(1) * n_sub + j
                is_first = blk == 0
                is_last = blk == n_blocks - 1
                table = jnp.where(is_first, jnp.where(is_last, 3, 0), jnp.where(is_last, 2, 1))
                s_loc = _nt_dot(qs, k_band[j * T:(j + 3) * T]) + bias_ref[table]
                m = jnp.maximum(m, jnp.max(s_loc, axis=-1, keepdims=True))
            o = _dot(jnp.exp((s_ctx - m).astype(BF16)), vx)
            if local:
                o = o + _dot(jnp.exp((s_loc - m).astype(BF16)), v_band[j * T:(j + 3) * T])
            denom = o[:, HEAD_DIM:HEAD_DIM + 1] + jnp.exp(sink - m)
            o = o[:, :HEAD_DIM] * (1.0 / denom)
            for g in range(G):
                acc_ref[rows, (h * G + g) * HEAD_DIM:(h * G + g + 1) * HEAD_DIM] = o[g * T:(g + 1) * T]
    y = acc_ref[...]
    y = y * lax.rsqrt(jnp.mean(y * y, axis=-1, keepdims=True) + EPS) * g_ref[...]
    o_ref[...] = y.astype(o_ref.dtype)


def _attention(q, kv, kv_ctx, sink, g_attn):
    B, L, _ = q.shape
    C = kv_ctx.shape[1]
    local = kv is not None
    T = BLOCK
    nb = L // T
    n_sub = 4 if nb % 4 == 0 else (2 if nb % 2 == 0 else 1)
    sink_row = jnp.repeat(sink.astype(F32), HEAD_DIM).reshape(1, ATT_WIDTH)
    qspec = pl.BlockSpec((None, n_sub * T, ATT_WIDTH), lambda b, i: (b, i, 0))
    in_specs = [qspec]
    args = [q]
    cspecs = [pl.BlockSpec((None, C, KV_WIDTH), functools.partial(lambda b, i, part: (b, 0, part), part=part))
              for part in range(2)]
    row = pl.BlockSpec((1, ATT_WIDTH), lambda b, i: (0, 0))
    if local:
        for part in range(2):
            in_specs += [
                pl.BlockSpec((None, T, KV_WIDTH), functools.partial(
                    lambda b, i, part: (b, jnp.maximum(i * n_sub - 1, 0), part), part=part)),
                pl.BlockSpec((None, n_sub * T, KV_WIDTH), functools.partial(
                    lambda b, i, part: (b, i, part), part=part)),
                pl.BlockSpec((None, T, KV_WIDTH), functools.partial(
                    lambda b, i, part: (b, jnp.minimum((i + 1) * n_sub, nb - 1), part), part=part))]
        bias = _attn_bias_tables()
        in_specs += cspecs + [pl.BlockSpec(bias.shape, lambda b, i: (0, 0, 0))]
        args += [kv] * 6 + [kv_ctx, kv_ctx, bias]
    else:
        in_specs += cspecs
        args += [kv_ctx, kv_ctx]
    in_specs += [row, row]
    args += [sink_row, g_attn.reshape(1, ATT_WIDTH)]
    return pl.pallas_call(
        functools.partial(_attn_kernel, local=local, n_blocks=nb),
        out_shape=jax.ShapeDtypeStruct((B, L, ATT_WIDTH), BF16),
        grid=(B, nb // n_sub),
        in_specs=in_specs,
        out_specs=qspec,
        scratch_shapes=[pltpu.VMEM((n_sub * T, ATT_WIDTH), F32)],
        compiler_params=_cp(("parallel", "arbitrary")),
        name="attention_local" if local else "attention_context",
    )(*args)


def _halo_specs(ts, L, width, col_block):
    nb8 = L // HALO
    per = ts // HALO
    prev = pl.BlockSpec((None, HALO, width),
                        lambda b, i: (b, jnp.maximum(i * per - 1, 0), col_block))
    cur = pl.BlockSpec((None, ts, width), lambda b, i: (b, i, col_block))
    nxt = pl.BlockSpec((None, HALO, width),
                       lambda b, i: (b, jnp.minimum((i + 1) * per, nb8 - 1), col_block))
    return [prev, cur, nxt]


def _with_halo(prev_ref, cur_ref, next_ref):
    i = pl.program_id(1)
    last = pl.num_programs(1) - 1
    prev = jnp.where(i > 0, prev_ref[...], 0.0)
    nxt = jnp.where(i < last, next_ref[...], 0.0)
    return jnp.concatenate([prev, cur_ref[...], nxt], axis=0)


def _pool_kernel(prev_ref, cur_ref, next_ref, w_ref, scale_ref, g_ref, o_ref, *, seq_len):
    ts = cur_ref.shape[0]
    ext = _with_halo(prev_ref, cur_ref, next_ref)
    pos = pl.program_id(1) * ts + lax.broadcasted_iota(jnp.int32, (ts, 1), 0)
    outs = []
    for gidx, win in enumerate(POOL_WINDOWS):
        cols = slice(gidx * POOL_GROUP, (gidx + 1) * POOL_GROUP)
        run = ext[:, cols]
        step = 1
        while step < win:
            run = run[:run.shape[0] - step] + run[step:]
            step *= 2
        lo = HALO - win // 2
        total = run[lo:lo + ts]
        cnt = (jnp.minimum(pos + (win - win // 2), seq_len) - jnp.maximum(pos - win // 2, 0))
        mean = total * (1.0 / cnt.astype(F32))
        outs.append(_dot3(mean - cur_ref[:, cols], w_ref[gidx]))
    y = jnp.concatenate(outs, axis=-1) * scale_ref[...]
    y = y * lax.rsqrt(jnp.mean(y * y, axis=-1, keepdims=True) + EPS) * g_ref[...]
    o_ref[...] = y.astype(o_ref.dtype)


def _pool_mixer(proj, pool_w, pool_scale, g_pool):
    B, L, _ = proj.shape
    ts = _tile(L, 512)
    col_block = 0
    row = pl.BlockSpec((1, POOL_WIDTH), lambda b, i: (0, 0))
    return pl.pallas_call(
        functools.partial(_pool_kernel, seq_len=L),
        out_shape=jax.ShapeDtypeStruct((B, L, POOL_WIDTH), BF16),
        grid=(B, L // ts),
        in_specs=_halo_specs(ts, L, POOL_WIDTH, col_block) + [
            pl.BlockSpec(pool_w.shape, lambda b, i: (0, 0, 0)), row, row],
        out_specs=pl.BlockSpec((None, ts, POOL_WIDTH), lambda b, i: (b, i, 0)),
        compiler_params=_cp(("parallel", "arbitrary")),
        name="pool_mixer",
    )(proj, proj, proj, pool_w, pool_scale.reshape(1, POOL_WIDTH), g_pool.reshape(1, POOL_WIDTH))


def _short_conv_kernel(*refs):
    halo_refs = refs[:9]
    w_ref, b_ref = refs[9:11]
    out_refs = refs[11:]
    ts = out_refs[0].shape[0]
    for part in range(HY_ORDER + 1):
        ext = _with_halo(*halo_refs[3 * part:3 * part + 3])
        cols = slice(part * HY_WIDTH, (part + 1) * HY_WIDTH)
        acc = b_ref[:, cols] + ext[HALO - 1:HALO - 1 + ts] * w_ref[0:1, cols]
        acc = acc + ext[HALO:HALO + ts] * w_ref[1:2, cols]
        acc = acc + ext[HALO + 1:HALO + 1 + ts] * w_ref[2:3, cols]
        out_refs[part][...] = acc


def _short_conv(proj, conv_w, conv_b):
    B, L, _ = proj.shape
    ts = _tile(L, 1024)
    first = POOL_WIDTH // HY_WIDTH
    assert first * HY_WIDTH == POOL_WIDTH
    specs = []
    for part in range(HY_ORDER + 1):
        specs += _halo_specs(ts, L, HY_WIDTH, first + part)
    n_col = (HY_ORDER + 1) * HY_WIDTH
    out = pl.BlockSpec((None, ts, HY_WIDTH), lambda b, i: (b, i, 0))
    return pl.pallas_call(
        _short_conv_kernel,
        out_shape=tuple(jax.ShapeDtypeStruct((B, L, HY_WIDTH), F32) for _ in range(HY_ORDER + 1)),
        grid=(B, L // ts),
        in_specs=specs + [pl.BlockSpec((HY_SHORT, n_col), lambda b, i: (0, 0)),
                          pl.BlockSpec((1, n_col), lambda b, i: (0, 0))],
        out_specs=(out,) * (HY_ORDER + 1),
        compiler_params=_cp(("parallel", "arbitrary")),
        name="hyena_short_conv",
    )(*([proj] * 9), conv_w, conv_b.reshape(1, n_col))


def _filter_tables(L):
    m = np.arange(L, dtype=np.float32)
    t = (m / np.float32(max(L - 1, 1))).astype(np.float32)
    w = (np.float32(2.0 * math.pi) * m / np.float32(L)).astype(np.float32)
    f = np.linspace(1e-4, HY_BANDS - 1, HY_BANDS, dtype=np.float32)
    ang = (w[:, None] * f).astype(np.float64)
    z = np.concatenate([t[:, None].astype(np.float64), np.cos(ang), -np.sin(ang)], axis=-1)
    z = np.pad(z, ((0, 0), (0, LANES - HY_EMB)))
    backward = z[(L - np.arange(L)) % L]
    max_decay = math.log(HY_TARGET) / HY_SHORT_DECAY_PCT
    min_decay = math.log(HY_TARGET) / HY_LONG_DECAY_PCT
    deltas = np.abs(np.linspace(min_decay, max_decay, HY_WIDTH, dtype=np.float32))
    return jnp.asarray(np.concatenate([z, backward]), F32), jnp.asarray(deltas[None], F32)


def _filter_kernel(z_ref, rate_ref, w1_ref, b1_ref, w2_ref, b2_ref, w3a_ref, w3b_ref,
                   b3a_ref, b3b_ref, fr_ref, o_ref, *, seq_len, period):
    tl = z_ref.shape[0]
    n0 = pl.program_id(0) * tl
    active = (n0 < seq_len) | (n0 >= period - seq_len)

    @pl.when(active)
    def _():
        n = n0 + lax.broadcasted_iota(jnp.int32, (tl, 1), 0)
        lag = jnp.where(n < seq_len, n, period - n)
        t = lag.astype(F32) / float(max(seq_len - 1, 1))
        decay = jnp.exp(-t * rate_ref[...])
        fr = fr_ref[...]
        a = jnp.sin(fr * (_dot3(z_ref[...], w1_ref[...]) + b1_ref[...]))
        a = jnp.sin(fr * (_dot3(a, w2_ref[...]) + b2_ref[...]))
        taps = [(_dot3(a, w3_ref[...]) + b3_ref[...]) * decay
                for w3_ref, b3_ref in ((w3a_ref, b3a_ref), (w3b_ref, b3b_ref))]
        o_ref[...] = jnp.where(n == period - seq_len, 0.0, jnp.concatenate(taps, axis=1))

    @pl.when(jnp.logical_not(active))
    def _():
        o_ref[...] = jnp.zeros(o_ref.shape, o_ref.dtype)


def _hyena_filter_taps(L, period, w1, b1, w2, b2, w3, b3, freq):
    z, rates = _filter_tables(L)
    hid = w1.shape[1]
    tl = _tile(L, 512)
    nf = L // tl
    nt = period // tl
    assert HY_ORDER == 2 and period % tl == 0 and period >= 2 * L
    w1p = jnp.pad(w1, ((0, LANES - HY_EMB), (0, 0)))
    b3 = b3.reshape(1, -1)
    full = lambda a: pl.BlockSpec(a.shape, lambda i: (0,) * a.ndim)
    is_bwd = lambda i: jnp.where(i >= nt - nf, 1, 0)
    ztile = lambda i: (jnp.where(i < nf, i, nf + jnp.clip(i - (nt - nf), 0, nf - 1)), 0)
    w3spec = lambda o: pl.BlockSpec((hid, HY_WIDTH), lambda i: (0, 2 * o + is_bwd(i)))
    b3spec = lambda o: pl.BlockSpec((1, HY_WIDTH), lambda i: (0, 2 * o + is_bwd(i)))
    small = [w1p, b1.reshape(1, hid), w2, b2.reshape(1, hid)]
    return pl.pallas_call(
        functools.partial(_filter_kernel, seq_len=L, period=period),
        out_shape=jax.ShapeDtypeStruct((period, HY_ORDER * HY_WIDTH), F32),
        grid=(nt,),
        in_specs=[pl.BlockSpec((tl, LANES), ztile), full(rates)] + [full(a) for a in small]
                 + [w3spec(0), w3spec(1), b3spec(0), b3spec(1), pl.BlockSpec((1, hid), lambda i: (0, 0))],
        out_specs=pl.BlockSpec((tl, HY_ORDER * HY_WIDTH), lambda i: (i, 0)),
        compiler_params=_cp(("arbitrary",)),
        name="hyena_filter_taps",
    )(z, rates, *small, w3, w3, b3, b3, freq.reshape(1, hid))


def _fft_matrices(n1, a_in):
    n = n1 * LANES
    k1 = np.arange(n1)
    a = np.arange(a_in)
    th1 = 2.0 * np.pi * np.outer(k1, a) / n1
    c1, s1 = np.cos(th1), np.sin(th1)
    m1_complex = np.block([[c1, s1], [-s1, c1]])
    th_full = 2.0 * np.pi * np.outer(k1, k1) / n1
    m1_real = np.concatenate([np.cos(th_full), -np.sin(th_full)], axis=0)
    m3 = np.block([[c1.T, -s1.T], [s1.T, c1.T]])
    b = np.arange(LANES)
    k2 = np.arange(LANES)
    k = k1[:, None, None] + n1 * k2[None, :, None]
    th2 = 2.0 * np.pi * (k * b[None, None, :] % n) / n
    c2, s2 = np.cos(th2), np.sin(th2)
    fwd = np.concatenate([np.concatenate([c2, s2], axis=2),
                          np.concatenate([-s2, c2], axis=2)], axis=1)
    c2t, s2t = np.swapaxes(c2, 1, 2), np.swapaxes(s2, 1, 2)
    inv = np.concatenate([np.concatenate([c2t, -s2t], axis=2),
                          np.concatenate([s2t, c2t], axis=2)], axis=1)
    as_bf16 = lambda m: jnp.asarray(m, F32).astype(BF16)
    return as_bf16(m1_complex), as_bf16(m1_real), as_bf16(m3), as_bf16(fwd), as_bf16(inv)


def _fft_stage1_kernel(x_ref, m_ref, o_ref, *, complex_in):
    n1 = o_ref.shape[0]
    for s in range(o_ref.shape[1]):
        if complex_in:
            x = jnp.concatenate([x_ref[0, :, s, :], x_ref[1, :, s, :]], axis=0)
        else:
            x = x_ref[:, s, :]
        y = _dot(m_ref[...], x.astype(BF16))
        o_ref[:, s, :] = _pack_bf16(y[:n1], y[n1:])


def _fft_stage1(x, m1, n1, complex_in):
    P, A, Wt = x.shape[0], x.shape[-3], x.shape[-1]
    tw = _tile(Wt, 512, LANES)
    if complex_in:
        xspec = pl.BlockSpec((None, 2, A, FFT_ROWS, tw), lambda p, i, j: (p, 0, 0, i, j))
    else:
        xspec = pl.BlockSpec((None, A, FFT_ROWS, tw), lambda p, i, j: (p, 0, i, j))
    return pl.pallas_call(
        functools.partial(_fft_stage1_kernel, complex_in=complex_in),
        out_shape=jax.ShapeDtypeStruct((P, n1, LANES, Wt), jnp.uint32),
        grid=(P, LANES // FFT_ROWS, Wt // tw),
        in_specs=[xspec, pl.BlockSpec(m1.shape, lambda p, i, j: (0, 0))],
        out_specs=pl.BlockSpec((None, n1, FFT_ROWS, tw), lambda p, i, j: (p, 0, i, j)),
        compiler_params=_cp(("parallel", "arbitrary", "arbitrary")),
        name="fft_stage1",
    )(x, m1)


def _filter_spectrum_kernel(y_ref, g_ref, o0_ref, o1_ref, *, scale):
    kt = y_ref.shape[0]
    for t in range(kt):
        y = jnp.concatenate(_unpack_bf16_pairs(y_ref[t]), axis=0)
        f = _dot(g_ref[t], y)
        for o, o_ref in enumerate((o0_ref, o1_ref)):
            cols = slice(o * HY_WIDTH, (o + 1) * HY_WIDTH)
            o_ref[0, t] = (f[:LANES, cols] * scale).astype(o_ref.dtype)
            o_ref[1, t] = (f[LANES:, cols] * scale).astype(o_ref.dtype)


def _filter_spectrum(taps, mats, n1):
    n_col = taps.shape[1]
    _, m1_real, _, g_fwd, _ = mats
    y = _fft_stage1(taps.reshape(1, n1, LANES, n_col), m1_real, n1, complex_in=False)
    kt = _tile(n1, 8, 1)
    out = jax.ShapeDtypeStruct((2, n1, LANES, HY_WIDTH), BF16)
    ospec = pl.BlockSpec((2, kt, LANES, HY_WIDTH), lambda i: (0, i, 0, 0))
    return pl.pallas_call(
        functools.partial(_filter_spectrum_kernel, scale=1.0 / (n1 * LANES)),
        out_shape=(out, out),
        grid=(n1 // kt,),
        in_specs=[pl.BlockSpec((None, kt, LANES, n_col), lambda i: (0, i, 0, 0)),
                  pl.BlockSpec((kt, 2 * LANES, 2 * LANES), lambda i: (i, 0, 0))],
        out_specs=(ospec, ospec),
        compiler_params=_cp(("arbitrary",)),
        name="hyena_filter_spectrum",
    )(y, g_fwd)


def _fft_mid_kernel(y_ref, kf_ref, g_ref, gi_ref, o_ref):
    kt = y_ref.shape[0]
    for t in range(kt):
        y = jnp.concatenate(_unpack_bf16_pairs(y_ref[t]), axis=0)
        f = _dot(g_ref[t], y)
        fr, fi = f[:LANES], f[LANES:]
        kr, ki = kf_ref[0, t].astype(F32), kf_ref[1, t].astype(F32)
        p = jnp.concatenate([fr * kr - fi * ki, fr * ki + fi * kr], axis=0).astype(BF16)
        u = _dot(gi_ref[t], p)
        o_ref[t] = _pack_bf16(u[:LANES], u[LANES:])


def _fft_mid(y, kf, g_fwd, g_inv):
    P, n1, _, W = y.shape
    kt = _tile(n1, 16, 1)
    yspec = pl.BlockSpec((None, kt, LANES, W), lambda i, p: (p, i, 0, 0))
    gspec = pl.BlockSpec((kt, 2 * LANES, 2 * LANES), lambda i, p: (i, 0, 0))
    return pl.pallas_call(
        _fft_mid_kernel,
        out_shape=jax.ShapeDtypeStruct(y.shape, jnp.uint32),
        grid=(n1 // kt, P),
        in_specs=[yspec, pl.BlockSpec((2, kt, LANES, W), lambda i, p: (0, i, 0, 0)), gspec, gspec],
        out_specs=yspec,
        compiler_params=_cp(("arbitrary", "arbitrary")),
        name="fft_mid",
    )(y, kf, g_fwd, g_inv)


def _fft_stage3_kernel(u_ref, m_ref, z_ref, x_ref, bias_ref, g_ref, o_ref, ur_ref, ui_ref,
                       *, normalise):
    a_out = z_ref.shape[1]
    bias = bias_ref[...]
    packed = u_ref[...]
    ur_ref[...] = lax.bitcast_convert_type(packed << 16, F32)
    ui_ref[...] = lax.bitcast_convert_type(packed & jnp.uint32(0xFFFF0000), F32)
    for s in range(o_ref.shape[2]):
        u = jnp.concatenate([ur_ref[:, s, :], ui_ref[:, s, :]], axis=0).astype(BF16)
        conv = _dot(m_ref[...], u)
        for part in range(2):
            y = x_ref[part, :, s, :] * (
                conv[part * a_out:(part + 1) * a_out] + z_ref[part, :, s, :] * bias)
            if normalise:
                y = y * lax.rsqrt(jnp.mean(y * y, axis=-1, keepdims=True) + EPS) * g_ref[...]
            o_ref[part, :, s, :] = y


def _fft_stage3(u, m3, z, gate, bias_row, norm_g=None):
    P, n1, _, W = u.shape
    A = z.shape[2]
    pair = pl.BlockSpec((None, 2, A, FFT_ROWS, W), lambda p, i: (p, 0, 0, i, 0))
    row = pl.BlockSpec((1, W), lambda p, i: (0, 0))
    normalise = norm_g is not None
    return pl.pallas_call(
        functools.partial(_fft_stage3_kernel, normalise=normalise),
        out_shape=jax.ShapeDtypeStruct(z.shape, F32),
        grid=(P, LANES // FFT_ROWS),
        in_specs=[pl.BlockSpec((None, n1, FFT_ROWS, W), lambda p, i: (p, 0, i, 0)),
                  pl.BlockSpec(m3.shape, lambda p, i: (0, 0)),
                  pair, pair, row, row],
        out_specs=pair,
        scratch_shapes=[pltpu.VMEM((n1, FFT_ROWS, W), F32), pltpu.VMEM((n1, FFT_ROWS, W), F32)],
        compiler_params=_cp(("parallel", "arbitrary")),
        name="fft_stage3",
    )(u, m3, z, gate, bias_row, norm_g.reshape(1, W) if normalise else bias_row)


def _hyena_mixer(v, gates, kfs, d_bias, norm_g, mats, n1):
    B, L, W = v.shape
    pair_shape = (B // 2, 2, L // LANES, LANES, W)
    m1_complex, _, m3, g_fwd, g_inv = mats
    z = v.reshape(pair_shape)
    for o in range(HY_ORDER):
        y = _fft_stage1(z, m1_complex, n1, complex_in=True)
        u = _fft_mid(y, kfs[o], g_fwd, g_inv)
        z = _fft_stage3(u, m3, z, gates[o].reshape(pair_shape), d_bias[o].reshape(1, W),
                        norm_g if o == HY_ORDER - 1 else None)
    return z.reshape(B, L, W)


def kernel(x, c, ctx, c_ctx, w_mod, b_mod, norm1_g, w_in, attn_sink, pool_w, pool_scale,
           hy_conv_w, hy_conv_b, hy_f_w1, hy_f_b1, hy_f_w2, hy_f_b2, hy_f_w3, hy_f_b3,
           hy_f_freq, hy_bias, g_attn, g_pool, g_hyena, w_out, norm2_g,
           ff_w1, ff_w3, ff_w2, router_w, moe_w1, moe_w3, moe_w2, final_g):
    B, S, D = x.shape
    C = ctx.shape[1]
    depth = w_mod.shape[0]
    assert B % 2 == 0 and S % BLOCK == 0
    fft_mats = {}

    c_rows = jnp.concatenate([c, c_ctx[None], jnp.zeros((SUBLANES - B - 1, D), F32)], axis=0)
    xc = ctx

    def hyena_branch(l, proj, seq_len):
        rows = -(-seq_len // FFT_MIN_ROWS) * FFT_MIN_ROWS
        n1 = 2 * rows // LANES
        if rows not in fft_mats:
            fft_mats[rows] = _fft_matrices(n1, rows // LANES)
        mats = fft_mats[rows]
        taps = _hyena_filter_taps(seq_len, 2 * rows, hy_f_w1[l], hy_f_b1[l], hy_f_w2[l], hy_f_b2[l],
                                  hy_f_w3[l], hy_f_b3[l], hy_f_freq[l])
        v, x1, x2 = _short_conv(proj, hy_conv_w[l], hy_conv_b[l])
        if seq_len < rows:
            pad = lambda t: jnp.pad(t, ((0, 0), (0, rows - seq_len), (0, 0)))
            v, x1, x2 = pad(v), pad(x1), pad(x2)
        kfs = _filter_spectrum(taps, mats, n1)
        z = _hyena_mixer(v, (x1, x2), kfs, hy_bias[l], g_hyena[l], mats, n1)
        return z[:, :seq_len]

    def mix(l, proj, y_att, res, gate, w_out_l, seq_len, norm=None):
        y_pool = _pool_mixer(proj, pool_w[l], pool_scale[l], g_pool[l])
        y_hy = hyena_branch(l, proj, seq_len)
        return _mix_out([y_att, y_pool, y_hy], w_out_l, res, gate, norm)

    for l in range(depth):
        last = l == depth - 1
        mod = _adaln(c_rows, w_mod, b_mod, l)
        sh1, sc1, g1, sh2, sc2, g2 = [mod[:B, j * D:(j + 1) * D] for j in range(6)]
        csh1, csc1, cg1, csh2, csc2, cg2 = [mod[B:B + 1, j * D:(j + 1) * D] for j in range(6)]
        w_in_l = w_in[l].astype(BF16)
        w_out_l = w_out[l].astype(BF16)

        if last:
            hc = _norm_mod(xc, norm1_g[l], csh1, csc1)
            kv_w = w_in_l[:, ATT_WIDTH:ATT_WIDTH + 2 * KV_WIDTH]
            kv_c = _matmul(hc, kv_w, out_dtype=BF16)
        else:
            q_c, kv_c, proj_c = _in_proj(xc, norm1_g[l], csh1, csc1, w_in_l, rope=False)
            y_att_c = _attention(q_c, None, kv_c, attn_sink[l], g_attn[l])
            xc_new, hc2 = mix(l, proj_c, y_att_c, xc, cg1, w_out_l, C, (norm2_g[l], csh2, csc2))
            i = l // 2
            if l % 2 == 0:
                hid_c = _swiglu_hidden(hc2, ff_w1[i].astype(BF16), ff_w3[i].astype(BF16))
                xc_new = _matmul_gated_residual([hid_c], ff_w2[i].astype(BF16), xc_new, cg2)
            else:
                raise NotImplementedError("context tokens through an expert layer")

        q, kv, proj = _in_proj(x, norm1_g[l], sh1, sc1, w_in_l, rope=True)
        y_att = _attention(q, kv, kv_c, attn_sink[l], g_attn[l])

        i = l // 2
        if l % 2 == 0:
            x, h2 = mix(l, proj, y_att, x, g1, w_out_l, S, (norm2_g[l], sh2, sc2))
            hid = _swiglu_hidden(h2, ff_w1[i].astype(BF16), ff_w3[i].astype(BF16))
            x = _matmul_gated_residual([hid], ff_w2[i].astype(BF16), x, g2)
            if last:
                x = _rmsnorm(x, final_g, F32)
        else:
            assert last
            x = mix(l, proj, y_att, x, g1, w_out_l, S)
            h2, gi, gw, counts = _norm_mod_router(x, norm2_g[l], sh2, sc2, router_w[i])
            x = _moe_layer(x, h2, gi, gw, counts, g2, final_g, moe_w1[i], moe_w3[i], moe_w2[i])
        if not last:
            xc = xc_new
    return x
```
